```python
import jax, jax.numpy as jnp
from jax import lax
import numpy as np

D_MODEL = 2048
BATCH = 8
SEQ = 2048
DEPTH = 4

CHUNK = 64
Q_BLOCK = 128
N_MIXERS = 2
N_MLA_LAYERS = (DEPTH + 1) // 2
N_FOX_LAYERS = DEPTH // 2

MLA_HEADS = 16
MLA_Q_LORA = 512
MLA_KV_LORA = 512
MLA_NOPE_DIM = 128
MLA_ROPE_DIM = 64
MLA_V_DIM = 128
ROPE_THETA = 10000.0

FOX_HEADS = 16
FOX_HEAD_DIM = D_MODEL // FOX_HEADS

D_FF = -(-8 * D_MODEL // (3 * 256)) * 256

DEEPNORM_ALPHA = (2 * DEPTH) ** 0.25
DEEPNORM_BETA = (8 * DEPTH) ** -0.25
LN_EPS = 1e-5
RMS_EPS = 1e-6

kernel_name = "hybrid_mla_fox_deepnorm_adaln_trunk"


def _layer_norm(x, g, b):
    xf = x.astype(jnp.float32)
    mu = jnp.mean(xf, axis=-1, keepdims=True)
    var = jnp.mean(jnp.square(xf - mu), axis=-1, keepdims=True)
    y = (xf - mu) * lax.rsqrt(var + LN_EPS)
    return (y * g.astype(jnp.float32) + b.astype(jnp.float32)).astype(x.dtype)


def _rms_norm(x, g):
    xf = x.astype(jnp.float32)
    y = xf * lax.rsqrt(jnp.mean(jnp.square(xf), axis=-1, keepdims=True) + RMS_EPS)
    return (y * g.astype(jnp.float32)).astype(x.dtype)


def _rope_cos_sin(positions, dtype):
    inv_freq = ROPE_THETA ** (-jnp.arange(0, MLA_ROPE_DIM, 2, dtype=jnp.float32) / MLA_ROPE_DIM)
    ang = positions.astype(jnp.float32)[..., None] * inv_freq
    return jnp.cos(ang).astype(dtype), jnp.sin(ang).astype(dtype)


def _rope(x, cos, sin):
    x2 = x.reshape(*x.shape[:-1], MLA_ROPE_DIM // 2, 2)
    x0, x1 = x2[..., 0], x2[..., 1]
    out = jnp.stack([x0 * cos - x1 * sin, x0 * sin + x1 * cos], axis=-1)
    return out.reshape(x.shape)


def _mla(h, positions, w_down, q_norm, w_uq, kv_norm, w_uk, w_uv, w_o):
    B, S, _ = h.shape
    H = MLA_HEADS
    lat = h @ w_down
    q_lat, kv_lat, k_pe = jnp.split(lat, [MLA_Q_LORA, MLA_Q_LORA + MLA_KV_LORA], axis=-1)
    q = (_rms_norm(q_lat, q_norm) @ w_uq).reshape(B, S, H, MLA_NOPE_DIM + MLA_ROPE_DIM)
    q_nope, q_pe = q[..., :MLA_NOPE_DIM], q[..., MLA_NOPE_DIM:]
    cos, sin = _rope_cos_sin(positions, h.dtype)
    q_pe = _rope(q_pe, cos[:, :, None, :], sin[:, :, None, :])
    k_pe = _rope(k_pe, cos, sin)
    c_kv = _rms_norm(kv_lat, kv_norm)
    k_nope = (c_kv @ w_uk).reshape(B, S, H, MLA_NOPE_DIM)
    v = (c_kv @ w_uv).reshape(B, S, H, MLA_V_DIM)
    scale = (MLA_NOPE_DIM + MLA_ROPE_DIM) ** -0.5
    chunk_id = jnp.arange(S) // CHUNK
    outs = []
    for qs in range(0, S, Q_BLOCK):
        ke = qs + Q_BLOCK
        s = (jnp.einsum('bqhd,bkhd->bhqk', q_nope[:, qs:ke], k_nope[:, :ke])
             + jnp.einsum('bqhr,bkr->bhqk', q_pe[:, qs:ke], k_pe[:, :ke])).astype(jnp.float32) * scale
        mask = chunk_id[qs:ke, None] >= chunk_id[None, :ke]
        p = jax.nn.softmax(jnp.where(mask, s, -jnp.inf), axis=-1).astype(v.dtype)
        outs.append(jnp.einsum('bhqk,bkhd->bqhd', p, v[:, :ke]))
    o = jnp.concatenate(outs, axis=1).reshape(B, S, H * MLA_V_DIM)
    return o @ w_o


def _fox(h, w_in, b_f, w_o):
    B, S, D = h.shape
    H, Dh = FOX_HEADS, FOX_HEAD_DIM
    proj = h @ w_in
    q, k, v, f_logit = jnp.split(proj, [D, 2 * D, 3 * D], axis=-1)
    q = q.reshape(B, S, H, Dh)
    k = k.reshape(B, S, H, Dh)
    v = v.reshape(B, S, H, Dh)
    log_f = jax.nn.log_sigmoid((f_logit + b_f).astype(jnp.float32))
    cum = jnp.transpose(lax.cumsum(log_f, axis=1), (0, 2, 1))
    scale = Dh ** -0.5
    pos = jnp.arange(S)
    outs = []
    for qs in range(0, S, Q_BLOCK):
        ke = qs + Q_BLOCK
        s = jnp.einsum('bqhd,bkhd->bhqk', q[:, qs:ke], k[:, :ke]).astype(jnp.float32) * scale
        s = s + cum[:, :, qs:ke, None] - cum[:, :, None, :ke]
        mask = pos[qs:ke, None] >= pos[None, :ke]
        p = jax.nn.softmax(jnp.where(mask, s, -jnp.inf), axis=-1).astype(v.dtype)
        outs.append(jnp.einsum('bhqk,bkhd->bqhd', p, v[:, :ke]))
    o = jnp.concatenate(outs, axis=1).reshape(B, S, D)
    return o @ w_o


def _swiglu(h, w1, w3, w2):
    return (jax.nn.silu(h @ w1) * (h @ w3)) @ w2


def _dense(key, shape, fan_in, scale=1.0):
    return jax.random.normal(key, shape, jnp.float32) * (scale * fan_in ** -0.5)


def _fwd_setup_inputs(seed: int = 0) -> dict:
    key = jax.random.key(seed)
    ks = jax.random.split(key, 32)
    D, L, NA, NB = D_MODEL, DEPTH, N_MLA_LAYERS, N_FOX_LAYERS
    HA, HB = MLA_HEADS, FOX_HEADS
    x = jax.random.normal(ks[0], (BATCH, SEQ, D), jnp.float32)
    c = jax.random.normal(ks[1], (BATCH, D), jnp.float32)
    offset = jax.random.randint(ks[2], (BATCH, 1), 0, 16, dtype=jnp.int32) * CHUNK
    positions = offset + jnp.arange(SEQ, dtype=jnp.int32)[None, :]

    ada_w = _dense(ks[3], (L, D, 6 * D), D, 0.5)
    ada_b = 0.01 * jax.random.normal(ks[4], (L, 6 * D), jnp.float32)
    ln1_g = 1.0 + 0.02 * jax.random.normal(ks[5], (L, D), jnp.float32)
    ln1_b = 0.02 * jax.random.normal(ks[6], (L, D), jnp.float32)
    ln2_g = 1.0 + 0.02 * jax.random.normal(ks[7], (L, D), jnp.float32)
    ln2_b = 0.02 * jax.random.normal(ks[8], (L, D), jnp.float32)
    ffn_w1 = _dense(ks[9], (L, D, D_FF), D)
    ffn_w3 = _dense(ks[10], (L, D, D_FF), D)
    ffn_w2 = _dense(ks[11], (L, D_FF, D), D_FF, DEEPNORM_BETA)

    mla_w_down = _dense(ks[12], (NA, D, MLA_Q_LORA + MLA_KV_LORA + MLA_ROPE_DIM), D)
    mla_q_norm = 1.0 + 0.02 * jax.random.normal(ks[13], (NA, MLA_Q_LORA), jnp.float32)
    mla_w_uq = _dense(ks[14], (NA, MLA_Q_LORA, HA * (MLA_NOPE_DIM + MLA_ROPE_DIM)), MLA_Q_LORA)
    mla_kv_norm = 1.0 + 0.02 * jax.random.normal(ks[15], (NA, MLA_KV_LORA), jnp.float32)
    mla_w_uk = _dense(ks[16], (NA, MLA_KV_LORA, HA * MLA_NOPE_DIM), MLA_KV_LORA)
    mla_w_uv = _dense(ks[17], (NA, MLA_KV_LORA, HA * MLA_V_DIM), MLA_KV_LORA, DEEPNORM_BETA)
    mla_w_o = _dense(ks[18], (NA, HA * MLA_V_DIM, D), HA * MLA_V_DIM, DEEPNORM_BETA)

    fox_wq = _dense(ks[19], (NB, D, D), D)
    fox_wk = _dense(ks[20], (NB, D, D), D)
    fox_wv = _dense(ks[21], (NB, D, D), D, DEEPNORM_BETA)
    fox_wf = _dense(ks[22], (NB, D, HB), D)
    fox_w_in = jnp.concatenate([fox_wq, fox_wk, fox_wv, fox_wf], axis=-1)
    fox_b_f = jax.random.uniform(ks[23], (NB, HB), jnp.float32, 1.0, 4.0)
    fox_w_o = _dense(ks[24], (NB, D, D), D, DEEPNORM_BETA)

    return {
        "x": x, "c": c, "positions": positions,
        "ada_w": ada_w, "ada_b": ada_b,
        "ln1_g": ln1_g, "ln1_b": ln1_b, "ln2_g": ln2_g, "ln2_b": ln2_b,
        "ffn_w1": ffn_w1, "ffn_w3": ffn_w3, "ffn_w2": ffn_w2,
        "mla_w_down": mla_w_down, "mla_q_norm": mla_q_norm, "mla_w_uq": mla_w_uq,
        "mla_kv_norm": mla_kv_norm, "mla_w_uk": mla_w_uk, "mla_w_uv": mla_w_uv, "mla_w_o": mla_w_o,
        "fox_w_in": fox_w_in, "fox_b_f": fox_b_f, "fox_w_o": fox_w_o,
    }


def _fwd_reference(x, c, positions, ada_w, ada_b, ln1_g, ln1_b, ln2_g, ln2_b,
              ffn_w1, ffn_w3, ffn_w2,
              mla_w_down, mla_q_norm, mla_w_uq, mla_kv_norm, mla_w_uk, mla_w_uv, mla_w_o,
              fox_w_in, fox_b_f, fox_w_o):
    c_act = jax.nn.silu(c)
    for i in range(DEPTH):
        mod = c_act @ ada_w[i] + ada_b[i]
        sh_a, sc_a, g_a, sh_f, sc_f, g_f = [m[:, None, :] for m in jnp.split(mod, 6, axis=-1)]

        h = x * (1.0 + sc_a) + sh_a
        j = i // N_MIXERS
        if i % N_MIXERS == 0:
            y = _mla(h, positions, mla_w_down[j], mla_q_norm[j], mla_w_uq[j],
                     mla_kv_norm[j], mla_w_uk[j], mla_w_uv[j], mla_w_o[j])
        else:
            y = _fox(h, fox_w_in[j], fox_b_f[j], fox_w_o[j])
        x = _layer_norm(DEEPNORM_ALPHA * x + (1.0 + g_a) * y, ln1_g[i], ln1_b[i])

        h = x * (1.0 + sc_f) + sh_f
        y = _swiglu(h, ffn_w1[i], ffn_w3[i], ffn_w2[i])
        x = _layer_norm(DEEPNORM_ALPHA * x + (1.0 + g_f) * y, ln2_g[i], ln2_b[i])
    return x


import jax as _jax
import jax.numpy as _jnp

TWIN_FORMAT = 'train_step'
FWD_PARAMS = ['x', 'c', 'positions', 'ada_w', 'ada_b', 'ln1_g', 'ln1_b', 'ln2_g', 'ln2_b', 'ffn_w1', 'ffn_w3', 'ffn_w2', 'mla_w_down', 'mla_q_norm', 'mla_w_uq', 'mla_kv_norm', 'mla_w_uk', 'mla_w_uv', 'mla_w_o', 'fox_w_in', 'fox_b_f', 'fox_w_o']
TWIN_WEIGHTS = ['ada_w', 'ada_b', 'ln1_g', 'ln1_b', 'ln2_g', 'ln2_b', 'ffn_w1', 'ffn_w3', 'ffn_w2', 'mla_w_down', 'mla_q_norm', 'mla_w_uq', 'mla_kv_norm', 'mla_w_uk', 'mla_w_uv', 'mla_w_o', 'fox_w_in', 'fox_b_f', 'fox_w_o']
TWIN_DIFF_INPUT = 'x'
TWIN_INPUTS = ['x', 'c', 'positions', 'ada_w', 'ada_b', 'ln1_g', 'ln1_b', 'ln2_g', 'ln2_b', 'ffn_w1', 'ffn_w3', 'ffn_w2', 'mla_w_down', 'mla_q_norm', 'mla_w_uq', 'mla_kv_norm', 'mla_w_uk', 'mla_w_uv', 'mla_w_o', 'fox_w_in', 'fox_b_f', 'fox_w_o', 'loss_target', 'm_ada_w', 'm_ada_b', 'm_ln1_g', 'm_ln1_b', 'm_ln2_g', 'm_ln2_b', 'm_ffn_w1', 'm_ffn_w3', 'm_ffn_w2', 'm_mla_w_down', 'm_mla_q_norm', 'm_mla_w_uq', 'm_mla_kv_norm', 'm_mla_w_uk', 'm_mla_w_uv', 'm_mla_w_o', 'm_fox_w_in', 'm_fox_b_f', 'm_fox_w_o', 'v_ada_w', 'v_ada_b', 'v_ln1_g', 'v_ln1_b', 'v_ln2_g', 'v_ln2_b', 'v_ffn_w1', 'v_ffn_w3', 'v_ffn_w2', 'v_mla_w_down', 'v_mla_q_norm', 'v_mla_w_uq', 'v_mla_kv_norm', 'v_mla_w_uk', 'v_mla_w_uv', 'v_mla_w_o', 'v_fox_w_in', 'v_fox_b_f', 'v_fox_w_o']
TWIN_OUTPUTS = ['loss', 'grad_x', 'grad_ada_w', 'grad_ada_b', 'grad_ln1_g', 'grad_ln1_b', 'grad_ln2_g', 'grad_ln2_b', 'grad_ffn_w1', 'grad_ffn_w3', 'grad_ffn_w2', 'grad_mla_w_down', 'grad_mla_q_norm', 'grad_mla_w_uq', 'grad_mla_kv_norm', 'grad_mla_w_uk', 'grad_mla_w_uv', 'grad_mla_w_o', 'grad_fox_w_in', 'grad_fox_b_f', 'grad_fox_w_o', 'delta_ada_w', 'delta_ada_b', 'delta_ln1_g', 'delta_ln1_b', 'delta_ln2_g', 'delta_ln2_b', 'delta_ffn_w1', 'delta_ffn_w3', 'delta_ffn_w2', 'delta_mla_w_down', 'delta_mla_q_norm', 'delta_mla_w_uq', 'delta_mla_kv_norm', 'delta_mla_w_uk', 'delta_mla_w_uv', 'delta_mla_w_o', 'delta_fox_w_in', 'delta_fox_b_f', 'delta_fox_w_o', 'new_m_ada_w', 'new_m_ada_b', 'new_m_ln1_g', 'new_m_ln1_b', 'new_m_ln2_g', 'new_m_ln2_b', 'new_m_ffn_w1', 'new_m_ffn_w3', 'new_m_ffn_w2', 'new_m_mla_w_down', 'new_m_mla_q_norm', 'new_m_mla_w_uq', 'new_m_mla_kv_norm', 'new_m_mla_w_uk', 'new_m_mla_w_uv', 'new_m_mla_w_o', 'new_m_fox_w_in', 'new_m_fox_b_f', 'new_m_fox_w_o', 'new_v_ada_w', 'new_v_ada_b', 'new_v_ln1_g', 'new_v_ln1_b', 'new_v_ln2_g', 'new_v_ln2_b', 'new_v_ffn_w1', 'new_v_ffn_w3', 'new_v_ffn_w2', 'new_v_mla_w_down', 'new_v_mla_q_norm', 'new_v_mla_w_uq', 'new_v_mla_kv_norm', 'new_v_mla_w_uk', 'new_v_mla_w_uv', 'new_v_mla_w_o', 'new_v_fox_w_in', 'new_v_fox_b_f', 'new_v_fox_w_o']
TWIN_LEAF_KINDS = {'loss': 'loss', 'grad_x': 'grad_x', 'grad_ada_w': 'grad_w', 'grad_ada_b': 'grad_w', 'grad_ln1_g': 'grad_w', 'grad_ln1_b': 'grad_w', 'grad_ln2_g': 'grad_w', 'grad_ln2_b': 'grad_w', 'grad_ffn_w1': 'grad_w', 'grad_ffn_w3': 'grad_w', 'grad_ffn_w2': 'grad_w', 'grad_mla_w_down': 'grad_w', 'grad_mla_q_norm': 'grad_w', 'grad_mla_w_uq': 'grad_w', 'grad_mla_kv_norm': 'grad_w', 'grad_mla_w_uk': 'grad_w', 'grad_mla_w_uv': 'grad_w', 'grad_mla_w_o': 'grad_w', 'grad_fox_w_in': 'grad_w', 'grad_fox_b_f': 'grad_w', 'grad_fox_w_o': 'grad_w', 'delta_ada_w': 'delta_w', 'delta_ada_b': 'delta_w', 'delta_ln1_g': 'delta_w', 'delta_ln1_b': 'delta_w', 'delta_ln2_g': 'delta_w', 'delta_ln2_b': 'delta_w', 'delta_ffn_w1': 'delta_w', 'delta_ffn_w3': 'delta_w', 'delta_ffn_w2': 'delta_w', 'delta_mla_w_down': 'delta_w', 'delta_mla_q_norm': 'delta_w', 'delta_mla_w_uq': 'delta_w', 'delta_mla_kv_norm': 'delta_w', 'delta_mla_w_uk': 'delta_w', 'delta_mla_w_uv': 'delta_w', 'delta_mla_w_o': 'delta_w', 'delta_fox_w_in': 'delta_w', 'delta_fox_b_f': 'delta_w', 'delta_fox_w_o': 'delta_w', 'new_m_ada_w': 'new_m', 'new_m_ada_b': 'new_m', 'new_m_ln1_g': 'new_m', 'new_m_ln1_b': 'new_m', 'new_m_ln2_g': 'new_m', 'new_m_ln2_b': 'new_m', 'new_m_ffn_w1': 'new_m', 'new_m_ffn_w3': 'new_m', 'new_m_ffn_w2': 'new_m', 'new_m_mla_w_down': 'new_m', 'new_m_mla_q_norm': 'new_m', 'new_m_mla_w_uq': 'new_m', 'new_m_mla_kv_norm': 'new_m', 'new_m_mla_w_uk': 'new_m', 'new_m_mla_w_uv': 'new_m', 'new_m_mla_w_o': 'new_m', 'new_m_fox_w_in': 'new_m', 'new_m_fox_b_f': 'new_m', 'new_m_fox_w_o': 'new_m', 'new_v_ada_w': 'new_v', 'new_v_ada_b': 'new_v', 'new_v_ln1_g': 'new_v', 'new_v_ln1_b': 'new_v', 'new_v_ln2_g': 'new_v', 'new_v_ln2_b': 'new_v', 'new_v_ffn_w1': 'new_v', 'new_v_ffn_w3': 'new_v', 'new_v_ffn_w2': 'new_v', 'new_v_mla_w_down': 'new_v', 'new_v_mla_q_norm': 'new_v', 'new_v_mla_w_uq': 'new_v', 'new_v_mla_kv_norm': 'new_v', 'new_v_mla_w_uk': 'new_v', 'new_v_mla_w_uv': 'new_v', 'new_v_mla_w_o': 'new_v', 'new_v_fox_w_in': 'new_v', 'new_v_fox_b_f': 'new_v', 'new_v_fox_w_o': 'new_v'}


def _forward(args):
    return _fwd_reference(*[args[k] for k in FWD_PARAMS])


def _output_shape():
    out = _jax.eval_shape(lambda: _forward(_fwd_setup_inputs(0)))
    return out.shape, out.dtype

N_MICROBATCH = 1
ADAM_LR = 0.001
ADAM_B1 = 0.9
ADAM_B2 = 0.999
ADAM_EPS = 1e-08
ADAM_WD = 0.01
ADAM_STEP = 10
PER_EXAMPLE_BATCH_AXIS = {'x': 0, 'c': 0, 'positions': 0, 'loss_target': 0}
SHARED_INPUTS = []
_WEIGHT_DTYPES = {'ada_w': _jnp.float32, 'ada_b': _jnp.float32, 'ln1_g': _jnp.float32, 'ln1_b': _jnp.float32, 'ln2_g': _jnp.float32, 'ln2_b': _jnp.float32, 'ffn_w1': _jnp.float32, 'ffn_w3': _jnp.float32, 'ffn_w2': _jnp.float32, 'mla_w_down': _jnp.float32, 'mla_q_norm': _jnp.float32, 'mla_w_uq': _jnp.float32, 'mla_kv_norm': _jnp.float32, 'mla_w_uk': _jnp.float32, 'mla_w_uv': _jnp.float32, 'mla_w_o': _jnp.float32, 'fox_w_in': _jnp.float32, 'fox_b_f': _jnp.float32, 'fox_w_o': _jnp.float32}
MOMENT_SCALE = {'ada_w': 6.839471e-03, 'ada_b': 1.238720e-02, 'ln1_g': 2.639517e-01, 'ln1_b': 1.396238e-01, 'ln2_g': 4.024060e+00, 'ln2_b': 2.859903e-01, 'ffn_w1': 7.360037e-03, 'ffn_w3': 7.164444e-03, 'ffn_w2': 2.820388e-02, 'mla_w_down': 3.788804e-03, 'mla_q_norm': 1.904201e-03, 'mla_w_uq': 8.003651e-04, 'mla_kv_norm': 5.734781e-03, 'mla_w_uk': 7.870546e-04, 'mla_w_uv': 6.670029e-03, 'mla_w_o': 6.683670e-03, 'fox_w_in': 5.749420e-03, 'fox_b_f': 1.568333e-02, 'fox_w_o': 9.485568e-03}


def _to_microbatches(a, axis):
    t = _jnp.moveaxis(a, axis, 0)
    t = t.reshape((N_MICROBATCH, t.shape[0] // N_MICROBATCH) + t.shape[1:])
    return _jnp.moveaxis(t, 1, axis + 1)


def setup_inputs(seed: int = 0) -> dict:
    inp = _fwd_setup_inputs(seed)
    key = _jax.random.fold_in(_jax.random.key(seed), 7919)
    shape, _ = _output_shape()
    out = dict(inp)
    out["loss_target"] = _jax.random.normal(_jax.random.fold_in(key, 0), shape, _jnp.float32)
    for i, name in enumerate(TWIN_WEIGHTS):
        w = inp[name].astype(_jnp.float32)
        if MOMENT_SCALE is None:
            s = _jnp.sqrt(_jnp.mean(_jnp.square(w)) + 1e-30)
        else:
            s = MOMENT_SCALE[name]
        km, kv = _jax.random.split(_jax.random.fold_in(key, i + 1))
        out[name] = w
        out["m_" + name] = s * _jax.random.normal(km, w.shape, _jnp.float32)
        out["v_" + name] = (s * s) * _jax.random.uniform(kv, w.shape, _jnp.float32, 0.5, 1.5)
    if N_MICROBATCH > 1:
        for name, axis in PER_EXAMPLE_BATCH_AXIS.items():
            out[name] = _to_microbatches(out[name], axis)
    return {'x': out['x'], 'c': out['c'], 'positions': out['positions'], 'ada_w': out['ada_w'], 'ada_b': out['ada_b'], 'ln1_g': out['ln1_g'], 'ln1_b': out['ln1_b'], 'ln2_g': out['ln2_g'], 'ln2_b': out['ln2_b'], 'ffn_w1': out['ffn_w1'], 'ffn_w3': out['ffn_w3'], 'ffn_w2': out['ffn_w2'], 'mla_w_down': out['mla_w_down'], 'mla_q_norm': out['mla_q_norm'], 'mla_w_uq': out['mla_w_uq'], 'mla_kv_norm': out['mla_kv_norm'], 'mla_w_uk': out['mla_w_uk'], 'mla_w_uv': out['mla_w_uv'], 'mla_w_o': out['mla_w_o'], 'fox_w_in': out['fox_w_in'], 'fox_b_f': out['fox_b_f'], 'fox_w_o': out['fox_w_o'], 'loss_target': out['loss_target'], 'm_ada_w': out['m_ada_w'], 'm_ada_b': out['m_ada_b'], 'm_ln1_g': out['m_ln1_g'], 'm_ln1_b': out['m_ln1_b'], 'm_ln2_g': out['m_ln2_g'], 'm_ln2_b': out['m_ln2_b'], 'm_ffn_w1': out['m_ffn_w1'], 'm_ffn_w3': out['m_ffn_w3'], 'm_ffn_w2': out['m_ffn_w2'], 'm_mla_w_down': out['m_mla_w_down'], 'm_mla_q_norm': out['m_mla_q_norm'], 'm_mla_w_uq': out['m_mla_w_uq'], 'm_mla_kv_norm': out['m_mla_kv_norm'], 'm_mla_w_uk': out['m_mla_w_uk'], 'm_mla_w_uv': out['m_mla_w_uv'], 'm_mla_w_o': out['m_mla_w_o'], 'm_fox_w_in': out['m_fox_w_in'], 'm_fox_b_f': out['m_fox_b_f'], 'm_fox_w_o': out['m_fox_w_o'], 'v_ada_w': out['v_ada_w'], 'v_ada_b': out['v_ada_b'], 'v_ln1_g': out['v_ln1_g'], 'v_ln1_b': out['v_ln1_b'], 'v_ln2_g': out['v_ln2_g'], 'v_ln2_b': out['v_ln2_b'], 'v_ffn_w1': out['v_ffn_w1'], 'v_ffn_w3': out['v_ffn_w3'], 'v_ffn_w2': out['v_ffn_w2'], 'v_mla_w_down': out['v_mla_w_down'], 'v_mla_q_norm': out['v_mla_q_norm'], 'v_mla_w_uq': out['v_mla_w_uq'], 'v_mla_kv_norm': out['v_mla_kv_norm'], 'v_mla_w_uk': out['v_mla_w_uk'], 'v_mla_w_uv': out['v_mla_w_uv'], 'v_mla_w_o': out['v_mla_w_o'], 'v_fox_w_in': out['v_fox_w_in'], 'v_fox_b_f': out['v_fox_b_f'], 'v_fox_w_o': out['v_fox_w_o']}


def _loss(weights, diff, rest, loss_target):
    with _jax.named_scope("forward"):
        args = {**rest, TWIN_DIFF_INPUT: diff, **{k: w.astype(_WEIGHT_DTYPES[k]) for k, w in weights.items()}}
        y = _forward(args)
    with _jax.named_scope("loss_head"):
        err = _jnp.square(y.astype(_jnp.float32) - loss_target)
        return 0.5 * _jnp.sum(_jnp.mean(err, axis=-1)) if err.ndim else 0.5 * err


def _adamw(w, g, m, v):
    m = ADAM_B1 * m + (1.0 - ADAM_B1) * g
    v = ADAM_B2 * v + (1.0 - ADAM_B2) * _jnp.square(g)
    m_hat = m / (1.0 - ADAM_B1 ** ADAM_STEP)
    v_hat = v / (1.0 - ADAM_B2 ** ADAM_STEP)
    delta = -ADAM_LR * (m_hat / (_jnp.sqrt(v_hat) + ADAM_EPS) + ADAM_WD * w)
    return delta, m, v


def reference(x, c, positions, ada_w, ada_b, ln1_g, ln1_b, ln2_g, ln2_b, ffn_w1, ffn_w3, ffn_w2, mla_w_down, mla_q_norm, mla_w_uq, mla_kv_norm, mla_w_uk, mla_w_uv, mla_w_o, fox_w_in, fox_b_f, fox_w_o, loss_target, m_ada_w, m_ada_b, m_ln1_g, m_ln1_b, m_ln2_g, m_ln2_b, m_ffn_w1, m_ffn_w3, m_ffn_w2, m_mla_w_down, m_mla_q_norm, m_mla_w_uq, m_mla_kv_norm, m_mla_w_uk, m_mla_w_uv, m_mla_w_o, m_fox_w_in, m_fox_b_f, m_fox_w_o, v_ada_w, v_ada_b, v_ln1_g, v_ln1_b, v_ln2_g, v_ln2_b, v_ffn_w1, v_ffn_w3, v_ffn_w2, v_mla_w_down, v_mla_q_norm, v_mla_w_uq, v_mla_kv_norm, v_mla_w_uk, v_mla_w_uv, v_mla_w_o, v_fox_w_in, v_fox_b_f, v_fox_w_o):
    given = dict(x=x, c=c, positions=positions, ada_w=ada_w, ada_b=ada_b, ln1_g=ln1_g, ln1_b=ln1_b, ln2_g=ln2_g, ln2_b=ln2_b, ffn_w1=ffn_w1, ffn_w3=ffn_w3, ffn_w2=ffn_w2, mla_w_down=mla_w_down, mla_q_norm=mla_q_norm, mla_w_uq=mla_w_uq, mla_kv_norm=mla_kv_norm, mla_w_uk=mla_w_uk, mla_w_uv=mla_w_uv, mla_w_o=mla_w_o, fox_w_in=fox_w_in, fox_b_f=fox_b_f, fox_w_o=fox_w_o, loss_target=loss_target, m_ada_w=m_ada_w, m_ada_b=m_ada_b, m_ln1_g=m_ln1_g, m_ln1_b=m_ln1_b, m_ln2_g=m_ln2_g, m_ln2_b=m_ln2_b, m_ffn_w1=m_ffn_w1, m_ffn_w3=m_ffn_w3, m_ffn_w2=m_ffn_w2, m_mla_w_down=m_mla_w_down, m_mla_q_norm=m_mla_q_norm, m_mla_w_uq=m_mla_w_uq, m_mla_kv_norm=m_mla_kv_norm, m_mla_w_uk=m_mla_w_uk, m_mla_w_uv=m_mla_w_uv, m_mla_w_o=m_mla_w_o, m_fox_w_in=m_fox_w_in, m_fox_b_f=m_fox_b_f, m_fox_w_o=m_fox_w_o, v_ada_w=v_ada_w, v_ada_b=v_ada_b, v_ln1_g=v_ln1_g, v_ln1_b=v_ln1_b, v_ln2_g=v_ln2_g, v_ln2_b=v_ln2_b, v_ffn_w1=v_ffn_w1, v_ffn_w3=v_ffn_w3, v_ffn_w2=v_ffn_w2, v_mla_w_down=v_mla_w_down, v_mla_q_norm=v_mla_q_norm, v_mla_w_uq=v_mla_w_uq, v_mla_kv_norm=v_mla_kv_norm, v_mla_w_uk=v_mla_w_uk, v_mla_w_uv=v_mla_w_uv, v_mla_w_o=v_mla_w_o, v_fox_w_in=v_fox_w_in, v_fox_b_f=v_fox_b_f, v_fox_w_o=v_fox_w_o)
    weights = {n: given[n] for n in TWIN_WEIGHTS}
    shared = {n: given[n] for n in SHARED_INPUTS}
    per_example = {n: given[n] for n in ['x', 'c', 'positions']}
    grad_fn = _jax.value_and_grad(_loss, argnums=(0, 1))

    def one_microbatch(ex, loss_target):
        ex = dict(ex)
        diff = ex.pop(TWIN_DIFF_INPUT)
        return grad_fn(weights, diff, {**shared, **ex}, loss_target)

    if N_MICROBATCH == 1:
        loss, (grad_w, grad_x) = one_microbatch(per_example, given["loss_target"])
    else:
        def body(carry, xs):
            loss_sum, grad_sum = carry
            l_k, (gw_k, gx_k) = one_microbatch(xs[0], xs[1])
            with _jax.named_scope("update"):
                return (loss_sum + l_k, _jax.tree.map(_jnp.add, grad_sum, gw_k)), gx_k

        init = (_jnp.zeros((), _jnp.float32), _jax.tree.map(_jnp.zeros_like, weights))
        (loss, grad_w), grad_x = _jax.lax.scan(body, init, (per_example, given["loss_target"]))
    with _jax.named_scope("update"):
        delta_w, new_m, new_v = {}, {}, {}
        for n in TWIN_WEIGHTS:
            delta_w[n], new_m[n], new_v[n] = _adamw(weights[n], grad_w[n], given["m_" + n], given["v_" + n])
    return (loss, grad_x, *[grad_w[n] for n in TWIN_WEIGHTS], *[delta_w[n] for n in TWIN_WEIGHTS],
            *[new_m[n] for n in TWIN_WEIGHTS], *[new_v[n] for n in TWIN_WEIGHTS])
```

```python
import math

import numpy as np
import jax
import jax.numpy as jnp
from jax import lax
from jax.experimental import pallas as pl
from jax.experimental.pallas import tpu as pltpu

F32 = jnp.float32
BF = jnp.bfloat16
N_DEV = 8

CHUNK_SHIFT = 6
NOPE = 128
ROPE = 64
HEAD = 128
ROPE_THETA = 10000.0
LN_EPS = 1e-5
RMS_EPS = 1e-6
ADAM_LR = 0.001
ADAM_B1 = 0.9
ADAM_B2 = 0.999
ADAM_EPS = 1e-08
ADAM_WD = 0.01
ADAM_STEP = 10

VMEM_LIMIT_BYTES = 48 * 1024 * 1024
LANE = 128
NEG = -1e30


def _cp(*sem):
    return pltpu.CompilerParams(dimension_semantics=sem, vmem_limit_bytes=VMEM_LIMIT_BYTES)


def _pick(dim, pref):
    if dim <= pref:
        return dim
    for t in range(pref - pref % LANE, LANE - 1, -LANE):
        if dim % t == 0:
            return t
    return dim


def _round_up(v, m):
    return (v + m - 1) // m * m


def _sum0(v):
    return jnp.sum(v, axis=0, keepdims=True)


def _mean1(v):
    return jnp.mean(v, axis=-1, keepdims=True)


_NN = (((1,), (0,)), ((), ()))
_NT = (((1,), (1,)), ((), ()))
_TN = (((0,), (0,)), ((), ()))


def _mm_call(name, grid, ins, in_specs, dns, out_shape, out_spec, add=None, add_spec=None):
    npairs = len(dns)
    nk = grid[-1]
    has_add = add is not None

    def body(*refs):
        add_ref = refs[2 * npairs] if has_add else None
        o_ref = refs[2 * npairs + (1 if has_add else 0)]
        r = None
        for p in range(npairs):
            t = lax.dot_general(refs[2 * p][...].astype(BF), refs[2 * p + 1][...].astype(BF), dns[p], preferred_element_type=F32)
            r = t if r is None else r + t

        def fin(val):
            if has_add:
                val = val + add_ref[...]
            o_ref[...] = val.astype(o_ref.dtype)

        if nk == 1:
            fin(r)
        else:
            acc = refs[-1]
            k = pl.program_id(len(grid) - 1)

            @pl.when(k == 0)
            def _():
                acc[...] = r

            @pl.when(k > 0)
            def _():
                acc[...] += r

            @pl.when(k == nk - 1)
            def _():
                fin(acc[...])

    acc_shape = tuple(b for b in out_spec.block_shape if b is not None)
    args, specs = list(ins), list(in_specs)
    if has_add:
        args.append(add)
        specs.append(add_spec)
    return pl.pallas_call(
        body, name=name, grid=grid, in_specs=specs, out_specs=out_spec, out_shape=out_shape,
        scratch_shapes=[pltpu.VMEM(acc_shape, F32)] if nk > 1 else [],
        compiler_params=_cp(*(["parallel"] * (len(grid) - 1) + ["arbitrary"])),
    )(*args)


def mm(name, a, b, ta=False, tb=False, out_dtype=F32, add=None, bm=1024, bn=1024, bk=2048):
    M, K = (a.shape[1], a.shape[0]) if ta else a.shape
    N = b.shape[0] if tb else b.shape[1]
    bm, bn, bk = _pick(M, bm), _pick(N, bn), _pick(K, bk)
    a_spec = pl.BlockSpec((bk, bm), lambda i, j, k: (k, i)) if ta else pl.BlockSpec((bm, bk), lambda i, j, k: (i, k))
    b_spec = pl.BlockSpec((bn, bk), lambda i, j, k: (j, k)) if tb else pl.BlockSpec((bk, bn), lambda i, j, k: (k, j))
    dn = (((0 if ta else 1,), (1 if tb else 0,)), ((), ()))
    tile = pl.BlockSpec((bm, bn), lambda i, j, k: (i, j))
    return _mm_call(name, (M // bm, N // bn, K // bk), [a, b], [a_spec, b_spec], [dn],
                    jax.ShapeDtypeStruct((M, N), out_dtype), tile, add, tile)


def rowwise(name, fn, rows, vecs, out_rows, out_accs=(), tm=256):
    T = rows[0].shape[0]
    tm = min(tm, T)
    nr, nv, no = len(rows), len(vecs), len(out_rows)

    def body(*refs):
        outs = fn(*[r[...] for r in refs[:nr + nv]])
        if not isinstance(outs, (tuple, list)):
            outs = (outs,)
        o_refs = refs[nr + nv:nr + nv + no]
        a_refs = refs[nr + nv + no:]
        for r, o in zip(o_refs, outs[:no]):
            r[...] = o.astype(r.dtype)
        if a_refs:
            @pl.when(pl.program_id(0) == 0)
            def _():
                for r in a_refs:
                    r[...] = jnp.zeros_like(r)

            for r, o in zip(a_refs, outs[no:]):
                r[...] += o

    in_specs = [pl.BlockSpec((tm, r.shape[1]), lambda i: (i, 0)) for r in rows]
    in_specs += [pl.BlockSpec(v.shape, lambda i: (0, 0)) for v in vecs]
    out_specs = [pl.BlockSpec((tm, w), lambda i: (i, 0)) for w, _ in out_rows]
    out_specs += [pl.BlockSpec((1, w), lambda i: (0, 0)) for w in out_accs]
    out_shape = [jax.ShapeDtypeStruct((T, w), dt) for w, dt in out_rows]
    out_shape += [jax.ShapeDtypeStruct((1, w), F32) for w in out_accs]
    return pl.pallas_call(
        body, name=name, grid=(T // tm,), in_specs=in_specs, out_specs=out_specs, out_shape=out_shape,
        compiler_params=_cp("arbitrary"),
    )(*rows, *vecs)


def _rot(v, p):
    hi = v.astype(BF)
    lo = (v - hi.astype(F32)).astype(BF)
    return jnp.dot(hi, p, preferred_element_type=F32) + jnp.dot(lo, p, preferred_element_type=F32)


def _pair_rotation(width):
    p = np.zeros((width, width), np.float32)
    idx = np.arange(0, width, 2)
    p[idx + 1, idx] = -1.0
    p[idx, idx + 1] = 1.0
    return jnp.asarray(p, BF)


def _ln_stats(alpha, x, y, g):
    z = alpha * x + (1.0 + g) * y
    zc = z - _mean1(z)
    rstd = lax.rsqrt(_mean1(zc * zc) + LN_EPS)
    return zc * rstd, rstd


def _visible(mode, qs, ks, tq, tk):
    rows = qs + lax.broadcasted_iota(jnp.int32, (tq, tk), 0)
    cols = ks + lax.broadcasted_iota(jnp.int32, (tq, tk), 1)
    if mode == "mla":
        return lax.shift_right_logical(rows, CHUNK_SHIFT) >= lax.shift_right_logical(cols, CHUNK_SHIFT)
    return rows >= cols


def _scores(mode, q, k, e_q, e_k, qs, ks, tq, tk, masked):
    s = lax.dot_general(q, k, _NT, preferred_element_type=F32)
    if mode == "mla":
        s = s + lax.dot_general(e_q, e_k, _NT, preferred_element_type=F32)
    else:
        s = s + e_q - e_k
    if masked:
        s = jnp.where(_visible(mode, qs, ks, tq, tk), s, NEG)
    return s


def _attn_specs(mode, H, T, tk, q_off, k_off, v_off):
    def col(off):
        return pl.BlockSpec((T, HEAD), lambda h: (0, off + h))

    if mode == "mla":
        e_specs = [pl.BlockSpec((None, T, ROPE), lambda h: (h, 0, 0)), pl.BlockSpec((T, ROPE), lambda h: (0, 0))]
    else:
        e_specs = [pl.BlockSpec((None, T, 1), lambda h: (h, 0, 0)), pl.BlockSpec((None, T // tk, 1, tk), lambda h: (h, 0, 0, 0))]
    return [col(q_off), col(k_off), col(v_off)], e_specs


def attn_fwd(name, mode, scale, H, T, q_src, q_off, k_src, k_off, v_src, v_off, e_q, e_k):
    tq, tk = min(128, T), min(256, T)
    nq, ratio = T // tq, tk // tq
    mla = mode == "mla"

    def body(q_ref, k_ref, v_ref, eq_ref, ek_ref, o_ref, lse_ref):
        def q_loop(qi, _):
            qs = pl.multiple_of(qi * tq, tq)
            q = (q_ref[pl.ds(qs, tq), :].astype(F32) * scale).astype(BF)
            eq = eq_ref[pl.ds(qs, tq), :]

            def step(j, carry, masked):
                m, l, acc = carry
                ks = pl.multiple_of(j * tk, tk)
                k = k_ref[pl.ds(ks, tk), :].astype(BF)
                ek = ek_ref[pl.ds(ks, tk), :] if mla else ek_ref[j]
                s = _scores(mode, q, k, eq, ek, qs, ks, tq, tk, masked)
                m_new = jnp.maximum(m, jnp.max(s, axis=-1, keepdims=True))
                p = jnp.exp(s - m_new)
                corr = jnp.exp(m - m_new)
                l = corr * l + jnp.sum(p, axis=-1, keepdims=True)
                v = v_ref[pl.ds(ks, tk), :].astype(BF)
                acc = corr * acc + jnp.dot(p.astype(BF), v, preferred_element_type=F32)
                return m_new, l, acc

            init = (jnp.full((tq, 1), NEG, F32), jnp.zeros((tq, 1), F32), jnp.zeros((tq, HEAD), F32))
            jd = qi // ratio
            carry = lax.fori_loop(0, jd, lambda j, cr: step(j, cr, False), init)
            m, l, acc = step(jd, carry, True)
            o_ref[pl.ds(qs, tq), :] = (acc / l).astype(o_ref.dtype)
            lse_ref[pl.ds(qs, tq), :] = m + jnp.log(l)
            return 0

        lax.fori_loop(0, nq, q_loop, 0)

    cols, e_specs = _attn_specs(mode, H, T, tk, q_off, k_off, v_off)
    return pl.pallas_call(
        body, name=name, grid=(H,), in_specs=cols + e_specs,
        out_specs=[pl.BlockSpec((T, HEAD), lambda h: (0, h)), pl.BlockSpec((None, T, 1), lambda h: (h, 0, 0))],
        out_shape=[jax.ShapeDtypeStruct((T, H * HEAD), BF), jax.ShapeDtypeStruct((H, T, 1), F32)],
        compiler_params=_cp("parallel"),
    )(q_src, k_src, v_src, e_q, e_k)


def attn_bwd(name, mode, scale, H, T, q_src, q_off, k_src, k_off, v_src, v_off, e_q, e_k, o, do, lse):
    tq, tk = min(128, T), min(256, T)
    nq, ratio = T // tq, tk // tq
    mla = mode == "mla"

    def body(q_ref, k_ref, v_ref, eq_ref, ek_ref, o_ref, do_ref, lse_ref,
             dq_ref, dk_ref, dv_ref, deq_ref, dek_ref, dk_acc, dv_acc):
        dk_acc[...] = jnp.zeros_like(dk_acc)
        dv_acc[...] = jnp.zeros_like(dv_acc)
        if mla:
            @pl.when(pl.program_id(0) == 0)
            def _():
                dek_ref[...] = jnp.zeros_like(dek_ref)
        else:
            dek_ref[...] = jnp.zeros_like(dek_ref)

        def q_loop(qi, _):
            qs = pl.multiple_of(qi * tq, tq)
            q = (q_ref[pl.ds(qs, tq), :].astype(F32) * scale).astype(BF)
            eq = eq_ref[pl.ds(qs, tq), :]
            dout = do_ref[pl.ds(qs, tq), :]
            lse_q = lse_ref[pl.ds(qs, tq), :]
            delta = jnp.sum(dout.astype(F32) * o_ref[pl.ds(qs, tq), :].astype(F32), axis=-1, keepdims=True)

            def step(j, carry, masked):
                dq, deq = carry
                ks = pl.multiple_of(j * tk, tk)
                k = k_ref[pl.ds(ks, tk), :].astype(BF)
                v = v_ref[pl.ds(ks, tk), :].astype(BF)
                ek = ek_ref[pl.ds(ks, tk), :] if mla else ek_ref[j]
                s = _scores(mode, q, k, eq, ek, qs, ks, tq, tk, masked)
                p = jnp.exp(s - lse_q)
                dv_acc[pl.ds(ks, tk), :] += lax.dot_general(p.astype(BF), dout, _TN, preferred_element_type=F32)
                dp = lax.dot_general(dout, v, _NT, preferred_element_type=F32)
                ds = p * (dp - delta)
                dsb = ds.astype(BF)
                dk_acc[pl.ds(ks, tk), :] += lax.dot_general(dsb, q, _TN, preferred_element_type=F32)
                dq = dq + jnp.dot(dsb, k, preferred_element_type=F32)
                if mla:
                    deq = deq + jnp.dot(dsb, ek, preferred_element_type=F32)
                    dek_ref[pl.ds(ks, tk), :] += lax.dot_general(dsb, eq, _TN, preferred_element_type=F32)
                else:
                    deq = deq + jnp.sum(ds, axis=-1, keepdims=True)
                    dek_ref[j] -= jnp.sum(ds, axis=0, keepdims=True)
                return dq, deq

            init = (jnp.zeros((tq, HEAD), F32), jnp.zeros((tq, ROPE if mla else 1), F32))
            jd = qi // ratio
            carry = lax.fori_loop(0, jd, lambda j, cr: step(j, cr, False), init)
            dq, deq = step(jd, carry, True)
            dq_ref[pl.ds(qs, tq), :] = (dq * scale).astype(dq_ref.dtype)
            deq_ref[pl.ds(qs, tq), :] = deq
            return 0

        lax.fori_loop(0, nq, q_loop, 0)
        dk_ref[...] = dk_acc[...].astype(dk_ref.dtype)
        dv_ref[...] = dv_acc[...].astype(dv_ref.dtype)

    cols, e_specs = _attn_specs(mode, H, T, tk, q_off, k_off, v_off)
    head = pl.BlockSpec((T, HEAD), lambda h: (0, h))
    if mla:
        de_shapes = [jax.ShapeDtypeStruct((H, T, ROPE), F32), jax.ShapeDtypeStruct((T, ROPE), F32)]
    else:
        de_shapes = [jax.ShapeDtypeStruct((H, T, 1), F32), jax.ShapeDtypeStruct((H, T // tk, 1, tk), F32)]
    return pl.pallas_call(
        body, name=name, grid=(H,),
        in_specs=cols + e_specs + [head, head, pl.BlockSpec((None, T, 1), lambda h: (h, 0, 0))],
        out_specs=[head, head, head] + e_specs,
        out_shape=[jax.ShapeDtypeStruct((T, H * HEAD), BF)] * 3 + de_shapes,
        scratch_shapes=[pltpu.VMEM((T, HEAD), F32), pltpu.VMEM((T, HEAD), F32)],
        compiler_params=_cp("arbitrary"),
    )(q_src, k_src, v_src, e_q, e_k, o, do, lse)


def fox_cum(name, f, b):
    T = f.shape[0]

    def body(f_ref, b_ref, cum_ref):
        z = f_ref[...] + b_ref[...]
        v = jnp.minimum(z, 0.0) - jnp.log(1.0 + jnp.exp(-jnp.abs(z)))
        row = lax.broadcasted_iota(jnp.int32, v.shape, 0)
        s = 1
        while s < T:
            v = v + jnp.where(row >= s, pltpu.roll(v, s, 0), 0.0)
            s *= 2
        cum_ref[...] = v

    return pl.pallas_call(body, name=name, out_shape=jax.ShapeDtypeStruct(f.shape, F32),
                          compiler_params=pltpu.CompilerParams(vmem_limit_bytes=VMEM_LIMIT_BYTES))(f, b)


def fox_cum_bwd(name, f, b, dcum):
    T = f.shape[0]

    def body(f_ref, b_ref, dc_ref, df_ref, db_ref):
        v = dc_ref[...]
        row = lax.broadcasted_iota(jnp.int32, v.shape, 0)
        s = 1
        while s < T:
            v = v + jnp.where(row < T - s, pltpu.roll(v, T - s, 0), 0.0)
            s *= 2
        z = f_ref[...] + b_ref[...]
        df = v / (1.0 + jnp.exp(z))
        df_ref[...] = df.astype(df_ref.dtype)
        db_ref[...] = _sum0(df)

    return pl.pallas_call(body, name=name,
                          out_shape=[jax.ShapeDtypeStruct(f.shape, BF), jax.ShapeDtypeStruct((1, f.shape[1]), F32)],
                          compiler_params=pltpu.CompilerParams(vmem_limit_bytes=VMEM_LIMIT_BYTES))(f, b, dcum)


def ffn_up(name, h, w1g, w3g, l):
    T, D = h.shape
    FP = w1g.shape[3]
    bm = _pick(T, 1024)

    def body(h_ref, w1_ref, w3_ref, a_ref, b_ref, g_ref):
        hv = h_ref[...]
        a = jnp.dot(hv, w1_ref[...], preferred_element_type=F32)
        b = jnp.dot(hv, w3_ref[...], preferred_element_type=F32)
        a_ref[...] = a.astype(BF)
        b_ref[...] = b.astype(BF)
        g_ref[...] = (a * (1.0 / (1.0 + jnp.exp(-a))) * b).astype(BF)

    w_spec = pl.BlockSpec((None, None, D, FP), lambda i, d: (d, l, 0, 0))
    tile = pl.BlockSpec((None, bm, FP), lambda i, d: (d, i, 0))
    return pl.pallas_call(
        body, name=name, grid=(T // bm, N_DEV), in_specs=[pl.BlockSpec((bm, D), lambda i, d: (i, 0)), w_spec, w_spec],
        out_specs=[tile, tile, tile], out_shape=[jax.ShapeDtypeStruct((N_DEV, T, FP), BF)] * 3,
        compiler_params=_cp("parallel", "parallel"),
    )(h, w1g, w3g)


def ffn_down(name, g, w2g, l):
    _, T, FP = g.shape
    D = w2g.shape[3]
    bm, bn = _pick(T, 1024), _pick(D, 1024)
    return _mm_call(
        name, (T // bm, D // bn, N_DEV), [g, w2g],
        [pl.BlockSpec((None, bm, FP), lambda i, j, d: (d, i, 0)), pl.BlockSpec((None, None, FP, bn), lambda i, j, d: (d, l, 0, j))],
        [_NN], jax.ShapeDtypeStruct((T, D), F32), pl.BlockSpec((bm, bn), lambda i, j, d: (i, j)))


def ffn_down_bwd(name, dy, w2g, a, b, l):
    T, D = dy.shape
    FP = w2g.shape[2]
    bm = _pick(T, 1024)

    def body(dy_ref, w2_ref, a_ref, b_ref, da_ref, db_ref):
        dg = lax.dot_general(dy_ref[...], w2_ref[...], _NT, preferred_element_type=F32)
        av = a_ref[...].astype(F32)
        bv = b_ref[...].astype(F32)
        sig = 1.0 / (1.0 + jnp.exp(-av))
        da_ref[...] = (dg * bv * sig * (1.0 + av * (1.0 - sig))).astype(BF)
        db_ref[...] = (dg * av * sig).astype(BF)

    tile = pl.BlockSpec((None, bm, FP), lambda i, d: (d, i, 0))
    return pl.pallas_call(
        body, name=name, grid=(T // bm, N_DEV),
        in_specs=[pl.BlockSpec((bm, D), lambda i, d: (i, 0)), pl.BlockSpec((None, None, FP, D), lambda i, d: (d, l, 0, 0)), tile, tile],
        out_specs=[tile, tile], out_shape=[jax.ShapeDtypeStruct((N_DEV, T, FP), BF)] * 2,
        compiler_params=_cp("parallel", "parallel"),
    )(dy, w2g, a, b)


def ffn_dh(name, da, db, w1g, w3g, l):
    _, T, FP = da.shape
    D = w1g.shape[2]
    bm, bn = _pick(T, 1024), _pick(D, 1024)
    act = pl.BlockSpec((None, bm, FP), lambda i, j, d: (d, i, 0))
    wgt = pl.BlockSpec((None, None, bn, FP), lambda i, j, d: (d, l, j, 0))
    return _mm_call(name, (T // bm, D // bn, N_DEV), [da, w1g, db, w3g], [act, wgt, act, wgt], [_NT, _NT],
                    jax.ShapeDtypeStruct((T, D), F32), pl.BlockSpec((bm, bn), lambda i, j, d: (i, j)))


def ffn_in_grad(name, h, dact):
    T, D = h.shape
    FP = dact.shape[2]
    bm = _pick(D, 1024)
    return _mm_call(
        name, (D // bm, N_DEV, 1), [h, dact],
        [pl.BlockSpec((T, bm), lambda i, d, k: (0, i)), pl.BlockSpec((None, T, FP), lambda i, d, k: (d, 0, 0))],
        [_TN], jax.ShapeDtypeStruct((N_DEV, D, FP), BF), pl.BlockSpec((None, bm, FP), lambda i, d, k: (d, i, 0)))


def ffn_out_grad(name, g, dy):
    _, T, FP = g.shape
    D = dy.shape[1]
    bn = _pick(D, 1024)
    return _mm_call(
        name, (N_DEV, D // bn, 1), [g, dy],
        [pl.BlockSpec((None, T, FP), lambda d, j, k: (d, 0, 0)), pl.BlockSpec((T, bn), lambda d, j, k: (0, j))],
        [_TN], jax.ShapeDtypeStruct((N_DEV, FP, D), BF), pl.BlockSpec((None, FP, bn), lambda d, j, k: (d, 0, j)))


def _adamw_math(w, g, m, v):
    m = ADAM_B1 * m + (1.0 - ADAM_B1) * g
    v = ADAM_B2 * v + (1.0 - ADAM_B2) * (g * g)
    m_hat = m / (1.0 - ADAM_B1 ** ADAM_STEP)
    v_hat = v / (1.0 - ADAM_B2 ** ADAM_STEP)
    delta = -ADAM_LR * (m_hat / (jnp.sqrt(v_hat) + ADAM_EPS) + ADAM_WD * w)
    return delta, m, v


def adamw(name, w, g, m, v):
    shape = w.shape
    n = shape[-1]
    rows = math.prod(shape[:-1])
    tr = rows if rows <= 512 else 256
    assert rows % tr == 0

    def body(w_ref, g_ref, m_ref, v_ref, d_ref, mo_ref, vo_ref):
        d, mn, vn = _adamw_math(w_ref[...], g_ref[...], m_ref[...], v_ref[...])
        d_ref[...] = d
        mo_ref[...] = mn
        vo_ref[...] = vn

    blk = pl.BlockSpec((tr, n), lambda i: (i, 0))
    outs = pl.pallas_call(
        body, name=name, grid=(rows // tr,), in_specs=[blk] * 4, out_specs=[blk] * 3,
        out_shape=[jax.ShapeDtypeStruct((rows, n), F32)] * 3, compiler_params=_cp("parallel"),
    )(*[t.reshape(rows, n) for t in (w, g, m, v)])
    return [t.reshape(shape) for t in outs]


def _silu(v):
    return v * (1.0 / (1.0 + jnp.exp(-v)))


def ada_fwd(name, c_all, ada_w, ada_b_cols):
    L, D, n = ada_w.shape
    bn = _pick(n, 512)

    def body(c_ref, w_ref, b_ref, o_ref):
        act = _silu(c_ref[...]).astype(BF)
        o_ref[...] = jnp.dot(act, w_ref[...].astype(BF), preferred_element_type=F32) + b_ref[...]

    return pl.pallas_call(
        body, name=name, grid=(L, n // bn),
        in_specs=[pl.BlockSpec((N_DEV, D), lambda l, j: (0, 0)), pl.BlockSpec((None, D, bn), lambda l, j: (l, 0, j)),
                  pl.BlockSpec((None, 1, bn), lambda l, j: (l, 0, j))],
        out_specs=pl.BlockSpec((None, N_DEV, bn), lambda l, j: (l, 0, j)),
        out_shape=jax.ShapeDtypeStruct((L, N_DEV, n), F32), compiler_params=_cp("parallel", "parallel"),
    )(c_all, ada_w, ada_b_cols.reshape(L, 1, n))


def ada_bwd_adamw(name, c_all, dmod_cols, w, m, v):
    L, D, n = w.shape
    tr = _pick(D, 256)

    def body(c_ref, dm_ref, w_ref, m_ref, v_ref, g_ref, d_ref, mo_ref, vo_ref):
        act = _silu(c_ref[...]).astype(BF)
        g = lax.dot_general(act, dm_ref[...].astype(BF), _TN, preferred_element_type=F32)
        d, mn, vn = _adamw_math(w_ref[...], g, m_ref[...], v_ref[...])
        g_ref[...] = g
        d_ref[...] = d
        mo_ref[...] = mn
        vo_ref[...] = vn

    blk = pl.BlockSpec((None, tr, n), lambda l, i: (l, i, 0))
    return pl.pallas_call(
        body, name=name, grid=(L, D // tr),
        in_specs=[pl.BlockSpec((N_DEV, tr), lambda l, i: (0, i)), pl.BlockSpec((None, N_DEV, n), lambda l, i: (l, 0, 0)),
                  blk, blk, blk],
        out_specs=[blk] * 4, out_shape=[jax.ShapeDtypeStruct((L, D, n), F32)] * 4,
        compiler_params=_cp("parallel", "parallel"),
    )(c_all, dmod_cols, w, m, v)


_MESH = pl.DeviceIdType.MESH
_ANY = pl.BlockSpec(memory_space=pl.ANY)
_CHIP_FLIPS = ((1, 0), (0, 1), (1, 1))


def _place():
    return lax.axis_index("x"), lax.axis_index("y"), lax.axis_index("c")


def _flip(v, f):
    return 1 - v if f else v


def all_gather(name, shards, in_vmem=False):
    nt = len(shards)

    def body(*refs):
        x_refs, out_refs = refs[:nt], refs[nt:2 * nt]
        send_sems, recv_sems, local_sems = refs[2 * nt:]
        x, y, c = _place()
        me, sibling = (x, y, c), (x, y, 1 - c)
        chips = [(_flip(x, fx), _flip(y, fy)) for fx, fy in _CHIP_FLIPS]

        def rows(t, px, py, pc):
            m_per = shards[t].shape[0]
            return out_refs[t].at[pl.ds((4 * px + 2 * py + pc) * m_per, m_per), :]

        def copy(t, k, block, to, src=None):
            return pltpu.make_async_remote_copy(
                src_ref=rows(t, *block) if src is None else src, dst_ref=rows(t, *block),
                send_sem=send_sems.at[7 * t + k], recv_sem=recv_sems.at[7 * t + k], device_id=to, device_id_type=_MESH)

        mine = [pltpu.make_async_copy(x_refs[t], rows(t, *me), local_sems.at[t]) for t in range(nt)]
        for cp in mine:
            cp.start()
        first = []
        for j, chip in enumerate(chips):
            first += [copy(t, 1 + j, me, (*chip, c), src=x_refs[t]) for t in range(nt)]
        first += [copy(t, 0, me, sibling, src=x_refs[t]) for t in range(nt)]
        for cp in first:
            cp.start()
        passed = []
        for j, chip in enumerate(chips):
            for t in range(nt):
                copy(t, 1 + j, (*chip, c), me).wait_recv()
                passed.append(copy(t, 4 + j, (*chip, c), sibling))
                passed[-1].start()
        for t in range(nt):
            copy(t, 0, sibling, me).wait_recv()
            for j, chip in enumerate(chips):
                copy(t, 4 + j, (*chip, 1 - c), me).wait_recv()
        for cp in first + passed:
            cp.wait_send()
        for cp in mine:
            cp.wait()

    space = pl.BlockSpec(memory_space=pltpu.VMEM) if in_vmem else _ANY
    return pl.pallas_call(
        body, name=name, out_shape=[jax.ShapeDtypeStruct((N_DEV * s.shape[0], s.shape[1]), s.dtype) for s in shards],
        in_specs=[space] * nt, out_specs=[space] * nt,
        scratch_shapes=[pltpu.SemaphoreType.DMA((7 * nt,)), pltpu.SemaphoreType.DMA((7 * nt,)), pltpu.SemaphoreType.DMA((nt,))],
    )(*shards)


def pair_exchange(name, gs):
    nt = len(gs)

    def body(*refs):
        g_refs, out_refs = refs[:nt], refs[nt:2 * nt]
        send_sems, recv_sems = refs[2 * nt:]
        x, y, c = _place()
        copies = [pltpu.make_async_remote_copy(
            src_ref=g_refs[t].at[2 * k + 1 - c], dst_ref=out_refs[t].at[k], send_sem=send_sems.at[4 * t + k],
            recv_sem=recv_sems.at[4 * t + k], device_id=(x, y, 1 - c), device_id_type=_MESH) for t in range(nt) for k in range(4)]
        for cp in copies:
            cp.start()
        for cp in copies:
            cp.wait()

    return pl.pallas_call(
        body, name=name, out_shape=[jax.ShapeDtypeStruct((4,) + g.shape[1:], g.dtype) for g in gs],
        in_specs=[_ANY] * nt, out_specs=[_ANY] * nt,
        scratch_shapes=[pltpu.SemaphoreType.DMA((4 * nt,)), pltpu.SemaphoreType.DMA((4 * nt,))],
    )(*gs)


def chip_exchange(name, parts):
    nt = len(parts)

    def body(*refs):
        p_refs, out_refs = refs[:nt], refs[nt:2 * nt]
        send_sems, recv_sems = refs[2 * nt:]
        x, y, c = _place()
        copies = []
        for j, (fx, fy) in enumerate(_CHIP_FLIPS):
            px, py = _flip(x, fx), _flip(y, fy)
            copies += [pltpu.make_async_remote_copy(
                src_ref=p_refs[t].at[2 * px + py], dst_ref=out_refs[t].at[j], send_sem=send_sems.at[3 * t + j],
                recv_sem=recv_sems.at[3 * t + j], device_id=(px, py, c), device_id_type=_MESH) for t in range(nt)]
        for cp in copies:
            cp.start()
        for cp in copies:
            cp.wait()

    return pl.pallas_call(
        body, name=name, out_shape=[jax.ShapeDtypeStruct((3,) + p.shape[1:], p.dtype) for p in parts],
        in_specs=[_ANY] * nt, out_specs=[_ANY] * nt,
        scratch_shapes=[pltpu.SemaphoreType.DMA((3 * nt,)), pltpu.SemaphoreType.DMA((3 * nt,))],
    )(*parts)


def _row_tile(rows):
    return rows if rows <= 256 else 256


def pair_add(name, g, recv, core):
    _, R, C = g.shape
    tr = _row_tile(R)

    def body(core_ref, g_ref, r_ref, o_ref):
        o_ref[...] = (g_ref[...].astype(F32) + r_ref[...].astype(F32)).astype(o_ref.dtype)

    return pl.pallas_call(
        body, name=name, out_shape=jax.ShapeDtypeStruct((4, R, C), BF),
        grid_spec=pltpu.PrefetchScalarGridSpec(
            num_scalar_prefetch=1, grid=(4, R // tr),
            in_specs=[pl.BlockSpec((None, tr, C), lambda k, i, core_ref: (2 * k + core_ref[0], i, 0)),
                      pl.BlockSpec((None, tr, C), lambda k, i, core_ref: (k, i, 0))],
            out_specs=pl.BlockSpec((None, tr, C), lambda k, i, core_ref: (k, i, 0))),
        compiler_params=_cp("parallel", "parallel"),
    )(core, g, recv)


def chip_add(name, part, recv, chip):
    _, R, C = part.shape
    tr = _row_tile(R)

    def body(chip_ref, p_ref, r_ref, o_ref):
        o_ref[...] = ((p_ref[...].astype(F32) + r_ref[0].astype(F32)) + r_ref[1].astype(F32)) + r_ref[2].astype(F32)

    return pl.pallas_call(
        body, name=name, out_shape=jax.ShapeDtypeStruct((R, C), F32),
        grid_spec=pltpu.PrefetchScalarGridSpec(
            num_scalar_prefetch=1, grid=(R // tr,),
            in_specs=[pl.BlockSpec((None, tr, C), lambda i, chip_ref: (chip_ref[0], i, 0)),
                      pl.BlockSpec((3, tr, C), lambda i, chip_ref: (0, i, 0))],
            out_specs=pl.BlockSpec((tr, C), lambda i, chip_ref: (i, 0))),
        compiler_params=_cp("parallel"),
    )(chip, part, recv)


def reduce_scatter(tag, gs, core, chip):
    from_pair = pair_exchange("pair_exchange_" + tag, gs)
    parts = [pair_add("pair_add_%s_%d" % (tag, t), g, r, core) for t, (g, r) in enumerate(zip(gs, from_pair))]
    from_chips = chip_exchange("chip_exchange_" + tag, parts)
    return [chip_add("chip_add_%s_%d" % (tag, t), p, r, chip) for t, (p, r) in enumerate(zip(parts, from_chips))]


def sum_devices(name, g):
    _, R, C = g.shape

    def body(g_ref, o_ref):
        acc = g_ref[0]
        for d in range(1, N_DEV):
            acc = acc + g_ref[d]
        o_ref[...] = acc

    return pl.pallas_call(body, name=name, out_shape=jax.ShapeDtypeStruct((R, C), F32),
                          compiler_params=pltpu.CompilerParams(vmem_limit_bytes=VMEM_LIMIT_BYTES))(g)


_WEIGHTS = ["ada_w", "ada_b", "ln1_g", "ln1_b", "ln2_g", "ln2_b", "ffn_w1", "ffn_w3", "ffn_w2", "mla_w_down", "mla_q_norm",
            "mla_w_uq", "mla_kv_norm", "mla_w_uk", "mla_w_uv", "mla_w_o", "fox_w_in", "fox_b_f", "fox_w_o"]
_SMALL = ["ada_b", "ln1_g", "ln1_b", "ln2_g", "ln2_b", "mla_q_norm", "mla_kv_norm", "fox_b_f"]


def _pad_cols(a, width):
    return jnp.pad(a, ((0, 0), (0, width - a.shape[1])))


def _small_pack(parts, cols):
    rows = [jnp.pad(p.reshape(-1), (0, _round_up(p.size, cols) - p.size)).reshape(-1, cols) for p in parts]
    out = jnp.concatenate(rows, axis=0)
    return jnp.pad(out, ((0, _round_up(out.shape[0], 8) - out.shape[0]), (0, 0)))


def _gather2d(name, shards):
    outs = all_gather(name, shards)
    return [o.reshape((N_DEV,) + s.shape) for o, s in zip(outs, shards)]


def kernel(x, c, positions, ada_w, ada_b, ln1_g, ln1_b, ln2_g, ln2_b, ffn_w1, ffn_w3, ffn_w2, mla_w_down, mla_q_norm, mla_w_uq, mla_kv_norm, mla_w_uk, mla_w_uv, mla_w_o, fox_w_in, fox_b_f, fox_w_o, loss_target, m_ada_w, m_ada_b, m_ln1_g, m_ln1_b, m_ln2_g, m_ln2_b, m_ffn_w1, m_ffn_w3, m_ffn_w2, m_mla_w_down, m_mla_q_norm, m_mla_w_uq, m_mla_kv_norm, m_mla_w_uk, m_mla_w_uv, m_mla_w_o, m_fox_w_in, m_fox_b_f, m_fox_w_o, v_ada_w, v_ada_b, v_ln1_g, v_ln1_b, v_ln2_g, v_ln2_b, v_ffn_w1, v_ffn_w3, v_ffn_w2, v_mla_w_down, v_mla_q_norm, v_mla_w_uq, v_mla_kv_norm, v_mla_w_uk, v_mla_w_uv, v_mla_w_o, v_fox_w_in, v_fox_b_f, v_fox_w_o):
    env = dict(locals())
    W = {n: env[n] for n in _WEIGHTS}
    M = {n: env["m_" + n] for n in _WEIGHTS}
    V = {n: env["v_" + n] for n in _WEIGHTS}

    T, D = x.shape[1], x.shape[2]
    L = ada_w.shape[0]
    QL, KVL = mla_q_norm.shape[1], mla_kv_norm.shape[1]
    HA = mla_w_uq.shape[2] * N_DEV // (NOPE + ROPE)
    HB = fox_b_f.shape[1]
    f_loc = ffn_w1.shape[2]
    FP = _round_up(f_loc, LANE)
    n_ada = ada_w.shape[2]
    n_in = fox_w_in.shape[2]
    n_in_pad = _round_up(n_in, LANE)
    k_loc = mla_w_down.shape[1]
    n_uq, n_uk = mla_w_uq.shape[2], mla_w_uk.shape[2]
    assert D == HB * HEAD == HA * HEAD and n_ada * N_DEV == 6 * D
    alpha = (2 * L) ** 0.25
    xi, yi, ci = _place()
    me = 4 * xi + 2 * yi + ci
    core = jnp.reshape(ci, (1,)).astype(jnp.int32)
    chip = jnp.reshape(2 * xi + yi, (1,)).astype(jnp.int32)

    x2 = x[0]
    tgt = loss_target[0]

    c_all = all_gather("gather_c", [jnp.pad(c, ((0, 7), (0, 0)))], True)[0].reshape(N_DEV, 8, D)[:, 0]
    ada_b_cols = lax.dynamic_slice_in_dim(ada_b, me * n_ada, n_ada, axis=1)
    mod_part = ada_fwd("ada_fwd", c_all, ada_w, ada_b_cols)
    mod_all = all_gather("gather_mod", [mod_part.reshape(L * N_DEV, n_ada)], True)[0].reshape(N_DEV, L, N_DEV, n_ada)
    mod = lax.dynamic_index_in_dim(mod_all, me, axis=2, keepdims=False)
    mod = jnp.moveaxis(mod, 0, 1).reshape(L, 6, D)

    def modv(i, k):
        return mod[i, k][None, :]

    w1p = jnp.pad(ffn_w1.astype(BF), ((0, 0), (0, 0), (0, FP - f_loc)))
    w3p = jnp.pad(ffn_w3.astype(BF), ((0, 0), (0, 0), (0, FP - f_loc)))
    w2p = jnp.pad(ffn_w2.astype(BF), ((0, 0), (0, FP - f_loc), (0, 0)))
    w1g, w3g, w2g = _gather2d("gather_ffn", [w1p.reshape(L * D, FP), w3p.reshape(L * D, FP), w2p.reshape(L * FP, D)])
    w1g, w3g, w2g = w1g.reshape(N_DEV, L, D, FP), w3g.reshape(N_DEV, L, D, FP), w2g.reshape(N_DEV, L, FP, D)

    n_mla, n_fox = mla_w_down.shape[0], fox_w_in.shape[0]
    wdn, wuq, wuk, wuv, wo_a = _gather2d("gather_mla", [
        mla_w_down.astype(BF).reshape(n_mla * k_loc, -1), mla_w_uq.astype(BF).reshape(n_mla * QL, n_uq),
        mla_w_uk.astype(BF).reshape(n_mla * KVL, n_uk), mla_w_uv.astype(BF).reshape(n_mla * KVL, n_uk),
        mla_w_o.astype(BF).reshape(n_mla * k_loc, D)])
    win, wo_b = _gather2d("gather_fox", [
        jnp.pad(fox_w_in.astype(BF), ((0, 0), (0, 0), (0, n_in_pad - n_in))).reshape(n_fox * D, n_in_pad),
        fox_w_o.astype(BF).reshape(n_fox * k_loc, D)])
    fox_cols = 3 * D + LANE

    def rows_whole(g, n_layers, j):
        r = g.shape[1] // n_layers
        return g.reshape(N_DEV, n_layers, r, -1)[:, j].reshape(N_DEV * r, -1)

    def cols_whole(g, n_layers, j, keep=None):
        r = g.shape[1] // n_layers
        blk = g.reshape(N_DEV, n_layers, r, -1)[:, j, :, :keep]
        return jnp.moveaxis(blk, 0, 1).reshape(r, -1)

    w_down = [rows_whole(wdn, n_mla, j) for j in range(n_mla)]
    w_o_a = [rows_whole(wo_a, n_mla, j) for j in range(n_mla)]
    w_o_b = [rows_whole(wo_b, n_fox, j) for j in range(n_fox)]
    w_uq, w_ukv, w_in = [], [], []
    for j in range(n_mla):
        q3 = cols_whole(wuq, n_mla, j).reshape(QL, HA, NOPE + ROPE)
        w_uq.append(jnp.concatenate([q3[..., :NOPE].reshape(QL, HA * NOPE), q3[..., NOPE:].reshape(QL, HA * ROPE)], axis=-1))
        w_ukv.append(jnp.concatenate([cols_whole(wuk, n_mla, j), cols_whole(wuv, n_mla, j)], axis=-1))
    for j in range(n_fox):
        w_in.append(_pad_cols(cols_whole(win, n_fox, j, keep=n_in), fox_cols))
    b_f = _pad_cols(fox_b_f, LANE)

    inv_freq = ROPE_THETA ** (-jnp.arange(0, ROPE, 2, dtype=F32) / ROPE)
    ang = jnp.repeat(positions[0].astype(F32)[:, None] * inv_freq, 2, axis=1)
    cos_k, sin_k = jnp.cos(ang), jnp.sin(ang)
    cos_q, sin_q = jnp.tile(cos_k, (1, HA)), jnp.tile(sin_k, (1, HA))
    rot_k, rot_q = _pair_rotation(ROPE), _pair_rotation(HA * ROPE)
    tk = min(256, T)
    mla_scale = (NOPE + ROPE) ** -0.5
    fox_scale = HEAD ** -0.5

    def modulate(tag, xin, sc, sh):
        return rowwise("modulate_" + tag, lambda xv, scv, shv: xv * (1.0 + scv) + shv, [xin], [sc, sh], [(D, BF)])[0]

    def post(tag, xin, y, g, gam, bet):
        def f(xv, yv, gv, gamv, betv):
            xhat, _ = _ln_stats(alpha, xv, yv, gv)
            return xhat * gamv + betv
        return rowwise("post_" + tag, f, [xin, y], [g, gam, bet], [(D, F32)])[0]

    def post_bwd(tag, dxo, xin, y, g, gam):
        def f(dv, xv, yv, gv, gamv):
            xhat, rstd = _ln_stats(alpha, xv, yv, gv)
            dxh = dv * gamv
            dz = rstd * (dxh - _mean1(dxh) - xhat * _mean1(dxh * xhat))
            return alpha * dz, (1.0 + gv) * dz, _sum0(dv * xhat), _sum0(dv), _sum0(dz * yv)
        return rowwise("post_bwd_" + tag, f, [dxo, xin, y], [g, gam], [(D, F32), (D, BF)], [D, D, D])

    def pre_bwd(tag, dxa, dh, xin, sc):
        def f(dxav, dhv, xv, scv):
            return dxav + dhv * (1.0 + scv), _sum0(dhv * xv), _sum0(dhv)
        return rowwise("pre_bwd_" + tag, f, [dxa, dh, xin], [sc], [(D, F32)], [D, D])

    saved = []
    xc = x2
    for i in range(L):
        j = i // 2
        tag = "l%d" % i
        s = {"x_a": xc}
        h = modulate(tag + "a", xc, modv(i, 1), modv(i, 0))
        s["h_a"] = h
        if i % 2 == 0:
            lat = mm("mla_down_" + tag, h, w_down[j])

            def mid(latv, ck, sk, qg, kg, p):
                ql, kl, kp = latv[:, :QL], latv[:, QL:QL + KVL], latv[:, QL + KVL:]
                qn = ql * lax.rsqrt(_mean1(ql * ql) + RMS_EPS) * qg
                cn = kl * lax.rsqrt(_mean1(kl * kl) + RMS_EPS) * kg
                return qn, cn, kp * ck + _rot(kp, p) * sk

            qn, ckv, kpe = rowwise("mla_mid_" + tag, mid, [lat, cos_k, sin_k],
                                   [mla_q_norm[j][None], mla_kv_norm[j][None], rot_k], [(QL, BF), (KVL, BF), (ROPE, BF)])
            qf = mm("mla_uq_" + tag, qn, w_uq[j])

            def qrope(qv, cq, sq, p):
                qp = qv[:, HA * NOPE:]
                return (qp * cq + _rot(qp, p) * sq) * mla_scale

            qpe = rowwise("mla_qrope_" + tag, qrope, [qf, cos_q, sin_q], [rot_q], [(HA * ROPE, BF)])[0]
            qpe_h = jnp.moveaxis(qpe.reshape(T, HA, ROPE), 1, 0)
            kv = mm("mla_ukv_" + tag, ckv, w_ukv[j])
            o, lse = attn_fwd("mla_attn_" + tag, "mla", mla_scale, HA, T, qf, 0, kv, 0, kv, HA, qpe_h, kpe)
            y = mm("mla_o_" + tag, o, w_o_a[j])
            s.update(lat=lat, qn=qn, ckv=ckv, kpe=kpe, qf=qf, qpe_h=qpe_h, kv=kv, o=o, lse=lse)
        else:
            proj = mm("fox_in_" + tag, h, w_in[j], bn=896)
            fz = proj[:, 3 * D:]
            cum = fox_cum("fox_cum_" + tag, fz, b_f[j][None])
            cum_h = cum[:, :HB].T
            o, lse = attn_fwd("fox_attn_" + tag, "fox", fox_scale, HB, T, proj, 0, proj, HB, proj, 2 * HB,
                              cum_h.reshape(HB, T, 1), cum_h.reshape(HB, T // tk, 1, tk))
            y = mm("fox_o_" + tag, o, w_o_b[j])
            s.update(proj=proj, fz=fz, cum_h=cum_h, o=o, lse=lse)
        s["y_a"] = y
        xc = post(tag + "a", xc, y, modv(i, 2), ln1_g[i][None], ln1_b[i][None])
        s["x_f"] = xc
        h = modulate(tag + "f", xc, modv(i, 4), modv(i, 3))
        a, b, gte = ffn_up("ffn_up_" + tag, h, w1g, w3g, i)
        y = ffn_down("ffn_down_" + tag, gte, w2g, i)
        s.update(h_f=h, a=a, b=b, gte=gte, y_f=y)
        xc = post(tag + "f", xc, y, modv(i, 5), ln2_g[i][None], ln2_b[i][None])
        saved.append(s)

    def loss_fn(yv, tv):
        e = yv - tv
        return e * (1.0 / D), jnp.zeros((1, LANE), F32) + _sum0(jnp.sum(e * e, axis=1, keepdims=True)) * (0.5 / D)

    dx, loss_part = rowwise("loss", loss_fn, [xc, tgt], [], [(D, F32)], [LANE])

    G = {n: [None] * W[n].shape[0] for n in _WEIGHTS if n != "ada_w"}
    dmod = [[None] * 6 for _ in range(L)]
    for i in reversed(range(L)):
        j = i // 2
        tag = "l%d" % i
        s = saved[i]
        dxa, dy, G["ln2_g"][i], G["ln2_b"][i], dmod[i][5] = post_bwd(tag + "f", dx, s["x_f"], s["y_f"], modv(i, 5), ln2_g[i][None])
        g_w2 = ffn_out_grad("ffn_w2_grad_" + tag, s["gte"], dy)
        da, db = ffn_down_bwd("ffn_down_bwd_" + tag, dy, w2g, s["a"], s["b"], i)
        g_w1 = ffn_in_grad("ffn_w1_grad_" + tag, s["h_f"], da)
        g_w3 = ffn_in_grad("ffn_w3_grad_" + tag, s["h_f"], db)
        dh = ffn_dh("ffn_dh_" + tag, da, db, w1g, w3g, i)
        r_w1, r_w3, r_w2 = reduce_scatter("ffn_" + tag, [g_w1, g_w3, g_w2], core, chip)
        G["ffn_w1"][i], G["ffn_w3"][i], G["ffn_w2"][i] = r_w1[:, :f_loc], r_w3[:, :f_loc], r_w2[:f_loc]
        dx, dmod[i][4], dmod[i][3] = pre_bwd(tag + "f", dxa, dh, s["x_f"], modv(i, 4))
        dxa, dy, G["ln1_g"][i], G["ln1_b"][i], dmod[i][2] = post_bwd(tag + "a", dx, s["x_a"], s["y_a"], modv(i, 2), ln1_g[i][None])
        if i % 2 == 0:
            g_o = mm("mla_o_grad_" + tag, s["o"], dy, ta=True, out_dtype=BF)
            do = mm("mla_do_" + tag, dy, w_o_a[j], tb=True, out_dtype=BF)
            dqn_, dkn_, dv_, dqpe_h, dkpe = attn_bwd("mla_attn_bwd_" + tag, "mla", mla_scale, HA, T, s["qf"], 0, s["kv"], 0,
                                                     s["kv"], HA, s["qpe_h"], s["kpe"], s["o"], do, s["lse"])
            dkv = jnp.concatenate([dkn_, dv_], axis=1)
            g_kv = mm("mla_ukv_grad_" + tag, s["ckv"], dkv, ta=True, out_dtype=BF)
            dckv = mm("mla_dckv_" + tag, dkv, w_ukv[j], tb=True)

            def qrope_bwd(dv, cq, sq, p):
                dv = dv * mla_scale
                return dv * cq - _rot(dv * sq, p)

            dqpe = rowwise("mla_qrope_bwd_" + tag, qrope_bwd, [jnp.moveaxis(dqpe_h, 0, 1).reshape(T, HA * ROPE), cos_q, sin_q],
                           [rot_q], [(HA * ROPE, BF)])[0]
            dq = jnp.concatenate([dqn_, dqpe], axis=1)
            g_uq = mm("mla_uq_grad_" + tag, s["qn"], dq, ta=True, out_dtype=BF)
            g_uq = jnp.concatenate([g_uq[:, :HA * NOPE].reshape(QL, HA, NOPE), g_uq[:, HA * NOPE:].reshape(QL, HA, ROPE)],
                                   axis=-1).reshape(QL, N_DEV, n_uq)
            dqn = mm("mla_dqn_" + tag, dq, w_uq[j], tb=True)

            def mid_bwd(latv, dqv, dcv, dkp, ck, sk, qg, kg, p):
                ql, kl = latv[:, :QL], latv[:, QL:QL + KVL]
                rq = lax.rsqrt(_mean1(ql * ql) + RMS_EPS)
                rk = lax.rsqrt(_mean1(kl * kl) + RMS_EPS)
                uq, uk = dqv * qg, dcv * kg
                dql = rq * uq - ql * (rq * rq * rq) * _mean1(uq * ql)
                dkl = rk * uk - kl * (rk * rk * rk) * _mean1(uk * kl)
                return dql, dkl, dkp * ck - _rot(dkp * sk, p), _sum0(dqv * ql * rq), _sum0(dcv * kl * rk)

            dql, dkl, dkp, G["mla_q_norm"][j], G["mla_kv_norm"][j] = rowwise(
                "mla_mid_bwd_" + tag, mid_bwd, [s["lat"], dqn, dckv, dkpe, cos_k, sin_k],
                [mla_q_norm[j][None], mla_kv_norm[j][None], rot_k], [(QL, BF), (KVL, BF), (ROPE, BF)], [QL, KVL])
            dlat = jnp.concatenate([dql, dkl, dkp], axis=1)
            g_dn = mm("mla_down_grad_" + tag, s["h_a"], dlat, ta=True, out_dtype=BF)
            dh = mm("mla_dh_" + tag, dlat, w_down[j], tb=True)
            g_kv = jnp.moveaxis(g_kv.reshape(KVL, 2, N_DEV, n_uk), 2, 0).reshape(N_DEV, KVL, 2 * n_uk)
            r_dn, r_uq, r_kv, r_o = reduce_scatter("mla_" + tag, [
                g_dn.reshape(N_DEV, k_loc, -1), jnp.moveaxis(g_uq, 1, 0), g_kv, g_o.reshape(N_DEV, k_loc, D)], core, chip)
            G["mla_w_down"][j], G["mla_w_uq"][j], G["mla_w_o"][j] = r_dn, r_uq, r_o
            G["mla_w_uk"][j], G["mla_w_uv"][j] = r_kv[:, :n_uk], r_kv[:, n_uk:]
        else:
            g_o = mm("fox_o_grad_" + tag, s["o"], dy, ta=True, out_dtype=BF)
            do = mm("fox_do_" + tag, dy, w_o_b[j], tb=True, out_dtype=BF)
            cum_h = s["cum_h"]
            dq_, dk_, dv_, dcq, dck = attn_bwd("fox_attn_bwd_" + tag, "fox", fox_scale, HB, T, s["proj"], 0, s["proj"], HB,
                                               s["proj"], 2 * HB, cum_h.reshape(HB, T, 1), cum_h.reshape(HB, T // tk, 1, tk),
                                               s["o"], do, s["lse"])
            dcum = _pad_cols((dcq.reshape(HB, T) + dck.reshape(HB, T)).T, LANE)
            df, db_f = fox_cum_bwd("fox_cum_bwd_" + tag, s["fz"], b_f[j][None], dcum)
            G["fox_b_f"][j] = db_f[:, :HB]
            dproj = jnp.concatenate([dq_, dk_, dv_, df], axis=1)
            g_in = mm("fox_in_grad_" + tag, s["h_a"], dproj, ta=True, out_dtype=BF, bn=896)[:, :N_DEV * n_in]
            dh = mm("fox_dh_" + tag, dproj, w_in[j], tb=True, bk=896)
            g_in = jnp.pad(jnp.moveaxis(g_in.reshape(D, N_DEV, n_in), 1, 0), ((0, 0), (0, 0), (0, n_in_pad - n_in)))
            r_in, r_o = reduce_scatter("fox_" + tag, [g_in, g_o.reshape(N_DEV, k_loc, D)], core, chip)
            G["fox_w_in"][j], G["fox_w_o"][j] = r_in[:, :n_in], r_o
        dx, dmod[i][1], dmod[i][0] = pre_bwd(tag + "a", dxa, dh, s["x_a"], modv(i, 1))

    dmod_mine = jnp.concatenate([jnp.concatenate(r, axis=1) for r in dmod], axis=0)
    small_parts = [dmod_mine] + [jnp.concatenate(G[n], axis=0) for n in _SMALL[1:]] + [loss_part]
    small = _small_pack(small_parts, 512)
    small_all = all_gather("gather_small", [small], True)[0].reshape(N_DEV, -1, 512)
    small_sum = sum_devices("sum_small", small_all)
    sums, off = [], 0
    for p in small_parts:
        nrow = _round_up(p.size, 512) // 512
        sums.append(small_sum[off:off + nrow].reshape(-1)[:p.size].reshape(p.shape))
        off += nrow
    grads = {n: gsum for n, gsum in zip(_SMALL, sums[:-1])}
    loss = sums[-1][0, 0]
    nrow = L * 6 * D // 512
    dmod_all = small_all[:, :nrow].reshape(N_DEV, L, 6 * D)
    dmod_cols = jnp.moveaxis(lax.dynamic_slice_in_dim(dmod_all, me * n_ada, n_ada, axis=2), 0, 1)

    for n in ("ffn_w1", "ffn_w3", "ffn_w2", "mla_w_down", "mla_w_uq", "mla_w_uk", "mla_w_uv", "mla_w_o", "fox_w_in", "fox_w_o"):
        grads[n] = jnp.stack(G[n])

    delta, new_m, new_v = {}, {}, {}
    grads["ada_w"], delta["ada_w"], new_m["ada_w"], new_v["ada_w"] = ada_bwd_adamw(
        "ada_grad_adamw", c_all, dmod_cols, ada_w, m_ada_w, v_ada_w)
    for n in _WEIGHTS:
        if n != "ada_w":
            delta[n], new_m[n], new_v[n] = adamw("adamw_" + n, W[n], grads[n], M[n], V[n])

    return (loss, dx[None], *[grads[n] for n in _WEIGHTS], *[delta[n] for n in _WEIGHTS],
            *[new_m[n] for n in _WEIGHTS], *[new_v[n] for n in _WEIGHTS])
```

```python
import math

import numpy as np
import jax
import jax.numpy as jnp
from jax import lax
from jax.experimental import pallas as pl
from jax.experimental.pallas import tpu as pltpu

F32 = jnp.float32
BF = jnp.bfloat16
N_DEV = 8

CHUNK_SHIFT = 6
NOPE = 128
ROPE = 64
HEAD = 128
ROPE_THETA = 10000.0
LN_EPS = 1e-5
RMS_EPS = 1e-6
ADAM_LR = 0.001
ADAM_B1 = 0.9
ADAM_B2 = 0.999
ADAM_EPS = 1e-08
ADAM_WD = 0.01
ADAM_STEP = 10

VMEM_LIMIT_BYTES = 48 * 1024 * 1024
LANE = 128
NEG = -1e30


def _cp(*sem):
    return pltpu.CompilerParams(dimension_semantics=sem, vmem_limit_bytes=VMEM_LIMIT_BYTES)


def _pick(dim, pref):
    if dim <= pref:
        return dim
    for t in range(pref - pref % LANE, LANE - 1, -LANE):
        if dim % t == 0:
            return t
    return dim


def _round_up(v, m):
    return (v + m - 1) // m * m


def _sum0(v):
    return jnp.sum(v, axis=0, keepdims=True)


def _mean1(v):
    return jnp.mean(v, axis=-1, keepdims=True)


_NN = (((1,), (0,)), ((), ()))
_NT = (((1,), (1,)), ((), ()))
_TN = (((0,), (0,)), ((), ()))


def _mm_call(name, grid, ins, in_specs, dns, out_shape, out_spec, add=None, add_spec=None):
    npairs = len(dns)
    nk = grid[-1]
    has_add = add is not None

    def body(*refs):
        add_ref = refs[2 * npairs] if has_add else None
        o_ref = refs[2 * npairs + (1 if has_add else 0)]
        r = None
        for p in range(npairs):
            t = lax.dot_general(refs[2 * p][...].astype(BF), refs[2 * p + 1][...].astype(BF), dns[p], preferred_element_type=F32)
            r = t if r is None else r + t

        def fin(val):
            if has_add:
                val = val + add_ref[...]
            o_ref[...] = val.astype(o_ref.dtype)

        if nk == 1:
            fin(r)
        else:
            acc = refs[-1]
            k = pl.program_id(len(grid) - 1)

            @pl.when(k == 0)
            def _():
                acc[...] = r

            @pl.when(k > 0)
            def _():
                acc[...] += r

            @pl.when(k == nk - 1)
            def _():
                fin(acc[...])

    acc_shape = tuple(b for b in out_spec.block_shape if b is not None)
    args, specs = list(ins), list(in_specs)
    if has_add:
        args.append(add)
        specs.append(add_spec)
    return pl.pallas_call(
        body, name=name, grid=grid, in_specs=specs, out_specs=out_spec, out_shape=out_shape,
        scratch_shapes=[pltpu.VMEM(acc_shape, F32)] if nk > 1 else [],
        compiler_params=_cp(*(["parallel"] * (len(grid) - 1) + ["arbitrary"])),
    )(*args)


def mm(name, a, b, ta=False, tb=False, out_dtype=F32, add=None, bm=1024, bn=1024, bk=2048):
    M, K = (a.shape[1], a.shape[0]) if ta else a.shape
    N = b.shape[0] if tb else b.shape[1]
    bm, bn, bk = _pick(M, bm), _pick(N, bn), _pick(K, bk)
    a_spec = pl.BlockSpec((bk, bm), lambda i, j, k: (k, i)) if ta else pl.BlockSpec((bm, bk), lambda i, j, k: (i, k))
    b_spec = pl.BlockSpec((bn, bk), lambda i, j, k: (j, k)) if tb else pl.BlockSpec((bk, bn), lambda i, j, k: (k, j))
    dn = (((0 if ta else 1,), (1 if tb else 0,)), ((), ()))
    tile = pl.BlockSpec((bm, bn), lambda i, j, k: (i, j))
    return _mm_call(name, (M // bm, N // bn, K // bk), [a, b], [a_spec, b_spec], [dn],
                    jax.ShapeDtypeStruct((M, N), out_dtype), tile, add, tile)


def rowwise(name, fn, rows, vecs, out_rows, out_accs=(), tm=256):
    T = rows[0].shape[0]
    tm = min(tm, T)
    nr, nv, no = len(rows), len(vecs), len(out_rows)

    def body(*refs):
        outs = fn(*[r[...] for r in refs[:nr + nv]])
        if not isinstance(outs, (tuple, list)):
            outs = (outs,)
        o_refs = refs[nr + nv:nr + nv + no]
        a_refs = refs[nr + nv + no:]
        for r, o in zip(o_refs, outs[:no]):
            r[...] = o.astype(r.dtype)
        if a_refs:
            @pl.when(pl.program_id(0) == 0)
            def _():
                for r in a_refs:
                    r[...] = jnp.zeros_like(r)

            for r, o in zip(a_refs, outs[no:]):
                r[...] += o

    in_specs = [pl.BlockSpec((tm, r.shape[1]), lambda i: (i, 0)) for r in rows]
    in_specs += [pl.BlockSpec(v.shape, lambda i: (0, 0)) for v in vecs]
    out_specs = [pl.BlockSpec((tm, w), lambda i: (i, 0)) for w, _ in out_rows]
    out_specs += [pl.BlockSpec((1, w), lambda i: (0, 0)) for w in out_accs]
    out_shape = [jax.ShapeDtypeStruct((T, w), dt) for w, dt in out_rows]
    out_shape += [jax.ShapeDtypeStruct((1, w), F32) for w in out_accs]
    return pl.pallas_call(
        body, name=name, grid=(T // tm,), in_specs=in_specs, out_specs=out_specs, out_shape=out_shape,
        compiler_params=_cp("arbitrary"),
    )(*rows, *vecs)


def _rot(v, p):
    hi = v.astype(BF)
    lo = (v - hi.astype(F32)).astype(BF)
    return jnp.dot(hi, p, preferred_element_type=F32) + jnp.dot(lo, p, preferred_element_type=F32)


def _pair_rotation(width):
    p = np.zeros((width, width), np.float32)
    idx = np.arange(0, width, 2)
    p[idx + 1, idx] = -1.0
    p[idx, idx + 1] = 1.0
    return jnp.asarray(p, BF)


def _ln_stats(alpha, x, y, g):
    z = alpha * x + (1.0 + g) * y
    zc = z - _mean1(z)
    rstd = lax.rsqrt(_mean1(zc * zc) + LN_EPS)
    return zc * rstd, rstd


ATT_G = 2
ATT_TILE = 256


def _visible(mode, ks, qs, t):
    krow = ks + lax.broadcasted_iota(jnp.int32, (t, t), 0)
    qcol = qs + lax.broadcasted_iota(jnp.int32, (t, t), 1)
    if mode == "mla":
        return lax.shift_right_logical(qcol, CHUNK_SHIFT) >= lax.shift_right_logical(krow, CHUNK_SHIFT)
    return qcol >= krow


def _scores(mode, k, q, e_k, e_q, ks, qs, t, masked):
    s = lax.dot_general(k, q, _NT, preferred_element_type=F32)
    if mode == "mla":
        s = s + lax.dot_general(e_k, e_q, _NT, preferred_element_type=F32)
    else:
        s = s + e_q - e_k
    if masked:
        s = jnp.where(_visible(mode, ks, qs, t), s, NEG)
    return s


def _attn_specs(mode, T, t, q_off, k_off, v_off):
    G = ATT_G

    def col(off):
        return pl.BlockSpec((T, G * HEAD), lambda h: (0, off // G + h))

    stat = pl.BlockSpec((G, T // t, 1, t), lambda h: (h, 0, 0, 0))
    if mode == "mla":
        e_specs = [pl.BlockSpec((G, T, ROPE), lambda h: (h, 0, 0)), pl.BlockSpec((T, ROPE), lambda h: (0, 0))]
    else:
        e_specs = [stat, pl.BlockSpec((G, T, 1), lambda h: (h, 0, 0))]
    return [col(q_off), col(k_off), col(v_off)], e_specs, stat


def _head(ref, rows, g):
    return ref[rows, g * HEAD:(g + 1) * HEAD]


def attn_fwd(name, mode, scale, H, T, q_src, q_off, k_src, k_off, v_src, v_off, e_q, e_k):
    G, t = ATT_G, min(ATT_TILE, T)
    nq = T // t
    mla = mode == "mla"

    def body(q_ref, k_ref, v_ref, eq_ref, ek_ref, o_ref, lse_ref, acc_ref):
        def q_loop(qi, _):
            qs = pl.multiple_of(qi * t, t)
            qrows = pl.ds(qs, t)
            qh = [(_head(q_ref, qrows, g).astype(F32) * scale).astype(BF) for g in range(G)]
            eqh = [eq_ref[g, qrows, :] if mla else eq_ref[g, qi] for g in range(G)]
            acc_ref[...] = jnp.zeros_like(acc_ref)

            def step(j, carry, masked):
                ks = pl.multiple_of(j * t, t)
                krows = pl.ds(ks, t)
                out = []
                for g in range(G):
                    m, l = carry[g]
                    k = _head(k_ref, krows, g).astype(BF)
                    ek = ek_ref[krows, :] if mla else ek_ref[g, krows, :]
                    s = _scores(mode, k, qh[g], ek, eqh[g], ks, qs, t, masked)
                    m_new = jnp.maximum(m, jnp.max(s, axis=0, keepdims=True))
                    p = jnp.exp(s - m_new)
                    corr = jnp.exp(m - m_new)
                    l = corr * l + jnp.sum(p, axis=0, keepdims=True)
                    v = _head(v_ref, krows, g).astype(BF)
                    acc_ref[g] = corr * acc_ref[g] + lax.dot_general(v, p.astype(BF), _TN, preferred_element_type=F32)
                    out.append((m_new, l))
                return tuple(out)

            init = tuple((jnp.full((1, t), NEG, F32), jnp.zeros((1, t), F32)) for _ in range(G))
            carry = lax.fori_loop(0, qi, lambda j, cr: step(j, cr, False), init)
            carry = step(qi, carry, True)
            for g in range(G):
                m, l = carry[g]
                o_ref[qrows, g * HEAD:(g + 1) * HEAD] = (acc_ref[g] / l).T.astype(o_ref.dtype)
                lse_ref[g, qi] = m + jnp.log(l)
            return 0

        lax.fori_loop(0, nq, q_loop, 0)

    cols, e_specs, stat = _attn_specs(mode, T, t, q_off, k_off, v_off)
    return pl.pallas_call(
        body, name=name, grid=(H // G,), in_specs=cols + e_specs,
        out_specs=[pl.BlockSpec((T, G * HEAD), lambda h: (0, h)), stat],
        out_shape=[jax.ShapeDtypeStruct((T, H * HEAD), BF), jax.ShapeDtypeStruct((H, nq, 1, t), F32)],
        scratch_shapes=[pltpu.VMEM((G, HEAD, t), F32)],
        compiler_params=_cp("parallel"),
    )(q_src, k_src, v_src, e_q, e_k)


def attn_bwd(name, mode, scale, H, T, q_src, q_off, k_src, k_off, v_src, v_off, e_q, e_k, do, lse, delta):
    G, t = ATT_G, min(ATT_TILE, T)
    nq = T // t
    mla = mode == "mla"

    def body(q_ref, k_ref, v_ref, eq_ref, ek_ref, do_ref, lse_ref, dl_ref,
             dq_ref, dk_ref, dv_ref, deq_ref, dek_ref, dk_acc, dv_acc, dq_acc, deq_acc):
        dk_acc[...] = jnp.zeros_like(dk_acc)
        dv_acc[...] = jnp.zeros_like(dv_acc)
        if mla:
            @pl.when(pl.program_id(0) == 0)
            def _():
                dek_ref[...] = jnp.zeros_like(dek_ref)
        else:
            dek_ref[...] = jnp.zeros_like(dek_ref)

        def q_loop(qi, _):
            qs = pl.multiple_of(qi * t, t)
            qrows = pl.ds(qs, t)
            qh = [(_head(q_ref, qrows, g).astype(F32) * scale).astype(BF) for g in range(G)]
            eqh = [eq_ref[g, qrows, :] if mla else eq_ref[g, qi] for g in range(G)]
            doh = [_head(do_ref, qrows, g) for g in range(G)]
            lse_q = [lse_ref[g, qi] for g in range(G)]
            dl_q = [dl_ref[g, qi] for g in range(G)]
            dq_acc[...] = jnp.zeros_like(dq_acc)
            deq_acc[...] = jnp.zeros_like(deq_acc)

            def step(j, carry, masked):
                ks = pl.multiple_of(j * t, t)
                krows = pl.ds(ks, t)
                for g in range(G):
                    k = _head(k_ref, krows, g).astype(BF)
                    v = _head(v_ref, krows, g).astype(BF)
                    ek = ek_ref[krows, :] if mla else ek_ref[g, krows, :]
                    s = _scores(mode, k, qh[g], ek, eqh[g], ks, qs, t, masked)
                    p = jnp.exp(s - lse_q[g])
                    dv_acc[g, krows, :] += jnp.dot(p.astype(BF), doh[g], preferred_element_type=F32)
                    dp = lax.dot_general(v, doh[g], _NT, preferred_element_type=F32)
                    ds = p * (dp - dl_q[g])
                    dsb = ds.astype(BF)
                    dk_acc[g, krows, :] += jnp.dot(dsb, qh[g], preferred_element_type=F32)
                    dq_acc[g] += lax.dot_general(dsb, k, _TN, preferred_element_type=F32)
                    if mla:
                        deq_acc[g] += lax.dot_general(dsb, ek, _TN, preferred_element_type=F32)
                        dek_ref[krows, :] += jnp.dot(dsb, eqh[g], preferred_element_type=F32)
                    else:
                        deq_acc[g] += jnp.sum(ds, axis=0, keepdims=True)
                        dek_ref[g, krows, :] -= jnp.sum(ds, axis=1, keepdims=True)
                return carry

            lax.fori_loop(0, qi, lambda j, cr: step(j, cr, False), 0)
            step(qi, 0, True)
            for g in range(G):
                dq_ref[qrows, g * HEAD:(g + 1) * HEAD] = (dq_acc[g] * scale).astype(dq_ref.dtype)
                if mla:
                    deq_ref[g, qrows, :] = deq_acc[g]
                else:
                    deq_ref[g, qi] = deq_acc[g]
            return 0

        lax.fori_loop(0, nq, q_loop, 0)
        for g in range(G):
            dk_ref[:, g * HEAD:(g + 1) * HEAD] = dk_acc[g].astype(dk_ref.dtype)
            dv_ref[:, g * HEAD:(g + 1) * HEAD] = dv_acc[g].astype(dv_ref.dtype)

    cols, e_specs, stat = _attn_specs(mode, T, t, q_off, k_off, v_off)
    heads = pl.BlockSpec((T, G * HEAD), lambda h: (0, h))
    if mla:
        de_shapes = [jax.ShapeDtypeStruct((H, T, ROPE), F32), jax.ShapeDtypeStruct((T, ROPE), F32)]
        deq_scratch = pltpu.VMEM((G, t, ROPE), F32)
    else:
        de_shapes = [jax.ShapeDtypeStruct((H, nq, 1, t), F32), jax.ShapeDtypeStruct((H, T, 1), F32)]
        deq_scratch = pltpu.VMEM((G, 1, t), F32)
    return pl.pallas_call(
        body, name=name, grid=(H // G,),
        in_specs=cols + e_specs + [heads, stat, stat],
        out_specs=[heads, heads, heads] + e_specs,
        out_shape=[jax.ShapeDtypeStruct((T, H * HEAD), BF)] * 3 + de_shapes,
        scratch_shapes=[pltpu.VMEM((G, T, HEAD), F32), pltpu.VMEM((G, T, HEAD), F32), pltpu.VMEM((G, t, HEAD), F32), deq_scratch],
        compiler_params=_cp("arbitrary"),
    )(q_src, k_src, v_src, e_q, e_k, do, lse, delta)


def attn_delta(name, do, o, H, t):
    T, W = do.shape
    ind = np.zeros((W, LANE), np.float32)
    ind[np.arange(W), np.arange(W) // HEAD] = 1.0

    def f(dv, ov, e):
        return _rot(dv.astype(F32) * ov.astype(F32), e)

    d = rowwise(name, f, [do, o], [jnp.asarray(ind, BF)], [(LANE, F32)])[0]
    return d[:, :H].T.reshape(H, T // t, 1, t)


def fox_cum(name, f, b):
    T = f.shape[0]

    def body(f_ref, b_ref, cum_ref):
        z = f_ref[...] + b_ref[...]
        v = jnp.minimum(z, 0.0) - jnp.log(1.0 + jnp.exp(-jnp.abs(z)))
        row = lax.broadcasted_iota(jnp.int32, v.shape, 0)
        s = 1
        while s < T:
            v = v + jnp.where(row >= s, pltpu.roll(v, s, 0), 0.0)
            s *= 2
        cum_ref[...] = v

    return pl.pallas_call(body, name=name, out_shape=jax.ShapeDtypeStruct(f.shape, F32),
                          compiler_params=pltpu.CompilerParams(vmem_limit_bytes=VMEM_LIMIT_BYTES))(f, b)


def fox_cum_bwd(name, f, b, dcum):
    T = f.shape[0]

    def body(f_ref, b_ref, dc_ref, df_ref, db_ref):
        v = dc_ref[...]
        row = lax.broadcasted_iota(jnp.int32, v.shape, 0)
        s = 1
        while s < T:
            v = v + jnp.where(row < T - s, pltpu.roll(v, T - s, 0), 0.0)
            s *= 2
        z = f_ref[...] + b_ref[...]
        df = v / (1.0 + jnp.exp(z))
        df_ref[...] = df.astype(df_ref.dtype)
        db_ref[...] = _sum0(df)

    return pl.pallas_call(body, name=name,
                          out_shape=[jax.ShapeDtypeStruct(f.shape, BF), jax.ShapeDtypeStruct((1, f.shape[1]), F32)],
                          compiler_params=pltpu.CompilerParams(vmem_limit_bytes=VMEM_LIMIT_BYTES))(f, b, dcum)


def ffn_up(name, h, w1g, w3g, l):
    T, D = h.shape
    FP = w1g.shape[3]
    bm = _pick(T, 1024)

    def body(h_ref, w1_ref, w3_ref, a_ref, b_ref, g_ref):
        hv = h_ref[...]
        a = jnp.dot(hv, w1_ref[...], preferred_element_type=F32)
        b = jnp.dot(hv, w3_ref[...], preferred_element_type=F32)
        a_ref[...] = a.astype(BF)
        b_ref[...] = b.astype(BF)
        g_ref[...] = (a * (1.0 / (1.0 + jnp.exp(-a))) * b).astype(BF)

    w_spec = pl.BlockSpec((None, None, D, FP), lambda i, d: (d, l, 0, 0))
    tile = pl.BlockSpec((None, bm, FP), lambda i, d: (d, i, 0))
    return pl.pallas_call(
        body, name=name, grid=(T // bm, N_DEV), in_specs=[pl.BlockSpec((bm, D), lambda i, d: (i, 0)), w_spec, w_spec],
        out_specs=[tile, tile, tile], out_shape=[jax.ShapeDtypeStruct((N_DEV, T, FP), BF)] * 3,
        compiler_params=_cp("parallel", "parallel"),
    )(h, w1g, w3g)


def ffn_down(name, g, w2g, l):
    _, T, FP = g.shape
    D = w2g.shape[3]
    bm, bn = _pick(T, 1024), _pick(D, 1024)
    return _mm_call(
        name, (T // bm, D // bn, N_DEV), [g, w2g],
        [pl.BlockSpec((None, bm, FP), lambda i, j, d: (d, i, 0)), pl.BlockSpec((None, None, FP, bn), lambda i, j, d: (d, l, 0, j))],
        [_NN], jax.ShapeDtypeStruct((T, D), F32), pl.BlockSpec((bm, bn), lambda i, j, d: (i, j)))


def ffn_down_bwd(name, dy, w2g, a, b, l):
    T, D = dy.shape
    FP = w2g.shape[2]
    bm = _pick(T, 1024)

    def body(dy_ref, w2_ref, a_ref, b_ref, da_ref, db_ref):
        dg = lax.dot_general(dy_ref[...], w2_ref[...], _NT, preferred_element_type=F32)
        av = a_ref[...].astype(F32)
        bv = b_ref[...].astype(F32)
        sig = 1.0 / (1.0 + jnp.exp(-av))
        da_ref[...] = (dg * bv * sig * (1.0 + av * (1.0 - sig))).astype(BF)
        db_ref[...] = (dg * av * sig).astype(BF)

    tile = pl.BlockSpec((None, bm, FP), lambda i, d: (d, i, 0))
    return pl.pallas_call(
        body, name=name, grid=(T // bm, N_DEV),
        in_specs=[pl.BlockSpec((bm, D), lambda i, d: (i, 0)), pl.BlockSpec((None, None, FP, D), lambda i, d: (d, l, 0, 0)), tile, tile],
        out_specs=[tile, tile], out_shape=[jax.ShapeDtypeStruct((N_DEV, T, FP), BF)] * 2,
        compiler_params=_cp("parallel", "parallel"),
    )(dy, w2g, a, b)


def ffn_dh(name, da, db, w1g, w3g, l):
    _, T, FP = da.shape
    D = w1g.shape[2]
    bm, bn = _pick(T, 1024), _pick(D, 1024)
    act = pl.BlockSpec((None, bm, FP), lambda i, j, d: (d, i, 0))
    wgt = pl.BlockSpec((None, None, bn, FP), lambda i, j, d: (d, l, j, 0))
    return _mm_call(name, (T // bm, D // bn, N_DEV), [da, w1g, db, w3g], [act, wgt, act, wgt], [_NT, _NT],
                    jax.ShapeDtypeStruct((T, D), F32), pl.BlockSpec((bm, bn), lambda i, j, d: (i, j)))


def ffn_in_grad(name, h, dact):
    T, D = h.shape
    FP = dact.shape[2]
    bm = _pick(D, 1024)
    return _mm_call(
        name, (D // bm, N_DEV, 1), [h, dact],
        [pl.BlockSpec((T, bm), lambda i, d, k: (0, i)), pl.BlockSpec((None, T, FP), lambda i, d, k: (d, 0, 0))],
        [_TN], jax.ShapeDtypeStruct((N_DEV, D, FP), BF), pl.BlockSpec((None, bm, FP), lambda i, d, k: (d, i, 0)))


def ffn_out_grad(name, g, dy):
    _, T, FP = g.shape
    D = dy.shape[1]
    bn = _pick(D, 1024)
    return _mm_call(
        name, (N_DEV, D // bn, 1), [g, dy],
        [pl.BlockSpec((None, T, FP), lambda d, j, k: (d, 0, 0)), pl.BlockSpec((T, bn), lambda d, j, k: (0, j))],
        [_TN], jax.ShapeDtypeStruct((N_DEV, FP, D), BF), pl.BlockSpec((None, FP, bn), lambda d, j, k: (d, 0, j)))


def _adamw_math(w, g, m, v):
    m = ADAM_B1 * m + (1.0 - ADAM_B1) * g
    v = ADAM_B2 * v + (1.0 - ADAM_B2) * (g * g)
    m_hat = m / (1.0 - ADAM_B1 ** ADAM_STEP)
    v_hat = v / (1.0 - ADAM_B2 ** ADAM_STEP)
    delta = -ADAM_LR * (m_hat / (jnp.sqrt(v_hat) + ADAM_EPS) + ADAM_WD * w)
    return delta, m, v


def adamw(name, w, g, m, v):
    shape = w.shape
    n = shape[-1]
    rows = math.prod(shape[:-1])
    tr = rows if rows <= 512 else 256
    assert rows % tr == 0

    def body(w_ref, g_ref, m_ref, v_ref, d_ref, mo_ref, vo_ref):
        d, mn, vn = _adamw_math(w_ref[...], g_ref[...], m_ref[...], v_ref[...])
        d_ref[...] = d
        mo_ref[...] = mn
        vo_ref[...] = vn

    blk = pl.BlockSpec((tr, n), lambda i: (i, 0))
    outs = pl.pallas_call(
        body, name=name, grid=(rows // tr,), in_specs=[blk] * 4, out_specs=[blk] * 3,
        out_shape=[jax.ShapeDtypeStruct((rows, n), F32)] * 3, compiler_params=_cp("parallel"),
    )(*[t.reshape(rows, n) for t in (w, g, m, v)])
    return [t.reshape(shape) for t in outs]


def _silu(v):
    return v * (1.0 / (1.0 + jnp.exp(-v)))


def ada_fwd(name, c_all, ada_w, ada_b_cols):
    L, D, n = ada_w.shape
    bn = _pick(n, 512)

    def body(c_ref, w_ref, b_ref, o_ref):
        act = _silu(c_ref[...]).astype(BF)
        o_ref[...] = jnp.dot(act, w_ref[...].astype(BF), preferred_element_type=F32) + b_ref[...]

    return pl.pallas_call(
        body, name=name, grid=(L, n // bn),
        in_specs=[pl.BlockSpec((N_DEV, D), lambda l, j: (0, 0)), pl.BlockSpec((None, D, bn), lambda l, j: (l, 0, j)),
                  pl.BlockSpec((None, 1, bn), lambda l, j: (l, 0, j))],
        out_specs=pl.BlockSpec((None, N_DEV, bn), lambda l, j: (l, 0, j)),
        out_shape=jax.ShapeDtypeStruct((L, N_DEV, n), F32), compiler_params=_cp("parallel", "parallel"),
    )(c_all, ada_w, ada_b_cols.reshape(L, 1, n))


def ada_bwd_adamw(name, c_all, dmod_cols, w, m, v):
    L, D, n = w.shape
    tr = _pick(D, 256)

    def body(c_ref, dm_ref, w_ref, m_ref, v_ref, g_ref, d_ref, mo_ref, vo_ref):
        act = _silu(c_ref[...]).astype(BF)
        g = lax.dot_general(act, dm_ref[...].astype(BF), _TN, preferred_element_type=F32)
        d, mn, vn = _adamw_math(w_ref[...], g, m_ref[...], v_ref[...])
        g_ref[...] = g
        d_ref[...] = d
        mo_ref[...] = mn
        vo_ref[...] = vn

    blk = pl.BlockSpec((None, tr, n), lambda l, i: (l, i, 0))
    return pl.pallas_call(
        body, name=name, grid=(L, D // tr),
        in_specs=[pl.BlockSpec((N_DEV, tr), lambda l, i: (0, i)), pl.BlockSpec((None, N_DEV, n), lambda l, i: (l, 0, 0)),
                  blk, blk, blk],
        out_specs=[blk] * 4, out_shape=[jax.ShapeDtypeStruct((L, D, n), F32)] * 4,
        compiler_params=_cp("parallel", "parallel"),
    )(c_all, dmod_cols, w, m, v)


_MESH = pl.DeviceIdType.MESH
_ANY = pl.BlockSpec(memory_space=pl.ANY)
_CHIP_FLIPS = ((1, 0), (0, 1), (1, 1))


def _place():
    return lax.axis_index("x"), lax.axis_index("y"), lax.axis_index("c")


def _flip(v, f):
    return 1 - v if f else v


def all_gather(name, shards, in_vmem=False):
    nt = len(shards)

    def body(*refs):
        x_refs, out_refs = refs[:nt], refs[nt:2 * nt]
        send_sems, recv_sems, local_sems = refs[2 * nt:]
        x, y, c = _place()
        me, sibling = (x, y, c), (x, y, 1 - c)
        chips = [(_flip(x, fx), _flip(y, fy)) for fx, fy in _CHIP_FLIPS]

        def rows(t, px, py, pc):
            m_per = shards[t].shape[0]
            return out_refs[t].at[pl.ds((4 * px + 2 * py + pc) * m_per, m_per), :]

        def copy(t, k, block, to, src=None):
            return pltpu.make_async_remote_copy(
                src_ref=rows(t, *block) if src is None else src, dst_ref=rows(t, *block),
                send_sem=send_sems.at[7 * t + k], recv_sem=recv_sems.at[7 * t + k], device_id=to, device_id_type=_MESH)

        mine = [pltpu.make_async_copy(x_refs[t], rows(t, *me), local_sems.at[t]) for t in range(nt)]
        for cp in mine:
            cp.start()
        first = []
        for j, chip in enumerate(chips):
            first += [copy(t, 1 + j, me, (*chip, c), src=x_refs[t]) for t in range(nt)]
        first += [copy(t, 0, me, sibling, src=x_refs[t]) for t in range(nt)]
        for cp in first:
            cp.start()
        passed = []
        for j, chip in enumerate(chips):
            for t in range(nt):
                copy(t, 1 + j, (*chip, c), me).wait_recv()
                passed.append(copy(t, 4 + j, (*chip, c), sibling))
                passed[-1].start()
        for t in range(nt):
            copy(t, 0, sibling, me).wait_recv()
            for j, chip in enumerate(chips):
                copy(t, 4 + j, (*chip, 1 - c), me).wait_recv()
        for cp in first + passed:
            cp.wait_send()
        for cp in mine:
            cp.wait()

    space = pl.BlockSpec(memory_space=pltpu.VMEM) if in_vmem else _ANY
    return pl.pallas_call(
        body, name=name, out_shape=[jax.ShapeDtypeStruct((N_DEV * s.shape[0], s.shape[1]), s.dtype) for s in shards],
        in_specs=[space] * nt, out_specs=[space] * nt,
        scratch_shapes=[pltpu.SemaphoreType.DMA((7 * nt,)), pltpu.SemaphoreType.DMA((7 * nt,)), pltpu.SemaphoreType.DMA((nt,))],
    )(*shards)


def pair_exchange(name, gs):
    nt = len(gs)

    def body(*refs):
        g_refs, out_refs = refs[:nt], refs[nt:2 * nt]
        send_sems, recv_sems = refs[2 * nt:]
        x, y, c = _place()
        copies = [pltpu.make_async_remote_copy(
            src_ref=g_refs[t].at[2 * k + 1 - c], dst_ref=out_refs[t].at[k], send_sem=send_sems.at[4 * t + k],
            recv_sem=recv_sems.at[4 * t + k], device_id=(x, y, 1 - c), device_id_type=_MESH) for t in range(nt) for k in range(4)]
        for cp in copies:
            cp.start()
        for cp in copies:
            cp.wait()

    return pl.pallas_call(
        body, name=name, out_shape=[jax.ShapeDtypeStruct((4,) + g.shape[1:], g.dtype) for g in gs],
        in_specs=[_ANY] * nt, out_specs=[_ANY] * nt,
        scratch_shapes=[pltpu.SemaphoreType.DMA((4 * nt,)), pltpu.SemaphoreType.DMA((4 * nt,))],
    )(*gs)


_HBM = pl.BlockSpec(memory_space=pltpu.HBM)
_SEM = pl.BlockSpec(memory_space=pltpu.SEMAPHORE)
_EFFECT = pltpu.SideEffectType.DATAFLOW_SIDE_EFFECTING


def _in_hbm(a):
    return pltpu.with_memory_space_constraint(a, pltpu.HBM)


def _gather_first_copies(x_refs, land_refs, send_sems, recv_sems, local_sems):
    nt = len(x_refs)
    x, y, c = _place()
    me = (x, y, c)

    def rows(t, px, py, pc):
        m_per = x_refs[t].shape[0]
        return land_refs[t].at[pl.ds((4 * px + 2 * py + pc) * m_per, m_per), :]

    local = [pltpu.make_async_copy(x_refs[t], rows(t, *me), local_sems.at[t]) for t in range(nt)]
    remote = []
    peers = [(_flip(x, fx), _flip(y, fy), c) for fx, fy in _CHIP_FLIPS] + [(x, y, 1 - c)]
    for k, peer in enumerate(peers):
        remote += [(pltpu.make_async_remote_copy(
            src_ref=x_refs[t], dst_ref=rows(t, *me), send_sem=send_sems.at[4 * t + k], recv_sem=recv_sems.at[4 * t + k],
            device_id=peer, device_id_type=_MESH), rows(t, *peer), 4 * t + k) for t in range(nt)]
    return local, remote


def gather_start(name, shards, after):
    nt = len(shards)

    def body(*refs):
        x_refs, land_refs = refs[:nt], refs[nt:2 * nt]
        send_sems, recv_sems, local_sems = refs[2 * nt + 1:2 * nt + 4]
        token = refs[-1]
        local, remote = _gather_first_copies(x_refs, land_refs, send_sems, recv_sems, local_sems)
        for cp in local:
            cp.start()
        for cp, _, _ in remote:
            cp.start()
        token[...] = jnp.zeros_like(token)

    lands = [lax.empty((N_DEV * s.shape[0], s.shape[1]), s.dtype) for s in shards]
    outs = pl.pallas_call(
        body, name=name,
        out_shape=[pltpu.SemaphoreType.DMA((4 * nt,)), pltpu.SemaphoreType.DMA((4 * nt,)), pltpu.SemaphoreType.DMA((nt,))]
        + [pltpu.HBM(s.shape, s.dtype) for s in shards] + [pltpu.HBM(a.shape, a.dtype) for a in lands]
        + [jax.ShapeDtypeStruct((8, LANE), F32)],
        in_specs=[_HBM] * (2 * nt) + [_ANY], out_specs=[_SEM] * 3 + [_HBM] * (2 * nt) + [pl.BlockSpec(memory_space=pltpu.VMEM)],
        input_output_aliases={i: 3 + i for i in range(2 * nt)},
        compiler_params=pltpu.CompilerParams(has_side_effects=_EFFECT),
    )(*[_in_hbm(s) for s in shards], *[_in_hbm(a) for a in lands], after)
    return outs[:3], outs[3:3 + nt], outs[3 + nt:3 + 2 * nt], outs[-1]


def gather_wait(name, handle, after):
    sems, shards, lands, _ = handle
    nt = len(shards)

    def body(*refs):
        x_refs, land_refs = refs[:nt], refs[nt:2 * nt]
        send_sems, recv_sems, local_sems = refs[2 * nt:2 * nt + 3]
        local, remote = _gather_first_copies(x_refs, land_refs, send_sems, recv_sems, local_sems)
        for cp, landed, k in remote:
            cp.wait_send()
            pltpu.make_async_remote_copy(src_ref=landed, dst_ref=landed, send_sem=send_sems.at[k], recv_sem=recv_sems.at[k],
                                         device_id=_place(), device_id_type=_MESH).wait_recv()
        for cp in local:
            cp.wait()

    outs = pl.pallas_call(
        body, name=name, out_shape=[pltpu.HBM(s.shape, s.dtype) for s in shards] + [pltpu.HBM(a.shape, a.dtype) for a in lands],
        in_specs=[_HBM] * (2 * nt) + [_SEM] * 3 + [_ANY], out_specs=[_HBM] * (2 * nt),
        input_output_aliases={i: i for i in range(2 * nt)},
        compiler_params=pltpu.CompilerParams(has_side_effects=_EFFECT),
    )(*shards, *lands, *sems, after)
    return outs[nt:]


def gather_forward(name, lands):
    nt = len(lands)

    def body(*refs):
        land_refs = refs[:nt]
        send_sems, recv_sems = refs[2 * nt:]
        x, y, c = _place()
        copies = []
        for j, (fx, fy) in enumerate(_CHIP_FLIPS):
            px, py = _flip(x, fx), _flip(y, fy)
            for t in range(nt):
                m_per = land_refs[t].shape[0] // N_DEV
                mine = land_refs[t].at[pl.ds((4 * px + 2 * py + c) * m_per, m_per), :]
                theirs = land_refs[t].at[pl.ds((4 * px + 2 * py + 1 - c) * m_per, m_per), :]
                copies.append((pltpu.make_async_remote_copy(
                    src_ref=mine, dst_ref=mine, send_sem=send_sems.at[3 * t + j], recv_sem=recv_sems.at[3 * t + j],
                    device_id=(x, y, 1 - c), device_id_type=_MESH), theirs, 3 * t + j))
        for cp, _, _ in copies:
            cp.start()
        for cp, theirs, k in copies:
            cp.wait_send()
            pltpu.make_async_remote_copy(src_ref=theirs, dst_ref=theirs, send_sem=send_sems.at[k], recv_sem=recv_sems.at[k],
                                         device_id=(x, y, 1 - c), device_id_type=_MESH).wait_recv()

    return pl.pallas_call(
        body, name=name, out_shape=[jax.ShapeDtypeStruct(a.shape, a.dtype) for a in lands],
        in_specs=[_ANY] * nt, out_specs=[_ANY] * nt, input_output_aliases={i: i for i in range(nt)},
        scratch_shapes=[pltpu.SemaphoreType.DMA((3 * nt,)), pltpu.SemaphoreType.DMA((3 * nt,))],
    )(*lands)


def _chip_copies(p_refs, land_refs, send_sems, recv_sems):
    nt = len(p_refs)
    x, y, c = _place()
    copies = []
    for j, (fx, fy) in enumerate(_CHIP_FLIPS):
        px, py = _flip(x, fx), _flip(y, fy)
        copies += [(pltpu.make_async_remote_copy(
            src_ref=p_refs[t].at[2 * px + py], dst_ref=land_refs[t].at[j], send_sem=send_sems.at[3 * t + j],
            recv_sem=recv_sems.at[3 * t + j], device_id=(px, py, c), device_id_type=_MESH), land_refs[t].at[j], 3 * t + j)
            for t in range(nt)]
    return copies


def chip_exchange_start(name, parts):
    nt = len(parts)

    def body(*refs):
        p_refs, land_refs = refs[:nt], refs[nt:2 * nt]
        send_sems, recv_sems = refs[2 * nt:2 * nt + 2]
        token = refs[-1]
        for cp, _, _ in _chip_copies(p_refs, land_refs, send_sems, recv_sems):
            cp.start()
        token[...] = jnp.zeros_like(token)

    lands = [lax.empty((3,) + p.shape[1:], p.dtype) for p in parts]
    outs = pl.pallas_call(
        body, name=name,
        out_shape=[pltpu.SemaphoreType.DMA((3 * nt,)), pltpu.SemaphoreType.DMA((3 * nt,))]
        + [pltpu.HBM(p.shape, p.dtype) for p in parts] + [pltpu.HBM(a.shape, a.dtype) for a in lands]
        + [jax.ShapeDtypeStruct((8, LANE), F32)],
        in_specs=[_HBM] * (2 * nt), out_specs=[_SEM] * 2 + [_HBM] * (2 * nt) + [pl.BlockSpec(memory_space=pltpu.VMEM)],
        input_output_aliases={i: 2 + i for i in range(2 * nt)},
        compiler_params=pltpu.CompilerParams(has_side_effects=_EFFECT),
    )(*[_in_hbm(p) for p in parts], *[_in_hbm(a) for a in lands])
    return outs[:2], outs[2:2 + nt], outs[2 + nt:2 + 2 * nt], outs[-1]


def chip_exchange_wait(name, handle, after):
    sems, parts, lands, _ = handle
    nt = len(parts)

    def body(*refs):
        p_refs, land_refs = refs[:nt], refs[nt:2 * nt]
        send_sems, recv_sems = refs[2 * nt:2 * nt + 2]
        for cp, landed, k in _chip_copies(p_refs, land_refs, send_sems, recv_sems):
            cp.wait_send()
            pltpu.make_async_remote_copy(src_ref=landed, dst_ref=landed, send_sem=send_sems.at[k], recv_sem=recv_sems.at[k],
                                         device_id=_place(), device_id_type=_MESH).wait_recv()

    outs = pl.pallas_call(
        body, name=name, out_shape=[pltpu.HBM(p.shape, p.dtype) for p in parts] + [pltpu.HBM(a.shape, a.dtype) for a in lands],
        in_specs=[_HBM] * (2 * nt) + [_SEM] * 2 + [_ANY], out_specs=[_HBM] * (2 * nt),
        input_output_aliases={i: i for i in range(2 * nt)},
        compiler_params=pltpu.CompilerParams(has_side_effects=_EFFECT),
    )(*parts, *lands, *sems, after)
    return outs[:nt], outs[nt:]


def _row_tile(rows):
    return rows if rows <= 256 else 256


def pair_add(name, g, recv, core):
    _, R, C = g.shape
    tr = _row_tile(R)

    def body(core_ref, g_ref, r_ref, o_ref):
        o_ref[...] = (g_ref[...].astype(F32) + r_ref[...].astype(F32)).astype(o_ref.dtype)

    return pl.pallas_call(
        body, name=name, out_shape=jax.ShapeDtypeStruct((4, R, C), BF),
        grid_spec=pltpu.PrefetchScalarGridSpec(
            num_scalar_prefetch=1, grid=(4, R // tr),
            in_specs=[pl.BlockSpec((None, tr, C), lambda k, i, core_ref: (2 * k + core_ref[0], i, 0)),
                      pl.BlockSpec((None, tr, C), lambda k, i, core_ref: (k, i, 0))],
            out_specs=pl.BlockSpec((None, tr, C), lambda k, i, core_ref: (k, i, 0))),
        compiler_params=_cp("parallel", "parallel"),
    )(core, g, recv)


def chip_add(name, part, recv, chip):
    _, R, C = part.shape
    tr = _row_tile(R)

    def body(chip_ref, p_ref, r_ref, o_ref):
        o_ref[...] = ((p_ref[...].astype(F32) + r_ref[0].astype(F32)) + r_ref[1].astype(F32)) + r_ref[2].astype(F32)

    return pl.pallas_call(
        body, name=name, out_shape=jax.ShapeDtypeStruct((R, C), F32),
        grid_spec=pltpu.PrefetchScalarGridSpec(
            num_scalar_prefetch=1, grid=(R // tr,),
            in_specs=[pl.BlockSpec((None, tr, C), lambda i, chip_ref: (chip_ref[0], i, 0)),
                      pl.BlockSpec((3, tr, C), lambda i, chip_ref: (0, i, 0))],
            out_specs=pl.BlockSpec((tr, C), lambda i, chip_ref: (i, 0))),
        compiler_params=_cp("parallel"),
    )(chip, part, recv)


def reduce_scatter_start(tag, gs, core):
    from_pair = pair_exchange("pair_exchange_" + tag, gs)
    parts = [pair_add("pair_add_%s_%d" % (tag, t), g, r, core) for t, (g, r) in enumerate(zip(gs, from_pair))]
    return chip_exchange_start("chip_exchange_start_" + tag, parts)


def reduce_scatter_finish(tag, handle, chip, after):
    parts, landed = chip_exchange_wait("chip_exchange_wait_" + tag, handle, after)
    return [chip_add("chip_add_%s_%d" % (tag, t), p, r, chip) for t, (p, r) in enumerate(zip(parts, landed))]


def sum_devices(name, g):
    _, R, C = g.shape

    def body(g_ref, o_ref):
        acc = g_ref[0]
        for d in range(1, N_DEV):
            acc = acc + g_ref[d]
        o_ref[...] = acc

    return pl.pallas_call(body, name=name, out_shape=jax.ShapeDtypeStruct((R, C), F32),
                          compiler_params=pltpu.CompilerParams(vmem_limit_bytes=VMEM_LIMIT_BYTES))(g)


_WEIGHTS = ["ada_w", "ada_b", "ln1_g", "ln1_b", "ln2_g", "ln2_b", "ffn_w1", "ffn_w3", "ffn_w2", "mla_w_down", "mla_q_norm",
            "mla_w_uq", "mla_kv_norm", "mla_w_uk", "mla_w_uv", "mla_w_o", "fox_w_in", "fox_b_f", "fox_w_o"]
_SMALL = ["ada_b", "ln1_g", "ln1_b", "ln2_g", "ln2_b", "mla_q_norm", "mla_kv_norm", "fox_b_f"]


def _pad_cols(a, width):
    return jnp.pad(a, ((0, 0), (0, width - a.shape[1])))


def _small_pack(parts, cols):
    rows = [jnp.pad(p.reshape(-1), (0, _round_up(p.size, cols) - p.size)).reshape(-1, cols) for p in parts]
    out = jnp.concatenate(rows, axis=0)
    return jnp.pad(out, ((0, _round_up(out.shape[0], 8) - out.shape[0]), (0, 0)))


def kernel(x, c, positions, ada_w, ada_b, ln1_g, ln1_b, ln2_g, ln2_b, ffn_w1, ffn_w3, ffn_w2, mla_w_down, mla_q_norm, mla_w_uq, mla_kv_norm, mla_w_uk, mla_w_uv, mla_w_o, fox_w_in, fox_b_f, fox_w_o, loss_target, m_ada_w, m_ada_b, m_ln1_g, m_ln1_b, m_ln2_g, m_ln2_b, m_ffn_w1, m_ffn_w3, m_ffn_w2, m_mla_w_down, m_mla_q_norm, m_mla_w_uq, m_mla_kv_norm, m_mla_w_uk, m_mla_w_uv, m_mla_w_o, m_fox_w_in, m_fox_b_f, m_fox_w_o, v_ada_w, v_ada_b, v_ln1_g, v_ln1_b, v_ln2_g, v_ln2_b, v_ffn_w1, v_ffn_w3, v_ffn_w2, v_mla_w_down, v_mla_q_norm, v_mla_w_uq, v_mla_kv_norm, v_mla_w_uk, v_mla_w_uv, v_mla_w_o, v_fox_w_in, v_fox_b_f, v_fox_w_o):
    env = dict(locals())
    W = {n: env[n] for n in _WEIGHTS}
    M = {n: env["m_" + n] for n in _WEIGHTS}
    V = {n: env["v_" + n] for n in _WEIGHTS}

    T, D = x.shape[1], x.shape[2]
    L = ada_w.shape[0]
    QL, KVL = mla_q_norm.shape[1], mla_kv_norm.shape[1]
    HA = mla_w_uq.shape[2] * N_DEV // (NOPE + ROPE)
    HB = fox_b_f.shape[1]
    f_loc = ffn_w1.shape[2]
    FP = _round_up(f_loc, LANE)
    n_ada = ada_w.shape[2]
    n_in = fox_w_in.shape[2]
    n_in_pad = _round_up(n_in, LANE)
    k_loc = mla_w_down.shape[1]
    n_uq, n_uk = mla_w_uq.shape[2], mla_w_uk.shape[2]
    assert D == HB * HEAD == HA * HEAD and n_ada * N_DEV == 6 * D
    alpha = (2 * L) ** 0.25
    xi, yi, ci = _place()
    me = 4 * xi + 2 * yi + ci
    core = jnp.reshape(ci, (1,)).astype(jnp.int32)
    chip = jnp.reshape(2 * xi + yi, (1,)).astype(jnp.int32)

    x2 = x[0]
    tgt = loss_target[0]

    fox_cols = 3 * D + LANE

    def bf(a, pad_rows=0, pad_cols=0):
        return jnp.pad(a.astype(BF), ((0, pad_rows), (0, pad_cols)))

    groups = []
    for i in range(L):
        j = i // 2
        if i % 2 == 0:
            groups.append(("mla%d" % i, [bf(mla_w_down[j]), bf(mla_w_uq[j]), bf(mla_w_uk[j]), bf(mla_w_uv[j]), bf(mla_w_o[j])]))
        else:
            groups.append(("fox%d" % i, [bf(fox_w_in[j], 0, n_in_pad - n_in), bf(fox_w_o[j])]))
        groups.append(("ffn%d" % i, [bf(ffn_w1[i], 0, FP - f_loc), bf(ffn_w3[i], 0, FP - f_loc), bf(ffn_w2[i], FP - f_loc, 0)]))
    handles = {}
    tok = jnp.zeros((8, LANE), F32)
    for gname, shards in groups:
        handles[gname] = gather_start("gather_start_" + gname, shards, tok)
        tok = handles[gname][3]

    def gathered(gname, after):
        lands = gather_wait("gather_wait_" + gname, handles[gname], after)
        lands = gather_forward("gather_forward_" + gname, lands)
        return [a.reshape((N_DEV,) + s.shape) for a, s in zip(lands, dict(groups)[gname])]

    def cols_whole(g, keep=None):
        return jnp.moveaxis(g[:, :, :keep], 0, 1).reshape(g.shape[1], -1)

    c_all = all_gather("gather_c", [jnp.pad(c + tok[0, 0], ((0, 7), (0, 0)))], True)[0].reshape(N_DEV, 8, D)[:, 0]
    ada_b_cols = lax.dynamic_slice_in_dim(ada_b, me * n_ada, n_ada, axis=1)
    mod_part = ada_fwd("ada_fwd", c_all, ada_w, ada_b_cols)
    mod_all = all_gather("gather_mod", [mod_part.reshape(L * N_DEV, n_ada)], True)[0].reshape(N_DEV, L, N_DEV, n_ada)
    mod = lax.dynamic_index_in_dim(mod_all, me, axis=2, keepdims=False)
    mod = jnp.moveaxis(mod, 0, 1).reshape(L, 6, D)

    def modv(i, k):
        return mod[i, k][None, :]

    b_f = _pad_cols(fox_b_f, LANE)

    inv_freq = ROPE_THETA ** (-jnp.arange(0, ROPE, 2, dtype=F32) / ROPE)
    ang = jnp.repeat(positions[0].astype(F32)[:, None] * inv_freq, 2, axis=1)
    cos_k, sin_k = jnp.cos(ang), jnp.sin(ang)
    cos_q, sin_q = jnp.tile(cos_k, (1, HA)), jnp.tile(sin_k, (1, HA))
    rot_k, rot_q = _pair_rotation(ROPE), _pair_rotation(HA * ROPE)
    tk = min(ATT_TILE, T)
    mla_scale = (NOPE + ROPE) ** -0.5
    fox_scale = HEAD ** -0.5

    def modulate(tag, xin, sc, sh):
        return rowwise("modulate_" + tag, lambda xv, scv, shv: xv * (1.0 + scv) + shv, [xin], [sc, sh], [(D, BF)])[0]

    def post(tag, xin, y, g, gam, bet):
        def f(xv, yv, gv, gamv, betv):
            xhat, _ = _ln_stats(alpha, xv, yv, gv)
            return xhat * gamv + betv
        return rowwise("post_" + tag, f, [xin, y], [g, gam, bet], [(D, F32)])[0]

    def post_bwd(tag, dxo, xin, y, g, gam):
        def f(dv, xv, yv, gv, gamv):
            xhat, rstd = _ln_stats(alpha, xv, yv, gv)
            dxh = dv * gamv
            dz = rstd * (dxh - _mean1(dxh) - xhat * _mean1(dxh * xhat))
            return alpha * dz, (1.0 + gv) * dz, _sum0(dv * xhat), _sum0(dv), _sum0(dz * yv)
        return rowwise("post_bwd_" + tag, f, [dxo, xin, y], [g, gam], [(D, F32), (D, BF)], [D, D, D])

    def pre_bwd(tag, dxa, dh, xin, sc):
        def f(dxav, dhv, xv, scv):
            return dxav + dhv * (1.0 + scv), _sum0(dhv * xv), _sum0(dhv)
        return rowwise("pre_bwd_" + tag, f, [dxa, dh, xin], [sc], [(D, F32)], [D, D])

    saved = []
    xc = x2
    for i in range(L):
        j = i // 2
        tag = "l%d" % i
        s = {"x_a": xc}
        h = modulate(tag + "a", xc, modv(i, 1), modv(i, 0))
        s["h_a"] = h
        if i % 2 == 0:
            wdn, wuq, wuk, wuv, wo = gathered("mla%d" % i, xc)
            w_down, w_o = wdn.reshape(D, -1), wo.reshape(D, D)
            q3 = cols_whole(wuq).reshape(QL, HA, NOPE + ROPE)
            w_uq = jnp.concatenate([q3[..., :NOPE].reshape(QL, HA * NOPE), q3[..., NOPE:].reshape(QL, HA * ROPE)], axis=-1)
            w_ukv = jnp.concatenate([cols_whole(wuk), cols_whole(wuv)], axis=-1)
            s.update(w_down=w_down, w_o=w_o, w_uq=w_uq, w_ukv=w_ukv)
            lat = mm("mla_down_" + tag, h, w_down)

            def mid(latv, ck, sk, qg, kg, p):
                ql, kl, kp = latv[:, :QL], latv[:, QL:QL + KVL], latv[:, QL + KVL:]
                qn = ql * lax.rsqrt(_mean1(ql * ql) + RMS_EPS) * qg
                cn = kl * lax.rsqrt(_mean1(kl * kl) + RMS_EPS) * kg
                return qn, cn, kp * ck + _rot(kp, p) * sk

            qn, ckv, kpe = rowwise("mla_mid_" + tag, mid, [lat, cos_k, sin_k],
                                   [mla_q_norm[j][None], mla_kv_norm[j][None], rot_k], [(QL, BF), (KVL, BF), (ROPE, BF)])
            qf = mm("mla_uq_" + tag, qn, w_uq)

            def qrope(qv, cq, sq, p):
                qp = qv[:, HA * NOPE:]
                return (qp * cq + _rot(qp, p) * sq) * mla_scale

            qpe = rowwise("mla_qrope_" + tag, qrope, [qf, cos_q, sin_q], [rot_q], [(HA * ROPE, BF)])[0]
            qpe_h = jnp.moveaxis(qpe.reshape(T, HA, ROPE), 1, 0)
            kv = mm("mla_ukv_" + tag, ckv, w_ukv)
            o, lse = attn_fwd("mla_attn_" + tag, "mla", mla_scale, HA, T, qf, 0, kv, 0, kv, HA, qpe_h, kpe)
            y = mm("mla_o_" + tag, o, w_o)
            s.update(lat=lat, qn=qn, ckv=ckv, kpe=kpe, qf=qf, qpe_h=qpe_h, kv=kv, o=o, lse=lse)
        else:
            win, wo = gathered("fox%d" % i, xc)
            w_in, w_o = _pad_cols(cols_whole(win, keep=n_in), fox_cols), wo.reshape(D, D)
            s.update(w_in=w_in, w_o=w_o)
            proj = mm("fox_in_" + tag, h, w_in, bn=896)
            fz = proj[:, 3 * D:]
            cum = fox_cum("fox_cum_" + tag, fz, b_f[j][None])
            cum_h = cum[:, :HB].T
            o, lse = attn_fwd("fox_attn_" + tag, "fox", fox_scale, HB, T, proj, 0, proj, HB, proj, 2 * HB,
                              cum_h.reshape(HB, T // tk, 1, tk), cum_h.reshape(HB, T, 1))
            y = mm("fox_o_" + tag, o, w_o)
            s.update(proj=proj, fz=fz, cum_h=cum_h, o=o, lse=lse)
        s["y_a"] = y
        xc = post(tag + "a", xc, y, modv(i, 2), ln1_g[i][None], ln1_b[i][None])
        s["x_f"] = xc
        h = modulate(tag + "f", xc, modv(i, 4), modv(i, 3))
        w1g, w3g, w2g = [w[:, None] for w in gathered("ffn%d" % i, xc)]
        a, b, gte = ffn_up("ffn_up_" + tag, h, w1g, w3g, 0)
        y = ffn_down("ffn_down_" + tag, gte, w2g, 0)
        s.update(h_f=h, a=a, b=b, gte=gte, y_f=y, w1g=w1g, w3g=w3g, w2g=w2g)
        xc = post(tag + "f", xc, y, modv(i, 5), ln2_g[i][None], ln2_b[i][None])
        saved.append(s)

    def loss_fn(yv, tv):
        e = yv - tv
        return e * (1.0 / D), jnp.zeros((1, LANE), F32) + _sum0(jnp.sum(e * e, axis=1, keepdims=True)) * (0.5 / D)

    dx, loss_part = rowwise("loss", loss_fn, [xc, tgt], [], [(D, F32)], [LANE])

    G = {n: [None] * W[n].shape[0] for n in _WEIGHTS if n != "ada_w"}
    pending = []
    dmod = [[None] * 6 for _ in range(L)]
    for i in reversed(range(L)):
        j = i // 2
        tag = "l%d" % i
        s = saved[i]
        dxa, dy, G["ln2_g"][i], G["ln2_b"][i], dmod[i][5] = post_bwd(tag + "f", dx, s["x_f"], s["y_f"], modv(i, 5), ln2_g[i][None])
        g_w2 = ffn_out_grad("ffn_w2_grad_" + tag, s["gte"], dy)
        da, db = ffn_down_bwd("ffn_down_bwd_" + tag, dy, s["w2g"], s["a"], s["b"], 0)
        g_w1 = ffn_in_grad("ffn_w1_grad_" + tag, s["h_f"], da)
        g_w3 = ffn_in_grad("ffn_w3_grad_" + tag, s["h_f"], db)
        dh = ffn_dh("ffn_dh_" + tag, da, db, s["w1g"], s["w3g"], 0)
        pending.append(("ffn", i, reduce_scatter_start("ffn_" + tag, [g_w1, g_w3, g_w2], core)))
        dx, dmod[i][4], dmod[i][3] = pre_bwd(tag + "f", dxa, dh, s["x_f"], modv(i, 4) + pending[-1][2][3][0, 0])
        dxa, dy, G["ln1_g"][i], G["ln1_b"][i], dmod[i][2] = post_bwd(tag + "a", dx, s["x_a"], s["y_a"], modv(i, 2), ln1_g[i][None])
        if i % 2 == 0:
            g_o = mm("mla_o_grad_" + tag, s["o"], dy, ta=True, out_dtype=BF)
            do = mm("mla_do_" + tag, dy, s["w_o"], tb=True, out_dtype=BF)
            delta = attn_delta("mla_delta_" + tag, do, s["o"], HA, tk)
            dqn_, dkn_, dv_, dqpe_h, dkpe = attn_bwd("mla_attn_bwd_" + tag, "mla", mla_scale, HA, T, s["qf"], 0, s["kv"], 0,
                                                     s["kv"], HA, s["qpe_h"], s["kpe"], do, s["lse"], delta)
            dkv = jnp.concatenate([dkn_, dv_], axis=1)
            g_kv = mm("mla_ukv_grad_" + tag, s["ckv"], dkv, ta=True, out_dtype=BF)
            dckv = mm("mla_dckv_" + tag, dkv, s["w_ukv"], tb=True)

            def qrope_bwd(dv, cq, sq, p):
                dv = dv * mla_scale
                return dv * cq - _rot(dv * sq, p)

            dqpe = rowwise("mla_qrope_bwd_" + tag, qrope_bwd, [jnp.moveaxis(dqpe_h, 0, 1).reshape(T, HA * ROPE), cos_q, sin_q],
                           [rot_q], [(HA * ROPE, BF)])[0]
            dq = jnp.concatenate([dqn_, dqpe], axis=1)
            g_uq = mm("mla_uq_grad_" + tag, s["qn"], dq, ta=True, out_dtype=BF)
            g_uq = jnp.concatenate([g_uq[:, :HA * NOPE].reshape(QL, HA, NOPE), g_uq[:, HA * NOPE:].reshape(QL, HA, ROPE)],
                                   axis=-1).reshape(QL, N_DEV, n_uq)
            dqn = mm("mla_dqn_" + tag, dq, s["w_uq"], tb=True)

            def mid_bwd(latv, dqv, dcv, dkp, ck, sk, qg, kg, p):
                ql, kl = latv[:, :QL], latv[:, QL:QL + KVL]
                rq = lax.rsqrt(_mean1(ql * ql) + RMS_EPS)
                rk = lax.rsqrt(_mean1(kl * kl) + RMS_EPS)
                uq, uk = dqv * qg, dcv * kg
                dql = rq * uq - ql * (rq * rq * rq) * _mean1(uq * ql)
                dkl = rk * uk - kl * (rk * rk * rk) * _mean1(uk * kl)
                return dql, dkl, dkp * ck - _rot(dkp * sk, p), _sum0(dqv * ql * rq), _sum0(dcv * kl * rk)

            dql, dkl, dkp, G["mla_q_norm"][j], G["mla_kv_norm"][j] = rowwise(
                "mla_mid_bwd_" + tag, mid_bwd, [s["lat"], dqn, dckv, dkpe, cos_k, sin_k],
                [mla_q_norm[j][None], mla_kv_norm[j][None], rot_k], [(QL, BF), (KVL, BF), (ROPE, BF)], [QL, KVL])
            dlat = jnp.concatenate([dql, dkl, dkp], axis=1)
            g_dn = mm("mla_down_grad_" + tag, s["h_a"], dlat, ta=True, out_dtype=BF)
            dh = mm("mla_dh_" + tag, dlat, s["w_down"], tb=True)
            g_kv = jnp.moveaxis(g_kv.reshape(KVL, 2, N_DEV, n_uk), 2, 0).reshape(N_DEV, KVL, 2 * n_uk)
            pending.append(("mla", j, reduce_scatter_start("mla_" + tag, [
                g_dn.reshape(N_DEV, k_loc, -1), jnp.moveaxis(g_uq, 1, 0), g_kv, g_o.reshape(N_DEV, k_loc, D)], core)))
        else:
            g_o = mm("fox_o_grad_" + tag, s["o"], dy, ta=True, out_dtype=BF)
            do = mm("fox_do_" + tag, dy, s["w_o"], tb=True, out_dtype=BF)
            cum_h = s["cum_h"]
            delta = attn_delta("fox_delta_" + tag, do, s["o"], HB, tk)
            dq_, dk_, dv_, dcq, dck = attn_bwd("fox_attn_bwd_" + tag, "fox", fox_scale, HB, T, s["proj"], 0, s["proj"], HB,
                                               s["proj"], 2 * HB, cum_h.reshape(HB, T // tk, 1, tk), cum_h.reshape(HB, T, 1),
                                               do, s["lse"], delta)
            dcum = _pad_cols((dcq.reshape(HB, T) + dck.reshape(HB, T)).T, LANE)
            df, db_f = fox_cum_bwd("fox_cum_bwd_" + tag, s["fz"], b_f[j][None], dcum)
            G["fox_b_f"][j] = db_f[:, :HB]
            dproj = jnp.concatenate([dq_, dk_, dv_, df], axis=1)
            g_in = mm("fox_in_grad_" + tag, s["h_a"], dproj, ta=True, out_dtype=BF, bn=896)[:, :N_DEV * n_in]
            dh = mm("fox_dh_" + tag, dproj, s["w_in"], tb=True, bk=896)
            g_in = jnp.pad(jnp.moveaxis(g_in.reshape(D, N_DEV, n_in), 1, 0), ((0, 0), (0, 0), (0, n_in_pad - n_in)))
            pending.append(("fox", j, reduce_scatter_start("fox_" + tag, [g_in, g_o.reshape(N_DEV, k_loc, D)], core)))
        dx, dmod[i][1], dmod[i][0] = pre_bwd(tag + "a", dxa, dh, s["x_a"], modv(i, 1) + pending[-1][2][3][0, 0])

    for kind, idx, handle in pending:
        res = reduce_scatter_finish("%s_%d" % (kind, idx), handle, chip, dx)
        if kind == "ffn":
            G["ffn_w1"][idx], G["ffn_w3"][idx], G["ffn_w2"][idx] = res[0][:, :f_loc], res[1][:, :f_loc], res[2][:f_loc]
        elif kind == "mla":
            G["mla_w_down"][idx], G["mla_w_uq"][idx], G["mla_w_o"][idx] = res[0], res[1], res[3]
            G["mla_w_uk"][idx], G["mla_w_uv"][idx] = res[2][:, :n_uk], res[2][:, n_uk:]
        else:
            G["fox_w_in"][idx], G["fox_w_o"][idx] = res[0][:, :n_in], res[1]

    dmod_mine = jnp.concatenate([jnp.concatenate(r, axis=1) for r in dmod], axis=0)
    small_parts = [dmod_mine] + [jnp.concatenate(G[n], axis=0) for n in _SMALL[1:]] + [loss_part]
    small = _small_pack(small_parts, 512)
    small_all = all_gather("gather_small", [small], True)[0].reshape(N_DEV, -1, 512)
    small_sum = sum_devices("sum_small", small_all)
    sums, off = [], 0
    for p in small_parts:
        nrow = _round_up(p.size, 512) // 512
        sums.append(small_sum[off:off + nrow].reshape(-1)[:p.size].reshape(p.shape))
        off += nrow
    grads = {n: gsum for n, gsum in zip(_SMALL, sums[:-1])}
    loss = sums[-1][0, 0]
    nrow = L * 6 * D // 512
    dmod_all = small_all[:, :nrow].reshape(N_DEV, L, 6 * D)
    dmod_cols = jnp.moveaxis(lax.dynamic_slice_in_dim(dmod_all, me * n_ada, n_ada, axis=2), 0, 1)

    for n in ("ffn_w1", "ffn_w3", "ffn_w2", "mla_w_down", "mla_w_uq", "mla_w_uk", "mla_w_uv", "mla_w_o", "fox_w_in", "fox_w_o"):
        grads[n] = jnp.stack(G[n])

    delta, new_m, new_v = {}, {}, {}
    grads["ada_w"], delta["ada_w"], new_m["ada_w"], new_v["ada_w"] = ada_bwd_adamw(
        "ada_grad_adamw", c_all, dmod_cols, ada_w, m_ada_w, v_ada_w)
    for n in _WEIGHTS:
        if n != "ada_w":
            delta[n], new_m[n], new_v[n] = adamw("adamw_" + n, W[n], grads[n], M[n], V[n])

    return (loss, dx[None], *[grads[n] for n in _WEIGHTS], *[delta[n] for n in _WEIGHTS],
            *[new_m[n] for n in _WEIGHTS], *[new_v[n] for n in _WEIGHTS])
```

```python
import math

import numpy as np
import jax
import jax.numpy as jnp
from jax import lax
from jax.experimental import pallas as pl
from jax.experimental.pallas import tpu as pltpu

F32 = jnp.float32
BF = jnp.bfloat16
N_DEV = 8

CHUNK_SHIFT = 6
NOPE = 128
ROPE = 64
HEAD = 128
ROPE_THETA = 10000.0
LN_EPS = 1e-5
RMS_EPS = 1e-6
ADAM_LR = 0.001
ADAM_B1 = 0.9
ADAM_B2 = 0.999
ADAM_EPS = 1e-08
ADAM_WD = 0.01
ADAM_STEP = 10

VMEM_LIMIT_BYTES = 48 * 1024 * 1024
LANE = 128
NEG = -1e30


def _cp(*sem):
    return pltpu.CompilerParams(dimension_semantics=sem, vmem_limit_bytes=VMEM_LIMIT_BYTES)


def _pick(dim, pref):
    if dim <= pref:
        return dim
    for t in range(pref - pref % LANE, LANE - 1, -LANE):
        if dim % t == 0:
            return t
    return dim


def _round_up(v, m):
    return (v + m - 1) // m * m


def _sum0(v):
    return jnp.sum(v, axis=0, keepdims=True)


def _mean1(v):
    return jnp.mean(v, axis=-1, keepdims=True)


_NN = (((1,), (0,)), ((), ()))
_NT = (((1,), (1,)), ((), ()))
_TN = (((0,), (0,)), ((), ()))


def _mm_call(name, grid, ins, in_specs, dns, out_shape, out_spec, add=None, add_spec=None):
    npairs = len(dns)
    nk = grid[-1]
    has_add = add is not None

    def body(*refs):
        add_ref = refs[2 * npairs] if has_add else None
        o_ref = refs[2 * npairs + (1 if has_add else 0)]
        r = None
        for p in range(npairs):
            t = lax.dot_general(refs[2 * p][...].astype(BF), refs[2 * p + 1][...].astype(BF), dns[p], preferred_element_type=F32)
            r = t if r is None else r + t

        def fin(val):
            if has_add:
                val = val + add_ref[...]
            o_ref[...] = val.astype(o_ref.dtype)

        if nk == 1:
            fin(r)
        else:
            acc = refs[-1]
            k = pl.program_id(len(grid) - 1)

            @pl.when(k == 0)
            def _():
                acc[...] = r

            @pl.when(k > 0)
            def _():
                acc[...] += r

            @pl.when(k == nk - 1)
            def _():
                fin(acc[...])

    acc_shape = tuple(b for b in out_spec.block_shape if b is not None)
    args, specs = list(ins), list(in_specs)
    if has_add:
        args.append(add)
        specs.append(add_spec)
    return pl.pallas_call(
        body, name=name, grid=grid, in_specs=specs, out_specs=out_spec, out_shape=out_shape,
        scratch_shapes=[pltpu.VMEM(acc_shape, F32)] if nk > 1 else [],
        compiler_params=_cp(*(["parallel"] * (len(grid) - 1) + ["arbitrary"])),
    )(*args)


def mm(name, a, b, ta=False, tb=False, out_dtype=F32, add=None, bm=1024, bn=1024, bk=2048):
    M, K = (a.shape[1], a.shape[0]) if ta else a.shape
    N = b.shape[0] if tb else b.shape[1]
    bm, bn, bk = _pick(M, bm), _pick(N, bn), _pick(K, bk)
    a_spec = pl.BlockSpec((bk, bm), lambda i, j, k: (k, i)) if ta else pl.BlockSpec((bm, bk), lambda i, j, k: (i, k))
    b_spec = pl.BlockSpec((bn, bk), lambda i, j, k: (j, k)) if tb else pl.BlockSpec((bk, bn), lambda i, j, k: (k, j))
    dn = (((0 if ta else 1,), (1 if tb else 0,)), ((), ()))
    tile = pl.BlockSpec((bm, bn), lambda i, j, k: (i, j))
    return _mm_call(name, (M // bm, N // bn, K // bk), [a, b], [a_spec, b_spec], [dn],
                    jax.ShapeDtypeStruct((M, N), out_dtype), tile, add, tile)


def rowwise(name, fn, rows, vecs, out_rows, out_accs=(), tm=256):
    T = rows[0].shape[0]
    tm = min(tm, T)
    nr, nv, no = len(rows), len(vecs), len(out_rows)

    def body(*refs):
        outs = fn(*[r[...] for r in refs[:nr + nv]])
        if not isinstance(outs, (tuple, list)):
            outs = (outs,)
        o_refs = refs[nr + nv:nr + nv + no]
        a_refs = refs[nr + nv + no:]
        for r, o in zip(o_refs, outs[:no]):
            r[...] = o.astype(r.dtype)
        if a_refs:
            @pl.when(pl.program_id(0) == 0)
            def _():
                for r in a_refs:
                    r[...] = jnp.zeros_like(r)

            for r, o in zip(a_refs, outs[no:]):
                r[...] += o

    in_specs = [pl.BlockSpec((tm, r.shape[1]), lambda i: (i, 0)) for r in rows]
    in_specs += [pl.BlockSpec(v.shape, lambda i: (0, 0)) for v in vecs]
    out_specs = [pl.BlockSpec((tm, w), lambda i: (i, 0)) for w, _ in out_rows]
    out_specs += [pl.BlockSpec((1, w), lambda i: (0, 0)) for w in out_accs]
    out_shape = [jax.ShapeDtypeStruct((T, w), dt) for w, dt in out_rows]
    out_shape += [jax.ShapeDtypeStruct((1, w), F32) for w in out_accs]
    return pl.pallas_call(
        body, name=name, grid=(T // tm,), in_specs=in_specs, out_specs=out_specs, out_shape=out_shape,
        compiler_params=_cp("arbitrary"),
    )(*rows, *vecs)


def _rot(v, p):
    hi = v.astype(BF)
    lo = (v - hi.astype(F32)).astype(BF)
    return jnp.dot(hi, p, preferred_element_type=F32) + jnp.dot(lo, p, preferred_element_type=F32)


def _pair_rotation(width):
    p = np.zeros((width, width), np.float32)
    idx = np.arange(0, width, 2)
    p[idx + 1, idx] = -1.0
    p[idx, idx + 1] = 1.0
    return jnp.asarray(p, BF)


def _ln_stats(alpha, x, y, g):
    z = alpha * x + (1.0 + g) * y
    zc = z - _mean1(z)
    rstd = lax.rsqrt(_mean1(zc * zc) + LN_EPS)
    return zc * rstd, rstd


ATT_G_FWD = 4
ATT_G_BWD = 2
ATT_TILE = 256


def _visible(mode, ks, qs, t):
    krow = ks + lax.broadcasted_iota(jnp.int32, (t, t), 0)
    qcol = qs + lax.broadcasted_iota(jnp.int32, (t, t), 1)
    if mode == "mla":
        return lax.shift_right_logical(qcol, CHUNK_SHIFT) >= lax.shift_right_logical(krow, CHUNK_SHIFT)
    return qcol >= krow


def _scores(mode, k, q, e_k, e_q, ks, qs, t, masked):
    s = lax.dot_general(k, q, _NT, preferred_element_type=F32)
    if mode == "mla":
        s = s + lax.dot_general(e_k, e_q, _NT, preferred_element_type=F32)
    else:
        s = s + e_q - e_k
    if masked:
        s = jnp.where(_visible(mode, ks, qs, t), s, NEG)
    return s


def _attn_specs(mode, G, T, t, q_off, k_off, v_off):
    def col(off):
        return pl.BlockSpec((T, G * HEAD), lambda h: (0, off // G + h))

    stat = pl.BlockSpec((G, T // t, 1, t), lambda h: (h, 0, 0, 0))
    if mode == "mla":
        e_specs = [pl.BlockSpec((G, T, ROPE), lambda h: (h, 0, 0)), pl.BlockSpec((T, ROPE), lambda h: (0, 0))]
    else:
        e_specs = [stat, pl.BlockSpec((G, T, 1), lambda h: (h, 0, 0))]
    return [col(q_off), col(k_off), col(v_off)], e_specs, stat


def _head(ref, rows, g):
    return ref[rows, g * HEAD:(g + 1) * HEAD]


def attn_fwd(name, mode, scale, H, T, q_src, q_off, k_src, k_off, v_src, v_off, e_q, e_k):
    G, t = ATT_G_FWD, min(ATT_TILE, T)
    nq = T // t
    mla = mode == "mla"

    def body(q_ref, k_ref, v_ref, eq_ref, ek_ref, o_ref, lse_ref, acc_ref):
        def q_loop(qi, _):
            qs = pl.multiple_of(qi * t, t)
            qrows = pl.ds(qs, t)
            qh = [(_head(q_ref, qrows, g).astype(F32) * scale).astype(BF) for g in range(G)]
            eqh = [eq_ref[g, qrows, :] if mla else eq_ref[g, qi] for g in range(G)]
            acc_ref[...] = jnp.zeros_like(acc_ref)

            def step(j, carry, masked):
                ks = pl.multiple_of(j * t, t)
                krows = pl.ds(ks, t)
                out, ps, corrs = [], [], []
                ss = [_scores(mode, _head(k_ref, krows, g).astype(BF), qh[g], ek_ref[krows, :] if mla else ek_ref[g, krows, :],
                              eqh[g], ks, qs, t, masked) for g in range(G)]
                for g in range(G):
                    m, l = carry[g]
                    m_new = jnp.maximum(m, jnp.max(ss[g], axis=0, keepdims=True))
                    p = jnp.exp(ss[g] - m_new)
                    corr = jnp.exp(m - m_new)
                    out.append((m_new, corr * l + jnp.sum(p, axis=0, keepdims=True)))
                    ps.append(p.astype(BF))
                    corrs.append(corr)
                pvs = [lax.dot_general(_head(v_ref, krows, g).astype(BF), ps[g], _TN, preferred_element_type=F32) for g in range(G)]
                for g in range(G):
                    acc_ref[g] = corrs[g] * acc_ref[g] + pvs[g]
                return tuple(out)

            init = tuple((jnp.full((1, t), NEG, F32), jnp.zeros((1, t), F32)) for _ in range(G))
            carry = lax.fori_loop(0, qi, lambda j, cr: step(j, cr, False), init)
            carry = step(qi, carry, True)
            for g in range(G):
                m, l = carry[g]
                o_ref[qrows, g * HEAD:(g + 1) * HEAD] = (acc_ref[g] / l).T.astype(o_ref.dtype)
                lse_ref[g, qi] = m + jnp.log(l)
            return 0

        lax.fori_loop(0, nq, q_loop, 0)

    cols, e_specs, stat = _attn_specs(mode, G, T, t, q_off, k_off, v_off)
    return pl.pallas_call(
        body, name=name, grid=(H // G,), in_specs=cols + e_specs,
        out_specs=[pl.BlockSpec((T, G * HEAD), lambda h: (0, h)), stat],
        out_shape=[jax.ShapeDtypeStruct((T, H * HEAD), BF), jax.ShapeDtypeStruct((H, nq, 1, t), F32)],
        scratch_shapes=[pltpu.VMEM((G, HEAD, t), F32)],
        compiler_params=_cp("parallel"),
    )(q_src, k_src, v_src, e_q, e_k)


def attn_bwd(name, mode, scale, H, T, q_src, q_off, k_src, k_off, v_src, v_off, e_q, e_k, do, lse, delta):
    G, t = ATT_G_BWD, min(ATT_TILE, T)
    nq = T // t
    mla = mode == "mla"

    def body(q_ref, k_ref, v_ref, eq_ref, ek_ref, do_ref, lse_ref, dl_ref,
             dq_ref, dk_ref, dv_ref, deq_ref, dek_ref, dk_acc, dv_acc, dq_acc, deq_acc):
        dk_acc[...] = jnp.zeros_like(dk_acc)
        dv_acc[...] = jnp.zeros_like(dv_acc)
        if mla:
            @pl.when(pl.program_id(0) == 0)
            def _():
                dek_ref[...] = jnp.zeros_like(dek_ref)
        else:
            dek_ref[...] = jnp.zeros_like(dek_ref)

        def q_loop(qi, _):
            qs = pl.multiple_of(qi * t, t)
            qrows = pl.ds(qs, t)
            qh = [(_head(q_ref, qrows, g).astype(F32) * scale).astype(BF) for g in range(G)]
            eqh = [eq_ref[g, qrows, :] if mla else eq_ref[g, qi] for g in range(G)]
            doh = [_head(do_ref, qrows, g) for g in range(G)]
            lse_q = [lse_ref[g, qi] for g in range(G)]
            dl_q = [dl_ref[g, qi] for g in range(G)]
            dq_acc[...] = jnp.zeros_like(dq_acc)
            deq_acc[...] = jnp.zeros_like(deq_acc)

            def step(j, carry, masked):
                ks = pl.multiple_of(j * t, t)
                krows = pl.ds(ks, t)
                kh = [_head(k_ref, krows, g).astype(BF) for g in range(G)]
                ekh = [ek_ref[krows, :] if mla else ek_ref[g, krows, :] for g in range(G)]
                ss = [_scores(mode, kh[g], qh[g], ekh[g], eqh[g], ks, qs, t, masked) for g in range(G)]
                dps = [lax.dot_general(_head(v_ref, krows, g).astype(BF), doh[g], _NT, preferred_element_type=F32) for g in range(G)]
                pbs, dss, dsbs = [], [], []
                for g in range(G):
                    p = jnp.exp(ss[g] - lse_q[g])
                    ds = p * (dps[g] - dl_q[g])
                    pbs.append(p.astype(BF))
                    dss.append(ds)
                    dsbs.append(ds.astype(BF))
                for g in range(G):
                    dv_acc[g, krows, :] += jnp.dot(pbs[g], doh[g], preferred_element_type=F32)
                    dk_acc[g, krows, :] += jnp.dot(dsbs[g], qh[g], preferred_element_type=F32)
                    dq_acc[g] += lax.dot_general(dsbs[g], kh[g], _TN, preferred_element_type=F32)
                    if mla:
                        deq_acc[g] += lax.dot_general(dsbs[g], ekh[g], _TN, preferred_element_type=F32)
                        dek_ref[krows, :] += jnp.dot(dsbs[g], eqh[g], preferred_element_type=F32)
                    else:
                        deq_acc[g] += jnp.sum(dss[g], axis=0, keepdims=True)
                        dek_ref[g, krows, :] -= jnp.sum(dss[g], axis=1, keepdims=True)
                return carry

            lax.fori_loop(0, qi, lambda j, cr: step(j, cr, False), 0)
            step(qi, 0, True)
            for g in range(G):
                dq_ref[qrows, g * HEAD:(g + 1) * HEAD] = (dq_acc[g] * scale).astype(dq_ref.dtype)
                if mla:
                    deq_ref[g, qrows, :] = deq_acc[g]
                else:
                    deq_ref[g, qi] = deq_acc[g]
            return 0

        lax.fori_loop(0, nq, q_loop, 0)
        for g in range(G):
            dk_ref[:, g * HEAD:(g + 1) * HEAD] = dk_acc[g].astype(dk_ref.dtype)
            dv_ref[:, g * HEAD:(g + 1) * HEAD] = dv_acc[g].astype(dv_ref.dtype)

    cols, e_specs, stat = _attn_specs(mode, G, T, t, q_off, k_off, v_off)
    heads = pl.BlockSpec((T, G * HEAD), lambda h: (0, h))
    if mla:
        de_shapes = [jax.ShapeDtypeStruct((H, T, ROPE), F32), jax.ShapeDtypeStruct((T, ROPE), F32)]
        deq_scratch = pltpu.VMEM((G, t, ROPE), F32)
    else:
        de_shapes = [jax.ShapeDtypeStruct((H, nq, 1, t), F32), jax.ShapeDtypeStruct((H, T, 1), F32)]
        deq_scratch = pltpu.VMEM((G, 1, t), F32)
    return pl.pallas_call(
        body, name=name, grid=(H // G,),
        in_specs=cols + e_specs + [heads, stat, stat],
        out_specs=[heads, heads, heads] + e_specs,
        out_shape=[jax.ShapeDtypeStruct((T, H * HEAD), BF)] * 3 + de_shapes,
        scratch_shapes=[pltpu.VMEM((G, T, HEAD), F32), pltpu.VMEM((G, T, HEAD), F32), pltpu.VMEM((G, t, HEAD), F32), deq_scratch],
        compiler_params=_cp("arbitrary"),
    )(q_src, k_src, v_src, e_q, e_k, do, lse, delta)


def attn_delta(name, do, o, H, t):
    T, W = do.shape
    ind = np.zeros((W, LANE), np.float32)
    ind[np.arange(W), np.arange(W) // HEAD] = 1.0

    def f(dv, ov, e):
        return _rot(dv.astype(F32) * ov.astype(F32), e)

    d = rowwise(name, f, [do, o], [jnp.asarray(ind, BF)], [(LANE, F32)])[0]
    return d[:, :H].T.reshape(H, T // t, 1, t)


def fox_cum(name, f, b):
    T = f.shape[0]

    def body(f_ref, b_ref, cum_ref):
        z = f_ref[...] + b_ref[...]
        v = jnp.minimum(z, 0.0) - jnp.log(1.0 + jnp.exp(-jnp.abs(z)))
        row = lax.broadcasted_iota(jnp.int32, v.shape, 0)
        s = 1
        while s < T:
            v = v + jnp.where(row >= s, pltpu.roll(v, s, 0), 0.0)
            s *= 2
        cum_ref[...] = v

    return pl.pallas_call(body, name=name, out_shape=jax.ShapeDtypeStruct(f.shape, F32),
                          compiler_params=pltpu.CompilerParams(vmem_limit_bytes=VMEM_LIMIT_BYTES))(f, b)


def fox_cum_bwd(name, f, b, dcum):
    T = f.shape[0]

    def body(f_ref, b_ref, dc_ref, df_ref, db_ref):
        v = dc_ref[...]
        row = lax.broadcasted_iota(jnp.int32, v.shape, 0)
        s = 1
        while s < T:
            v = v + jnp.where(row < T - s, pltpu.roll(v, T - s, 0), 0.0)
            s *= 2
        z = f_ref[...] + b_ref[...]
        df = v / (1.0 + jnp.exp(z))
        df_ref[...] = df.astype(df_ref.dtype)
        db_ref[...] = _sum0(df)

    return pl.pallas_call(body, name=name,
                          out_shape=[jax.ShapeDtypeStruct(f.shape, BF), jax.ShapeDtypeStruct((1, f.shape[1]), F32)],
                          compiler_params=pltpu.CompilerParams(vmem_limit_bytes=VMEM_LIMIT_BYTES))(f, b, dcum)


def ffn_up(name, h, w1g, w3g, l):
    T, D = h.shape
    FP = w1g.shape[3]
    bm = _pick(T, 1024)

    def body(h_ref, w1_ref, w3_ref, a_ref, b_ref, g_ref):
        hv = h_ref[...]
        a = jnp.dot(hv, w1_ref[...], preferred_element_type=F32)
        b = jnp.dot(hv, w3_ref[...], preferred_element_type=F32)
        a_ref[...] = a.astype(BF)
        b_ref[...] = b.astype(BF)
        g_ref[...] = (a * (1.0 / (1.0 + jnp.exp(-a))) * b).astype(BF)

    w_spec = pl.BlockSpec((None, None, D, FP), lambda i, d: (d, l, 0, 0))
    tile = pl.BlockSpec((None, bm, FP), lambda i, d: (d, i, 0))
    return pl.pallas_call(
        body, name=name, grid=(T // bm, N_DEV), in_specs=[pl.BlockSpec((bm, D), lambda i, d: (i, 0)), w_spec, w_spec],
        out_specs=[tile, tile, tile], out_shape=[jax.ShapeDtypeStruct((N_DEV, T, FP), BF)] * 3,
        compiler_params=_cp("parallel", "parallel"),
    )(h, w1g, w3g)


def ffn_down(name, g, w2g, l):
    _, T, FP = g.shape
    D = w2g.shape[3]
    bm, bn = _pick(T, 1024), _pick(D, 1024)
    return _mm_call(
        name, (T // bm, D // bn, N_DEV), [g, w2g],
        [pl.BlockSpec((None, bm, FP), lambda i, j, d: (d, i, 0)), pl.BlockSpec((None, None, FP, bn), lambda i, j, d: (d, l, 0, j))],
        [_NN], jax.ShapeDtypeStruct((T, D), F32), pl.BlockSpec((bm, bn), lambda i, j, d: (i, j)))


def ffn_down_bwd(name, dy, w2g, a, b, l):
    T, D = dy.shape
    FP = w2g.shape[2]
    bm = _pick(T, 1024)

    def body(dy_ref, w2_ref, a_ref, b_ref, da_ref, db_ref):
        dg = lax.dot_general(dy_ref[...], w2_ref[...], _NT, preferred_element_type=F32)
        av = a_ref[...].astype(F32)
        bv = b_ref[...].astype(F32)
        sig = 1.0 / (1.0 + jnp.exp(-av))
        da_ref[...] = (dg * bv * sig * (1.0 + av * (1.0 - sig))).astype(BF)
        db_ref[...] = (dg * av * sig).astype(BF)

    tile = pl.BlockSpec((None, bm, FP), lambda i, d: (d, i, 0))
    return pl.pallas_call(
        body, name=name, grid=(T // bm, N_DEV),
        in_specs=[pl.BlockSpec((bm, D), lambda i, d: (i, 0)), pl.BlockSpec((None, None, FP, D), lambda i, d: (d, l, 0, 0)), tile, tile],
        out_specs=[tile, tile], out_shape=[jax.ShapeDtypeStruct((N_DEV, T, FP), BF)] * 2,
        compiler_params=_cp("parallel", "parallel"),
    )(dy, w2g, a, b)


def ffn_dh(name, da, db, w1g, w3g, l):
    _, T, FP = da.shape
    D = w1g.shape[2]
    bm, bn = _pick(T, 1024), _pick(D, 1024)
    act = pl.BlockSpec((None, bm, FP), lambda i, j, d: (d, i, 0))
    wgt = pl.BlockSpec((None, None, bn, FP), lambda i, j, d: (d, l, j, 0))
    return _mm_call(name, (T // bm, D // bn, N_DEV), [da, w1g, db, w3g], [act, wgt, act, wgt], [_NT, _NT],
                    jax.ShapeDtypeStruct((T, D), F32), pl.BlockSpec((bm, bn), lambda i, j, d: (i, j)))


def ffn_in_grad(name, h, dact):
    T, D = h.shape
    FP = dact.shape[2]
    bm = _pick(D, 1024)
    return _mm_call(
        name, (D // bm, N_DEV, 1), [h, dact],
        [pl.BlockSpec((T, bm), lambda i, d, k: (0, i)), pl.BlockSpec((None, T, FP), lambda i, d, k: (d, 0, 0))],
        [_TN], jax.ShapeDtypeStruct((N_DEV, D, FP), BF), pl.BlockSpec((None, bm, FP), lambda i, d, k: (d, i, 0)))


def ffn_out_grad(name, g, dy):
    _, T, FP = g.shape
    D = dy.shape[1]
    bn = _pick(D, 1024)
    return _mm_call(
        name, (N_DEV, D // bn, 1), [g, dy],
        [pl.BlockSpec((None, T, FP), lambda d, j, k: (d, 0, 0)), pl.BlockSpec((T, bn), lambda d, j, k: (0, j))],
        [_TN], jax.ShapeDtypeStruct((N_DEV, FP, D), BF), pl.BlockSpec((None, FP, bn), lambda d, j, k: (d, 0, j)))


def _adamw_math(w, g, m, v):
    m = ADAM_B1 * m + (1.0 - ADAM_B1) * g
    v = ADAM_B2 * v + (1.0 - ADAM_B2) * (g * g)
    m_hat = m / (1.0 - ADAM_B1 ** ADAM_STEP)
    v_hat = v / (1.0 - ADAM_B2 ** ADAM_STEP)
    delta = -ADAM_LR * (m_hat / (jnp.sqrt(v_hat) + ADAM_EPS) + ADAM_WD * w)
    return delta, m, v


def adamw(name, w, g, m, v):
    shape = w.shape
    n = shape[-1]
    rows = math.prod(shape[:-1])
    tr = rows if rows <= 512 else 256
    assert rows % tr == 0

    def body(w_ref, g_ref, m_ref, v_ref, d_ref, mo_ref, vo_ref):
        d, mn, vn = _adamw_math(w_ref[...], g_ref[...], m_ref[...], v_ref[...])
        d_ref[...] = d
        mo_ref[...] = mn
        vo_ref[...] = vn

    blk = pl.BlockSpec((tr, n), lambda i: (i, 0))
    outs = pl.pallas_call(
        body, name=name, grid=(rows // tr,), in_specs=[blk] * 4, out_specs=[blk] * 3,
        out_shape=[jax.ShapeDtypeStruct((rows, n), F32)] * 3, compiler_params=_cp("parallel"),
    )(*[t.reshape(rows, n) for t in (w, g, m, v)])
    return [t.reshape(shape) for t in outs]


def _silu(v):
    return v * (1.0 / (1.0 + jnp.exp(-v)))


def ada_fwd(name, c_all, ada_w, ada_b_cols):
    L, D, n = ada_w.shape
    bn = _pick(n, 512)

    def body(c_ref, w_ref, b_ref, o_ref):
        act = _silu(c_ref[...]).astype(BF)
        o_ref[...] = jnp.dot(act, w_ref[...].astype(BF), preferred_element_type=F32) + b_ref[...]

    return pl.pallas_call(
        body, name=name, grid=(L, n // bn),
        in_specs=[pl.BlockSpec((N_DEV, D), lambda l, j: (0, 0)), pl.BlockSpec((None, D, bn), lambda l, j: (l, 0, j)),
                  pl.BlockSpec((None, 1, bn), lambda l, j: (l, 0, j))],
        out_specs=pl.BlockSpec((None, N_DEV, bn), lambda l, j: (l, 0, j)),
        out_shape=jax.ShapeDtypeStruct((L, N_DEV, n), F32), compiler_params=_cp("parallel", "parallel"),
    )(c_all, ada_w, ada_b_cols.reshape(L, 1, n))


def ada_bwd_adamw(name, c_all, dmod_cols, w, m, v):
    L, D, n = w.shape
    tr = _pick(D, 256)

    def body(c_ref, dm_ref, w_ref, m_ref, v_ref, g_ref, d_ref, mo_ref, vo_ref):
        act = _silu(c_ref[...]).astype(BF)
        g = lax.dot_general(act, dm_ref[...].astype(BF), _TN, preferred_element_type=F32)
        d, mn, vn = _adamw_math(w_ref[...], g, m_ref[...], v_ref[...])
        g_ref[...] = g
        d_ref[...] = d
        mo_ref[...] = mn
        vo_ref[...] = vn

    blk = pl.BlockSpec((None, tr, n), lambda l, i: (l, i, 0))
    return pl.pallas_call(
        body, name=name, grid=(L, D // tr),
        in_specs=[pl.BlockSpec((N_DEV, tr), lambda l, i: (0, i)), pl.BlockSpec((None, N_DEV, n), lambda l, i: (l, 0, 0)),
                  blk, blk, blk],
        out_specs=[blk] * 4, out_shape=[jax.ShapeDtypeStruct((L, D, n), F32)] * 4,
        compiler_params=_cp("parallel", "parallel"),
    )(c_all, dmod_cols, w, m, v)


_MESH = pl.DeviceIdType.MESH
_ANY = pl.BlockSpec(memory_space=pl.ANY)
_CHIP_FLIPS = ((1, 0), (0, 1), (1, 1))


def _place():
    return lax.axis_index("x"), lax.axis_index("y"), lax.axis_index("c")


def _flip(v, f):
    return 1 - v if f else v


def all_gather(name, shards, in_vmem=False):
    nt = len(shards)

    def body(*refs):
        x_refs, out_refs = refs[:nt], refs[nt:2 * nt]
        send_sems, recv_sems, local_sems = refs[2 * nt:]
        x, y, c = _place()
        me, sibling = (x, y, c), (x, y, 1 - c)
        chips = [(_flip(x, fx), _flip(y, fy)) for fx, fy in _CHIP_FLIPS]

        def rows(t, px, py, pc):
            m_per = shards[t].shape[0]
            return out_refs[t].at[pl.ds((4 * px + 2 * py + pc) * m_per, m_per), :]

        def copy(t, k, block, to, src=None):
            return pltpu.make_async_remote_copy(
                src_ref=rows(t, *block) if src is None else src, dst_ref=rows(t, *block),
                send_sem=send_sems.at[7 * t + k], recv_sem=recv_sems.at[7 * t + k], device_id=to, device_id_type=_MESH)

        mine = [pltpu.make_async_copy(x_refs[t], rows(t, *me), local_sems.at[t]) for t in range(nt)]
        for cp in mine:
            cp.start()
        first = []
        for j, chip in enumerate(chips):
            first += [copy(t, 1 + j, me, (*chip, c), src=x_refs[t]) for t in range(nt)]
        first += [copy(t, 0, me, sibling, src=x_refs[t]) for t in range(nt)]
        for cp in first:
            cp.start()
        passed = []
        for j, chip in enumerate(chips):
            for t in range(nt):
                copy(t, 1 + j, (*chip, c), me).wait_recv()
                passed.append(copy(t, 4 + j, (*chip, c), sibling))
                passed[-1].start()
        for t in range(nt):
            copy(t, 0, sibling, me).wait_recv()
            for j, chip in enumerate(chips):
                copy(t, 4 + j, (*chip, 1 - c), me).wait_recv()
        for cp in first + passed:
            cp.wait_send()
        for cp in mine:
            cp.wait()

    space = pl.BlockSpec(memory_space=pltpu.VMEM) if in_vmem else _ANY
    return pl.pallas_call(
        body, name=name, out_shape=[jax.ShapeDtypeStruct((N_DEV * s.shape[0], s.shape[1]), s.dtype) for s in shards],
        in_specs=[space] * nt, out_specs=[space] * nt,
        scratch_shapes=[pltpu.SemaphoreType.DMA((7 * nt,)), pltpu.SemaphoreType.DMA((7 * nt,)), pltpu.SemaphoreType.DMA((nt,))],
    )(*shards)


def pair_exchange(name, gs):
    nt = len(gs)

    def body(*refs):
        g_refs, out_refs = refs[:nt], refs[nt:2 * nt]
        send_sems, recv_sems = refs[2 * nt:]
        x, y, c = _place()
        copies = [pltpu.make_async_remote_copy(
            src_ref=g_refs[t].at[2 * k + 1 - c], dst_ref=out_refs[t].at[k], send_sem=send_sems.at[4 * t + k],
            recv_sem=recv_sems.at[4 * t + k], device_id=(x, y, 1 - c), device_id_type=_MESH) for t in range(nt) for k in range(4)]
        for cp in copies:
            cp.start()
        for cp in copies:
            cp.wait()

    return pl.pallas_call(
        body, name=name, out_shape=[jax.ShapeDtypeStruct((4,) + g.shape[1:], g.dtype) for g in gs],
        in_specs=[_ANY] * nt, out_specs=[_ANY] * nt,
        scratch_shapes=[pltpu.SemaphoreType.DMA((4 * nt,)), pltpu.SemaphoreType.DMA((4 * nt,))],
    )(*gs)


_HBM = pl.BlockSpec(memory_space=pltpu.HBM)
_SEM = pl.BlockSpec(memory_space=pltpu.SEMAPHORE)
_EFFECT = pltpu.SideEffectType.DATAFLOW_SIDE_EFFECTING


def _in_hbm(a):
    return pltpu.with_memory_space_constraint(a, pltpu.HBM)


def _gather_first_copies(x_refs, land_refs, send_sems, recv_sems, local_sems):
    nt = len(x_refs)
    x, y, c = _place()
    me = (x, y, c)

    def rows(t, px, py, pc):
        m_per = x_refs[t].shape[0]
        return land_refs[t].at[pl.ds((4 * px + 2 * py + pc) * m_per, m_per), :]

    local = [pltpu.make_async_copy(x_refs[t], rows(t, *me), local_sems.at[t]) for t in range(nt)]
    remote = []
    peers = [(_flip(x, fx), _flip(y, fy), c) for fx, fy in _CHIP_FLIPS] + [(x, y, 1 - c)]
    for k, peer in enumerate(peers):
        remote += [(pltpu.make_async_remote_copy(
            src_ref=x_refs[t], dst_ref=rows(t, *me), send_sem=send_sems.at[4 * t + k], recv_sem=recv_sems.at[4 * t + k],
            device_id=peer, device_id_type=_MESH), rows(t, *peer), 4 * t + k) for t in range(nt)]
    return local, remote


def gather_start(name, shards, after):
    nt = len(shards)

    def body(*refs):
        x_refs, land_refs = refs[:nt], refs[nt:2 * nt]
        send_sems, recv_sems, local_sems = refs[2 * nt + 1:2 * nt + 4]
        token = refs[-1]
        local, remote = _gather_first_copies(x_refs, land_refs, send_sems, recv_sems, local_sems)
        for cp in local:
            cp.start()
        for cp, _, _ in remote:
            cp.start()
        token[...] = jnp.zeros_like(token)

    lands = [lax.empty((N_DEV * s.shape[0], s.shape[1]), s.dtype) for s in shards]
    outs = pl.pallas_call(
        body, name=name,
        out_shape=[pltpu.SemaphoreType.DMA((4 * nt,)), pltpu.SemaphoreType.DMA((4 * nt,)), pltpu.SemaphoreType.DMA((nt,))]
        + [pltpu.HBM(s.shape, s.dtype) for s in shards] + [pltpu.HBM(a.shape, a.dtype) for a in lands]
        + [jax.ShapeDtypeStruct((8, LANE), F32)],
        in_specs=[_HBM] * (2 * nt) + [_ANY], out_specs=[_SEM] * 3 + [_HBM] * (2 * nt) + [pl.BlockSpec(memory_space=pltpu.VMEM)],
        input_output_aliases={i: 3 + i for i in range(2 * nt)},
        compiler_params=pltpu.CompilerParams(has_side_effects=_EFFECT),
    )(*[_in_hbm(s) for s in shards], *[_in_hbm(a) for a in lands], after)
    return outs[:3], outs[3:3 + nt], outs[3 + nt:3 + 2 * nt], outs[-1]


def gather_wait(name, handle, after):
    sems, shards, lands, _ = handle
    nt = len(shards)

    def body(*refs):
        x_refs, land_refs = refs[:nt], refs[nt:2 * nt]
        send_sems, recv_sems, local_sems = refs[2 * nt:2 * nt + 3]
        local, remote = _gather_first_copies(x_refs, land_refs, send_sems, recv_sems, local_sems)
        for cp, landed, k in remote:
            cp.wait_send()
            pltpu.make_async_remote_copy(src_ref=landed, dst_ref=landed, send_sem=send_sems.at[k], recv_sem=recv_sems.at[k],
                                         device_id=_place(), device_id_type=_MESH).wait_recv()
        for cp in local:
            cp.wait()

    outs = pl.pallas_call(
        body, name=name, out_shape=[pltpu.HBM(s.shape, s.dtype) for s in shards] + [pltpu.HBM(a.shape, a.dtype) for a in lands],
        in_specs=[_HBM] * (2 * nt) + [_SEM] * 3 + [_ANY], out_specs=[_HBM] * (2 * nt),
        input_output_aliases={i: i for i in range(2 * nt)},
        compiler_params=pltpu.CompilerParams(has_side_effects=_EFFECT),
    )(*shards, *lands, *sems, after)
    return outs[nt:]


def gather_forward(name, lands):
    nt = len(lands)

    def body(*refs):
        land_refs = refs[:nt]
        send_sems, recv_sems = refs[2 * nt:]
        x, y, c = _place()
        copies = []
        for j, (fx, fy) in enumerate(_CHIP_FLIPS):
            px, py = _flip(x, fx), _flip(y, fy)
            for t in range(nt):
                m_per = land_refs[t].shape[0] // N_DEV
                mine = land_refs[t].at[pl.ds((4 * px + 2 * py + c) * m_per, m_per), :]
                theirs = land_refs[t].at[pl.ds((4 * px + 2 * py + 1 - c) * m_per, m_per), :]
                copies.append((pltpu.make_async_remote_copy(
                    src_ref=mine, dst_ref=mine, send_sem=send_sems.at[3 * t + j], recv_sem=recv_sems.at[3 * t + j],
                    device_id=(x, y, 1 - c), device_id_type=_MESH), theirs, 3 * t + j))
        for cp, _, _ in copies:
            cp.start()
        for cp, theirs, k in copies:
            cp.wait_send()
            pltpu.make_async_remote_copy(src_ref=theirs, dst_ref=theirs, send_sem=send_sems.at[k], recv_sem=recv_sems.at[k],
                                         device_id=(x, y, 1 - c), device_id_type=_MESH).wait_recv()

    return pl.pallas_call(
        body, name=name, out_shape=[jax.ShapeDtypeStruct(a.shape, a.dtype) for a in lands],
        in_specs=[_ANY] * nt, out_specs=[_ANY] * nt, input_output_aliases={i: i for i in range(nt)},
        scratch_shapes=[pltpu.SemaphoreType.DMA((3 * nt,)), pltpu.SemaphoreType.DMA((3 * nt,))],
    )(*lands)


def _chip_copies(p_refs, land_refs, send_sems, recv_sems):
    nt = len(p_refs)
    x, y, c = _place()
    copies = []
    for j, (fx, fy) in enumerate(_CHIP_FLIPS):
        px, py = _flip(x, fx), _flip(y, fy)
        copies += [(pltpu.make_async_remote_copy(
            src_ref=p_refs[t].at[2 * px + py], dst_ref=land_refs[t].at[j], send_sem=send_sems.at[3 * t + j],
            recv_sem=recv_sems.at[3 * t + j], device_id=(px, py, c), device_id_type=_MESH), land_refs[t].at[j], 3 * t + j)
            for t in range(nt)]
    return copies


def chip_exchange_start(name, parts):
    nt = len(parts)

    def body(*refs):
        p_refs, land_refs = refs[:nt], refs[nt:2 * nt]
        send_sems, recv_sems = refs[2 * nt:2 * nt + 2]
        token = refs[-1]
        for cp, _, _ in _chip_copies(p_refs, land_refs, send_sems, recv_sems):
            cp.start()
        token[...] = jnp.zeros_like(token)

    lands = [lax.empty((3,) + p.shape[1:], p.dtype) for p in parts]
    outs = pl.pallas_call(
        body, name=name,
        out_shape=[pltpu.SemaphoreType.DMA((3 * nt,)), pltpu.SemaphoreType.DMA((3 * nt,))]
        + [pltpu.HBM(p.shape, p.dtype) for p in parts] + [pltpu.HBM(a.shape, a.dtype) for a in lands]
        + [jax.ShapeDtypeStruct((8, LANE), F32)],
        in_specs=[_HBM] * (2 * nt), out_specs=[_SEM] * 2 + [_HBM] * (2 * nt) + [pl.BlockSpec(memory_space=pltpu.VMEM)],
        input_output_aliases={i: 2 + i for i in range(2 * nt)},
        compiler_params=pltpu.CompilerParams(has_side_effects=_EFFECT),
    )(*[_in_hbm(p) for p in parts], *[_in_hbm(a) for a in lands])
    return outs[:2], outs[2:2 + nt], outs[2 + nt:2 + 2 * nt], outs[-1]


def chip_exchange_wait(name, handle, after):
    sems, parts, lands, _ = handle
    nt = len(parts)

    def body(*refs):
        p_refs, land_refs = refs[:nt], refs[nt:2 * nt]
        send_sems, recv_sems = refs[2 * nt:2 * nt + 2]
        for cp, landed, k in _chip_copies(p_refs, land_refs, send_sems, recv_sems):
            cp.wait_send()
            pltpu.make_async_remote_copy(src_ref=landed, dst_ref=landed, send_sem=send_sems.at[k], recv_sem=recv_sems.at[k],
                                         device_id=_place(), device_id_type=_MESH).wait_recv()

    outs = pl.pallas_call(
        body, name=name, out_shape=[pltpu.HBM(p.shape, p.dtype) for p in parts] + [pltpu.HBM(a.shape, a.dtype) for a in lands],
        in_specs=[_HBM] * (2 * nt) + [_SEM] * 2 + [_ANY], out_specs=[_HBM] * (2 * nt),
        input_output_aliases={i: i for i in range(2 * nt)},
        compiler_params=pltpu.CompilerParams(has_side_effects=_EFFECT),
    )(*parts, *lands, *sems, after)
    return outs[:nt], outs[nt:]


def _row_tile(rows):
    return rows if rows <= 256 else 256


def pair_add(name, g, recv, core):
    _, R, C = g.shape
    tr = _row_tile(R)

    def body(core_ref, g_ref, r_ref, o_ref):
        o_ref[...] = (g_ref[...].astype(F32) + r_ref[...].astype(F32)).astype(o_ref.dtype)

    return pl.pallas_call(
        body, name=name, out_shape=jax.ShapeDtypeStruct((4, R, C), BF),
        grid_spec=pltpu.PrefetchScalarGridSpec(
            num_scalar_prefetch=1, grid=(4, R // tr),
            in_specs=[pl.BlockSpec((None, tr, C), lambda k, i, core_ref: (2 * k + core_ref[0], i, 0)),
                      pl.BlockSpec((None, tr, C), lambda k, i, core_ref: (k, i, 0))],
            out_specs=pl.BlockSpec((None, tr, C), lambda k, i, core_ref: (k, i, 0))),
        compiler_params=_cp("parallel", "parallel"),
    )(core, g, recv)


def chip_add(name, part, recv, chip):
    _, R, C = part.shape
    tr = _row_tile(R)

    def body(chip_ref, p_ref, r_ref, o_ref):
        o_ref[...] = ((p_ref[...].astype(F32) + r_ref[0].astype(F32)) + r_ref[1].astype(F32)) + r_ref[2].astype(F32)

    return pl.pallas_call(
        body, name=name, out_shape=jax.ShapeDtypeStruct((R, C), F32),
        grid_spec=pltpu.PrefetchScalarGridSpec(
            num_scalar_prefetch=1, grid=(R // tr,),
            in_specs=[pl.BlockSpec((None, tr, C), lambda i, chip_ref: (chip_ref[0], i, 0)),
                      pl.BlockSpec((3, tr, C), lambda i, chip_ref: (0, i, 0))],
            out_specs=pl.BlockSpec((tr, C), lambda i, chip_ref: (i, 0))),
        compiler_params=_cp("parallel"),
    )(chip, part, recv)


def reduce_scatter_start(tag, gs, core):
    from_pair = pair_exchange("pair_exchange_" + tag, gs)
    parts = [pair_add("pair_add_%s_%d" % (tag, t), g, r, core) for t, (g, r) in enumerate(zip(gs, from_pair))]
    return chip_exchange_start("chip_exchange_start_" + tag, parts)


def reduce_scatter_finish(tag, handle, chip, after):
    parts, landed = chip_exchange_wait("chip_exchange_wait_" + tag, handle, after)
    return [chip_add("chip_add_%s_%d" % (tag, t), p, r, chip) for t, (p, r) in enumerate(zip(parts, landed))]


def sum_devices(name, g):
    _, R, C = g.shape

    def body(g_ref, o_ref):
        acc = g_ref[0]
        for d in range(1, N_DEV):
            acc = acc + g_ref[d]
        o_ref[...] = acc

    return pl.pallas_call(body, name=name, out_shape=jax.ShapeDtypeStruct((R, C), F32),
                          compiler_params=pltpu.CompilerParams(vmem_limit_bytes=VMEM_LIMIT_BYTES))(g)


_WEIGHTS = ["ada_w", "ada_b", "ln1_g", "ln1_b", "ln2_g", "ln2_b", "ffn_w1", "ffn_w3", "ffn_w2", "mla_w_down", "mla_q_norm",
            "mla_w_uq", "mla_kv_norm", "mla_w_uk", "mla_w_uv", "mla_w_o", "fox_w_in", "fox_b_f", "fox_w_o"]
_SMALL = ["ada_b", "ln1_g", "ln1_b", "ln2_g", "ln2_b", "mla_q_norm", "mla_kv_norm", "fox_b_f"]


def _pad_cols(a, width):
    return jnp.pad(a, ((0, 0), (0, width - a.shape[1])))


def _small_pack(parts, cols):
    rows = [jnp.pad(p.reshape(-1), (0, _round_up(p.size, cols) - p.size)).reshape(-1, cols) for p in parts]
    out = jnp.concatenate(rows, axis=0)
    return jnp.pad(out, ((0, _round_up(out.shape[0], 8) - out.shape[0]), (0, 0)))


def kernel(x, c, positions, ada_w, ada_b, ln1_g, ln1_b, ln2_g, ln2_b, ffn_w1, ffn_w3, ffn_w2, mla_w_down, mla_q_norm, mla_w_uq, mla_kv_norm, mla_w_uk, mla_w_uv, mla_w_o, fox_w_in, fox_b_f, fox_w_o, loss_target, m_ada_w, m_ada_b, m_ln1_g, m_ln1_b, m_ln2_g, m_ln2_b, m_ffn_w1, m_ffn_w3, m_ffn_w2, m_mla_w_down, m_mla_q_norm, m_mla_w_uq, m_mla_kv_norm, m_mla_w_uk, m_mla_w_uv, m_mla_w_o, m_fox_w_in, m_fox_b_f, m_fox_w_o, v_ada_w, v_ada_b, v_ln1_g, v_ln1_b, v_ln2_g, v_ln2_b, v_ffn_w1, v_ffn_w3, v_ffn_w2, v_mla_w_down, v_mla_q_norm, v_mla_w_uq, v_mla_kv_norm, v_mla_w_uk, v_mla_w_uv, v_mla_w_o, v_fox_w_in, v_fox_b_f, v_fox_w_o):
    env = dict(locals())
    W = {n: env[n] for n in _WEIGHTS}
    M = {n: env["m_" + n] for n in _WEIGHTS}
    V = {n: env["v_" + n] for n in _WEIGHTS}

    T, D = x.shape[1], x.shape[2]
    L = ada_w.shape[0]
    QL, KVL = mla_q_norm.shape[1], mla_kv_norm.shape[1]
    HA = mla_w_uq.shape[2] * N_DEV // (NOPE + ROPE)
    HB = fox_b_f.shape[1]
    f_loc = ffn_w1.shape[2]
    FP = _round_up(f_loc, LANE)
    n_ada = ada_w.shape[2]
    n_in = fox_w_in.shape[2]
    n_in_pad = _round_up(n_in, LANE)
    k_loc = mla_w_down.shape[1]
    n_uq, n_uk = mla_w_uq.shape[2], mla_w_uk.shape[2]
    assert D == HB * HEAD == HA * HEAD and n_ada * N_DEV == 6 * D
    alpha = (2 * L) ** 0.25
    xi, yi, ci = _place()
    me = 4 * xi + 2 * yi + ci
    core = jnp.reshape(ci, (1,)).astype(jnp.int32)
    chip = jnp.reshape(2 * xi + yi, (1,)).astype(jnp.int32)

    x2 = x[0]
    tgt = loss_target[0]

    c_all = all_gather("gather_c", [jnp.pad(c, ((0, 7), (0, 0)))], True)[0].reshape(N_DEV, 8, D)[:, 0]
    ada_b_cols = lax.dynamic_slice_in_dim(ada_b, me * n_ada, n_ada, axis=1)
    mod_part = ada_fwd("ada_fwd", c_all, ada_w, ada_b_cols)
    mod_all = all_gather("gather_mod", [mod_part.reshape(L * N_DEV, n_ada)], True)[0].reshape(N_DEV, L, N_DEV, n_ada)
    mod = lax.dynamic_index_in_dim(mod_all, me, axis=2, keepdims=False)
    mod = jnp.moveaxis(mod, 0, 1).reshape(L, 6, D)

    fox_cols = 3 * D + LANE

    def bf(a, pad_rows=0, pad_cols=0):
        return jnp.pad(a.astype(BF), ((0, pad_rows), (0, pad_cols)))

    groups = []
    for i in range(L):
        j = i // 2
        if i % 2 == 0:
            groups.append(("mla%d" % i, [bf(mla_w_down[j]), bf(mla_w_uq[j]), bf(mla_w_uk[j]), bf(mla_w_uv[j]), bf(mla_w_o[j])]))
        else:
            groups.append(("fox%d" % i, [bf(fox_w_in[j], 0, n_in_pad - n_in), bf(fox_w_o[j])]))
        groups.append(("ffn%d" % i, [bf(ffn_w1[i], 0, FP - f_loc), bf(ffn_w3[i], 0, FP - f_loc), bf(ffn_w2[i], FP - f_loc, 0)]))
    handles = {}
    tok = mod
    for gname, shards in groups:
        handles[gname] = gather_start("gather_start_" + gname, shards, tok)
        tok = handles[gname][3]
    mod = mod + tok[0, 0]

    def gathered(gname, after):
        lands = gather_wait("gather_wait_" + gname, handles[gname], after)
        lands = gather_forward("gather_forward_" + gname, lands)
        return [a.reshape((N_DEV,) + s.shape) for a, s in zip(lands, dict(groups)[gname])]

    def cols_whole(g, keep=None):
        return jnp.moveaxis(g[:, :, :keep], 0, 1).reshape(g.shape[1], -1)

    def modv(i, k):
        return mod[i, k][None, :]

    b_f = _pad_cols(fox_b_f, LANE)

    inv_freq = ROPE_THETA ** (-jnp.arange(0, ROPE, 2, dtype=F32) / ROPE)
    ang = jnp.repeat(positions[0].astype(F32)[:, None] * inv_freq, 2, axis=1)
    cos_k, sin_k = jnp.cos(ang), jnp.sin(ang)
    cos_q, sin_q = jnp.tile(cos_k, (1, HA)), jnp.tile(sin_k, (1, HA))
    rot_k, rot_q = _pair_rotation(ROPE), _pair_rotation(HA * ROPE)
    tk = min(ATT_TILE, T)
    mla_scale = (NOPE + ROPE) ** -0.5
    fox_scale = HEAD ** -0.5

    def modulate(tag, xin, sc, sh):
        return rowwise("modulate_" + tag, lambda xv, scv, shv: xv * (1.0 + scv) + shv, [xin], [sc, sh], [(D, BF)])[0]

    def post(tag, xin, y, g, gam, bet):
        def f(xv, yv, gv, gamv, betv):
            xhat, _ = _ln_stats(alpha, xv, yv, gv)
            return xhat * gamv + betv
        return rowwise("post_" + tag, f, [xin, y], [g, gam, bet], [(D, F32)])[0]

    def post_bwd(tag, dxo, xin, y, g, gam):
        def f(dv, xv, yv, gv, gamv):
            xhat, rstd = _ln_stats(alpha, xv, yv, gv)
            dxh = dv * gamv
            dz = rstd * (dxh - _mean1(dxh) - xhat * _mean1(dxh * xhat))
            return alpha * dz, (1.0 + gv) * dz, _sum0(dv * xhat), _sum0(dv), _sum0(dz * yv)
        return rowwise("post_bwd_" + tag, f, [dxo, xin, y], [g, gam], [(D, F32), (D, BF)], [D, D, D])

    def pre_bwd(tag, dxa, dh, xin, sc):
        def f(dxav, dhv, xv, scv):
            return dxav + dhv * (1.0 + scv), _sum0(dhv * xv), _sum0(dhv)
        return rowwise("pre_bwd_" + tag, f, [dxa, dh, xin], [sc], [(D, F32)], [D, D])

    saved = []
    xc = x2
    for i in range(L):
        j = i // 2
        tag = "l%d" % i
        s = {"x_a": xc}
        h = modulate(tag + "a", xc, modv(i, 1), modv(i, 0))
        s["h_a"] = h
        if i % 2 == 0:
            wdn, wuq, wuk, wuv, wo = gathered("mla%d" % i, xc)
            w_down, w_o = wdn.reshape(D, -1), wo.reshape(D, D)
            q3 = cols_whole(wuq).reshape(QL, HA, NOPE + ROPE)
            w_uq = jnp.concatenate([q3[..., :NOPE].reshape(QL, HA * NOPE), q3[..., NOPE:].reshape(QL, HA * ROPE)], axis=-1)
            w_ukv = jnp.concatenate([cols_whole(wuk), cols_whole(wuv)], axis=-1)
            s.update(w_down=w_down, w_o=w_o, w_uq=w_uq, w_ukv=w_ukv)
            lat = mm("mla_down_" + tag, h, w_down)

            def mid(latv, ck, sk, qg, kg, p):
                ql, kl, kp = latv[:, :QL], latv[:, QL:QL + KVL], latv[:, QL + KVL:]
                qn = ql * lax.rsqrt(_mean1(ql * ql) + RMS_EPS) * qg
                cn = kl * lax.rsqrt(_mean1(kl * kl) + RMS_EPS) * kg
                return qn, cn, kp * ck + _rot(kp, p) * sk

            qn, ckv, kpe = rowwise("mla_mid_" + tag, mid, [lat, cos_k, sin_k],
                                   [mla_q_norm[j][None], mla_kv_norm[j][None], rot_k], [(QL, BF), (KVL, BF), (ROPE, BF)])
            qf = mm("mla_uq_" + tag, qn, w_uq)

            def qrope(qv, cq, sq, p):
                qp = qv[:, HA * NOPE:]
                return (qp * cq + _rot(qp, p) * sq) * mla_scale

            qpe = rowwise("mla_qrope_" + tag, qrope, [qf, cos_q, sin_q], [rot_q], [(HA * ROPE, BF)])[0]
            qpe_h = jnp.moveaxis(qpe.reshape(T, HA, ROPE), 1, 0)
            kv = mm("mla_ukv_" + tag, ckv, w_ukv)
            o, lse = attn_fwd("mla_attn_" + tag, "mla", mla_scale, HA, T, qf, 0, kv, 0, kv, HA, qpe_h, kpe)
            y = mm("mla_o_" + tag, o, w_o)
            s.update(lat=lat, qn=qn, ckv=ckv, kpe=kpe, qf=qf, qpe_h=qpe_h, kv=kv, o=o, lse=lse)
        else:
            win, wo = gathered("fox%d" % i, xc)
            w_in, w_o = _pad_cols(cols_whole(win, keep=n_in), fox_cols), wo.reshape(D, D)
            s.update(w_in=w_in, w_o=w_o)
            proj = mm("fox_in_" + tag, h, w_in, bn=896)
            fz = proj[:, 3 * D:]
            cum = fox_cum("fox_cum_" + tag, fz, b_f[j][None])
            cum_h = cum[:, :HB].T
            o, lse = attn_fwd("fox_attn_" + tag, "fox", fox_scale, HB, T, proj, 0, proj, HB, proj, 2 * HB,
                              cum_h.reshape(HB, T // tk, 1, tk), cum_h.reshape(HB, T, 1))
            y = mm("fox_o_" + tag, o, w_o)
            s.update(proj=proj, fz=fz, cum_h=cum_h, o=o, lse=lse)
        s["y_a"] = y
        xc = post(tag + "a", xc, y, modv(i, 2), ln1_g[i][None], ln1_b[i][None])
        s["x_f"] = xc
        h = modulate(tag + "f", xc, modv(i, 4), modv(i, 3))
        w1g, w3g, w2g = [w[:, None] for w in gathered("ffn%d" % i, xc)]
        a, b, gte = ffn_up("ffn_up_" + tag, h, w1g, w3g, 0)
        y = ffn_down("ffn_down_" + tag, gte, w2g, 0)
        s.update(h_f=h, a=a, b=b, gte=gte, y_f=y, w1g=w1g, w3g=w3g, w2g=w2g)
        xc = post(tag + "f", xc, y, modv(i, 5), ln2_g[i][None], ln2_b[i][None])
        saved.append(s)

    def loss_fn(yv, tv):
        e = yv - tv
        return e * (1.0 / D), jnp.zeros((1, LANE), F32) + _sum0(jnp.sum(e * e, axis=1, keepdims=True)) * (0.5 / D)

    dx, loss_part = rowwise("loss", loss_fn, [xc, tgt], [], [(D, F32)], [LANE])

    G = {n: [None] * W[n].shape[0] for n in _WEIGHTS if n != "ada_w"}
    pending = []
    dmod = [[None] * 6 for _ in range(L)]
    for i in reversed(range(L)):
        j = i // 2
        tag = "l%d" % i
        s = saved[i]
        dxa, dy, G["ln2_g"][i], G["ln2_b"][i], dmod[i][5] = post_bwd(tag + "f", dx, s["x_f"], s["y_f"], modv(i, 5), ln2_g[i][None])
        g_w2 = ffn_out_grad("ffn_w2_grad_" + tag, s["gte"], dy)
        da, db = ffn_down_bwd("ffn_down_bwd_" + tag, dy, s["w2g"], s["a"], s["b"], 0)
        g_w1 = ffn_in_grad("ffn_w1_grad_" + tag, s["h_f"], da)
        g_w3 = ffn_in_grad("ffn_w3_grad_" + tag, s["h_f"], db)
        dh = ffn_dh("ffn_dh_" + tag, da, db, s["w1g"], s["w3g"], 0)
        pending.append(("ffn", i, reduce_scatter_start("ffn_" + tag, [g_w1, g_w3, g_w2], core)))
        dx, dmod[i][4], dmod[i][3] = pre_bwd(tag + "f", dxa, dh, s["x_f"], modv(i, 4) + pending[-1][2][3][0, 0])
        dxa, dy, G["ln1_g"][i], G["ln1_b"][i], dmod[i][2] = post_bwd(tag + "a", dx, s["x_a"], s["y_a"], modv(i, 2), ln1_g[i][None])
        if i % 2 == 0:
            g_o = mm("mla_o_grad_" + tag, s["o"], dy, ta=True, out_dtype=BF)
            do = mm("mla_do_" + tag, dy, s["w_o"], tb=True, out_dtype=BF)
            delta = attn_delta("mla_delta_" + tag, do, s["o"], HA, tk)
            dqn_, dkn_, dv_, dqpe_h, dkpe = attn_bwd("mla_attn_bwd_" + tag, "mla", mla_scale, HA, T, s["qf"], 0, s["kv"], 0,
                                                     s["kv"], HA, s["qpe_h"], s["kpe"], do, s["lse"], delta)
            dkv = jnp.concatenate([dkn_, dv_], axis=1)
            g_kv = mm("mla_ukv_grad_" + tag, s["ckv"], dkv, ta=True, out_dtype=BF)
            dckv = mm("mla_dckv_" + tag, dkv, s["w_ukv"], tb=True)

            def qrope_bwd(dv, cq, sq, p):
                dv = dv * mla_scale
                return dv * cq - _rot(dv * sq, p)

            dqpe = rowwise("mla_qrope_bwd_" + tag, qrope_bwd, [jnp.moveaxis(dqpe_h, 0, 1).reshape(T, HA * ROPE), cos_q, sin_q],
                           [rot_q], [(HA * ROPE, BF)])[0]
            dq = jnp.concatenate([dqn_, dqpe], axis=1)
            g_uq = mm("mla_uq_grad_" + tag, s["qn"], dq, ta=True, out_dtype=BF)
            g_uq = jnp.concatenate([g_uq[:, :HA * NOPE].reshape(QL, HA, NOPE), g_uq[:, HA * NOPE:].reshape(QL, HA, ROPE)],
                                   axis=-1).reshape(QL, N_DEV, n_uq)
            dqn = mm("mla_dqn_" + tag, dq, s["w_uq"], tb=True)

            def mid_bwd(latv, dqv, dcv, dkp, ck, sk, qg, kg, p):
                ql, kl = latv[:, :QL], latv[:, QL:QL + KVL]
                rq = lax.rsqrt(_mean1(ql * ql) + RMS_EPS)
                rk = lax.rsqrt(_mean1(kl * kl) + RMS_EPS)
                uq, uk = dqv * qg, dcv * kg
                dql = rq * uq - ql * (rq * rq * rq) * _mean1(uq * ql)
                dkl = rk * uk - kl * (rk * rk * rk) * _mean1(uk * kl)
                return dql, dkl, dkp * ck - _rot(dkp * sk, p), _sum0(dqv * ql * rq), _sum0(dcv * kl * rk)

            dql, dkl, dkp, G["mla_q_norm"][j], G["mla_kv_norm"][j] = rowwise(
                "mla_mid_bwd_" + tag, mid_bwd, [s["lat"], dqn, dckv, dkpe, cos_k, sin_k],
                [mla_q_norm[j][None], mla_kv_norm[j][None], rot_k], [(QL, BF), (KVL, BF), (ROPE, BF)], [QL, KVL])
            dlat = jnp.concatenate([dql, dkl, dkp], axis=1)
            g_dn = mm("mla_down_grad_" + tag, s["h_a"], dlat, ta=True, out_dtype=BF)
            dh = mm("mla_dh_" + tag, dlat, s["w_down"], tb=True)
            g_kv = jnp.moveaxis(g_kv.reshape(KVL, 2, N_DEV, n_uk), 2, 0).reshape(N_DEV, KVL, 2 * n_uk)
            pending.append(("mla", j, reduce_scatter_start("mla_" + tag, [
                g_dn.reshape(N_DEV, k_loc, -1), jnp.moveaxis(g_uq, 1, 0), g_kv, g_o.reshape(N_DEV, k_loc, D)], core)))
        else:
            g_o = mm("fox_o_grad_" + tag, s["o"], dy, ta=True, out_dtype=BF)
            do = mm("fox_do_" + tag, dy, s["w_o"], tb=True, out_dtype=BF)
            cum_h = s["cum_h"]
            delta = attn_delta("fox_delta_" + tag, do, s["o"], HB, tk)
            dq_, dk_, dv_, dcq, dck = attn_bwd("fox_attn_bwd_" + tag, "fox", fox_scale, HB, T, s["proj"], 0, s["proj"], HB,
                                               s["proj"], 2 * HB, cum_h.reshape(HB, T // tk, 1, tk), cum_h.reshape(HB, T, 1),
                                               do, s["lse"], delta)
            dcum = _pad_cols((dcq.reshape(HB, T) + dck.reshape(HB, T)).T, LANE)
            df, db_f = fox_cum_bwd("fox_cum_bwd_" + tag, s["fz"], b_f[j][None], dcum)
            G["fox_b_f"][j] = db_f[:, :HB]
            dproj = jnp.concatenate([dq_, dk_, dv_, df], axis=1)
            g_in = mm("fox_in_grad_" + tag, s["h_a"], dproj, ta=True, out_dtype=BF, bn=896)[:, :N_DEV * n_in]
            dh = mm("fox_dh_" + tag, dproj, s["w_in"], tb=True, bk=896)
            g_in = jnp.pad(jnp.moveaxis(g_in.reshape(D, N_DEV, n_in), 1, 0), ((0, 0), (0, 0), (0, n_in_pad - n_in)))
            pending.append(("fox", j, reduce_scatter_start("fox_" + tag, [g_in, g_o.reshape(N_DEV, k_loc, D)], core)))
        dx, dmod[i][1], dmod[i][0] = pre_bwd(tag + "a", dxa, dh, s["x_a"], modv(i, 1) + pending[-1][2][3][0, 0])

    for kind, idx, handle in pending:
        res = reduce_scatter_finish("%s_%d" % (kind, idx), handle, chip, dx)
        if kind == "ffn":
            G["ffn_w1"][idx], G["ffn_w3"][idx], G["ffn_w2"][idx] = res[0][:, :f_loc], res[1][:, :f_loc], res[2][:f_loc]
        elif kind == "mla":
            G["mla_w_down"][idx], G["mla_w_uq"][idx], G["mla_w_o"][idx] = res[0], res[1], res[3]
            G["mla_w_uk"][idx], G["mla_w_uv"][idx] = res[2][:, :n_uk], res[2][:, n_uk:]
        else:
            G["fox_w_in"][idx], G["fox_w_o"][idx] = res[0][:, :n_in], res[1]

    dmod_mine = jnp.concatenate([jnp.concatenate(r, axis=1) for r in dmod], axis=0)
    small_parts = [dmod_mine] + [jnp.concatenate(G[n], axis=0) for n in _SMALL[1:]] + [loss_part]
    small = _small_pack(small_parts, 512)
    small_all = all_gather("gather_small", [small], True)[0].reshape(N_DEV, -1, 512)
    small_sum = sum_devices("sum_small", small_all)
    sums, off = [], 0
    for p in small_parts:
        nrow = _round_up(p.size, 512) // 512
        sums.append(small_sum[off:off + nrow].reshape(-1)[:p.size].reshape(p.shape))
        off += nrow
    grads = {n: gsum for n, gsum in zip(_SMALL, sums[:-1])}
    loss = sums[-1][0, 0]
    nrow = L * 6 * D // 512
    dmod_all = small_all[:, :nrow].reshape(N_DEV, L, 6 * D)
    dmod_cols = jnp.moveaxis(lax.dynamic_slice_in_dim(dmod_all, me * n_ada, n_ada, axis=2), 0, 1)

    for n in ("ffn_w1", "ffn_w3", "ffn_w2", "mla_w_down", "mla_w_uq", "mla_w_uk", "mla_w_uv", "mla_w_o", "fox_w_in", "fox_w_o"):
        grads[n] = jnp.stack(G[n])

    delta, new_m, new_v = {}, {}, {}
    grads["ada_w"], delta["ada_w"], new_m["ada_w"], new_v["ada_w"] = ada_bwd_adamw(
        "ada_grad_adamw", c_all, dmod_cols, ada_w, m_ada_w, v_ada_w)
    for n in _WEIGHTS:
        if n != "ada_w":
            delta[n], new_m[n], new_v[n] = adamw("adamw_" + n, W[n], grads[n], M[n], V[n])

    return (loss, dx[None], *[grads[n] for n in _WEIGHTS], *[delta[n] for n in _WEIGHTS],
            *[new_m[n] for n in _WEIGHTS], *[new_v[n] for n in _WEIGHTS])
```

```python
import math

import numpy as np
import jax
import jax.numpy as jnp
from jax import lax
from jax.experimental import pallas as pl
from jax.experimental.pallas import tpu as pltpu

F32 = jnp.float32
BF = jnp.bfloat16
N_DEV = 8

CHUNK_SHIFT = 6
NOPE = 128
ROPE = 64
HEAD = 128
ROPE_THETA = 10000.0
LN_EPS = 1e-5
RMS_EPS = 1e-6
ADAM_LR = 0.001
ADAM_B1 = 0.9
ADAM_B2 = 0.999
ADAM_EPS = 1e-08
ADAM_WD = 0.01
ADAM_STEP = 10

VMEM_LIMIT_BYTES = 48 * 1024 * 1024
LANE = 128
NEG = -1e30


def _cp(*sem):
    return pltpu.CompilerParams(dimension_semantics=sem, vmem_limit_bytes=VMEM_LIMIT_BYTES)


def _pick(dim, pref):
    if dim <= pref:
        return dim
    for t in range(pref - pref % LANE, LANE - 1, -LANE):
        if dim % t == 0:
            return t
    return dim


def _round_up(v, m):
    return (v + m - 1) // m * m


def _sum0(v):
    return jnp.sum(v, axis=0, keepdims=True)


def _mean1(v):
    return jnp.mean(v, axis=-1, keepdims=True)


_NN = (((1,), (0,)), ((), ()))
_NT = (((1,), (1,)), ((), ()))
_TN = (((0,), (0,)), ((), ()))


def _mm_call(name, grid, ins, in_specs, dns, out_shape, out_spec, add=None, add_spec=None):
    npairs = len(dns)
    nk = grid[-1]
    has_add = add is not None

    def body(*refs):
        add_ref = refs[2 * npairs] if has_add else None
        o_ref = refs[2 * npairs + (1 if has_add else 0)]
        r = None
        for p in range(npairs):
            t = lax.dot_general(refs[2 * p][...].astype(BF), refs[2 * p + 1][...].astype(BF), dns[p], preferred_element_type=F32)
            r = t if r is None else r + t

        def fin(val):
            if has_add:
                val = val + add_ref[...]
            o_ref[...] = val.astype(o_ref.dtype)

        if nk == 1:
            fin(r)
        else:
            acc = refs[-1]
            k = pl.program_id(len(grid) - 1)

            @pl.when(k == 0)
            def _():
                acc[...] = r

            @pl.when(k > 0)
            def _():
                acc[...] += r

            @pl.when(k == nk - 1)
            def _():
                fin(acc[...])

    acc_shape = tuple(b for b in out_spec.block_shape if b is not None)
    args, specs = list(ins), list(in_specs)
    if has_add:
        args.append(add)
        specs.append(add_spec)
    return pl.pallas_call(
        body, name=name, grid=grid, in_specs=specs, out_specs=out_spec, out_shape=out_shape,
        scratch_shapes=[pltpu.VMEM(acc_shape, F32)] if nk > 1 else [],
        compiler_params=_cp(*(["parallel"] * (len(grid) - 1) + ["arbitrary"])),
    )(*args)


def mm(name, a, b, ta=False, tb=False, out_dtype=F32, add=None, bm=1024, bn=1024, bk=2048):
    M, K = (a.shape[1], a.shape[0]) if ta else a.shape
    N = b.shape[0] if tb else b.shape[1]
    bm, bn, bk = _pick(M, bm), _pick(N, bn), _pick(K, bk)
    a_spec = pl.BlockSpec((bk, bm), lambda i, j, k: (k, i)) if ta else pl.BlockSpec((bm, bk), lambda i, j, k: (i, k))
    b_spec = pl.BlockSpec((bn, bk), lambda i, j, k: (j, k)) if tb else pl.BlockSpec((bk, bn), lambda i, j, k: (k, j))
    dn = (((0 if ta else 1,), (1 if tb else 0,)), ((), ()))
    tile = pl.BlockSpec((bm, bn), lambda i, j, k: (i, j))
    return _mm_call(name, (M // bm, N // bn, K // bk), [a, b], [a_spec, b_spec], [dn],
                    jax.ShapeDtypeStruct((M, N), out_dtype), tile, add, tile)


def rowwise(name, fn, rows, vecs, out_rows, out_accs=(), tm=256):
    T = rows[0].shape[0]
    tm = min(tm, T)
    nr, nv, no = len(rows), len(vecs), len(out_rows)

    def body(*refs):
        outs = fn(*[r[...] for r in refs[:nr + nv]])
        if not isinstance(outs, (tuple, list)):
            outs = (outs,)
        o_refs = refs[nr + nv:nr + nv + no]
        a_refs = refs[nr + nv + no:]
        for r, o in zip(o_refs, outs[:no]):
            r[...] = o.astype(r.dtype)
        if a_refs:
            @pl.when(pl.program_id(0) == 0)
            def _():
                for r in a_refs:
                    r[...] = jnp.zeros_like(r)

            for r, o in zip(a_refs, outs[no:]):
                r[...] += o

    in_specs = [pl.BlockSpec((tm, r.shape[1]), lambda i: (i, 0)) for r in rows]
    in_specs += [pl.BlockSpec(v.shape, lambda i: (0, 0)) for v in vecs]
    out_specs = [pl.BlockSpec((tm, w), lambda i: (i, 0)) for w, _ in out_rows]
    out_specs += [pl.BlockSpec((1, w), lambda i: (0, 0)) for w in out_accs]
    out_shape = [jax.ShapeDtypeStruct((T, w), dt) for w, dt in out_rows]
    out_shape += [jax.ShapeDtypeStruct((1, w), F32) for w in out_accs]
    return pl.pallas_call(
        body, name=name, grid=(T // tm,), in_specs=in_specs, out_specs=out_specs, out_shape=out_shape,
        compiler_params=_cp("arbitrary"),
    )(*rows, *vecs)


def _rot(v, p):
    hi = v.astype(BF)
    lo = (v - hi.astype(F32)).astype(BF)
    return jnp.dot(hi, p, preferred_element_type=F32) + jnp.dot(lo, p, preferred_element_type=F32)


def _pair_rotation(width):
    p = np.zeros((width, width), np.float32)
    idx = np.arange(0, width, 2)
    p[idx + 1, idx] = -1.0
    p[idx, idx + 1] = 1.0
    return jnp.asarray(p, BF)


def _ln_stats(alpha, x, y, g):
    z = alpha * x + (1.0 + g) * y
    zc = z - _mean1(z)
    rstd = lax.rsqrt(_mean1(zc * zc) + LN_EPS)
    return zc * rstd, rstd


ATT_G_FWD = 4
ATT_G_BWD = 2
ATT_TILE = 256


def _visible(mode, ks, qs, t):
    krow = ks + lax.broadcasted_iota(jnp.int32, (t, t), 0)
    qcol = qs + lax.broadcasted_iota(jnp.int32, (t, t), 1)
    if mode == "mla":
        return lax.shift_right_logical(qcol, CHUNK_SHIFT) >= lax.shift_right_logical(krow, CHUNK_SHIFT)
    return qcol >= krow


def _scores(mode, k, q, e_k, e_q, ks, qs, t, masked):
    s = lax.dot_general(k, q, _NT, preferred_element_type=F32)
    if mode == "mla":
        s = s + lax.dot_general(e_k, e_q, _NT, preferred_element_type=F32)
    else:
        s = s + e_q - e_k
    if masked:
        s = jnp.where(_visible(mode, ks, qs, t), s, NEG)
    return s


def _attn_specs(mode, G, T, t, q_off, k_off, v_off):
    def col(off):
        return pl.BlockSpec((T, G * HEAD), lambda h: (0, off // G + h))

    stat = pl.BlockSpec((G, T // t, 1, t), lambda h: (h, 0, 0, 0))
    if mode == "mla":
        e_specs = [pl.BlockSpec((G, T, ROPE), lambda h: (h, 0, 0)), pl.BlockSpec((T, ROPE), lambda h: (0, 0))]
    else:
        e_specs = [stat, pl.BlockSpec((G, T, 1), lambda h: (h, 0, 0))]
    return [col(q_off), col(k_off), col(v_off)], e_specs, stat


def _head(ref, rows, g):
    return ref[rows, g * HEAD:(g + 1) * HEAD]


def attn_fwd(name, mode, scale, H, T, q_src, q_off, k_src, k_off, v_src, v_off, e_q, e_k):
    G, t = ATT_G_FWD, min(ATT_TILE, T)
    nq = T // t
    mla = mode == "mla"

    def body(q_ref, k_ref, v_ref, eq_ref, ek_ref, o_ref, lse_ref, acc_ref):
        def q_loop(qi, _):
            qs = pl.multiple_of(qi * t, t)
            qrows = pl.ds(qs, t)
            qh = [(_head(q_ref, qrows, g).astype(F32) * scale).astype(BF) for g in range(G)]
            eqh = [eq_ref[g, qrows, :] if mla else eq_ref[g, qi] for g in range(G)]
            acc_ref[...] = jnp.zeros_like(acc_ref)

            def step(j, carry, masked):
                ks = pl.multiple_of(j * t, t)
                krows = pl.ds(ks, t)
                out, ps, corrs = [], [], []
                ss = [_scores(mode, _head(k_ref, krows, g).astype(BF), qh[g], ek_ref[krows, :] if mla else ek_ref[g, krows, :],
                              eqh[g], ks, qs, t, masked) for g in range(G)]
                for g in range(G):
                    m, l = carry[g]
                    m_new = jnp.maximum(m, jnp.max(ss[g], axis=0, keepdims=True))
                    p = jnp.exp(ss[g] - m_new)
                    corr = jnp.exp(m - m_new)
                    out.append((m_new, corr * l + jnp.sum(p, axis=0, keepdims=True)))
                    ps.append(p.astype(BF))
                    corrs.append(corr)
                pvs = [lax.dot_general(_head(v_ref, krows, g).astype(BF), ps[g], _TN, preferred_element_type=F32) for g in range(G)]
                for g in range(G):
                    acc_ref[g] = corrs[g] * acc_ref[g] + pvs[g]
                return tuple(out)

            init = tuple((jnp.full((1, t), NEG, F32), jnp.zeros((1, t), F32)) for _ in range(G))
            carry = lax.fori_loop(0, qi, lambda j, cr: step(j, cr, False), init)
            carry = step(qi, carry, True)
            for g in range(G):
                m, l = carry[g]
                o_ref[qrows, g * HEAD:(g + 1) * HEAD] = (acc_ref[g] / l).T.astype(o_ref.dtype)
                lse_ref[g, qi] = m + jnp.log(l)
            return 0

        lax.fori_loop(0, nq, q_loop, 0)

    cols, e_specs, stat = _attn_specs(mode, G, T, t, q_off, k_off, v_off)
    return pl.pallas_call(
        body, name=name, grid=(H // G,), in_specs=cols + e_specs,
        out_specs=[pl.BlockSpec((T, G * HEAD), lambda h: (0, h)), stat],
        out_shape=[jax.ShapeDtypeStruct((T, H * HEAD), BF), jax.ShapeDtypeStruct((H, nq, 1, t), F32)],
        scratch_shapes=[pltpu.VMEM((G, HEAD, t), F32)],
        compiler_params=_cp("parallel"),
    )(q_src, k_src, v_src, e_q, e_k)


def attn_bwd(name, mode, scale, H, T, q_src, q_off, k_src, k_off, v_src, v_off, e_q, e_k, do, lse, delta):
    G, t = ATT_G_BWD, min(ATT_TILE, T)
    nq = T // t
    mla = mode == "mla"

    def body(q_ref, k_ref, v_ref, eq_ref, ek_ref, do_ref, lse_ref, dl_ref,
             dq_ref, dk_ref, dv_ref, deq_ref, dek_ref, dk_acc, dv_acc, dq_acc, deq_acc):
        dk_acc[...] = jnp.zeros_like(dk_acc)
        dv_acc[...] = jnp.zeros_like(dv_acc)
        if mla:
            @pl.when(pl.program_id(0) == 0)
            def _():
                dek_ref[...] = jnp.zeros_like(dek_ref)
        else:
            dek_ref[...] = jnp.zeros_like(dek_ref)

        def q_loop(qi, _):
            qs = pl.multiple_of(qi * t, t)
            qrows = pl.ds(qs, t)
            qh = [(_head(q_ref, qrows, g).astype(F32) * scale).astype(BF) for g in range(G)]
            eqh = [eq_ref[g, qrows, :] if mla else eq_ref[g, qi] for g in range(G)]
            doh = [_head(do_ref, qrows, g) for g in range(G)]
            lse_q = [lse_ref[g, qi] for g in range(G)]
            dl_q = [dl_ref[g, qi] for g in range(G)]
            dq_acc[...] = jnp.zeros_like(dq_acc)
            deq_acc[...] = jnp.zeros_like(deq_acc)

            def step(j, carry, masked):
                ks = pl.multiple_of(j * t, t)
                krows = pl.ds(ks, t)
                kh = [_head(k_ref, krows, g).astype(BF) for g in range(G)]
                ekh = [ek_ref[krows, :] if mla else ek_ref[g, krows, :] for g in range(G)]
                ss = [_scores(mode, kh[g], qh[g], ekh[g], eqh[g], ks, qs, t, masked) for g in range(G)]
                dps = [lax.dot_general(_head(v_ref, krows, g).astype(BF), doh[g], _NT, preferred_element_type=F32) for g in range(G)]
                pbs, dss, dsbs = [], [], []
                for g in range(G):
                    p = jnp.exp(ss[g] - lse_q[g])
                    ds = p * (dps[g] - dl_q[g])
                    pbs.append(p.astype(BF))
                    dss.append(ds)
                    dsbs.append(ds.astype(BF))
                for g in range(G):
                    dv_acc[g, krows, :] += jnp.dot(pbs[g], doh[g], preferred_element_type=F32)
                    dk_acc[g, krows, :] += jnp.dot(dsbs[g], qh[g], preferred_element_type=F32)
                    dq_acc[g] += lax.dot_general(dsbs[g], kh[g], _TN, preferred_element_type=F32)
                    if mla:
                        deq_acc[g] += lax.dot_general(dsbs[g], ekh[g], _TN, preferred_element_type=F32)
                        dek_ref[krows, :] += jnp.dot(dsbs[g], eqh[g], preferred_element_type=F32)
                    else:
                        deq_acc[g] += jnp.sum(dss[g], axis=0, keepdims=True)
                        dek_ref[g, krows, :] -= jnp.sum(dss[g], axis=1, keepdims=True)
                return carry

            lax.fori_loop(0, qi, lambda j, cr: step(j, cr, False), 0)
            step(qi, 0, True)
            for g in range(G):
                dq_ref[qrows, g * HEAD:(g + 1) * HEAD] = (dq_acc[g] * scale).astype(dq_ref.dtype)
                if mla:
                    deq_ref[g, qrows, :] = deq_acc[g]
                else:
                    deq_ref[g, qi] = deq_acc[g]
            return 0

        lax.fori_loop(0, nq, q_loop, 0)
        for g in range(G):
            dk_ref[:, g * HEAD:(g + 1) * HEAD] = dk_acc[g].astype(dk_ref.dtype)
            dv_ref[:, g * HEAD:(g + 1) * HEAD] = dv_acc[g].astype(dv_ref.dtype)

    cols, e_specs, stat = _attn_specs(mode, G, T, t, q_off, k_off, v_off)
    heads = pl.BlockSpec((T, G * HEAD), lambda h: (0, h))
    if mla:
        de_shapes = [jax.ShapeDtypeStruct((H, T, ROPE), F32), jax.ShapeDtypeStruct((T, ROPE), F32)]
        deq_scratch = pltpu.VMEM((G, t, ROPE), F32)
    else:
        de_shapes = [jax.ShapeDtypeStruct((H, nq, 1, t), F32), jax.ShapeDtypeStruct((H, T, 1), F32)]
        deq_scratch = pltpu.VMEM((G, 1, t), F32)
    return pl.pallas_call(
        body, name=name, grid=(H // G,),
        in_specs=cols + e_specs + [heads, stat, stat],
        out_specs=[heads, heads, heads] + e_specs,
        out_shape=[jax.ShapeDtypeStruct((T, H * HEAD), BF)] * 3 + de_shapes,
        scratch_shapes=[pltpu.VMEM((G, T, HEAD), F32), pltpu.VMEM((G, T, HEAD), F32), pltpu.VMEM((G, t, HEAD), F32), deq_scratch],
        compiler_params=_cp("arbitrary"),
    )(q_src, k_src, v_src, e_q, e_k, do, lse, delta)


def attn_delta(name, do, o, H, t):
    T, W = do.shape
    ind = np.zeros((W, LANE), np.float32)
    ind[np.arange(W), np.arange(W) // HEAD] = 1.0

    def f(dv, ov, e):
        return _rot(dv.astype(F32) * ov.astype(F32), e)

    d = rowwise(name, f, [do, o], [jnp.asarray(ind, BF)], [(LANE, F32)])[0]
    return d[:, :H].T.reshape(H, T // t, 1, t)


def fox_cum(name, f, b):
    T = f.shape[0]

    def body(f_ref, b_ref, cum_ref):
        z = f_ref[...] + b_ref[...]
        v = jnp.minimum(z, 0.0) - jnp.log(1.0 + jnp.exp(-jnp.abs(z)))
        row = lax.broadcasted_iota(jnp.int32, v.shape, 0)
        s = 1
        while s < T:
            v = v + jnp.where(row >= s, pltpu.roll(v, s, 0), 0.0)
            s *= 2
        cum_ref[...] = v

    return pl.pallas_call(body, name=name, out_shape=jax.ShapeDtypeStruct(f.shape, F32),
                          compiler_params=pltpu.CompilerParams(vmem_limit_bytes=VMEM_LIMIT_BYTES))(f, b)


def fox_cum_bwd(name, f, b, dcum):
    T = f.shape[0]

    def body(f_ref, b_ref, dc_ref, df_ref, db_ref):
        v = dc_ref[...]
        row = lax.broadcasted_iota(jnp.int32, v.shape, 0)
        s = 1
        while s < T:
            v = v + jnp.where(row < T - s, pltpu.roll(v, T - s, 0), 0.0)
            s *= 2
        z = f_ref[...] + b_ref[...]
        df = v / (1.0 + jnp.exp(z))
        df_ref[...] = df.astype(df_ref.dtype)
        db_ref[...] = _sum0(df)

    return pl.pallas_call(body, name=name,
                          out_shape=[jax.ShapeDtypeStruct(f.shape, BF), jax.ShapeDtypeStruct((1, f.shape[1]), F32)],
                          compiler_params=pltpu.CompilerParams(vmem_limit_bytes=VMEM_LIMIT_BYTES))(f, b, dcum)


def ffn_up(name, h, w1g, w3g, l):
    T, D = h.shape
    FP = w1g.shape[3]
    bm = _pick(T, 1024)

    def body(h_ref, w1_ref, w3_ref, a_ref, b_ref, g_ref):
        hv = h_ref[...]
        a = jnp.dot(hv, w1_ref[...], preferred_element_type=F32)
        b = jnp.dot(hv, w3_ref[...], preferred_element_type=F32)
        a_ref[...] = a.astype(BF)
        b_ref[...] = b.astype(BF)
        g_ref[...] = (a * (1.0 / (1.0 + jnp.exp(-a))) * b).astype(BF)

    w_spec = pl.BlockSpec((None, None, D, FP), lambda i, d: (d, l, 0, 0))
    tile = pl.BlockSpec((None, bm, FP), lambda i, d: (d, i, 0))
    return pl.pallas_call(
        body, name=name, grid=(T // bm, N_DEV), in_specs=[pl.BlockSpec((bm, D), lambda i, d: (i, 0)), w_spec, w_spec],
        out_specs=[tile, tile, tile], out_shape=[jax.ShapeDtypeStruct((N_DEV, T, FP), BF)] * 3,
        compiler_params=_cp("parallel", "parallel"),
    )(h, w1g, w3g)


def ffn_down(name, g, w2g, l):
    _, T, FP = g.shape
    D = w2g.shape[3]
    bm, bn = _pick(T, 1024), _pick(D, 1024)
    return _mm_call(
        name, (T // bm, D // bn, N_DEV), [g, w2g],
        [pl.BlockSpec((None, bm, FP), lambda i, j, d: (d, i, 0)), pl.BlockSpec((None, None, FP, bn), lambda i, j, d: (d, l, 0, j))],
        [_NN], jax.ShapeDtypeStruct((T, D), F32), pl.BlockSpec((bm, bn), lambda i, j, d: (i, j)))


def ffn_down_bwd(name, dy, w2g, a, b, l):
    T, D = dy.shape
    FP = w2g.shape[2]
    bm = _pick(T, 1024)

    def body(dy_ref, w2_ref, a_ref, b_ref, da_ref, db_ref):
        dg = lax.dot_general(dy_ref[...], w2_ref[...], _NT, preferred_element_type=F32)
        av = a_ref[...].astype(F32)
        bv = b_ref[...].astype(F32)
        sig = 1.0 / (1.0 + jnp.exp(-av))
        da_ref[...] = (dg * bv * sig * (1.0 + av * (1.0 - sig))).astype(BF)
        db_ref[...] = (dg * av * sig).astype(BF)

    tile = pl.BlockSpec((None, bm, FP), lambda i, d: (d, i, 0))
    return pl.pallas_call(
        body, name=name, grid=(T // bm, N_DEV),
        in_specs=[pl.BlockSpec((bm, D), lambda i, d: (i, 0)), pl.BlockSpec((None, None, FP, D), lambda i, d: (d, l, 0, 0)), tile, tile],
        out_specs=[tile, tile], out_shape=[jax.ShapeDtypeStruct((N_DEV, T, FP), BF)] * 2,
        compiler_params=_cp("parallel", "parallel"),
    )(dy, w2g, a, b)


def ffn_dh(name, da, db, w1g, w3g, l):
    _, T, FP = da.shape
    D = w1g.shape[2]
    bm, bn = _pick(T, 1024), _pick(D, 1024)
    act = pl.BlockSpec((None, bm, FP), lambda i, j, d: (d, i, 0))
    wgt = pl.BlockSpec((None, None, bn, FP), lambda i, j, d: (d, l, j, 0))
    return _mm_call(name, (T // bm, D // bn, N_DEV), [da, w1g, db, w3g], [act, wgt, act, wgt], [_NT, _NT],
                    jax.ShapeDtypeStruct((T, D), F32), pl.BlockSpec((bm, bn), lambda i, j, d: (i, j)))


def ffn_in_grad(name, h, dact):
    T, D = h.shape
    FP = dact.shape[2]
    bm = _pick(D, 1024)
    return _mm_call(
        name, (D // bm, N_DEV, 1), [h, dact],
        [pl.BlockSpec((T, bm), lambda i, d, k: (0, i)), pl.BlockSpec((None, T, FP), lambda i, d, k: (d, 0, 0))],
        [_TN], jax.ShapeDtypeStruct((N_DEV, D, FP), BF), pl.BlockSpec((None, bm, FP), lambda i, d, k: (d, i, 0)))


def ffn_out_grad(name, g, dy):
    _, T, FP = g.shape
    D = dy.shape[1]
    bn = _pick(D, 1024)
    return _mm_call(
        name, (N_DEV, D // bn, 1), [g, dy],
        [pl.BlockSpec((None, T, FP), lambda d, j, k: (d, 0, 0)), pl.BlockSpec((T, bn), lambda d, j, k: (0, j))],
        [_TN], jax.ShapeDtypeStruct((N_DEV, FP, D), BF), pl.BlockSpec((None, FP, bn), lambda d, j, k: (d, 0, j)))


def _adamw_math(w, g, m, v):
    m = ADAM_B1 * m + (1.0 - ADAM_B1) * g
    v = ADAM_B2 * v + (1.0 - ADAM_B2) * (g * g)
    m_hat = m / (1.0 - ADAM_B1 ** ADAM_STEP)
    v_hat = v / (1.0 - ADAM_B2 ** ADAM_STEP)
    delta = -ADAM_LR * (m_hat / (jnp.sqrt(v_hat) + ADAM_EPS) + ADAM_WD * w)
    return delta, m, v


def adamw(name, w, g, m, v):
    shape = w.shape
    n = shape[-1]
    rows = math.prod(shape[:-1])
    tr = rows if rows <= 512 else 256
    assert rows % tr == 0

    def body(w_ref, g_ref, m_ref, v_ref, d_ref, mo_ref, vo_ref):
        d, mn, vn = _adamw_math(w_ref[...], g_ref[...], m_ref[...], v_ref[...])
        d_ref[...] = d
        mo_ref[...] = mn
        vo_ref[...] = vn

    blk = pl.BlockSpec((tr, n), lambda i: (i, 0))
    outs = pl.pallas_call(
        body, name=name, grid=(rows // tr,), in_specs=[blk] * 4, out_specs=[blk] * 3,
        out_shape=[jax.ShapeDtypeStruct((rows, n), F32)] * 3, compiler_params=_cp("parallel"),
    )(*[t.reshape(rows, n) for t in (w, g, m, v)])
    return [t.reshape(shape) for t in outs]


def _silu(v):
    return v * (1.0 / (1.0 + jnp.exp(-v)))


def ada_fwd(name, c_all, ada_w, ada_b_cols):
    L, D, n = ada_w.shape
    bn = _pick(n, 512)

    def body(c_ref, w_ref, b_ref, o_ref):
        act = _silu(c_ref[...]).astype(BF)
        o_ref[...] = jnp.dot(act, w_ref[...].astype(BF), preferred_element_type=F32) + b_ref[...]

    return pl.pallas_call(
        body, name=name, grid=(L, n // bn),
        in_specs=[pl.BlockSpec((N_DEV, D), lambda l, j: (0, 0)), pl.BlockSpec((None, D, bn), lambda l, j: (l, 0, j)),
                  pl.BlockSpec((None, 1, bn), lambda l, j: (l, 0, j))],
        out_specs=pl.BlockSpec((None, N_DEV, bn), lambda l, j: (l, 0, j)),
        out_shape=jax.ShapeDtypeStruct((L, N_DEV, n), F32), compiler_params=_cp("parallel", "parallel"),
    )(c_all, ada_w, ada_b_cols.reshape(L, 1, n))


def ada_bwd_adamw(name, c_all, dmod_cols, w, m, v):
    L, D, n = w.shape
    tr = _pick(D, 256)

    def body(c_ref, dm_ref, w_ref, m_ref, v_ref, g_ref, d_ref, mo_ref, vo_ref):
        act = _silu(c_ref[...]).astype(BF)
        g = lax.dot_general(act, dm_ref[...].astype(BF), _TN, preferred_element_type=F32)
        d, mn, vn = _adamw_math(w_ref[...], g, m_ref[...], v_ref[...])
        g_ref[...] = g
        d_ref[...] = d
        mo_ref[...] = mn
        vo_ref[...] = vn

    blk = pl.BlockSpec((None, tr, n), lambda l, i: (l, i, 0))
    return pl.pallas_call(
        body, name=name, grid=(L, D // tr),
        in_specs=[pl.BlockSpec((N_DEV, tr), lambda l, i: (0, i)), pl.BlockSpec((None, N_DEV, n), lambda l, i: (l, 0, 0)),
                  blk, blk, blk],
        out_specs=[blk] * 4, out_shape=[jax.ShapeDtypeStruct((L, D, n), F32)] * 4,
        compiler_params=_cp("parallel", "parallel"),
    )(c_all, dmod_cols, w, m, v)


_MESH = pl.DeviceIdType.MESH
_ANY = pl.BlockSpec(memory_space=pl.ANY)
_CHIP_FLIPS = ((1, 0), (0, 1), (1, 1))


def _place():
    return lax.axis_index("x"), lax.axis_index("y"), lax.axis_index("c")


def _flip(v, f):
    return 1 - v if f else v


def all_gather(name, shards, in_vmem=False):
    nt = len(shards)

    def body(*refs):
        x_refs, out_refs = refs[:nt], refs[nt:2 * nt]
        send_sems, recv_sems, local_sems = refs[2 * nt:]
        x, y, c = _place()
        me, sibling = (x, y, c), (x, y, 1 - c)
        chips = [(_flip(x, fx), _flip(y, fy)) for fx, fy in _CHIP_FLIPS]

        def rows(t, px, py, pc):
            m_per = shards[t].shape[0]
            return out_refs[t].at[pl.ds((4 * px + 2 * py + pc) * m_per, m_per), :]

        def copy(t, k, block, to, src=None):
            return pltpu.make_async_remote_copy(
                src_ref=rows(t, *block) if src is None else src, dst_ref=rows(t, *block),
                send_sem=send_sems.at[7 * t + k], recv_sem=recv_sems.at[7 * t + k], device_id=to, device_id_type=_MESH)

        mine = [pltpu.make_async_copy(x_refs[t], rows(t, *me), local_sems.at[t]) for t in range(nt)]
        for cp in mine:
            cp.start()
        first = []
        for j, chip in enumerate(chips):
            first += [copy(t, 1 + j, me, (*chip, c), src=x_refs[t]) for t in range(nt)]
        first += [copy(t, 0, me, sibling, src=x_refs[t]) for t in range(nt)]
        for cp in first:
            cp.start()
        passed = []
        for j, chip in enumerate(chips):
            for t in range(nt):
                copy(t, 1 + j, (*chip, c), me).wait_recv()
                passed.append(copy(t, 4 + j, (*chip, c), sibling))
                passed[-1].start()
        for t in range(nt):
            copy(t, 0, sibling, me).wait_recv()
            for j, chip in enumerate(chips):
                copy(t, 4 + j, (*chip, 1 - c), me).wait_recv()
        for cp in first + passed:
            cp.wait_send()
        for cp in mine:
            cp.wait()

    space = pl.BlockSpec(memory_space=pltpu.VMEM) if in_vmem else _ANY
    return pl.pallas_call(
        body, name=name, out_shape=[jax.ShapeDtypeStruct((N_DEV * s.shape[0], s.shape[1]), s.dtype) for s in shards],
        in_specs=[space] * nt, out_specs=[space] * nt,
        scratch_shapes=[pltpu.SemaphoreType.DMA((7 * nt,)), pltpu.SemaphoreType.DMA((7 * nt,)), pltpu.SemaphoreType.DMA((nt,))],
    )(*shards)


_HBM = pl.BlockSpec(memory_space=pltpu.HBM)
_SEM = pl.BlockSpec(memory_space=pltpu.SEMAPHORE)
_EFFECT = pltpu.SideEffectType.DATAFLOW_SIDE_EFFECTING


def _in_hbm(a):
    return pltpu.with_memory_space_constraint(a, pltpu.HBM)


def _gather_first_copies(x_refs, land_refs, send_sems, recv_sems, local_sems):
    nt = len(x_refs)
    x, y, c = _place()
    me = (x, y, c)

    def rows(t, px, py, pc):
        m_per = x_refs[t].shape[0]
        return land_refs[t].at[pl.ds((4 * px + 2 * py + pc) * m_per, m_per), :]

    local = [pltpu.make_async_copy(x_refs[t], rows(t, *me), local_sems.at[t]) for t in range(nt)]
    remote = []
    peers = [(_flip(x, fx), _flip(y, fy), c) for fx, fy in _CHIP_FLIPS] + [(x, y, 1 - c)]
    for k, peer in enumerate(peers):
        remote += [(pltpu.make_async_remote_copy(
            src_ref=x_refs[t], dst_ref=rows(t, *me), send_sem=send_sems.at[4 * t + k], recv_sem=recv_sems.at[4 * t + k],
            device_id=peer, device_id_type=_MESH), rows(t, *peer), 4 * t + k) for t in range(nt)]
    return local, remote


def gather_start(name, shards, after):
    nt = len(shards)

    def body(*refs):
        x_refs, land_refs = refs[:nt], refs[nt:2 * nt]
        send_sems, recv_sems, local_sems = refs[2 * nt + 1:2 * nt + 4]
        token = refs[-1]
        local, remote = _gather_first_copies(x_refs, land_refs, send_sems, recv_sems, local_sems)
        for cp in local:
            cp.start()
        for cp, _, _ in remote:
            cp.start()
        token[...] = jnp.zeros_like(token)

    lands = [lax.empty((N_DEV * s.shape[0], s.shape[1]), s.dtype) for s in shards]
    outs = pl.pallas_call(
        body, name=name,
        out_shape=[pltpu.SemaphoreType.DMA((4 * nt,)), pltpu.SemaphoreType.DMA((4 * nt,)), pltpu.SemaphoreType.DMA((nt,))]
        + [pltpu.HBM(s.shape, s.dtype) for s in shards] + [pltpu.HBM(a.shape, a.dtype) for a in lands]
        + [jax.ShapeDtypeStruct((8, LANE), F32)],
        in_specs=[_HBM] * (2 * nt) + [_ANY], out_specs=[_SEM] * 3 + [_HBM] * (2 * nt) + [pl.BlockSpec(memory_space=pltpu.VMEM)],
        input_output_aliases={i: 3 + i for i in range(2 * nt)},
        compiler_params=pltpu.CompilerParams(has_side_effects=_EFFECT),
    )(*[_in_hbm(s) for s in shards], *[_in_hbm(a) for a in lands], after)
    return outs[:3], outs[3:3 + nt], outs[3 + nt:3 + 2 * nt], outs[-1]


def gather_wait(name, handle, after):
    sems, shards, lands, _ = handle
    nt = len(shards)

    def body(*refs):
        x_refs, land_refs = refs[:nt], refs[nt:2 * nt]
        send_sems, recv_sems, local_sems = refs[2 * nt:2 * nt + 3]
        local, remote = _gather_first_copies(x_refs, land_refs, send_sems, recv_sems, local_sems)
        for cp, landed, k in remote:
            cp.wait_send()
            pltpu.make_async_remote_copy(src_ref=landed, dst_ref=landed, send_sem=send_sems.at[k], recv_sem=recv_sems.at[k],
                                         device_id=_place(), device_id_type=_MESH).wait_recv()
        for cp in local:
            cp.wait()

    outs = pl.pallas_call(
        body, name=name, out_shape=[pltpu.HBM(s.shape, s.dtype) for s in shards] + [pltpu.HBM(a.shape, a.dtype) for a in lands],
        in_specs=[_HBM] * (2 * nt) + [_SEM] * 3 + [_ANY], out_specs=[_HBM] * (2 * nt),
        input_output_aliases={i: i for i in range(2 * nt)},
        compiler_params=pltpu.CompilerParams(has_side_effects=_EFFECT),
    )(*shards, *lands, *sems, after)
    return outs[nt:]


def gather_forward(name, lands):
    nt = len(lands)

    def body(*refs):
        land_refs = refs[:nt]
        send_sems, recv_sems = refs[2 * nt:]
        x, y, c = _place()
        copies = []
        for j, (fx, fy) in enumerate(_CHIP_FLIPS):
            px, py = _flip(x, fx), _flip(y, fy)
            for t in range(nt):
                m_per = land_refs[t].shape[0] // N_DEV
                mine = land_refs[t].at[pl.ds((4 * px + 2 * py + c) * m_per, m_per), :]
                theirs = land_refs[t].at[pl.ds((4 * px + 2 * py + 1 - c) * m_per, m_per), :]
                copies.append((pltpu.make_async_remote_copy(
                    src_ref=mine, dst_ref=mine, send_sem=send_sems.at[3 * t + j], recv_sem=recv_sems.at[3 * t + j],
                    device_id=(x, y, 1 - c), device_id_type=_MESH), theirs, 3 * t + j))
        for cp, _, _ in copies:
            cp.start()
        for cp, theirs, k in copies:
            cp.wait_send()
            pltpu.make_async_remote_copy(src_ref=theirs, dst_ref=theirs, send_sem=send_sems.at[k], recv_sem=recv_sems.at[k],
                                         device_id=(x, y, 1 - c), device_id_type=_MESH).wait_recv()

    return pl.pallas_call(
        body, name=name, out_shape=[jax.ShapeDtypeStruct(a.shape, a.dtype) for a in lands],
        in_specs=[_ANY] * nt, out_specs=[_ANY] * nt, input_output_aliases={i: i for i in range(nt)},
        scratch_shapes=[pltpu.SemaphoreType.DMA((3 * nt,)), pltpu.SemaphoreType.DMA((3 * nt,))],
    )(*lands)


def _chip_copies(p_refs, land_refs, send_sems, recv_sems):
    nt = len(p_refs)
    x, y, c = _place()
    copies = []
    for j, (fx, fy) in enumerate(_CHIP_FLIPS):
        px, py = _flip(x, fx), _flip(y, fy)
        copies += [(pltpu.make_async_remote_copy(
            src_ref=p_refs[t].at[2 * px + py], dst_ref=land_refs[t].at[j], send_sem=send_sems.at[3 * t + j],
            recv_sem=recv_sems.at[3 * t + j], device_id=(px, py, c), device_id_type=_MESH), land_refs[t].at[j], 3 * t + j)
            for t in range(nt)]
    return copies


def _pair_copies(g_refs, land_refs, send_sems, recv_sems):
    nt = len(g_refs)
    x, y, c = _place()
    return [(pltpu.make_async_remote_copy(
        src_ref=g_refs[t].at[2 * k + 1 - c], dst_ref=land_refs[t].at[k], send_sem=send_sems.at[4 * t + k],
        recv_sem=recv_sems.at[4 * t + k], device_id=(x, y, 1 - c), device_id_type=_MESH), land_refs[t].at[k], 4 * t + k)
        for t in range(nt) for k in range(4)]


def exchange_start(name, copies_fn, n_land, parts):
    nt = len(parts)

    def body(*refs):
        p_refs, land_refs = refs[:nt], refs[nt:2 * nt]
        send_sems, recv_sems = refs[2 * nt:2 * nt + 2]
        token = refs[-1]
        for cp, _, _ in copies_fn(p_refs, land_refs, send_sems, recv_sems):
            cp.start()
        token[...] = jnp.zeros_like(token)

    lands = [lax.empty((n_land,) + p.shape[1:], p.dtype) for p in parts]
    outs = pl.pallas_call(
        body, name=name,
        out_shape=[pltpu.SemaphoreType.DMA((n_land * nt,)), pltpu.SemaphoreType.DMA((n_land * nt,))]
        + [pltpu.HBM(p.shape, p.dtype) for p in parts] + [pltpu.HBM(a.shape, a.dtype) for a in lands]
        + [jax.ShapeDtypeStruct((8, LANE), F32)],
        in_specs=[_HBM] * (2 * nt), out_specs=[_SEM] * 2 + [_HBM] * (2 * nt) + [pl.BlockSpec(memory_space=pltpu.VMEM)],
        input_output_aliases={i: 2 + i for i in range(2 * nt)},
        compiler_params=pltpu.CompilerParams(has_side_effects=_EFFECT),
    )(*[_in_hbm(p) for p in parts], *[_in_hbm(a) for a in lands])
    return outs[:2], outs[2:2 + nt], outs[2 + nt:2 + 2 * nt], outs[-1]


def exchange_wait(name, copies_fn, handle, after):
    sems, parts, lands, _ = handle
    nt = len(parts)

    def body(*refs):
        p_refs, land_refs = refs[:nt], refs[nt:2 * nt]
        send_sems, recv_sems = refs[2 * nt:2 * nt + 2]
        for cp, landed, k in copies_fn(p_refs, land_refs, send_sems, recv_sems):
            cp.wait_send()
            pltpu.make_async_remote_copy(src_ref=landed, dst_ref=landed, send_sem=send_sems.at[k], recv_sem=recv_sems.at[k],
                                         device_id=_place(), device_id_type=_MESH).wait_recv()

    outs = pl.pallas_call(
        body, name=name, out_shape=[pltpu.HBM(p.shape, p.dtype) for p in parts] + [pltpu.HBM(a.shape, a.dtype) for a in lands],
        in_specs=[_HBM] * (2 * nt) + [_SEM] * 2 + [_ANY], out_specs=[_HBM] * (2 * nt),
        input_output_aliases={i: i for i in range(2 * nt)},
        compiler_params=pltpu.CompilerParams(has_side_effects=_EFFECT),
    )(*parts, *lands, *sems, after)
    return outs[:nt], outs[nt:]


def _row_tile(rows):
    return rows if rows <= 1024 else 1024


def pair_add(name, g, recv, core):
    _, R, C = g.shape
    tr = _row_tile(R)

    def body(core_ref, g_ref, r_ref, o_ref):
        o_ref[...] = (g_ref[...].astype(F32) + r_ref[...].astype(F32)).astype(o_ref.dtype)

    return pl.pallas_call(
        body, name=name, out_shape=jax.ShapeDtypeStruct((4, R, C), BF),
        grid_spec=pltpu.PrefetchScalarGridSpec(
            num_scalar_prefetch=1, grid=(4, R // tr),
            in_specs=[pl.BlockSpec((None, tr, C), lambda k, i, core_ref: (2 * k + core_ref[0], i, 0)),
                      pl.BlockSpec((None, tr, C), lambda k, i, core_ref: (k, i, 0))],
            out_specs=pl.BlockSpec((None, tr, C), lambda k, i, core_ref: (k, i, 0))),
        compiler_params=_cp("parallel", "parallel"),
    )(core, g, recv)


def chip_add(name, part, recv, chip):
    _, R, C = part.shape
    tr = _row_tile(R)

    def body(chip_ref, p_ref, r_ref, o_ref):
        o_ref[...] = ((p_ref[...].astype(F32) + r_ref[0].astype(F32)) + r_ref[1].astype(F32)) + r_ref[2].astype(F32)

    return pl.pallas_call(
        body, name=name, out_shape=jax.ShapeDtypeStruct((R, C), F32),
        grid_spec=pltpu.PrefetchScalarGridSpec(
            num_scalar_prefetch=1, grid=(R // tr,),
            in_specs=[pl.BlockSpec((None, tr, C), lambda i, chip_ref: (chip_ref[0], i, 0)),
                      pl.BlockSpec((3, tr, C), lambda i, chip_ref: (0, i, 0))],
            out_specs=pl.BlockSpec((tr, C), lambda i, chip_ref: (i, 0))),
        compiler_params=_cp("parallel"),
    )(chip, part, recv)


def reduce_scatter_pair(tag, gs):
    return exchange_start("pair_exchange_start_" + tag, _pair_copies, 4, gs)


def reduce_scatter_chips(tag, handle, core, after):
    gs, from_pair = exchange_wait("pair_exchange_wait_" + tag, _pair_copies, handle, after)
    parts = [pair_add("pair_add_%s_%d" % (tag, t), g, r, core) for t, (g, r) in enumerate(zip(gs, from_pair))]
    return exchange_start("chip_exchange_start_" + tag, _chip_copies, 3, parts)


def reduce_scatter_finish(tag, handle, chip, after):
    parts, landed = exchange_wait("chip_exchange_wait_" + tag, _chip_copies, handle, after)
    return [chip_add("chip_add_%s_%d" % (tag, t), p, r, chip) for t, (p, r) in enumerate(zip(parts, landed))]


def sum_devices(name, g):
    _, R, C = g.shape

    def body(g_ref, o_ref):
        acc = g_ref[0]
        for d in range(1, N_DEV):
            acc = acc + g_ref[d]
        o_ref[...] = acc

    return pl.pallas_call(body, name=name, out_shape=jax.ShapeDtypeStruct((R, C), F32),
                          compiler_params=pltpu.CompilerParams(vmem_limit_bytes=VMEM_LIMIT_BYTES))(g)


_WEIGHTS = ["ada_w", "ada_b", "ln1_g", "ln1_b", "ln2_g", "ln2_b", "ffn_w1", "ffn_w3", "ffn_w2", "mla_w_down", "mla_q_norm",
            "mla_w_uq", "mla_kv_norm", "mla_w_uk", "mla_w_uv", "mla_w_o", "fox_w_in", "fox_b_f", "fox_w_o"]
_SMALL = ["ada_b", "ln1_g", "ln1_b", "ln2_g", "ln2_b", "mla_q_norm", "mla_kv_norm", "fox_b_f"]


def _pad_cols(a, width):
    return jnp.pad(a, ((0, 0), (0, width - a.shape[1])))


def _small_pack(parts, cols):
    rows = [jnp.pad(p.reshape(-1), (0, _round_up(p.size, cols) - p.size)).reshape(-1, cols) for p in parts]
    out = jnp.concatenate(rows, axis=0)
    return jnp.pad(out, ((0, _round_up(out.shape[0], 8) - out.shape[0]), (0, 0)))


def kernel(x, c, positions, ada_w, ada_b, ln1_g, ln1_b, ln2_g, ln2_b, ffn_w1, ffn_w3, ffn_w2, mla_w_down, mla_q_norm, mla_w_uq, mla_kv_norm, mla_w_uk, mla_w_uv, mla_w_o, fox_w_in, fox_b_f, fox_w_o, loss_target, m_ada_w, m_ada_b, m_ln1_g, m_ln1_b, m_ln2_g, m_ln2_b, m_ffn_w1, m_ffn_w3, m_ffn_w2, m_mla_w_down, m_mla_q_norm, m_mla_w_uq, m_mla_kv_norm, m_mla_w_uk, m_mla_w_uv, m_mla_w_o, m_fox_w_in, m_fox_b_f, m_fox_w_o, v_ada_w, v_ada_b, v_ln1_g, v_ln1_b, v_ln2_g, v_ln2_b, v_ffn_w1, v_ffn_w3, v_ffn_w2, v_mla_w_down, v_mla_q_norm, v_mla_w_uq, v_mla_kv_norm, v_mla_w_uk, v_mla_w_uv, v_mla_w_o, v_fox_w_in, v_fox_b_f, v_fox_w_o):
    env = dict(locals())
    W = {n: env[n] for n in _WEIGHTS}
    M = {n: env["m_" + n] for n in _WEIGHTS}
    V = {n: env["v_" + n] for n in _WEIGHTS}

    T, D = x.shape[1], x.shape[2]
    L = ada_w.shape[0]
    QL, KVL = mla_q_norm.shape[1], mla_kv_norm.shape[1]
    HA = mla_w_uq.shape[2] * N_DEV // (NOPE + ROPE)
    HB = fox_b_f.shape[1]
    f_loc = ffn_w1.shape[2]
    FP = _round_up(f_loc, LANE)
    n_ada = ada_w.shape[2]
    n_in = fox_w_in.shape[2]
    n_in_pad = _round_up(n_in, LANE)
    k_loc = mla_w_down.shape[1]
    n_uq, n_uk = mla_w_uq.shape[2], mla_w_uk.shape[2]
    assert D == HB * HEAD == HA * HEAD and n_ada * N_DEV == 6 * D
    alpha = (2 * L) ** 0.25
    xi, yi, ci = _place()
    me = 4 * xi + 2 * yi + ci
    core = jnp.reshape(ci, (1,)).astype(jnp.int32)
    chip = jnp.reshape(2 * xi + yi, (1,)).astype(jnp.int32)

    x2 = x[0]
    tgt = loss_target[0]

    c_all = all_gather("gather_c", [jnp.pad(c, ((0, 7), (0, 0)))], True)[0].reshape(N_DEV, 8, D)[:, 0]
    ada_b_cols = lax.dynamic_slice_in_dim(ada_b, me * n_ada, n_ada, axis=1)
    mod_part = ada_fwd("ada_fwd", c_all, ada_w, ada_b_cols)
    mod_all = all_gather("gather_mod", [mod_part.reshape(L * N_DEV, n_ada)], True)[0].reshape(N_DEV, L, N_DEV, n_ada)
    mod = lax.dynamic_index_in_dim(mod_all, me, axis=2, keepdims=False)
    mod = jnp.moveaxis(mod, 0, 1).reshape(L, 6, D)

    fox_cols = 3 * D + LANE

    def bf(a, pad_rows=0, pad_cols=0):
        return jnp.pad(a.astype(BF), ((0, pad_rows), (0, pad_cols)))

    groups = []
    for i in range(L):
        j = i // 2
        if i % 2 == 0:
            groups.append(("mla%d" % i, [bf(mla_w_down[j]), bf(mla_w_uq[j]), bf(mla_w_uk[j]), bf(mla_w_uv[j]), bf(mla_w_o[j])]))
        else:
            groups.append(("fox%d" % i, [bf(fox_w_in[j], 0, n_in_pad - n_in), bf(fox_w_o[j])]))
        groups.append(("ffn%d" % i, [bf(ffn_w1[i], 0, FP - f_loc), bf(ffn_w3[i], 0, FP - f_loc), bf(ffn_w2[i], FP - f_loc, 0)]))
    handles = {}
    tok = mod
    for gname, shards in groups:
        handles[gname] = gather_start("gather_start_" + gname, shards, tok)
        tok = handles[gname][3]
    mod = mod + tok[0, 0]

    def gathered(gname, after):
        lands = gather_wait("gather_wait_" + gname, handles[gname], after)
        lands = gather_forward("gather_forward_" + gname, lands)
        return [a.reshape((N_DEV,) + s.shape) for a, s in zip(lands, dict(groups)[gname])]

    def cols_whole(g, keep=None):
        return jnp.moveaxis(g[:, :, :keep], 0, 1).reshape(g.shape[1], -1)

    def modv(i, k):
        return mod[i, k][None, :]

    b_f = _pad_cols(fox_b_f, LANE)

    inv_freq = ROPE_THETA ** (-jnp.arange(0, ROPE, 2, dtype=F32) / ROPE)
    ang = jnp.repeat(positions[0].astype(F32)[:, None] * inv_freq, 2, axis=1)
    cos_k, sin_k = jnp.cos(ang), jnp.sin(ang)
    cos_q, sin_q = jnp.tile(cos_k, (1, HA)), jnp.tile(sin_k, (1, HA))
    rot_k, rot_q = _pair_rotation(ROPE), _pair_rotation(HA * ROPE)
    tk = min(ATT_TILE, T)
    mla_scale = (NOPE + ROPE) ** -0.5
    fox_scale = HEAD ** -0.5

    def modulate(tag, xin, sc, sh):
        return rowwise("modulate_" + tag, lambda xv, scv, shv: xv * (1.0 + scv) + shv, [xin], [sc, sh], [(D, BF)])[0]

    def post(tag, xin, y, g, gam, bet, nsc, nsh):
        def f(xv, yv, gv, gamv, betv, scv, shv):
            xhat, _ = _ln_stats(alpha, xv, yv, gv)
            out = xhat * gamv + betv
            return out, out * (1.0 + scv) + shv
        return rowwise("post_" + tag, f, [xin, y], [g, gam, bet, nsc, nsh], [(D, F32), (D, BF)])

    def post_bwd(tag, dxo, xin, y, g, gam):
        def f(dv, xv, yv, gv, gamv):
            xhat, rstd = _ln_stats(alpha, xv, yv, gv)
            dxh = dv * gamv
            dz = rstd * (dxh - _mean1(dxh) - xhat * _mean1(dxh * xhat))
            return alpha * dz, (1.0 + gv) * dz, _sum0(dv * xhat), _sum0(dv), _sum0(dz * yv)
        return rowwise("post_bwd_" + tag, f, [dxo, xin, y], [g, gam], [(D, F32), (D, BF)], [D, D, D])

    def pre_bwd(tag, dxa, dh, xin, sc):
        def f(dxav, dhv, xv, scv):
            return dxav + dhv * (1.0 + scv), _sum0(dhv * xv), _sum0(dhv)
        return rowwise("pre_bwd_" + tag, f, [dxa, dh, xin], [sc], [(D, F32)], [D, D])

    saved = []
    xc = x2
    h = modulate("l0a", xc, modv(0, 1), modv(0, 0))
    for i in range(L):
        j = i // 2
        tag = "l%d" % i
        s = {"x_a": xc}
        s["h_a"] = h
        if i % 2 == 0:
            wdn, wuq, wuk, wuv, wo = gathered("mla%d" % i, xc)
            w_down, w_o = wdn.reshape(D, -1), wo.reshape(D, D)
            q3 = cols_whole(wuq).reshape(QL, HA, NOPE + ROPE)
            w_uq = jnp.concatenate([q3[..., :NOPE].reshape(QL, HA * NOPE), q3[..., NOPE:].reshape(QL, HA * ROPE)], axis=-1)
            w_ukv = jnp.concatenate([cols_whole(wuk), cols_whole(wuv)], axis=-1)
            s.update(w_down=w_down, w_o=w_o, w_uq=w_uq, w_ukv=w_ukv)
            lat = mm("mla_down_" + tag, h, w_down)

            def mid(latv, ck, sk, qg, kg, p):
                ql, kl, kp = latv[:, :QL], latv[:, QL:QL + KVL], latv[:, QL + KVL:]
                qn = ql * lax.rsqrt(_mean1(ql * ql) + RMS_EPS) * qg
                cn = kl * lax.rsqrt(_mean1(kl * kl) + RMS_EPS) * kg
                return qn, cn, kp * ck + _rot(kp, p) * sk

            qn, ckv, kpe = rowwise("mla_mid_" + tag, mid, [lat, cos_k, sin_k],
                                   [mla_q_norm[j][None], mla_kv_norm[j][None], rot_k], [(QL, BF), (KVL, BF), (ROPE, BF)])
            qf = mm("mla_uq_" + tag, qn, w_uq)

            def qrope(qv, cq, sq, p):
                qp = qv[:, HA * NOPE:]
                return (qp * cq + _rot(qp, p) * sq) * mla_scale

            qpe = rowwise("mla_qrope_" + tag, qrope, [qf, cos_q, sin_q], [rot_q], [(HA * ROPE, BF)])[0]
            qpe_h = jnp.moveaxis(qpe.reshape(T, HA, ROPE), 1, 0)
            kv = mm("mla_ukv_" + tag, ckv, w_ukv)
            o, lse = attn_fwd("mla_attn_" + tag, "mla", mla_scale, HA, T, qf, 0, kv, 0, kv, HA, qpe_h, kpe)
            y = mm("mla_o_" + tag, o, w_o)
            s.update(lat=lat, qn=qn, ckv=ckv, kpe=kpe, qf=qf, qpe_h=qpe_h, kv=kv, o=o, lse=lse)
        else:
            win, wo = gathered("fox%d" % i, xc)
            w_in, w_o = _pad_cols(cols_whole(win, keep=n_in), fox_cols), wo.reshape(D, D)
            s.update(w_in=w_in, w_o=w_o)
            proj = mm("fox_in_" + tag, h, w_in, bn=896)
            fz = proj[:, 3 * D:]
            cum = fox_cum("fox_cum_" + tag, fz, b_f[j][None])
            cum_h = cum[:, :HB].T
            o, lse = attn_fwd("fox_attn_" + tag, "fox", fox_scale, HB, T, proj, 0, proj, HB, proj, 2 * HB,
                              cum_h.reshape(HB, T // tk, 1, tk), cum_h.reshape(HB, T, 1))
            y = mm("fox_o_" + tag, o, w_o)
            s.update(proj=proj, fz=fz, cum_h=cum_h, o=o, lse=lse)
        s["y_a"] = y
        xc, h = post(tag + "a", xc, y, modv(i, 2), ln1_g[i][None], ln1_b[i][None], modv(i, 4), modv(i, 3))
        s["x_f"] = xc
        w1g, w3g, w2g = [w[:, None] for w in gathered("ffn%d" % i, xc)]
        a, b, gte = ffn_up("ffn_up_" + tag, h, w1g, w3g, 0)
        y = ffn_down("ffn_down_" + tag, gte, w2g, 0)
        s.update(h_f=h, a=a, b=b, gte=gte, y_f=y, w1g=w1g, w3g=w3g, w2g=w2g)
        nxt = min(i + 1, L - 1)
        xc, h = post(tag + "f", xc, y, modv(i, 5), ln2_g[i][None], ln2_b[i][None], modv(nxt, 1), modv(nxt, 0))
        saved.append(s)

    def loss_fn(yv, tv):
        e = yv - tv
        return e * (1.0 / D), jnp.zeros((1, LANE), F32) + _sum0(jnp.sum(e * e, axis=1, keepdims=True)) * (0.5 / D)

    dx, loss_part = rowwise("loss", loss_fn, [xc, tgt], [], [(D, F32)], [LANE])

    G = {n: [None] * W[n].shape[0] for n in _WEIGHTS if n != "ada_w"}
    pending, pair_open = [], []

    def scatter(kind, idx, tag, gs, after):
        handle = reduce_scatter_pair(tag, gs)
        order = handle[3][0, 0]
        if pair_open:
            pkind, pidx, ptag, phandle = pair_open.pop()
            chips = reduce_scatter_chips(ptag, phandle, core, after)
            pending.append((pkind, pidx, chips))
            order = order + chips[3][0, 0]
        pair_open.append((kind, idx, tag, handle))
        return order
    dmod = [[None] * 6 for _ in range(L)]
    for i in reversed(range(L)):
        j = i // 2
        tag = "l%d" % i
        s = saved[i]
        dxa, dy, G["ln2_g"][i], G["ln2_b"][i], dmod[i][5] = post_bwd(tag + "f", dx, s["x_f"], s["y_f"], modv(i, 5), ln2_g[i][None])
        g_w2 = ffn_out_grad("ffn_w2_grad_" + tag, s["gte"], dy)
        da, db = ffn_down_bwd("ffn_down_bwd_" + tag, dy, s["w2g"], s["a"], s["b"], 0)
        g_w1 = ffn_in_grad("ffn_w1_grad_" + tag, s["h_f"], da)
        g_w3 = ffn_in_grad("ffn_w3_grad_" + tag, s["h_f"], db)
        dh = ffn_dh("ffn_dh_" + tag, da, db, s["w1g"], s["w3g"], 0)
        order = scatter("ffn", i, "ffn_" + tag, [g_w1, g_w3, g_w2], dh)
        dx, dmod[i][4], dmod[i][3] = pre_bwd(tag + "f", dxa, dh, s["x_f"], modv(i, 4) + order)
        dxa, dy, G["ln1_g"][i], G["ln1_b"][i], dmod[i][2] = post_bwd(tag + "a", dx, s["x_a"], s["y_a"], modv(i, 2), ln1_g[i][None])
        if i % 2 == 0:
            g_o = mm("mla_o_grad_" + tag, s["o"], dy, ta=True, out_dtype=BF)
            do = mm("mla_do_" + tag, dy, s["w_o"], tb=True, out_dtype=BF)
            delta = attn_delta("mla_delta_" + tag, do, s["o"], HA, tk)
            dqn_, dkn_, dv_, dqpe_h, dkpe = attn_bwd("mla_attn_bwd_" + tag, "mla", mla_scale, HA, T, s["qf"], 0, s["kv"], 0,
                                                     s["kv"], HA, s["qpe_h"], s["kpe"], do, s["lse"], delta)
            dkv = jnp.concatenate([dkn_, dv_], axis=1)
            g_kv = mm("mla_ukv_grad_" + tag, s["ckv"], dkv, ta=True, out_dtype=BF)
            dckv = mm("mla_dckv_" + tag, dkv, s["w_ukv"], tb=True)

            def qrope_bwd(dv, cq, sq, p):
                dv = dv * mla_scale
                return dv * cq - _rot(dv * sq, p)

            dqpe = rowwise("mla_qrope_bwd_" + tag, qrope_bwd, [jnp.moveaxis(dqpe_h, 0, 1).reshape(T, HA * ROPE), cos_q, sin_q],
                           [rot_q], [(HA * ROPE, BF)])[0]
            dq = jnp.concatenate([dqn_, dqpe], axis=1)
            g_uq = mm("mla_uq_grad_" + tag, s["qn"], dq, ta=True, out_dtype=BF)
            g_uq = jnp.concatenate([g_uq[:, :HA * NOPE].reshape(QL, HA, NOPE), g_uq[:, HA * NOPE:].reshape(QL, HA, ROPE)],
                                   axis=-1).reshape(QL, N_DEV, n_uq)
            dqn = mm("mla_dqn_" + tag, dq, s["w_uq"], tb=True)

            def mid_bwd(latv, dqv, dcv, dkp, ck, sk, qg, kg, p):
                ql, kl = latv[:, :QL], latv[:, QL:QL + KVL]
                rq = lax.rsqrt(_mean1(ql * ql) + RMS_EPS)
                rk = lax.rsqrt(_mean1(kl * kl) + RMS_EPS)
                uq, uk = dqv * qg, dcv * kg
                dql = rq * uq - ql * (rq * rq * rq) * _mean1(uq * ql)
                dkl = rk * uk - kl * (rk * rk * rk) * _mean1(uk * kl)
                return dql, dkl, dkp * ck - _rot(dkp * sk, p), _sum0(dqv * ql * rq), _sum0(dcv * kl * rk)

            dql, dkl, dkp, G["mla_q_norm"][j], G["mla_kv_norm"][j] = rowwise(
                "mla_mid_bwd_" + tag, mid_bwd, [s["lat"], dqn, dckv, dkpe, cos_k, sin_k],
                [mla_q_norm[j][None], mla_kv_norm[j][None], rot_k], [(QL, BF), (KVL, BF), (ROPE, BF)], [QL, KVL])
            dlat = jnp.concatenate([dql, dkl, dkp], axis=1)
            g_dn = mm("mla_down_grad_" + tag, s["h_a"], dlat, ta=True, out_dtype=BF)
            dh = mm("mla_dh_" + tag, dlat, s["w_down"], tb=True)
            g_kv = jnp.moveaxis(g_kv.reshape(KVL, 2, N_DEV, n_uk), 2, 0).reshape(N_DEV, KVL, 2 * n_uk)
            order = scatter("mla", j, "mla_" + tag, [
                g_dn.reshape(N_DEV, k_loc, -1), jnp.moveaxis(g_uq, 1, 0), g_kv, g_o.reshape(N_DEV, k_loc, D)], dh)
        else:
            g_o = mm("fox_o_grad_" + tag, s["o"], dy, ta=True, out_dtype=BF)
            do = mm("fox_do_" + tag, dy, s["w_o"], tb=True, out_dtype=BF)
            cum_h = s["cum_h"]
            delta = attn_delta("fox_delta_" + tag, do, s["o"], HB, tk)
            dq_, dk_, dv_, dcq, dck = attn_bwd("fox_attn_bwd_" + tag, "fox", fox_scale, HB, T, s["proj"], 0, s["proj"], HB,
                                               s["proj"], 2 * HB, cum_h.reshape(HB, T // tk, 1, tk), cum_h.reshape(HB, T, 1),
                                               do, s["lse"], delta)
            dcum = _pad_cols((dcq.reshape(HB, T) + dck.reshape(HB, T)).T, LANE)
            df, db_f = fox_cum_bwd("fox_cum_bwd_" + tag, s["fz"], b_f[j][None], dcum)
            G["fox_b_f"][j] = db_f[:, :HB]
            dproj = jnp.concatenate([dq_, dk_, dv_, df], axis=1)
            g_in = mm("fox_in_grad_" + tag, s["h_a"], dproj, ta=True, out_dtype=BF, bn=896)[:, :N_DEV * n_in]
            dh = mm("fox_dh_" + tag, dproj, s["w_in"], tb=True, bk=896)
            g_in = jnp.pad(jnp.moveaxis(g_in.reshape(D, N_DEV, n_in), 1, 0), ((0, 0), (0, 0), (0, n_in_pad - n_in)))
            order = scatter("fox", j, "fox_" + tag, [g_in, g_o.reshape(N_DEV, k_loc, D)], dh)
        dx, dmod[i][1], dmod[i][0] = pre_bwd(tag + "a", dxa, dh, s["x_a"], modv(i, 1) + order)

    pkind, pidx, ptag, phandle = pair_open.pop()
    pending.append((pkind, pidx, reduce_scatter_chips(ptag, phandle, core, dx)))
    for kind, idx, handle in pending:
        res = reduce_scatter_finish("%s_%d" % (kind, idx), handle, chip, dx)
        if kind == "ffn":
            G["ffn_w1"][idx], G["ffn_w3"][idx], G["ffn_w2"][idx] = res[0][:, :f_loc], res[1][:, :f_loc], res[2][:f_loc]
        elif kind == "mla":
            G["mla_w_down"][idx], G["mla_w_uq"][idx], G["mla_w_o"][idx] = res[0], res[1], res[3]
            G["mla_w_uk"][idx], G["mla_w_uv"][idx] = res[2][:, :n_uk], res[2][:, n_uk:]
        else:
            G["fox_w_in"][idx], G["fox_w_o"][idx] = res[0][:, :n_in], res[1]

    dmod_mine = jnp.concatenate([jnp.concatenate(r, axis=1) for r in dmod], axis=0)
    small_parts = [dmod_mine] + [jnp.concatenate(G[n], axis=0) for n in _SMALL[1:]] + [loss_part]
    small = _small_pack(small_parts, 512)
    small_all = all_gather("gather_small", [small], True)[0].reshape(N_DEV, -1, 512)
    small_sum = sum_devices("sum_small", small_all)
    sums, off = [], 0
    for p in small_parts:
        nrow = _round_up(p.size, 512) // 512
        sums.append(small_sum[off:off + nrow].reshape(-1)[:p.size].reshape(p.shape))
        off += nrow
    grads = {n: gsum for n, gsum in zip(_SMALL, sums[:-1])}
    loss = sums[-1][0, 0]
    nrow = L * 6 * D // 512
    dmod_all = small_all[:, :nrow].reshape(N_DEV, L, 6 * D)
    dmod_cols = jnp.moveaxis(lax.dynamic_slice_in_dim(dmod_all, me * n_ada, n_ada, axis=2), 0, 1)

    for n in ("ffn_w1", "ffn_w3", "ffn_w2", "mla_w_down", "mla_w_uq", "mla_w_uk", "mla_w_uv", "mla_w_o", "fox_w_in", "fox_w_o"):
        grads[n] = jnp.stack(G[n])

    delta, new_m, new_v = {}, {}, {}
    grads["ada_w"], delta["ada_w"], new_m["ada_w"], new_v["ada_w"] = ada_bwd_adamw(
        "ada_grad_adamw", c_all, dmod_cols, ada_w, m_ada_w, v_ada_w)
    for n in _WEIGHTS:
        if n != "ada_w":
            delta[n], new_m[n], new_v[n] = adamw("adamw_" + n, W[n], grads[n], M[n], V[n])

    return (loss, dx[None], *[grads[n] for n in _WEIGHTS], *[delta[n] for n in _WEIGHTS],
            *[new_m[n] for n in _WEIGHTS], *[new_v[n] for n in _WEIGHTS])
```

```python
import math

import numpy as np
import jax
import jax.numpy as jnp
from jax import lax
from jax.experimental import pallas as pl
from jax.experimental.pallas import tpu as pltpu

F32 = jnp.float32
BF = jnp.bfloat16
N_DEV = 8

CHUNK_SHIFT = 6
NOPE = 128
ROPE = 64
HEAD = 128
ROPE_THETA = 10000.0
LN_EPS = 1e-5
RMS_EPS = 1e-6
ADAM_LR = 0.001
ADAM_B1 = 0.9
ADAM_B2 = 0.999
ADAM_EPS = 1e-08
ADAM_WD = 0.01
ADAM_STEP = 10

VMEM_LIMIT_BYTES = 48 * 1024 * 1024
LANE = 128
NEG = -1e30


def _cp(*sem):
    return pltpu.CompilerParams(dimension_semantics=sem, vmem_limit_bytes=VMEM_LIMIT_BYTES)


def _pick(dim, pref):
    if dim <= pref:
        return dim
    for t in range(pref - pref % LANE, LANE - 1, -LANE):
        if dim % t == 0:
            return t
    return dim


def _round_up(v, m):
    return (v + m - 1) // m * m


def _sum0(v):
    return jnp.sum(v, axis=0, keepdims=True)


def _mean1(v):
    return jnp.mean(v, axis=-1, keepdims=True)


_NN = (((1,), (0,)), ((), ()))
_NT = (((1,), (1,)), ((), ()))
_TN = (((0,), (0,)), ((), ()))


def _mm_call(name, grid, ins, in_specs, dns, out_shape, out_spec, add=None, add_spec=None):
    npairs = len(dns)
    nk = grid[-1]
    has_add = add is not None

    def body(*refs):
        add_ref = refs[2 * npairs] if has_add else None
        o_ref = refs[2 * npairs + (1 if has_add else 0)]
        r = None
        for p in range(npairs):
            t = lax.dot_general(refs[2 * p][...].astype(BF), refs[2 * p + 1][...].astype(BF), dns[p], preferred_element_type=F32)
            r = t if r is None else r + t

        def fin(val):
            if has_add:
                val = val + add_ref[...]
            o_ref[...] = val.astype(o_ref.dtype)

        if nk == 1:
            fin(r)
        else:
            acc = refs[-1]
            k = pl.program_id(len(grid) - 1)

            @pl.when(k == 0)
            def _():
                acc[...] = r

            @pl.when(k > 0)
            def _():
                acc[...] += r

            @pl.when(k == nk - 1)
            def _():
                fin(acc[...])

    acc_shape = tuple(b for b in out_spec.block_shape if b is not None)
    args, specs = list(ins), list(in_specs)
    if has_add:
        args.append(add)
        specs.append(add_spec)
    return pl.pallas_call(
        body, name=name, grid=grid, in_specs=specs, out_specs=out_spec, out_shape=out_shape,
        scratch_shapes=[pltpu.VMEM(acc_shape, F32)] if nk > 1 else [],
        compiler_params=_cp(*(["parallel"] * (len(grid) - 1) + ["arbitrary"])),
    )(*args)


def mm(name, a, b, ta=False, tb=False, out_dtype=F32, add=None, bm=1024, bn=1024, bk=2048):
    M, K = (a.shape[1], a.shape[0]) if ta else a.shape
    N = b.shape[0] if tb else b.shape[1]
    bm, bn, bk = _pick(M, bm), _pick(N, bn), _pick(K, bk)
    a_spec = pl.BlockSpec((bk, bm), lambda i, j, k: (k, i)) if ta else pl.BlockSpec((bm, bk), lambda i, j, k: (i, k))
    b_spec = pl.BlockSpec((bn, bk), lambda i, j, k: (j, k)) if tb else pl.BlockSpec((bk, bn), lambda i, j, k: (k, j))
    dn = (((0 if ta else 1,), (1 if tb else 0,)), ((), ()))
    tile = pl.BlockSpec((bm, bn), lambda i, j, k: (i, j))
    return _mm_call(name, (M // bm, N // bn, K // bk), [a, b], [a_spec, b_spec], [dn],
                    jax.ShapeDtypeStruct((M, N), out_dtype), tile, add, tile)


def rowwise(name, fn, rows, vecs, out_rows, out_accs=(), tm=256):
    T = rows[0].shape[0]
    tm = min(tm, T)
    nr, nv, no = len(rows), len(vecs), len(out_rows)

    def body(*refs):
        outs = fn(*[r[...] for r in refs[:nr + nv]])
        if not isinstance(outs, (tuple, list)):
            outs = (outs,)
        o_refs = refs[nr + nv:nr + nv + no]
        a_refs = refs[nr + nv + no:]
        for r, o in zip(o_refs, outs[:no]):
            r[...] = o.astype(r.dtype)
        if a_refs:
            @pl.when(pl.program_id(0) == 0)
            def _():
                for r in a_refs:
                    r[...] = jnp.zeros_like(r)

            for r, o in zip(a_refs, outs[no:]):
                r[...] += o

    in_specs = [pl.BlockSpec((tm, r.shape[1]), lambda i: (i, 0)) for r in rows]
    in_specs += [pl.BlockSpec(v.shape, lambda i: (0, 0)) for v in vecs]
    out_specs = [pl.BlockSpec((tm, w), lambda i: (i, 0)) for w, _ in out_rows]
    out_specs += [pl.BlockSpec((1, w), lambda i: (0, 0)) for w in out_accs]
    out_shape = [jax.ShapeDtypeStruct((T, w), dt) for w, dt in out_rows]
    out_shape += [jax.ShapeDtypeStruct((1, w), F32) for w in out_accs]
    return pl.pallas_call(
        body, name=name, grid=(T // tm,), in_specs=in_specs, out_specs=out_specs, out_shape=out_shape,
        compiler_params=_cp("arbitrary"),
    )(*rows, *vecs)


def _rot(v, p):
    hi = v.astype(BF)
    lo = (v - hi.astype(F32)).astype(BF)
    return jnp.dot(hi, p, preferred_element_type=F32) + jnp.dot(lo, p, preferred_element_type=F32)


def _pair_rotation(width):
    p = np.zeros((width, width), np.float32)
    idx = np.arange(0, width, 2)
    p[idx + 1, idx] = -1.0
    p[idx, idx + 1] = 1.0
    return jnp.asarray(p, BF)


def _ln_stats(alpha, x, y, g):
    z = alpha * x + (1.0 + g) * y
    zc = z - _mean1(z)
    rstd = lax.rsqrt(_mean1(zc * zc) + LN_EPS)
    return zc * rstd, rstd


ATT_G_FWD = 4
ATT_G_BWD = 2
ATT_TILE = 256


def _visible(mode, ks, qs, t):
    krow = ks + lax.broadcasted_iota(jnp.int32, (t, t), 0)
    qcol = qs + lax.broadcasted_iota(jnp.int32, (t, t), 1)
    if mode == "mla":
        return lax.shift_right_logical(qcol, CHUNK_SHIFT) >= lax.shift_right_logical(krow, CHUNK_SHIFT)
    return qcol >= krow


def _scores(mode, k, q, e_k, e_q, ks, qs, t, masked):
    s = lax.dot_general(k, q, _NT, preferred_element_type=F32)
    if mode == "mla":
        s = s + lax.dot_general(e_k, e_q, _NT, preferred_element_type=F32)
    else:
        s = s + e_q - e_k
    if masked:
        s = jnp.where(_visible(mode, ks, qs, t), s, NEG)
    return s


def _attn_specs(mode, G, T, t, q_off, k_off, v_off):
    def col(off):
        return pl.BlockSpec((T, G * HEAD), lambda h: (0, off // G + h))

    stat = pl.BlockSpec((G, T // t, 1, t), lambda h: (h, 0, 0, 0))
    if mode == "mla":
        e_specs = [pl.BlockSpec((G, T, ROPE), lambda h: (h, 0, 0)), pl.BlockSpec((T, ROPE), lambda h: (0, 0))]
    else:
        e_specs = [stat, pl.BlockSpec((G, T, 1), lambda h: (h, 0, 0))]
    return [col(q_off), col(k_off), col(v_off)], e_specs, stat


def _head(ref, rows, g):
    return ref[rows, g * HEAD:(g + 1) * HEAD]


def attn_fwd(name, mode, scale, H, T, q_src, q_off, k_src, k_off, v_src, v_off, e_q, e_k):
    G, t = ATT_G_FWD, min(ATT_TILE, T)
    nq = T // t
    mla = mode == "mla"

    def body(q_ref, k_ref, v_ref, eq_ref, ek_ref, o_ref, lse_ref, acc_ref):
        def q_loop(qi, _):
            qs = pl.multiple_of(qi * t, t)
            qrows = pl.ds(qs, t)
            qh = [(_head(q_ref, qrows, g).astype(F32) * scale).astype(BF) for g in range(G)]
            eqh = [eq_ref[g, qrows, :] if mla else eq_ref[g, qi] for g in range(G)]
            acc_ref[...] = jnp.zeros_like(acc_ref)

            def step(j, carry, masked):
                ks = pl.multiple_of(j * t, t)
                krows = pl.ds(ks, t)
                out, ps, corrs = [], [], []
                ss = [_scores(mode, _head(k_ref, krows, g).astype(BF), qh[g], ek_ref[krows, :] if mla else ek_ref[g, krows, :],
                              eqh[g], ks, qs, t, masked) for g in range(G)]
                for g in range(G):
                    m, l = carry[g]
                    m_new = jnp.maximum(m, jnp.max(ss[g], axis=0, keepdims=True))
                    p = jnp.exp(ss[g] - m_new)
                    corr = jnp.exp(m - m_new)
                    out.append((m_new, corr * l + jnp.sum(p, axis=0, keepdims=True)))
                    ps.append(p.astype(BF))
                    corrs.append(corr)
                pvs = [lax.dot_general(_head(v_ref, krows, g).astype(BF), ps[g], _TN, preferred_element_type=F32) for g in range(G)]
                for g in range(G):
                    acc_ref[g] = corrs[g] * acc_ref[g] + pvs[g]
                return tuple(out)

            init = tuple((jnp.full((1, t), NEG, F32), jnp.zeros((1, t), F32)) for _ in range(G))
            carry = lax.fori_loop(0, qi, lambda j, cr: step(j, cr, False), init)
            carry = step(qi, carry, True)
            for g in range(G):
                m, l = carry[g]
                o_ref[qrows, g * HEAD:(g + 1) * HEAD] = (acc_ref[g] / l).T.astype(o_ref.dtype)
                lse_ref[g, qi] = m + jnp.log(l)
            return 0

        lax.fori_loop(0, nq, q_loop, 0)

    cols, e_specs, stat = _attn_specs(mode, G, T, t, q_off, k_off, v_off)
    return pl.pallas_call(
        body, name=name, grid=(H // G,), in_specs=cols + e_specs,
        out_specs=[pl.BlockSpec((T, G * HEAD), lambda h: (0, h)), stat],
        out_shape=[jax.ShapeDtypeStruct((T, H * HEAD), BF), jax.ShapeDtypeStruct((H, nq, 1, t), F32)],
        scratch_shapes=[pltpu.VMEM((G, HEAD, t), F32)],
        compiler_params=_cp("parallel"),
    )(q_src, k_src, v_src, e_q, e_k)


def attn_bwd(name, mode, scale, H, T, q_src, q_off, k_src, k_off, v_src, v_off, e_q, e_k, do, lse, delta):
    G, t = ATT_G_BWD, min(ATT_TILE, T)
    nq = T // t
    mla = mode == "mla"

    def body(q_ref, k_ref, v_ref, eq_ref, ek_ref, do_ref, lse_ref, dl_ref,
             dq_ref, dk_ref, dv_ref, deq_ref, dek_ref, dk_acc, dv_acc, dq_acc, deq_acc):
        dk_acc[...] = jnp.zeros_like(dk_acc)
        dv_acc[...] = jnp.zeros_like(dv_acc)
        if mla:
            @pl.when(pl.program_id(0) == 0)
            def _():
                dek_ref[...] = jnp.zeros_like(dek_ref)
        else:
            dek_ref[...] = jnp.zeros_like(dek_ref)

        def q_loop(qi, _):
            qs = pl.multiple_of(qi * t, t)
            qrows = pl.ds(qs, t)
            qh = [(_head(q_ref, qrows, g).astype(F32) * scale).astype(BF) for g in range(G)]
            eqh = [eq_ref[g, qrows, :] if mla else eq_ref[g, qi] for g in range(G)]
            doh = [_head(do_ref, qrows, g) for g in range(G)]
            lse_q = [lse_ref[g, qi] for g in range(G)]
            dl_q = [dl_ref[g, qi] for g in range(G)]
            dq_acc[...] = jnp.zeros_like(dq_acc)
            deq_acc[...] = jnp.zeros_like(deq_acc)

            def step(j, carry, masked):
                ks = pl.multiple_of(j * t, t)
                krows = pl.ds(ks, t)
                kh = [_head(k_ref, krows, g).astype(BF) for g in range(G)]
                ekh = [ek_ref[krows, :] if mla else ek_ref[g, krows, :] for g in range(G)]
                ss = [_scores(mode, kh[g], qh[g], ekh[g], eqh[g], ks, qs, t, masked) for g in range(G)]
                dps = [lax.dot_general(_head(v_ref, krows, g).astype(BF), doh[g], _NT, preferred_element_type=F32) for g in range(G)]
                pbs, dss, dsbs = [], [], []
                for g in range(G):
                    p = jnp.exp(ss[g] - lse_q[g])
                    ds = p * (dps[g] - dl_q[g])
                    pbs.append(p.astype(BF))
                    dss.append(ds)
                    dsbs.append(ds.astype(BF))
                for g in range(G):
                    dv_acc[g, krows, :] += jnp.dot(pbs[g], doh[g], preferred_element_type=F32)
                    dk_acc[g, krows, :] += jnp.dot(dsbs[g], qh[g], preferred_element_type=F32)
                    dq_acc[g] += lax.dot_general(dsbs[g], kh[g], _TN, preferred_element_type=F32)
                    if mla:
                        deq_acc[g] += lax.dot_general(dsbs[g], ekh[g], _TN, preferred_element_type=F32)
                        dek_ref[krows, :] += jnp.dot(dsbs[g], eqh[g], preferred_element_type=F32)
                    else:
                        deq_acc[g] += jnp.sum(dss[g], axis=0, keepdims=True)
                        dek_ref[g, krows, :] -= jnp.sum(dss[g], axis=1, keepdims=True)
                return carry

            lax.fori_loop(0, qi, lambda j, cr: step(j, cr, False), 0)
            step(qi, 0, True)
            for g in range(G):
                dq_ref[qrows, g * HEAD:(g + 1) * HEAD] = (dq_acc[g] * scale).astype(dq_ref.dtype)
                if mla:
                    deq_ref[g, qrows, :] = deq_acc[g]
                else:
                    deq_ref[g, qi] = deq_acc[g]
            return 0

        lax.fori_loop(0, nq, q_loop, 0)
        for g in range(G):
            dk_ref[:, g * HEAD:(g + 1) * HEAD] = dk_acc[g].astype(dk_ref.dtype)
            dv_ref[:, g * HEAD:(g + 1) * HEAD] = dv_acc[g].astype(dv_ref.dtype)

    cols, e_specs, stat = _attn_specs(mode, G, T, t, q_off, k_off, v_off)
    heads = pl.BlockSpec((T, G * HEAD), lambda h: (0, h))
    if mla:
        de_shapes = [jax.ShapeDtypeStruct((H, T, ROPE), F32), jax.ShapeDtypeStruct((T, ROPE), F32)]
        deq_scratch = pltpu.VMEM((G, t, ROPE), F32)
    else:
        de_shapes = [jax.ShapeDtypeStruct((H, nq, 1, t), F32), jax.ShapeDtypeStruct((H, T, 1), F32)]
        deq_scratch = pltpu.VMEM((G, 1, t), F32)
    return pl.pallas_call(
        body, name=name, grid=(H // G,),
        in_specs=cols + e_specs + [heads, stat, stat],
        out_specs=[heads, heads, heads] + e_specs,
        out_shape=[jax.ShapeDtypeStruct((T, H * HEAD), BF)] * 3 + de_shapes,
        scratch_shapes=[pltpu.VMEM((G, T, HEAD), F32), pltpu.VMEM((G, T, HEAD), F32), pltpu.VMEM((G, t, HEAD), F32), deq_scratch],
        compiler_params=_cp("arbitrary"),
    )(q_src, k_src, v_src, e_q, e_k, do, lse, delta)


def attn_delta(name, do, o, H, t):
    T, W = do.shape
    ind = np.zeros((W, LANE), np.float32)
    ind[np.arange(W), np.arange(W) // HEAD] = 1.0

    def f(dv, ov, e):
        return _rot(dv.astype(F32) * ov.astype(F32), e)

    d = rowwise(name, f, [do, o], [jnp.asarray(ind, BF)], [(LANE, F32)])[0]
    return d[:, :H].T.reshape(H, T // t, 1, t)


def fox_cum(name, f, b):
    T = f.shape[0]

    def body(f_ref, b_ref, cum_ref):
        z = f_ref[...] + b_ref[...]
        v = jnp.minimum(z, 0.0) - jnp.log(1.0 + jnp.exp(-jnp.abs(z)))
        row = lax.broadcasted_iota(jnp.int32, v.shape, 0)
        s = 1
        while s < T:
            v = v + jnp.where(row >= s, pltpu.roll(v, s, 0), 0.0)
            s *= 2
        cum_ref[...] = v

    return pl.pallas_call(body, name=name, out_shape=jax.ShapeDtypeStruct(f.shape, F32),
                          compiler_params=pltpu.CompilerParams(vmem_limit_bytes=VMEM_LIMIT_BYTES))(f, b)


def fox_cum_bwd(name, f, b, dcum):
    T = f.shape[0]

    def body(f_ref, b_ref, dc_ref, df_ref, db_ref):
        v = dc_ref[...]
        row = lax.broadcasted_iota(jnp.int32, v.shape, 0)
        s = 1
        while s < T:
            v = v + jnp.where(row < T - s, pltpu.roll(v, T - s, 0), 0.0)
            s *= 2
        z = f_ref[...] + b_ref[...]
        df = v / (1.0 + jnp.exp(z))
        df_ref[...] = df.astype(df_ref.dtype)
        db_ref[...] = _sum0(df)

    return pl.pallas_call(body, name=name,
                          out_shape=[jax.ShapeDtypeStruct(f.shape, BF), jax.ShapeDtypeStruct((1, f.shape[1]), F32)],
                          compiler_params=pltpu.CompilerParams(vmem_limit_bytes=VMEM_LIMIT_BYTES))(f, b, dcum)


def ffn_up(name, h, w1g, w3g, l):
    T, D = h.shape
    FP = w1g.shape[3]
    bm = _pick(T, 1024)

    def body(h_ref, w1_ref, w3_ref, a_ref, b_ref, g_ref):
        hv = h_ref[...]
        a = jnp.dot(hv, w1_ref[...], preferred_element_type=F32)
        b = jnp.dot(hv, w3_ref[...], preferred_element_type=F32)
        a_ref[...] = a.astype(BF)
        b_ref[...] = b.astype(BF)
        g_ref[...] = (a * (1.0 / (1.0 + jnp.exp(-a))) * b).astype(BF)

    w_spec = pl.BlockSpec((None, None, D, FP), lambda i, d: (d, l, 0, 0))
    tile = pl.BlockSpec((None, bm, FP), lambda i, d: (d, i, 0))
    return pl.pallas_call(
        body, name=name, grid=(T // bm, N_DEV), in_specs=[pl.BlockSpec((bm, D), lambda i, d: (i, 0)), w_spec, w_spec],
        out_specs=[tile, tile, tile], out_shape=[jax.ShapeDtypeStruct((N_DEV, T, FP), BF)] * 3,
        compiler_params=_cp("parallel", "parallel"),
    )(h, w1g, w3g)


def ffn_down(name, g, w2g, l):
    _, T, FP = g.shape
    D = w2g.shape[3]
    bm, bn = _pick(T, 1024), _pick(D, 1024)
    return _mm_call(
        name, (T // bm, D // bn, N_DEV), [g, w2g],
        [pl.BlockSpec((None, bm, FP), lambda i, j, d: (d, i, 0)), pl.BlockSpec((None, None, FP, bn), lambda i, j, d: (d, l, 0, j))],
        [_NN], jax.ShapeDtypeStruct((T, D), F32), pl.BlockSpec((bm, bn), lambda i, j, d: (i, j)))


def ffn_down_bwd(name, dy, w2g, a, b, l):
    T, D = dy.shape
    FP = w2g.shape[2]
    bm = _pick(T, 1024)

    def body(dy_ref, w2_ref, a_ref, b_ref, da_ref, db_ref):
        dg = lax.dot_general(dy_ref[...], w2_ref[...], _NT, preferred_element_type=F32)
        av = a_ref[...].astype(F32)
        bv = b_ref[...].astype(F32)
        sig = 1.0 / (1.0 + jnp.exp(-av))
        da_ref[...] = (dg * bv * sig * (1.0 + av * (1.0 - sig))).astype(BF)
        db_ref[...] = (dg * av * sig).astype(BF)

    tile = pl.BlockSpec((None, bm, FP), lambda i, d: (d, i, 0))
    return pl.pallas_call(
        body, name=name, grid=(T // bm, N_DEV),
        in_specs=[pl.BlockSpec((bm, D), lambda i, d: (i, 0)), pl.BlockSpec((None, None, FP, D), lambda i, d: (d, l, 0, 0)), tile, tile],
        out_specs=[tile, tile], out_shape=[jax.ShapeDtypeStruct((N_DEV, T, FP), BF)] * 2,
        compiler_params=_cp("parallel", "parallel"),
    )(dy, w2g, a, b)


def ffn_dh(name, da, db, w1g, w3g, l):
    _, T, FP = da.shape
    D = w1g.shape[2]
    bm, bn = _pick(T, 1024), _pick(D, 1024)
    act = pl.BlockSpec((None, bm, FP), lambda i, j, d: (d, i, 0))
    wgt = pl.BlockSpec((None, None, bn, FP), lambda i, j, d: (d, l, j, 0))
    return _mm_call(name, (T // bm, D // bn, N_DEV), [da, w1g, db, w3g], [act, wgt, act, wgt], [_NT, _NT],
                    jax.ShapeDtypeStruct((T, D), F32), pl.BlockSpec((bm, bn), lambda i, j, d: (i, j)))


def ffn_in_grad(name, h, dact):
    T, D = h.shape
    FP = dact.shape[2]
    bm = _pick(D, 1024)
    return _mm_call(
        name, (D // bm, N_DEV, 1), [h, dact],
        [pl.BlockSpec((T, bm), lambda i, d, k: (0, i)), pl.BlockSpec((None, T, FP), lambda i, d, k: (d, 0, 0))],
        [_TN], jax.ShapeDtypeStruct((N_DEV, D, FP), BF), pl.BlockSpec((None, bm, FP), lambda i, d, k: (d, i, 0)))


def ffn_out_grad(name, g, dy):
    _, T, FP = g.shape
    D = dy.shape[1]
    bn = _pick(D, 1024)
    return _mm_call(
        name, (N_DEV, D // bn, 1), [g, dy],
        [pl.BlockSpec((None, T, FP), lambda d, j, k: (d, 0, 0)), pl.BlockSpec((T, bn), lambda d, j, k: (0, j))],
        [_TN], jax.ShapeDtypeStruct((N_DEV, FP, D), BF), pl.BlockSpec((None, FP, bn), lambda d, j, k: (d, 0, j)))


def _adamw_math(w, g, m, v):
    m = ADAM_B1 * m + (1.0 - ADAM_B1) * g
    v = ADAM_B2 * v + (1.0 - ADAM_B2) * (g * g)
    m_hat = m / (1.0 - ADAM_B1 ** ADAM_STEP)
    v_hat = v / (1.0 - ADAM_B2 ** ADAM_STEP)
    delta = -ADAM_LR * (m_hat / (jnp.sqrt(v_hat) + ADAM_EPS) + ADAM_WD * w)
    return delta, m, v


def adamw(name, w, g, m, v):
    def body(w_ref, g_ref, m_ref, v_ref, d_ref, mo_ref, vo_ref):
        d, mn, vn = _adamw_math(w_ref[...], g_ref[...], m_ref[...], v_ref[...])
        d_ref[...] = d
        mo_ref[...] = mn
        vo_ref[...] = vn

    rows, n = w.shape[-2:]
    tr = max(t for t in range(1, rows + 1) if rows % t == 0 and (t % 8 == 0 or t == rows) and (t * n * 4 <= 3 << 19 or t <= 8))
    if w.ndim == 2:
        grid, blk, sem = (rows // tr,), pl.BlockSpec((tr, n), lambda i: (i, 0)), ("parallel",)
    else:
        grid, blk, sem = (w.shape[0], rows // tr), pl.BlockSpec((None, tr, n), lambda l, i: (l, i, 0)), ("parallel", "parallel")
    return pl.pallas_call(
        body, name=name, grid=grid, in_specs=[blk] * 4, out_specs=[blk] * 3,
        out_shape=[jax.ShapeDtypeStruct(w.shape, F32)] * 3, compiler_params=_cp(*sem),
    )(w, g, m, v)


def _silu(v):
    return v * (1.0 / (1.0 + jnp.exp(-v)))


def ada_fwd(name, c_all, ada_w, ada_b_cols):
    L, D, n = ada_w.shape
    bn = _pick(n, 512)

    def body(c_ref, w_ref, b_ref, o_ref):
        act = _silu(c_ref[...]).astype(BF)
        o_ref[...] = jnp.dot(act, w_ref[...].astype(BF), preferred_element_type=F32) + b_ref[...]

    return pl.pallas_call(
        body, name=name, grid=(L, n // bn),
        in_specs=[pl.BlockSpec((N_DEV, D), lambda l, j: (0, 0)), pl.BlockSpec((None, D, bn), lambda l, j: (l, 0, j)),
                  pl.BlockSpec((None, 1, bn), lambda l, j: (l, 0, j))],
        out_specs=pl.BlockSpec((None, N_DEV, bn), lambda l, j: (l, 0, j)),
        out_shape=jax.ShapeDtypeStruct((L, N_DEV, n), F32), compiler_params=_cp("parallel", "parallel"),
    )(c_all, ada_w, ada_b_cols.reshape(L, 1, n))


def ada_bwd_adamw(name, c_all, dmod_cols, w, m, v):
    L, D, n = w.shape
    tr = _pick(D, 256)

    def body(c_ref, dm_ref, w_ref, m_ref, v_ref, g_ref, d_ref, mo_ref, vo_ref):
        act = _silu(c_ref[...]).astype(BF)
        g = lax.dot_general(act, dm_ref[...].astype(BF), _TN, preferred_element_type=F32)
        d, mn, vn = _adamw_math(w_ref[...], g, m_ref[...], v_ref[...])
        g_ref[...] = g
        d_ref[...] = d
        mo_ref[...] = mn
        vo_ref[...] = vn

    blk = pl.BlockSpec((None, tr, n), lambda l, i: (l, i, 0))
    return pl.pallas_call(
        body, name=name, grid=(L, D // tr),
        in_specs=[pl.BlockSpec((N_DEV, tr), lambda l, i: (0, i)), pl.BlockSpec((None, N_DEV, n), lambda l, i: (l, 0, 0)),
                  blk, blk, blk],
        out_specs=[blk] * 4, out_shape=[jax.ShapeDtypeStruct((L, D, n), F32)] * 4,
        compiler_params=_cp("parallel", "parallel"),
    )(c_all, dmod_cols, w, m, v)


_MESH = pl.DeviceIdType.MESH
_ANY = pl.BlockSpec(memory_space=pl.ANY)
_CHIP_FLIPS = ((1, 0), (0, 1), (1, 1))


def _place():
    return lax.axis_index("x"), lax.axis_index("y"), lax.axis_index("c")


def _flip(v, f):
    return 1 - v if f else v


def all_gather(name, shards, in_vmem=False):
    nt = len(shards)

    def body(*refs):
        x_refs, out_refs = refs[:nt], refs[nt:2 * nt]
        send_sems, recv_sems, local_sems = refs[2 * nt:]
        x, y, c = _place()
        me, sibling = (x, y, c), (x, y, 1 - c)
        chips = [(_flip(x, fx), _flip(y, fy)) for fx, fy in _CHIP_FLIPS]

        def rows(t, px, py, pc):
            m_per = shards[t].shape[0]
            return out_refs[t].at[pl.ds((4 * px + 2 * py + pc) * m_per, m_per), :]

        def copy(t, k, block, to, src=None):
            return pltpu.make_async_remote_copy(
                src_ref=rows(t, *block) if src is None else src, dst_ref=rows(t, *block),
                send_sem=send_sems.at[7 * t + k], recv_sem=recv_sems.at[7 * t + k], device_id=to, device_id_type=_MESH)

        mine = [pltpu.make_async_copy(x_refs[t], rows(t, *me), local_sems.at[t]) for t in range(nt)]
        for cp in mine:
            cp.start()
        first = []
        for j, chip in enumerate(chips):
            first += [copy(t, 1 + j, me, (*chip, c), src=x_refs[t]) for t in range(nt)]
        first += [copy(t, 0, me, sibling, src=x_refs[t]) for t in range(nt)]
        for cp in first:
            cp.start()
        passed = []
        for j, chip in enumerate(chips):
            for t in range(nt):
                copy(t, 1 + j, (*chip, c), me).wait_recv()
                passed.append(copy(t, 4 + j, (*chip, c), sibling))
                passed[-1].start()
        for t in range(nt):
            copy(t, 0, sibling, me).wait_recv()
            for j, chip in enumerate(chips):
                copy(t, 4 + j, (*chip, 1 - c), me).wait_recv()
        for cp in first + passed:
            cp.wait_send()
        for cp in mine:
            cp.wait()

    space = pl.BlockSpec(memory_space=pltpu.VMEM) if in_vmem else _ANY
    return pl.pallas_call(
        body, name=name, out_shape=[jax.ShapeDtypeStruct((N_DEV * s.shape[0], s.shape[1]), s.dtype) for s in shards],
        in_specs=[space] * nt, out_specs=[space] * nt,
        scratch_shapes=[pltpu.SemaphoreType.DMA((7 * nt,)), pltpu.SemaphoreType.DMA((7 * nt,)), pltpu.SemaphoreType.DMA((nt,))],
    )(*shards)


_HBM = pl.BlockSpec(memory_space=pltpu.HBM)
_SEM = pl.BlockSpec(memory_space=pltpu.SEMAPHORE)
_EFFECT = pltpu.SideEffectType.DATAFLOW_SIDE_EFFECTING


def _in_hbm(a):
    return pltpu.with_memory_space_constraint(a, pltpu.HBM)


def _gather_first_copies(x_refs, land_refs, send_sems, recv_sems, local_sems):
    nt = len(x_refs)
    x, y, c = _place()
    me = (x, y, c)

    def rows(t, px, py, pc):
        m_per = x_refs[t].shape[0]
        return land_refs[t].at[pl.ds((4 * px + 2 * py + pc) * m_per, m_per), :]

    local = [pltpu.make_async_copy(x_refs[t], rows(t, *me), local_sems.at[t]) for t in range(nt)]
    remote = []
    peers = [(_flip(x, fx), _flip(y, fy), c) for fx, fy in _CHIP_FLIPS] + [(x, y, 1 - c)]
    for k, peer in enumerate(peers):
        remote += [(pltpu.make_async_remote_copy(
            src_ref=x_refs[t], dst_ref=rows(t, *me), send_sem=send_sems.at[4 * t + k], recv_sem=recv_sems.at[4 * t + k],
            device_id=peer, device_id_type=_MESH), rows(t, *peer), 4 * t + k) for t in range(nt)]
    return local, remote


def gather_start(name, shards, after):
    nt = len(shards)

    def body(*refs):
        x_refs, land_refs = refs[:nt], refs[nt:2 * nt]
        send_sems, recv_sems, local_sems = refs[2 * nt + 1:2 * nt + 4]
        token = refs[-1]
        local, remote = _gather_first_copies(x_refs, land_refs, send_sems, recv_sems, local_sems)
        for cp in local:
            cp.start()
        for cp, _, _ in remote:
            cp.start()
        token[...] = jnp.zeros_like(token)

    lands = [lax.empty((N_DEV * s.shape[0], s.shape[1]), s.dtype) for s in shards]
    outs = pl.pallas_call(
        body, name=name,
        out_shape=[pltpu.SemaphoreType.DMA((4 * nt,)), pltpu.SemaphoreType.DMA((4 * nt,)), pltpu.SemaphoreType.DMA((nt,))]
        + [pltpu.HBM(s.shape, s.dtype) for s in shards] + [pltpu.HBM(a.shape, a.dtype) for a in lands]
        + [jax.ShapeDtypeStruct((8, LANE), F32)],
        in_specs=[_HBM] * (2 * nt) + [_ANY], out_specs=[_SEM] * 3 + [_HBM] * (2 * nt) + [pl.BlockSpec(memory_space=pltpu.VMEM)],
        input_output_aliases={i: 3 + i for i in range(2 * nt)},
        compiler_params=pltpu.CompilerParams(has_side_effects=_EFFECT),
    )(*[_in_hbm(s) for s in shards], *[_in_hbm(a) for a in lands], after)
    return outs[:3], outs[3:3 + nt], outs[3 + nt:3 + 2 * nt], outs[-1]


def gather_wait(name, handle, after):
    sems, shards, lands, _ = handle
    nt = len(shards)

    def body(*refs):
        x_refs, land_refs = refs[:nt], refs[nt:2 * nt]
        send_sems, recv_sems, local_sems = refs[2 * nt:2 * nt + 3]
        local, remote = _gather_first_copies(x_refs, land_refs, send_sems, recv_sems, local_sems)
        for cp, landed, k in remote:
            cp.wait_send()
            pltpu.make_async_remote_copy(src_ref=landed, dst_ref=landed, send_sem=send_sems.at[k], recv_sem=recv_sems.at[k],
                                         device_id=_place(), device_id_type=_MESH).wait_recv()
        for cp in local:
            cp.wait()

    outs = pl.pallas_call(
        body, name=name, out_shape=[pltpu.HBM(s.shape, s.dtype) for s in shards] + [pltpu.HBM(a.shape, a.dtype) for a in lands],
        in_specs=[_HBM] * (2 * nt) + [_SEM] * 3 + [_ANY], out_specs=[_HBM] * (2 * nt),
        input_output_aliases={i: i for i in range(2 * nt)},
        compiler_params=pltpu.CompilerParams(has_side_effects=_EFFECT),
    )(*shards, *lands, *sems, after)
    return outs[nt:]


def gather_forward(name, lands):
    nt = len(lands)

    def body(*refs):
        land_refs = refs[:nt]
        send_sems, recv_sems = refs[2 * nt:]
        x, y, c = _place()
        copies = []
        for j, (fx, fy) in enumerate(_CHIP_FLIPS):
            px, py = _flip(x, fx), _flip(y, fy)
            for t in range(nt):
                m_per = land_refs[t].shape[0] // N_DEV
                mine = land_refs[t].at[pl.ds((4 * px + 2 * py + c) * m_per, m_per), :]
                theirs = land_refs[t].at[pl.ds((4 * px + 2 * py + 1 - c) * m_per, m_per), :]
                copies.append((pltpu.make_async_remote_copy(
                    src_ref=mine, dst_ref=mine, send_sem=send_sems.at[3 * t + j], recv_sem=recv_sems.at[3 * t + j],
                    device_id=(x, y, 1 - c), device_id_type=_MESH), theirs, 3 * t + j))
        for cp, _, _ in copies:
            cp.start()
        for cp, theirs, k in copies:
            cp.wait_send()
            pltpu.make_async_remote_copy(src_ref=theirs, dst_ref=theirs, send_sem=send_sems.at[k], recv_sem=recv_sems.at[k],
                                         device_id=(x, y, 1 - c), device_id_type=_MESH).wait_recv()

    return pl.pallas_call(
        body, name=name, out_shape=[jax.ShapeDtypeStruct(a.shape, a.dtype) for a in lands],
        in_specs=[_ANY] * nt, out_specs=[_ANY] * nt, input_output_aliases={i: i for i in range(nt)},
        scratch_shapes=[pltpu.SemaphoreType.DMA((3 * nt,)), pltpu.SemaphoreType.DMA((3 * nt,))],
    )(*lands)


def _chip_copies(p_refs, land_refs, send_sems, recv_sems):
    nt = len(p_refs)
    x, y, c = _place()
    copies = []
    for j, (fx, fy) in enumerate(_CHIP_FLIPS):
        px, py = _flip(x, fx), _flip(y, fy)
        copies += [(pltpu.make_async_remote_copy(
            src_ref=p_refs[t].at[2 * px + py], dst_ref=land_refs[t].at[j], send_sem=send_sems.at[3 * t + j],
            recv_sem=recv_sems.at[3 * t + j], device_id=(px, py, c), device_id_type=_MESH), land_refs[t].at[j], 3 * t + j)
            for t in range(nt)]
    return copies


def _pair_copies(g_refs, land_refs, send_sems, recv_sems):
    nt = len(g_refs)
    x, y, c = _place()
    return [(pltpu.make_async_remote_copy(
        src_ref=g_refs[t].at[2 * k + 1 - c], dst_ref=land_refs[t].at[k], send_sem=send_sems.at[4 * t + k],
        recv_sem=recv_sems.at[4 * t + k], device_id=(x, y, 1 - c), device_id_type=_MESH), land_refs[t].at[k], 4 * t + k)
        for t in range(nt) for k in range(4)]


def exchange_start(name, copies_fn, n_land, parts):
    nt = len(parts)

    def body(*refs):
        p_refs, land_refs = refs[:nt], refs[nt:2 * nt]
        send_sems, recv_sems = refs[2 * nt:2 * nt + 2]
        token = refs[-1]
        for cp, _, _ in copies_fn(p_refs, land_refs, send_sems, recv_sems):
            cp.start()
        token[...] = jnp.zeros_like(token)

    lands = [lax.empty((n_land,) + p.shape[1:], p.dtype) for p in parts]
    outs = pl.pallas_call(
        body, name=name,
        out_shape=[pltpu.SemaphoreType.DMA((n_land * nt,)), pltpu.SemaphoreType.DMA((n_land * nt,))]
        + [pltpu.HBM(p.shape, p.dtype) for p in parts] + [pltpu.HBM(a.shape, a.dtype) for a in lands]
        + [jax.ShapeDtypeStruct((8, LANE), F32)],
        in_specs=[_HBM] * (2 * nt), out_specs=[_SEM] * 2 + [_HBM] * (2 * nt) + [pl.BlockSpec(memory_space=pltpu.VMEM)],
        input_output_aliases={i: 2 + i for i in range(2 * nt)},
        compiler_params=pltpu.CompilerParams(has_side_effects=_EFFECT),
    )(*[_in_hbm(p) for p in parts], *[_in_hbm(a) for a in lands])
    return outs[:2], outs[2:2 + nt], outs[2 + nt:2 + 2 * nt], outs[-1]


def exchange_wait(name, copies_fn, handle, after):
    sems, parts, lands, _ = handle
    nt = len(parts)

    def body(*refs):
        p_refs, land_refs = refs[:nt], refs[nt:2 * nt]
        send_sems, recv_sems = refs[2 * nt:2 * nt + 2]
        for cp, landed, k in copies_fn(p_refs, land_refs, send_sems, recv_sems):
            cp.wait_send()
            pltpu.make_async_remote_copy(src_ref=landed, dst_ref=landed, send_sem=send_sems.at[k], recv_sem=recv_sems.at[k],
                                         device_id=_place(), device_id_type=_MESH).wait_recv()

    outs = pl.pallas_call(
        body, name=name, out_shape=[pltpu.HBM(p.shape, p.dtype) for p in parts] + [pltpu.HBM(a.shape, a.dtype) for a in lands],
        in_specs=[_HBM] * (2 * nt) + [_SEM] * 2 + [_ANY], out_specs=[_HBM] * (2 * nt),
        input_output_aliases={i: i for i in range(2 * nt)},
        compiler_params=pltpu.CompilerParams(has_side_effects=_EFFECT),
    )(*parts, *lands, *sems, after)
    return outs[:nt], outs[nt:]


def _row_tile(rows):
    return rows if rows <= 1024 else 1024


def pair_add(name, g, recv, core):
    _, R, C = g.shape
    tr = _row_tile(R)

    def body(core_ref, g_ref, r_ref, o_ref):
        o_ref[...] = (g_ref[...].astype(F32) + r_ref[...].astype(F32)).astype(o_ref.dtype)

    return pl.pallas_call(
        body, name=name, out_shape=jax.ShapeDtypeStruct((4, R, C), BF),
        grid_spec=pltpu.PrefetchScalarGridSpec(
            num_scalar_prefetch=1, grid=(4, R // tr),
            in_specs=[pl.BlockSpec((None, tr, C), lambda k, i, core_ref: (2 * k + core_ref[0], i, 0)),
                      pl.BlockSpec((None, tr, C), lambda k, i, core_ref: (k, i, 0))],
            out_specs=pl.BlockSpec((None, tr, C), lambda k, i, core_ref: (k, i, 0))),
        compiler_params=_cp("parallel", "parallel"),
    )(core, g, recv)


def chip_add(name, part, recv, chip):
    _, R, C = part.shape
    tr = _row_tile(R)

    def body(chip_ref, p_ref, r_ref, o_ref):
        o_ref[...] = ((p_ref[...].astype(F32) + r_ref[0].astype(F32)) + r_ref[1].astype(F32)) + r_ref[2].astype(F32)

    return pl.pallas_call(
        body, name=name, out_shape=jax.ShapeDtypeStruct((R, C), F32),
        grid_spec=pltpu.PrefetchScalarGridSpec(
            num_scalar_prefetch=1, grid=(R // tr,),
            in_specs=[pl.BlockSpec((None, tr, C), lambda i, chip_ref: (chip_ref[0], i, 0)),
                      pl.BlockSpec((3, tr, C), lambda i, chip_ref: (0, i, 0))],
            out_specs=pl.BlockSpec((tr, C), lambda i, chip_ref: (i, 0))),
        compiler_params=_cp("parallel"),
    )(chip, part, recv)


def reduce_scatter_pair(tag, gs):
    return exchange_start("pair_exchange_start_" + tag, _pair_copies, 4, gs)


def reduce_scatter_chips(tag, handle, core, after):
    gs, from_pair = exchange_wait("pair_exchange_wait_" + tag, _pair_copies, handle, after)
    parts = [pair_add("pair_add_%s_%d" % (tag, t), g, r, core) for t, (g, r) in enumerate(zip(gs, from_pair))]
    return exchange_start("chip_exchange_start_" + tag, _chip_copies, 3, parts)


def reduce_scatter_finish(tag, handle, chip, after):
    parts, landed = exchange_wait("chip_exchange_wait_" + tag, _chip_copies, handle, after)
    return [chip_add("chip_add_%s_%d" % (tag, t), p, r, chip) for t, (p, r) in enumerate(zip(parts, landed))]


def sum_devices(name, g):
    _, R, C = g.shape

    def body(g_ref, o_ref):
        acc = g_ref[0]
        for d in range(1, N_DEV):
            acc = acc + g_ref[d]
        o_ref[...] = acc

    return pl.pallas_call(body, name=name, out_shape=jax.ShapeDtypeStruct((R, C), F32),
                          compiler_params=pltpu.CompilerParams(vmem_limit_bytes=VMEM_LIMIT_BYTES))(g)


_WEIGHTS = ["ada_w", "ada_b", "ln1_g", "ln1_b", "ln2_g", "ln2_b", "ffn_w1", "ffn_w3", "ffn_w2", "mla_w_down", "mla_q_norm",
            "mla_w_uq", "mla_kv_norm", "mla_w_uk", "mla_w_uv", "mla_w_o", "fox_w_in", "fox_b_f", "fox_w_o"]
_SMALL = ["ada_b", "ln1_g", "ln1_b", "ln2_g", "ln2_b", "mla_q_norm", "mla_kv_norm", "fox_b_f"]


def _pad_cols(a, width):
    return jnp.pad(a, ((0, 0), (0, width - a.shape[1])))


def _small_pack(parts, cols):
    rows = [jnp.pad(p.reshape(-1), (0, _round_up(p.size, cols) - p.size)).reshape(-1, cols) for p in parts]
    out = jnp.concatenate(rows, axis=0)
    return jnp.pad(out, ((0, _round_up(out.shape[0], 8) - out.shape[0]), (0, 0)))


def kernel(x, c, positions, ada_w, ada_b, ln1_g, ln1_b, ln2_g, ln2_b, ffn_w1, ffn_w3, ffn_w2, mla_w_down, mla_q_norm, mla_w_uq, mla_kv_norm, mla_w_uk, mla_w_uv, mla_w_o, fox_w_in, fox_b_f, fox_w_o, loss_target, m_ada_w, m_ada_b, m_ln1_g, m_ln1_b, m_ln2_g, m_ln2_b, m_ffn_w1, m_ffn_w3, m_ffn_w2, m_mla_w_down, m_mla_q_norm, m_mla_w_uq, m_mla_kv_norm, m_mla_w_uk, m_mla_w_uv, m_mla_w_o, m_fox_w_in, m_fox_b_f, m_fox_w_o, v_ada_w, v_ada_b, v_ln1_g, v_ln1_b, v_ln2_g, v_ln2_b, v_ffn_w1, v_ffn_w3, v_ffn_w2, v_mla_w_down, v_mla_q_norm, v_mla_w_uq, v_mla_kv_norm, v_mla_w_uk, v_mla_w_uv, v_mla_w_o, v_fox_w_in, v_fox_b_f, v_fox_w_o):
    env = dict(locals())
    W = {n: env[n] for n in _WEIGHTS}
    M = {n: env["m_" + n] for n in _WEIGHTS}
    V = {n: env["v_" + n] for n in _WEIGHTS}

    T, D = x.shape[1], x.shape[2]
    L = ada_w.shape[0]
    QL, KVL = mla_q_norm.shape[1], mla_kv_norm.shape[1]
    HA = mla_w_uq.shape[2] * N_DEV // (NOPE + ROPE)
    HB = fox_b_f.shape[1]
    f_loc = ffn_w1.shape[2]
    FP = _round_up(f_loc, LANE)
    n_ada = ada_w.shape[2]
    n_in = fox_w_in.shape[2]
    n_in_pad = _round_up(n_in, LANE)
    k_loc = mla_w_down.shape[1]
    n_uq, n_uk = mla_w_uq.shape[2], mla_w_uk.shape[2]
    assert D == HB * HEAD == HA * HEAD and n_ada * N_DEV == 6 * D
    alpha = (2 * L) ** 0.25
    xi, yi, ci = _place()
    me = 4 * xi + 2 * yi + ci
    core = jnp.reshape(ci, (1,)).astype(jnp.int32)
    chip = jnp.reshape(2 * xi + yi, (1,)).astype(jnp.int32)

    x2 = x[0]
    tgt = loss_target[0]

    c_all = all_gather("gather_c", [jnp.pad(c, ((0, 7), (0, 0)))], True)[0].reshape(N_DEV, 8, D)[:, 0]
    ada_b_cols = lax.dynamic_slice_in_dim(ada_b, me * n_ada, n_ada, axis=1)
    mod_part = ada_fwd("ada_fwd", c_all, ada_w, ada_b_cols)
    mod_all = all_gather("gather_mod", [mod_part.reshape(L * N_DEV, n_ada)], True)[0].reshape(N_DEV, L, N_DEV, n_ada)
    mod = lax.dynamic_index_in_dim(mod_all, me, axis=2, keepdims=False)
    mod = jnp.moveaxis(mod, 0, 1).reshape(L, 6, D)

    fox_cols = 3 * D + LANE

    def bf(a, pad_rows=0, pad_cols=0):
        return jnp.pad(a.astype(BF), ((0, pad_rows), (0, pad_cols)))

    groups = []
    for i in range(L):
        j = i // 2
        if i % 2 == 0:
            groups.append(("mla%d" % i, [bf(mla_w_down[j]), bf(mla_w_uq[j]), bf(mla_w_uk[j]), bf(mla_w_uv[j]), bf(mla_w_o[j])]))
        else:
            groups.append(("fox%d" % i, [bf(fox_w_in[j], 0, n_in_pad - n_in), bf(fox_w_o[j])]))
        groups.append(("ffn%d" % i, [bf(ffn_w1[i], 0, FP - f_loc), bf(ffn_w3[i], 0, FP - f_loc), bf(ffn_w2[i], FP - f_loc, 0)]))
    handles = {}
    tok = mod
    for gname, shards in groups:
        handles[gname] = gather_start("gather_start_" + gname, shards, tok)
        tok = handles[gname][3]
    mod = mod + tok[0, 0]

    def gathered(gname, after):
        lands = gather_wait("gather_wait_" + gname, handles[gname], after)
        lands = gather_forward("gather_forward_" + gname, lands)
        return [a.reshape((N_DEV,) + s.shape) for a, s in zip(lands, dict(groups)[gname])]

    def cols_whole(g, keep=None):
        return jnp.moveaxis(g[:, :, :keep], 0, 1).reshape(g.shape[1], -1)

    def modv(i, k):
        return mod[i, k][None, :]

    b_f = _pad_cols(fox_b_f, LANE)

    inv_freq = ROPE_THETA ** (-jnp.arange(0, ROPE, 2, dtype=F32) / ROPE)
    ang = jnp.repeat(positions[0].astype(F32)[:, None] * inv_freq, 2, axis=1)
    cos_k, sin_k = jnp.cos(ang), jnp.sin(ang)
    cos_q, sin_q = jnp.tile(cos_k, (1, HA)), jnp.tile(sin_k, (1, HA))
    rot_k, rot_q = _pair_rotation(ROPE), _pair_rotation(HA * ROPE)
    tk = min(ATT_TILE, T)
    mla_scale = (NOPE + ROPE) ** -0.5
    fox_scale = HEAD ** -0.5

    def modulate(tag, xin, sc, sh):
        return rowwise("modulate_" + tag, lambda xv, scv, shv: xv * (1.0 + scv) + shv, [xin], [sc, sh], [(D, BF)])[0]

    def post(tag, xin, y, g, gam, bet, nsc, nsh):
        def f(xv, yv, gv, gamv, betv, scv, shv):
            xhat, _ = _ln_stats(alpha, xv, yv, gv)
            out = xhat * gamv + betv
            return out, out * (1.0 + scv) + shv
        return rowwise("post_" + tag, f, [xin, y], [g, gam, bet, nsc, nsh], [(D, F32), (D, BF)])

    def post_bwd(tag, dxo, xin, y, g, gam):
        def f(dv, xv, yv, gv, gamv):
            xhat, rstd = _ln_stats(alpha, xv, yv, gv)
            dxh = dv * gamv
            dz = rstd * (dxh - _mean1(dxh) - xhat * _mean1(dxh * xhat))
            return alpha * dz, (1.0 + gv) * dz, _sum0(dv * xhat), _sum0(dv), _sum0(dz * yv)
        return rowwise("post_bwd_" + tag, f, [dxo, xin, y], [g, gam], [(D, F32), (D, BF)], [D, D, D])

    def pre_bwd(tag, dxa, dh, xin, sc):
        def f(dxav, dhv, xv, scv):
            return dxav + dhv * (1.0 + scv), _sum0(dhv * xv), _sum0(dhv)
        return rowwise("pre_bwd_" + tag, f, [dxa, dh, xin], [sc], [(D, F32)], [D, D])

    saved = []
    xc = x2
    h = modulate("l0a", xc, modv(0, 1), modv(0, 0))
    for i in range(L):
        j = i // 2
        tag = "l%d" % i
        s = {"x_a": xc}
        s["h_a"] = h
        if i % 2 == 0:
            wdn, wuq, wuk, wuv, wo = gathered("mla%d" % i, xc)
            w_down, w_o = wdn.reshape(D, -1), wo.reshape(D, D)
            q3 = cols_whole(wuq).reshape(QL, HA, NOPE + ROPE)
            w_uq = jnp.concatenate([q3[..., :NOPE].reshape(QL, HA * NOPE), q3[..., NOPE:].reshape(QL, HA * ROPE)], axis=-1)
            w_ukv = jnp.concatenate([cols_whole(wuk), cols_whole(wuv)], axis=-1)
            s.update(w_down=w_down, w_o=w_o, w_uq=w_uq, w_ukv=w_ukv)
            lat = mm("mla_down_" + tag, h, w_down)

            def mid(latv, ck, sk, qg, kg, p):
                ql, kl, kp = latv[:, :QL], latv[:, QL:QL + KVL], latv[:, QL + KVL:]
                qn = ql * lax.rsqrt(_mean1(ql * ql) + RMS_EPS) * qg
                cn = kl * lax.rsqrt(_mean1(kl * kl) + RMS_EPS) * kg
                return qn, cn, kp * ck + _rot(kp, p) * sk

            qn, ckv, kpe = rowwise("mla_mid_" + tag, mid, [lat, cos_k, sin_k],
                                   [mla_q_norm[j][None], mla_kv_norm[j][None], rot_k], [(QL, BF), (KVL, BF), (ROPE, BF)])
            qf = mm("mla_uq_" + tag, qn, w_uq)

            def qrope(qv, cq, sq, p):
                qp = qv[:, HA * NOPE:]
                return (qp * cq + _rot(qp, p) * sq) * mla_scale

            qpe = rowwise("mla_qrope_" + tag, qrope, [qf, cos_q, sin_q], [rot_q], [(HA * ROPE, BF)])[0]
            qpe_h = jnp.moveaxis(qpe.reshape(T, HA, ROPE), 1, 0)
            kv = mm("mla_ukv_" + tag, ckv, w_ukv)
            o, lse = attn_fwd("mla_attn_" + tag, "mla", mla_scale, HA, T, qf, 0, kv, 0, kv, HA, qpe_h, kpe)
            y = mm("mla_o_" + tag, o, w_o)
            s.update(lat=lat, qn=qn, ckv=ckv, kpe=kpe, qf=qf, qpe_h=qpe_h, kv=kv, o=o, lse=lse)
        else:
            win, wo = gathered("fox%d" % i, xc)
            w_in, w_o = _pad_cols(cols_whole(win, keep=n_in), fox_cols), wo.reshape(D, D)
            s.update(w_in=w_in, w_o=w_o)
            proj = mm("fox_in_" + tag, h, w_in, bn=896)
            fz = proj[:, 3 * D:]
            cum = fox_cum("fox_cum_" + tag, fz, b_f[j][None])
            cum_h = cum[:, :HB].T
            o, lse = attn_fwd("fox_attn_" + tag, "fox", fox_scale, HB, T, proj, 0, proj, HB, proj, 2 * HB,
                              cum_h.reshape(HB, T // tk, 1, tk), cum_h.reshape(HB, T, 1))
            y = mm("fox_o_" + tag, o, w_o)
            s.update(proj=proj, fz=fz, cum_h=cum_h, o=o, lse=lse)
        s["y_a"] = y
        xc, h = post(tag + "a", xc, y, modv(i, 2), ln1_g[i][None], ln1_b[i][None], modv(i, 4), modv(i, 3))
        s["x_f"] = xc
        w1g, w3g, w2g = [w[:, None] for w in gathered("ffn%d" % i, xc)]
        a, b, gte = ffn_up("ffn_up_" + tag, h, w1g, w3g, 0)
        y = ffn_down("ffn_down_" + tag, gte, w2g, 0)
        s.update(h_f=h, a=a, b=b, gte=gte, y_f=y, w1g=w1g, w3g=w3g, w2g=w2g)
        nxt = min(i + 1, L - 1)
        xc, h = post(tag + "f", xc, y, modv(i, 5), ln2_g[i][None], ln2_b[i][None], modv(nxt, 1), modv(nxt, 0))
        saved.append(s)

    def loss_fn(yv, tv):
        e = yv - tv
        return e * (1.0 / D), jnp.zeros((1, LANE), F32) + _sum0(jnp.sum(e * e, axis=1, keepdims=True)) * (0.5 / D)

    dx, loss_part = rowwise("loss", loss_fn, [xc, tgt], [], [(D, F32)], [LANE])

    G = {n: [None] * W[n].shape[0] for n in _WEIGHTS if n != "ada_w"}
    pending, pair_open = [], []

    def scatter(kind, idx, tag, gs):
        handle = reduce_scatter_pair(tag, gs)
        pair_open.append((kind, idx, tag, handle))
        return handle[3][0, 0]

    def scatter_on(after):
        pkind, pidx, ptag, phandle = pair_open.pop()
        chips = reduce_scatter_chips(ptag, phandle, core, after)
        pending.append((pkind, pidx, chips))
        return chips[3][0, 0]
    dmod = [[None] * 6 for _ in range(L)]
    for i in reversed(range(L)):
        j = i // 2
        tag = "l%d" % i
        s = saved[i]
        dxa, dy, G["ln2_g"][i], G["ln2_b"][i], dmod[i][5] = post_bwd(tag + "f", dx, s["x_f"], s["y_f"], modv(i, 5), ln2_g[i][None])
        g_w2 = ffn_out_grad("ffn_w2_grad_" + tag, s["gte"], dy)
        da, db = ffn_down_bwd("ffn_down_bwd_" + tag, dy, s["w2g"], s["a"], s["b"], 0)
        order = scatter_on(da) if pair_open else 0.0
        g_w1 = ffn_in_grad("ffn_w1_grad_" + tag, s["h_f"], da)
        g_w3 = ffn_in_grad("ffn_w3_grad_" + tag, s["h_f"], db)
        dh = ffn_dh("ffn_dh_" + tag, da, db, s["w1g"], s["w3g"], 0)
        order = order + scatter("ffn", i, "ffn_" + tag, [g_w1, g_w3, g_w2])
        dx, dmod[i][4], dmod[i][3] = pre_bwd(tag + "f", dxa, dh, s["x_f"], modv(i, 4) + order)
        dxa, dy, G["ln1_g"][i], G["ln1_b"][i], dmod[i][2] = post_bwd(tag + "a", dx, s["x_a"], s["y_a"], modv(i, 2), ln1_g[i][None])
        if i % 2 == 0:
            g_o = mm("mla_o_grad_" + tag, s["o"], dy, ta=True, out_dtype=BF)
            do = mm("mla_do_" + tag, dy, s["w_o"], tb=True, out_dtype=BF)
            delta = attn_delta("mla_delta_" + tag, do, s["o"], HA, tk)
            dqn_, dkn_, dv_, dqpe_h, dkpe = attn_bwd("mla_attn_bwd_" + tag, "mla", mla_scale, HA, T, s["qf"], 0, s["kv"], 0,
                                                     s["kv"], HA, s["qpe_h"], s["kpe"], do, s["lse"], delta)
            order = scatter_on(dkpe)
            dkv = jnp.concatenate([dkn_, dv_], axis=1)
            g_kv = mm("mla_ukv_grad_" + tag, s["ckv"], dkv, ta=True, out_dtype=BF)
            dckv = mm("mla_dckv_" + tag, dkv, s["w_ukv"], tb=True)

            def qrope_bwd(dv, cq, sq, p):
                dv = dv * mla_scale
                return dv * cq - _rot(dv * sq, p)

            dqpe = rowwise("mla_qrope_bwd_" + tag, qrope_bwd, [jnp.moveaxis(dqpe_h, 0, 1).reshape(T, HA * ROPE), cos_q, sin_q],
                           [rot_q], [(HA * ROPE, BF)])[0]
            dq = jnp.concatenate([dqn_, dqpe], axis=1)
            g_uq = mm("mla_uq_grad_" + tag, s["qn"], dq, ta=True, out_dtype=BF)
            g_uq = jnp.concatenate([g_uq[:, :HA * NOPE].reshape(QL, HA, NOPE), g_uq[:, HA * NOPE:].reshape(QL, HA, ROPE)],
                                   axis=-1).reshape(QL, N_DEV, n_uq)
            dqn = mm("mla_dqn_" + tag, dq, s["w_uq"], tb=True)

            def mid_bwd(latv, dqv, dcv, dkp, ck, sk, qg, kg, p):
                ql, kl = latv[:, :QL], latv[:, QL:QL + KVL]
                rq = lax.rsqrt(_mean1(ql * ql) + RMS_EPS)
                rk = lax.rsqrt(_mean1(kl * kl) + RMS_EPS)
                uq, uk = dqv * qg, dcv * kg
                dql = rq * uq - ql * (rq * rq * rq) * _mean1(uq * ql)
                dkl = rk * uk - kl * (rk * rk * rk) * _mean1(uk * kl)
                return dql, dkl, dkp * ck - _rot(dkp * sk, p), _sum0(dqv * ql * rq), _sum0(dcv * kl * rk)

            dql, dkl, dkp, G["mla_q_norm"][j], G["mla_kv_norm"][j] = rowwise(
                "mla_mid_bwd_" + tag, mid_bwd, [s["lat"], dqn, dckv, dkpe, cos_k, sin_k],
                [mla_q_norm[j][None], mla_kv_norm[j][None], rot_k], [(QL, BF), (KVL, BF), (ROPE, BF)], [QL, KVL])
            dlat = jnp.concatenate([dql, dkl, dkp], axis=1)
            g_dn = mm("mla_down_grad_" + tag, s["h_a"], dlat, ta=True, out_dtype=BF)
            dh = mm("mla_dh_" + tag, dlat, s["w_down"], tb=True)
            g_kv = jnp.moveaxis(g_kv.reshape(KVL, 2, N_DEV, n_uk), 2, 0).reshape(N_DEV, KVL, 2 * n_uk)
            order = order + scatter("mla", j, "mla_" + tag, [
                g_dn.reshape(N_DEV, k_loc, -1), jnp.moveaxis(g_uq, 1, 0), g_kv, g_o.reshape(N_DEV, k_loc, D)])
        else:
            g_o = mm("fox_o_grad_" + tag, s["o"], dy, ta=True, out_dtype=BF)
            do = mm("fox_do_" + tag, dy, s["w_o"], tb=True, out_dtype=BF)
            cum_h = s["cum_h"]
            delta = attn_delta("fox_delta_" + tag, do, s["o"], HB, tk)
            dq_, dk_, dv_, dcq, dck = attn_bwd("fox_attn_bwd_" + tag, "fox", fox_scale, HB, T, s["proj"], 0, s["proj"], HB,
                                               s["proj"], 2 * HB, cum_h.reshape(HB, T // tk, 1, tk), cum_h.reshape(HB, T, 1),
                                               do, s["lse"], delta)
            order = scatter_on(dck)
            dcum = _pad_cols((dcq.reshape(HB, T) + dck.reshape(HB, T)).T, LANE)
            df, db_f = fox_cum_bwd("fox_cum_bwd_" + tag, s["fz"], b_f[j][None], dcum)
            G["fox_b_f"][j] = db_f[:, :HB]
            dproj = jnp.concatenate([dq_, dk_, dv_, df], axis=1)
            g_in = mm("fox_in_grad_" + tag, s["h_a"], dproj, ta=True, out_dtype=BF, bn=896)[:, :N_DEV * n_in]
            dh = mm("fox_dh_" + tag, dproj, s["w_in"], tb=True, bk=896)
            g_in = jnp.pad(jnp.moveaxis(g_in.reshape(D, N_DEV, n_in), 1, 0), ((0, 0), (0, 0), (0, n_in_pad - n_in)))
            order = order + scatter("fox", j, "fox_" + tag, [g_in, g_o.reshape(N_DEV, k_loc, D)])
        dx, dmod[i][1], dmod[i][0] = pre_bwd(tag + "a", dxa, dh, s["x_a"], modv(i, 1) + order)

    scatter_on(dx)
    for kind, idx, handle in pending:
        res = reduce_scatter_finish("%s_%d" % (kind, idx), handle, chip, dx)
        if kind == "ffn":
            G["ffn_w1"][idx], G["ffn_w3"][idx], G["ffn_w2"][idx] = res[0][:, :f_loc], res[1][:, :f_loc], res[2][:f_loc]
        elif kind == "mla":
            G["mla_w_down"][idx], G["mla_w_uq"][idx], G["mla_w_o"][idx] = res[0], res[1], res[3]
            G["mla_w_uk"][idx], G["mla_w_uv"][idx] = res[2][:, :n_uk], res[2][:, n_uk:]
        else:
            G["fox_w_in"][idx], G["fox_w_o"][idx] = res[0][:, :n_in], res[1]

    dmod_mine = jnp.concatenate([jnp.concatenate(r, axis=1) for r in dmod], axis=0)
    small_parts = [dmod_mine] + [jnp.concatenate(G[n], axis=0) for n in _SMALL[1:]] + [loss_part]
    small = _small_pack(small_parts, 512)
    small_all = all_gather("gather_small", [small], True)[0].reshape(N_DEV, -1, 512)
    small_sum = sum_devices("sum_small", small_all)
    sums, off = [], 0
    for p in small_parts:
        nrow = _round_up(p.size, 512) // 512
        sums.append(small_sum[off:off + nrow].reshape(-1)[:p.size].reshape(p.shape))
        off += nrow
    grads = {n: gsum for n, gsum in zip(_SMALL, sums[:-1])}
    loss = sums[-1][0, 0]
    nrow = L * 6 * D // 512
    dmod_all = small_all[:, :nrow].reshape(N_DEV, L, 6 * D)
    dmod_cols = jnp.moveaxis(lax.dynamic_slice_in_dim(dmod_all, me * n_ada, n_ada, axis=2), 0, 1)

    for n in ("ffn_w1", "ffn_w3", "ffn_w2", "mla_w_down", "mla_w_uq", "mla_w_uk", "mla_w_uv", "mla_w_o", "fox_w_in", "fox_w_o"):
        grads[n] = jnp.stack(G[n])

    delta, new_m, new_v = {}, {}, {}
    grads["ada_w"], delta["ada_w"], new_m["ada_w"], new_v["ada_w"] = ada_bwd_adamw(
        "ada_grad_adamw", c_all, dmod_cols, ada_w, m_ada_w, v_ada_w)
    for n in _WEIGHTS:
        if n != "ada_w":
            delta[n], new_m[n], new_v[n] = adamw("adamw_" + n, W[n], grads[n], M[n], V[n])

    return (loss, dx[None], *[grads[n] for n in _WEIGHTS], *[delta[n] for n in _WEIGHTS],
            *[new_m[n] for n in _WEIGHTS], *[new_v[n] for n in _WEIGHTS])
```

```python
import math

import numpy as np
import jax
import jax.numpy as jnp
from jax import lax
from jax.experimental import pallas as pl
from jax.experimental.pallas import tpu as pltpu

F32 = jnp.float32
BF = jnp.bfloat16
N_DEV = 8

CHUNK_SHIFT = 6
NOPE = 128
ROPE = 64
HEAD = 128
ROPE_THETA = 10000.0
LN_EPS = 1e-5
RMS_EPS = 1e-6
ADAM_LR = 0.001
ADAM_B1 = 0.9
ADAM_B2 = 0.999
ADAM_EPS = 1e-08
ADAM_WD = 0.01
ADAM_STEP = 10

VMEM_LIMIT_BYTES = 48 * 1024 * 1024
LANE = 128
NEG = -1e30


def _cp(*sem):
    return pltpu.CompilerParams(dimension_semantics=sem, vmem_limit_bytes=VMEM_LIMIT_BYTES)


def _pick(dim, pref):
    if dim <= pref:
        return dim
    for t in range(pref - pref % LANE, LANE - 1, -LANE):
        if dim % t == 0:
            return t
    return dim


def _round_up(v, m):
    return (v + m - 1) // m * m


def _sum0(v):
    return jnp.sum(v, axis=0, keepdims=True)


def _mean1(v):
    return jnp.mean(v, axis=-1, keepdims=True)


_NN = (((1,), (0,)), ((), ()))
_NT = (((1,), (1,)), ((), ()))
_TN = (((0,), (0,)), ((), ()))


def _mm_call(name, grid, ins, in_specs, dns, out_shape, out_spec, add=None, add_spec=None):
    npairs = len(dns)
    nk = grid[-1]
    has_add = add is not None

    def body(*refs):
        add_ref = refs[2 * npairs] if has_add else None
        o_ref = refs[2 * npairs + (1 if has_add else 0)]
        r = None
        for p in range(npairs):
            t = lax.dot_general(refs[2 * p][...].astype(BF), refs[2 * p + 1][...].astype(BF), dns[p], preferred_element_type=F32)
            r = t if r is None else r + t

        def fin(val):
            if has_add:
                val = val + add_ref[...]
            o_ref[...] = val.astype(o_ref.dtype)

        if nk == 1:
            fin(r)
        else:
            acc = refs[-1]
            k = pl.program_id(len(grid) - 1)

            @pl.when(k == 0)
            def _():
                acc[...] = r

            @pl.when(k > 0)
            def _():
                acc[...] += r

            @pl.when(k == nk - 1)
            def _():
                fin(acc[...])

    acc_shape = tuple(b for b in out_spec.block_shape if b is not None)
    args, specs = list(ins), list(in_specs)
    if has_add:
        args.append(add)
        specs.append(add_spec)
    return pl.pallas_call(
        body, name=name, grid=grid, in_specs=specs, out_specs=out_spec, out_shape=out_shape,
        scratch_shapes=[pltpu.VMEM(acc_shape, F32)] if nk > 1 else [],
        compiler_params=_cp(*(["parallel"] * (len(grid) - 1) + ["arbitrary"])),
    )(*args)


def mm(name, a, b, ta=False, tb=False, out_dtype=F32, add=None, bm=1024, bn=1024, bk=2048):
    M, K = (a.shape[1], a.shape[0]) if ta else a.shape
    N = b.shape[0] if tb else b.shape[1]
    bm, bn, bk = _pick(M, bm), _pick(N, bn), _pick(K, bk)
    a_spec = pl.BlockSpec((bk, bm), lambda i, j, k: (k, i)) if ta else pl.BlockSpec((bm, bk), lambda i, j, k: (i, k))
    b_spec = pl.BlockSpec((bn, bk), lambda i, j, k: (j, k)) if tb else pl.BlockSpec((bk, bn), lambda i, j, k: (k, j))
    dn = (((0 if ta else 1,), (1 if tb else 0,)), ((), ()))
    tile = pl.BlockSpec((bm, bn), lambda i, j, k: (i, j))
    return _mm_call(name, (M // bm, N // bn, K // bk), [a, b], [a_spec, b_spec], [dn],
                    jax.ShapeDtypeStruct((M, N), out_dtype), tile, add, tile)


def rowwise(name, fn, rows, vecs, out_rows, out_accs=(), tm=256):
    T = rows[0].shape[0]
    tm = min(tm, T)
    nr, nv, no = len(rows), len(vecs), len(out_rows)

    def body(*refs):
        outs = fn(*[r[...] for r in refs[:nr + nv]])
        if not isinstance(outs, (tuple, list)):
            outs = (outs,)
        o_refs = refs[nr + nv:nr + nv + no]
        a_refs = refs[nr + nv + no:]
        for r, o in zip(o_refs, outs[:no]):
            r[...] = o.astype(r.dtype)
        if a_refs:
            @pl.when(pl.program_id(0) == 0)
            def _():
                for r in a_refs:
                    r[...] = jnp.zeros_like(r)

            for r, o in zip(a_refs, outs[no:]):
                r[...] += o

    in_specs = [pl.BlockSpec((tm, r.shape[1]), lambda i: (i, 0)) for r in rows]
    in_specs += [pl.BlockSpec(v.shape, lambda i: (0, 0)) for v in vecs]
    out_specs = [pl.BlockSpec((tm, w), lambda i: (i, 0)) for w, _ in out_rows]
    out_specs += [pl.BlockSpec((1, w), lambda i: (0, 0)) for w in out_accs]
    out_shape = [jax.ShapeDtypeStruct((T, w), dt) for w, dt in out_rows]
    out_shape += [jax.ShapeDtypeStruct((1, w), F32) for w in out_accs]
    return pl.pallas_call(
        body, name=name, grid=(T // tm,), in_specs=in_specs, out_specs=out_specs, out_shape=out_shape,
        compiler_params=_cp("arbitrary"),
    )(*rows, *vecs)


def _rot(v, p):
    hi = v.astype(BF)
    lo = (v - hi.astype(F32)).astype(BF)
    return jnp.dot(hi, p, preferred_element_type=F32) + jnp.dot(lo, p, preferred_element_type=F32)


def _pair_rotation(width):
    p = np.zeros((width, width), np.float32)
    idx = np.arange(0, width, 2)
    p[idx + 1, idx] = -1.0
    p[idx, idx + 1] = 1.0
    return jnp.asarray(p, BF)


def _ln_stats(alpha, x, y, g):
    z = alpha * x + (1.0 + g) * y
    zc = z - _mean1(z)
    rstd = lax.rsqrt(_mean1(zc * zc) + LN_EPS)
    return zc * rstd, rstd


ATT_G_FWD = 4
ATT_G_BWD = 2
ATT_TILE = 256


def _visible(mode, ks, qs, t):
    krow = ks + lax.broadcasted_iota(jnp.int32, (t, t), 0)
    qcol = qs + lax.broadcasted_iota(jnp.int32, (t, t), 1)
    if mode == "mla":
        return lax.shift_right_logical(qcol, CHUNK_SHIFT) >= lax.shift_right_logical(krow, CHUNK_SHIFT)
    return qcol >= krow


def _scores(mode, k, q, e_k, e_q, ks, qs, t, masked):
    s = lax.dot_general(k, q, _NT, preferred_element_type=F32)
    if mode == "mla":
        s = s + lax.dot_general(e_k, e_q, _NT, preferred_element_type=F32)
    else:
        s = s + e_q - e_k
    if masked:
        s = jnp.where(_visible(mode, ks, qs, t), s, NEG)
    return s


def _attn_specs(mode, G, T, t, q_off, k_off, v_off):
    def col(off):
        return pl.BlockSpec((T, G * HEAD), lambda h: (0, off // G + h))

    stat = pl.BlockSpec((G, T // t, 1, t), lambda h: (h, 0, 0, 0))
    if mode == "mla":
        e_specs = [pl.BlockSpec((G, T, ROPE), lambda h: (h, 0, 0)), pl.BlockSpec((T, ROPE), lambda h: (0, 0))]
    else:
        e_specs = [stat, pl.BlockSpec((G, T, 1), lambda h: (h, 0, 0))]
    return [col(q_off), col(k_off), col(v_off)], e_specs, stat


def _head(ref, rows, g):
    return ref[rows, g * HEAD:(g + 1) * HEAD]


def attn_fwd(name, mode, scale, H, T, q_src, q_off, k_src, k_off, v_src, v_off, e_q, e_k):
    G, t = ATT_G_FWD, min(ATT_TILE, T)
    nq = T // t
    mla = mode == "mla"

    def body(q_ref, k_ref, v_ref, eq_ref, ek_ref, o_ref, lse_ref, acc_ref):
        def q_loop(qi, _):
            qs = pl.multiple_of(qi * t, t)
            qrows = pl.ds(qs, t)
            qh = [(_head(q_ref, qrows, g).astype(F32) * scale).astype(BF) for g in range(G)]
            eqh = [eq_ref[g, qrows, :] if mla else eq_ref[g, qi] for g in range(G)]
            acc_ref[...] = jnp.zeros_like(acc_ref)

            def step(j, carry, masked):
                ks = pl.multiple_of(j * t, t)
                krows = pl.ds(ks, t)
                out, ps, corrs = [], [], []
                ss = [_scores(mode, _head(k_ref, krows, g).astype(BF), qh[g], ek_ref[krows, :] if mla else ek_ref[g, krows, :],
                              eqh[g], ks, qs, t, masked) for g in range(G)]
                for g in range(G):
                    m, l = carry[g]
                    m_new = jnp.maximum(m, jnp.max(ss[g], axis=0, keepdims=True))
                    p = jnp.exp(ss[g] - m_new)
                    corr = jnp.exp(m - m_new)
                    out.append((m_new, corr * l + jnp.sum(p, axis=0, keepdims=True)))
                    ps.append(p.astype(BF))
                    corrs.append(corr)
                pvs = [lax.dot_general(_head(v_ref, krows, g).astype(BF), ps[g], _TN, preferred_element_type=F32) for g in range(G)]
                for g in range(G):
                    acc_ref[g] = corrs[g] * acc_ref[g] + pvs[g]
                return tuple(out)

            init = tuple((jnp.full((1, t), NEG, F32), jnp.zeros((1, t), F32)) for _ in range(G))
            carry = lax.fori_loop(0, qi, lambda j, cr: step(j, cr, False), init)
            carry = step(qi, carry, True)
            for g in range(G):
                m, l = carry[g]
                o_ref[qrows, g * HEAD:(g + 1) * HEAD] = (acc_ref[g] / l).T.astype(o_ref.dtype)
                lse_ref[g, qi] = m + jnp.log(l)
            return 0

        lax.fori_loop(0, nq, q_loop, 0)

    cols, e_specs, stat = _attn_specs(mode, G, T, t, q_off, k_off, v_off)
    return pl.pallas_call(
        body, name=name, grid=(H // G,), in_specs=cols + e_specs,
        out_specs=[pl.BlockSpec((T, G * HEAD), lambda h: (0, h)), stat],
        out_shape=[jax.ShapeDtypeStruct((T, H * HEAD), BF), jax.ShapeDtypeStruct((H, nq, 1, t), F32)],
        scratch_shapes=[pltpu.VMEM((G, HEAD, t), F32)],
        compiler_params=_cp("parallel"),
    )(q_src, k_src, v_src, e_q, e_k)


def attn_bwd(name, mode, scale, H, T, q_src, q_off, k_src, k_off, v_src, v_off, e_q, e_k, do, lse, delta):
    G, t = ATT_G_BWD, min(ATT_TILE, T)
    nq = T // t
    mla = mode == "mla"

    def body(q_ref, k_ref, v_ref, eq_ref, ek_ref, do_ref, lse_ref, dl_ref,
             dq_ref, dk_ref, dv_ref, deq_ref, dek_ref, dk_acc, dv_acc, dq_acc, deq_acc):
        dk_acc[...] = jnp.zeros_like(dk_acc)
        dv_acc[...] = jnp.zeros_like(dv_acc)
        if mla:
            @pl.when(pl.program_id(0) == 0)
            def _():
                dek_ref[...] = jnp.zeros_like(dek_ref)
        else:
            dek_ref[...] = jnp.zeros_like(dek_ref)

        def q_loop(qi, _):
            qs = pl.multiple_of(qi * t, t)
            qrows = pl.ds(qs, t)
            qh = [(_head(q_ref, qrows, g).astype(F32) * scale).astype(BF) for g in range(G)]
            eqh = [eq_ref[g, qrows, :] if mla else eq_ref[g, qi] for g in range(G)]
            doh = [_head(do_ref, qrows, g) for g in range(G)]
            lse_q = [lse_ref[g, qi] for g in range(G)]
            dl_q = [dl_ref[g, qi] for g in range(G)]
            dq_acc[...] = jnp.zeros_like(dq_acc)
            deq_acc[...] = jnp.zeros_like(deq_acc)

            def step(j, carry, masked):
                ks = pl.multiple_of(j * t, t)
                krows = pl.ds(ks, t)
                kh = [_head(k_ref, krows, g).astype(BF) for g in range(G)]
                ekh = [ek_ref[krows, :] if mla else ek_ref[g, krows, :] for g in range(G)]
                ss = [_scores(mode, kh[g], qh[g], ekh[g], eqh[g], ks, qs, t, masked) for g in range(G)]
                dps = [lax.dot_general(_head(v_ref, krows, g).astype(BF), doh[g], _NT, preferred_element_type=F32) for g in range(G)]
                pbs, dss, dsbs = [], [], []
                for g in range(G):
                    p = jnp.exp(ss[g] - lse_q[g])
                    ds = p * (dps[g] - dl_q[g])
                    pbs.append(p.astype(BF))
                    dss.append(ds)
                    dsbs.append(ds.astype(BF))
                for g in range(G):
                    dv_acc[g, krows, :] += jnp.dot(pbs[g], doh[g], preferred_element_type=F32)
                    dk_acc[g, krows, :] += jnp.dot(dsbs[g], qh[g], preferred_element_type=F32)
                    dq_acc[g] += lax.dot_general(dsbs[g], kh[g], _TN, preferred_element_type=F32)
                    if mla:
                        deq_acc[g] += lax.dot_general(dsbs[g], ekh[g], _TN, preferred_element_type=F32)
                        dek_ref[krows, :] += jnp.dot(dsbs[g], eqh[g], preferred_element_type=F32)
                    else:
                        deq_acc[g] += jnp.sum(dss[g], axis=0, keepdims=True)
                        dek_ref[g, krows, :] -= jnp.sum(dss[g], axis=1, keepdims=True)
                return carry

            lax.fori_loop(0, qi, lambda j, cr: step(j, cr, False), 0)
            step(qi, 0, True)
            for g in range(G):
                dq_ref[qrows, g * HEAD:(g + 1) * HEAD] = (dq_acc[g] * scale).astype(dq_ref.dtype)
                if mla:
                    deq_ref[g, qrows, :] = deq_acc[g]
                else:
                    deq_ref[g, qi] = deq_acc[g]
            return 0

        lax.fori_loop(0, nq, q_loop, 0)
        for g in range(G):
            dk_ref[:, g * HEAD:(g + 1) * HEAD] = dk_acc[g].astype(dk_ref.dtype)
            dv_ref[:, g * HEAD:(g + 1) * HEAD] = dv_acc[g].astype(dv_ref.dtype)

    cols, e_specs, stat = _attn_specs(mode, G, T, t, q_off, k_off, v_off)
    heads = pl.BlockSpec((T, G * HEAD), lambda h: (0, h))
    if mla:
        de_shapes = [jax.ShapeDtypeStruct((H, T, ROPE), F32), jax.ShapeDtypeStruct((T, ROPE), F32)]
        deq_scratch = pltpu.VMEM((G, t, ROPE), F32)
    else:
        de_shapes = [jax.ShapeDtypeStruct((H, nq, 1, t), F32), jax.ShapeDtypeStruct((H, T, 1), F32)]
        deq_scratch = pltpu.VMEM((G, 1, t), F32)
    return pl.pallas_call(
        body, name=name, grid=(H // G,),
        in_specs=cols + e_specs + [heads, stat, stat],
        out_specs=[heads, heads, heads] + e_specs,
        out_shape=[jax.ShapeDtypeStruct((T, H * HEAD), BF)] * 3 + de_shapes,
        scratch_shapes=[pltpu.VMEM((G, T, HEAD), F32), pltpu.VMEM((G, T, HEAD), F32), pltpu.VMEM((G, t, HEAD), F32), deq_scratch],
        compiler_params=_cp("arbitrary"),
    )(q_src, k_src, v_src, e_q, e_k, do, lse, delta)


def attn_delta(name, do, o, H, t):
    T, W = do.shape
    ind = np.zeros((W, LANE), np.float32)
    ind[np.arange(W), np.arange(W) // HEAD] = 1.0

    def f(dv, ov, e):
        return _rot(dv.astype(F32) * ov.astype(F32), e)

    d = rowwise(name, f, [do, o], [jnp.asarray(ind, BF)], [(LANE, F32)])[0]
    return d[:, :H].T.reshape(H, T // t, 1, t)


def fox_cum(name, f, b):
    T = f.shape[0]

    def body(f_ref, b_ref, cum_ref):
        z = f_ref[...] + b_ref[...]
        v = jnp.minimum(z, 0.0) - jnp.log(1.0 + jnp.exp(-jnp.abs(z)))
        row = lax.broadcasted_iota(jnp.int32, v.shape, 0)
        s = 1
        while s < T:
            v = v + jnp.where(row >= s, pltpu.roll(v, s, 0), 0.0)
            s *= 2
        cum_ref[...] = v

    return pl.pallas_call(body, name=name, out_shape=jax.ShapeDtypeStruct(f.shape, F32),
                          compiler_params=pltpu.CompilerParams(vmem_limit_bytes=VMEM_LIMIT_BYTES))(f, b)


def fox_cum_bwd(name, f, b, dcum):
    T = f.shape[0]

    def body(f_ref, b_ref, dc_ref, df_ref, db_ref):
        v = dc_ref[...]
        row = lax.broadcasted_iota(jnp.int32, v.shape, 0)
        s = 1
        while s < T:
            v = v + jnp.where(row < T - s, pltpu.roll(v, T - s, 0), 0.0)
            s *= 2
        z = f_ref[...] + b_ref[...]
        df = v / (1.0 + jnp.exp(z))
        df_ref[...] = df.astype(df_ref.dtype)
        db_ref[...] = _sum0(df)

    return pl.pallas_call(body, name=name,
                          out_shape=[jax.ShapeDtypeStruct(f.shape, BF), jax.ShapeDtypeStruct((1, f.shape[1]), F32)],
                          compiler_params=pltpu.CompilerParams(vmem_limit_bytes=VMEM_LIMIT_BYTES))(f, b, dcum)


def ffn_up(name, h, w1g, w3g, l):
    T, D = h.shape
    FP = w1g.shape[3]
    bm = _pick(T, 1024)

    def body(h_ref, w1_ref, w3_ref, a_ref, b_ref, g_ref):
        hv = h_ref[...]
        a = jnp.dot(hv, w1_ref[...], preferred_element_type=F32)
        b = jnp.dot(hv, w3_ref[...], preferred_element_type=F32)
        a_ref[...] = a.astype(BF)
        b_ref[...] = b.astype(BF)
        g_ref[...] = (a * (1.0 / (1.0 + jnp.exp(-a))) * b).astype(BF)

    w_spec = pl.BlockSpec((None, None, D, FP), lambda i, d: (d, l, 0, 0))
    tile = pl.BlockSpec((None, bm, FP), lambda i, d: (d, i, 0))
    return pl.pallas_call(
        body, name=name, grid=(T // bm, N_DEV), in_specs=[pl.BlockSpec((bm, D), lambda i, d: (i, 0)), w_spec, w_spec],
        out_specs=[tile, tile, tile], out_shape=[jax.ShapeDtypeStruct((N_DEV, T, FP), BF)] * 3,
        compiler_params=_cp("parallel", "parallel"),
    )(h, w1g, w3g)


def ffn_down(name, g, w2g, l):
    _, T, FP = g.shape
    D = w2g.shape[3]
    bm, bn = _pick(T, 1024), _pick(D, 1024)
    return _mm_call(
        name, (T // bm, D // bn, N_DEV), [g, w2g],
        [pl.BlockSpec((None, bm, FP), lambda i, j, d: (d, i, 0)), pl.BlockSpec((None, None, FP, bn), lambda i, j, d: (d, l, 0, j))],
        [_NN], jax.ShapeDtypeStruct((T, D), F32), pl.BlockSpec((bm, bn), lambda i, j, d: (i, j)))


def ffn_down_bwd(name, dy, w2g, a, b, l):
    T, D = dy.shape
    FP = w2g.shape[2]
    bm = _pick(T, 1024)

    def body(dy_ref, w2_ref, a_ref, b_ref, da_ref, db_ref):
        dg = lax.dot_general(dy_ref[...], w2_ref[...], _NT, preferred_element_type=F32)
        av = a_ref[...].astype(F32)
        bv = b_ref[...].astype(F32)
        sig = 1.0 / (1.0 + jnp.exp(-av))
        da_ref[...] = (dg * bv * sig * (1.0 + av * (1.0 - sig))).astype(BF)
        db_ref[...] = (dg * av * sig).astype(BF)

    tile = pl.BlockSpec((None, bm, FP), lambda i, d: (d, i, 0))
    return pl.pallas_call(
        body, name=name, grid=(T // bm, N_DEV),
        in_specs=[pl.BlockSpec((bm, D), lambda i, d: (i, 0)), pl.BlockSpec((None, None, FP, D), lambda i, d: (d, l, 0, 0)), tile, tile],
        out_specs=[tile, tile], out_shape=[jax.ShapeDtypeStruct((N_DEV, T, FP), BF)] * 2,
        compiler_params=_cp("parallel", "parallel"),
    )(dy, w2g, a, b)


def ffn_dh(name, da, db, w1g, w3g, l):
    _, T, FP = da.shape
    D = w1g.shape[2]
    bm, bn = _pick(T, 1024), _pick(D, 1024)
    act = pl.BlockSpec((None, bm, FP), lambda i, j, d: (d, i, 0))
    wgt = pl.BlockSpec((None, None, bn, FP), lambda i, j, d: (d, l, j, 0))
    return _mm_call(name, (T // bm, D // bn, N_DEV), [da, w1g, db, w3g], [act, wgt, act, wgt], [_NT, _NT],
                    jax.ShapeDtypeStruct((T, D), F32), pl.BlockSpec((bm, bn), lambda i, j, d: (i, j)))


def ffn_in_grad(name, h, dact):
    T, D = h.shape
    FP = dact.shape[2]
    bm = _pick(D, 1024)
    return _mm_call(
        name, (D // bm, N_DEV, 1), [h, dact],
        [pl.BlockSpec((T, bm), lambda i, d, k: (0, i)), pl.BlockSpec((None, T, FP), lambda i, d, k: (d, 0, 0))],
        [_TN], jax.ShapeDtypeStruct((N_DEV, D, FP), BF), pl.BlockSpec((None, bm, FP), lambda i, d, k: (d, i, 0)))


def ffn_out_grad(name, g, dy):
    _, T, FP = g.shape
    D = dy.shape[1]
    bn = _pick(D, 1024)
    return _mm_call(
        name, (N_DEV, D // bn, 1), [g, dy],
        [pl.BlockSpec((None, T, FP), lambda d, j, k: (d, 0, 0)), pl.BlockSpec((T, bn), lambda d, j, k: (0, j))],
        [_TN], jax.ShapeDtypeStruct((N_DEV, FP, D), BF), pl.BlockSpec((None, FP, bn), lambda d, j, k: (d, 0, j)))


def _adamw_math(w, g, m, v):
    m = ADAM_B1 * m + (1.0 - ADAM_B1) * g
    v = ADAM_B2 * v + (1.0 - ADAM_B2) * (g * g)
    m_hat = m / (1.0 - ADAM_B1 ** ADAM_STEP)
    v_hat = v / (1.0 - ADAM_B2 ** ADAM_STEP)
    delta = -ADAM_LR * (m_hat / (jnp.sqrt(v_hat) + ADAM_EPS) + ADAM_WD * w)
    return delta, m, v


def adamw(name, w, g, m, v):
    shape = w.shape
    n = shape[-1]
    rows = math.prod(shape[:-1])
    tr = rows if rows <= 512 else 256
    assert rows % tr == 0

    def body(w_ref, g_ref, m_ref, v_ref, d_ref, mo_ref, vo_ref):
        d, mn, vn = _adamw_math(w_ref[...], g_ref[...], m_ref[...], v_ref[...])
        d_ref[...] = d
        mo_ref[...] = mn
        vo_ref[...] = vn

    blk = pl.BlockSpec((tr, n), lambda i: (i, 0))
    outs = pl.pallas_call(
        body, name=name, grid=(rows // tr,), in_specs=[blk] * 4, out_specs=[blk] * 3,
        out_shape=[jax.ShapeDtypeStruct((rows, n), F32)] * 3, compiler_params=_cp("parallel"),
    )(*[t.reshape(rows, n) for t in (w, g, m, v)])
    return [t.reshape(shape) for t in outs]


def _silu(v):
    return v * (1.0 / (1.0 + jnp.exp(-v)))


def ada_fwd(name, c_all, ada_w, ada_b_cols):
    L, D, n = ada_w.shape
    bn = _pick(n, 512)

    def body(c_ref, w_ref, b_ref, o_ref):
        act = _silu(c_ref[...]).astype(BF)
        o_ref[...] = jnp.dot(act, w_ref[...].astype(BF), preferred_element_type=F32) + b_ref[...]

    return pl.pallas_call(
        body, name=name, grid=(L, n // bn),
        in_specs=[pl.BlockSpec((N_DEV, D), lambda l, j: (0, 0)), pl.BlockSpec((None, D, bn), lambda l, j: (l, 0, j)),
                  pl.BlockSpec((None, 1, bn), lambda l, j: (l, 0, j))],
        out_specs=pl.BlockSpec((None, N_DEV, bn), lambda l, j: (l, 0, j)),
        out_shape=jax.ShapeDtypeStruct((L, N_DEV, n), F32), compiler_params=_cp("parallel", "parallel"),
    )(c_all, ada_w, ada_b_cols.reshape(L, 1, n))


def ada_bwd_adamw(name, c_all, dmod_cols, w, m, v):
    L, D, n = w.shape
    tr = _pick(D, 256)

    def body(c_ref, dm_ref, w_ref, m_ref, v_ref, g_ref, d_ref, mo_ref, vo_ref):
        act = _silu(c_ref[...]).astype(BF)
        g = lax.dot_general(act, dm_ref[...].astype(BF), _TN, preferred_element_type=F32)
        d, mn, vn = _adamw_math(w_ref[...], g, m_ref[...], v_ref[...])
        g_ref[...] = g
        d_ref[...] = d
        mo_ref[...] = mn
        vo_ref[...] = vn

    blk = pl.BlockSpec((None, tr, n), lambda l, i: (l, i, 0))
    return pl.pallas_call(
        body, name=name, grid=(L, D // tr),
        in_specs=[pl.BlockSpec((N_DEV, tr), lambda l, i: (0, i)), pl.BlockSpec((None, N_DEV, n), lambda l, i: (l, 0, 0)),
                  blk, blk, blk],
        out_specs=[blk] * 4, out_shape=[jax.ShapeDtypeStruct((L, D, n), F32)] * 4,
        compiler_params=_cp("parallel", "parallel"),
    )(c_all, dmod_cols, w, m, v)


_MESH = pl.DeviceIdType.MESH
_ANY = pl.BlockSpec(memory_space=pl.ANY)
_CHIP_FLIPS = ((1, 0), (0, 1), (1, 1))


def _place():
    return lax.axis_index("x"), lax.axis_index("y"), lax.axis_index("c")


def _flip(v, f):
    return 1 - v if f else v


def all_gather(name, shards, in_vmem=False):
    nt = len(shards)

    def body(*refs):
        x_refs, out_refs = refs[:nt], refs[nt:2 * nt]
        send_sems, recv_sems, local_sems = refs[2 * nt:]
        x, y, c = _place()
        me, sibling = (x, y, c), (x, y, 1 - c)
        chips = [(_flip(x, fx), _flip(y, fy)) for fx, fy in _CHIP_FLIPS]

        def rows(t, px, py, pc):
            m_per = shards[t].shape[0]
            return out_refs[t].at[pl.ds((4 * px + 2 * py + pc) * m_per, m_per), :]

        def copy(t, k, block, to, src=None):
            return pltpu.make_async_remote_copy(
                src_ref=rows(t, *block) if src is None else src, dst_ref=rows(t, *block),
                send_sem=send_sems.at[7 * t + k], recv_sem=recv_sems.at[7 * t + k], device_id=to, device_id_type=_MESH)

        mine = [pltpu.make_async_copy(x_refs[t], rows(t, *me), local_sems.at[t]) for t in range(nt)]
        for cp in mine:
            cp.start()
        first = []
        for j, chip in enumerate(chips):
            first += [copy(t, 1 + j, me, (*chip, c), src=x_refs[t]) for t in range(nt)]
        first += [copy(t, 0, me, sibling, src=x_refs[t]) for t in range(nt)]
        for cp in first:
            cp.start()
        passed = []
        for j, chip in enumerate(chips):
            for t in range(nt):
                copy(t, 1 + j, (*chip, c), me).wait_recv()
                passed.append(copy(t, 4 + j, (*chip, c), sibling))
                passed[-1].start()
        for t in range(nt):
            copy(t, 0, sibling, me).wait_recv()
            for j, chip in enumerate(chips):
                copy(t, 4 + j, (*chip, 1 - c), me).wait_recv()
        for cp in first + passed:
            cp.wait_send()
        for cp in mine:
            cp.wait()

    space = pl.BlockSpec(memory_space=pltpu.VMEM) if in_vmem else _ANY
    return pl.pallas_call(
        body, name=name, out_shape=[jax.ShapeDtypeStruct((N_DEV * s.shape[0], s.shape[1]), s.dtype) for s in shards],
        in_specs=[space] * nt, out_specs=[space] * nt,
        scratch_shapes=[pltpu.SemaphoreType.DMA((7 * nt,)), pltpu.SemaphoreType.DMA((7 * nt,)), pltpu.SemaphoreType.DMA((nt,))],
    )(*shards)


_HBM = pl.BlockSpec(memory_space=pltpu.HBM)
_SEM = pl.BlockSpec(memory_space=pltpu.SEMAPHORE)
_EFFECT = pltpu.SideEffectType.DATAFLOW_SIDE_EFFECTING


def _in_hbm(a):
    return pltpu.with_memory_space_constraint(a, pltpu.HBM)


def _gather_first_copies(x_refs, land_refs, send_sems, recv_sems, local_sems):
    nt = len(x_refs)
    x, y, c = _place()
    me = (x, y, c)

    def rows(t, px, py, pc):
        m_per = x_refs[t].shape[0]
        return land_refs[t].at[pl.ds((4 * px + 2 * py + pc) * m_per, m_per), :]

    local = [pltpu.make_async_copy(x_refs[t], rows(t, *me), local_sems.at[t]) for t in range(nt)]
    remote = []
    peers = [(_flip(x, fx), _flip(y, fy), c) for fx, fy in _CHIP_FLIPS] + [(x, y, 1 - c)]
    for k, peer in enumerate(peers):
        remote += [(pltpu.make_async_remote_copy(
            src_ref=x_refs[t], dst_ref=rows(t, *me), send_sem=send_sems.at[4 * t + k], recv_sem=recv_sems.at[4 * t + k],
            device_id=peer, device_id_type=_MESH), rows(t, *peer), 4 * t + k) for t in range(nt)]
    return local, remote


def gather_start(name, shards, after):
    nt = len(shards)

    def body(*refs):
        x_refs, land_refs = refs[:nt], refs[nt:2 * nt]
        send_sems, recv_sems, local_sems = refs[2 * nt + 1:2 * nt + 4]
        token = refs[-1]
        local, remote = _gather_first_copies(x_refs, land_refs, send_sems, recv_sems, local_sems)
        for cp in local:
            cp.start()
        for cp, _, _ in remote:
            cp.start()
        token[...] = jnp.zeros_like(token)

    lands = [lax.empty((N_DEV * s.shape[0], s.shape[1]), s.dtype) for s in shards]
    outs = pl.pallas_call(
        body, name=name,
        out_shape=[pltpu.SemaphoreType.DMA((4 * nt,)), pltpu.SemaphoreType.DMA((4 * nt,)), pltpu.SemaphoreType.DMA((nt,))]
        + [pltpu.HBM(s.shape, s.dtype) for s in shards] + [pltpu.HBM(a.shape, a.dtype) for a in lands]
        + [jax.ShapeDtypeStruct((8, LANE), F32)],
        in_specs=[_HBM] * (2 * nt) + [_ANY], out_specs=[_SEM] * 3 + [_HBM] * (2 * nt) + [pl.BlockSpec(memory_space=pltpu.VMEM)],
        input_output_aliases={i: 3 + i for i in range(2 * nt)},
        compiler_params=pltpu.CompilerParams(has_side_effects=_EFFECT),
    )(*[_in_hbm(s) for s in shards], *[_in_hbm(a) for a in lands], after)
    return outs[:3], outs[3:3 + nt], outs[3 + nt:3 + 2 * nt], outs[-1]


def gather_wait(name, handle, after):
    sems, shards, lands, _ = handle
    nt = len(shards)

    def body(*refs):
        x_refs, land_refs = refs[:nt], refs[nt:2 * nt]
        send_sems, recv_sems, local_sems = refs[2 * nt:2 * nt + 3]
        local, remote = _gather_first_copies(x_refs, land_refs, send_sems, recv_sems, local_sems)
        for cp, landed, k in remote:
            cp.wait_send()
            pltpu.make_async_remote_copy(src_ref=landed, dst_ref=landed, send_sem=send_sems.at[k], recv_sem=recv_sems.at[k],
                                         device_id=_place(), device_id_type=_MESH).wait_recv()
        for cp in local:
            cp.wait()

    outs = pl.pallas_call(
        body, name=name, out_shape=[pltpu.HBM(s.shape, s.dtype) for s in shards] + [pltpu.HBM(a.shape, a.dtype) for a in lands],
        in_specs=[_HBM] * (2 * nt) + [_SEM] * 3 + [_ANY], out_specs=[_HBM] * (2 * nt),
        input_output_aliases={i: i for i in range(2 * nt)},
        compiler_params=pltpu.CompilerParams(has_side_effects=_EFFECT),
    )(*shards, *lands, *sems, after)
    return outs[nt:]


def gather_forward(name, lands):
    nt = len(lands)

    def body(*refs):
        land_refs = refs[:nt]
        send_sems, recv_sems = refs[2 * nt:]
        x, y, c = _place()
        copies = []
        for j, (fx, fy) in enumerate(_CHIP_FLIPS):
            px, py = _flip(x, fx), _flip(y, fy)
            for t in range(nt):
                m_per = land_refs[t].shape[0] // N_DEV
                mine = land_refs[t].at[pl.ds((4 * px + 2 * py + c) * m_per, m_per), :]
                theirs = land_refs[t].at[pl.ds((4 * px + 2 * py + 1 - c) * m_per, m_per), :]
                copies.append((pltpu.make_async_remote_copy(
                    src_ref=mine, dst_ref=mine, send_sem=send_sems.at[3 * t + j], recv_sem=recv_sems.at[3 * t + j],
                    device_id=(x, y, 1 - c), device_id_type=_MESH), theirs, 3 * t + j))
        for cp, _, _ in copies:
            cp.start()
        for cp, theirs, k in copies:
            cp.wait_send()
            pltpu.make_async_remote_copy(src_ref=theirs, dst_ref=theirs, send_sem=send_sems.at[k], recv_sem=recv_sems.at[k],
                                         device_id=(x, y, 1 - c), device_id_type=_MESH).wait_recv()

    return pl.pallas_call(
        body, name=name, out_shape=[jax.ShapeDtypeStruct(a.shape, a.dtype) for a in lands],
        in_specs=[_ANY] * nt, out_specs=[_ANY] * nt, input_output_aliases={i: i for i in range(nt)},
        scratch_shapes=[pltpu.SemaphoreType.DMA((3 * nt,)), pltpu.SemaphoreType.DMA((3 * nt,))],
    )(*lands)


def _chip_copies(p_refs, land_refs, send_sems, recv_sems):
    nt = len(p_refs)
    x, y, c = _place()
    copies = []
    for j, (fx, fy) in enumerate(_CHIP_FLIPS):
        px, py = _flip(x, fx), _flip(y, fy)
        copies += [(pltpu.make_async_remote_copy(
            src_ref=p_refs[t].at[2 * px + py], dst_ref=land_refs[t].at[j], send_sem=send_sems.at[3 * t + j],
            recv_sem=recv_sems.at[3 * t + j], device_id=(px, py, c), device_id_type=_MESH), land_refs[t].at[j], 3 * t + j)
            for t in range(nt)]
    return copies


def _pair_copies(g_refs, land_refs, send_sems, recv_sems):
    nt = len(g_refs)
    x, y, c = _place()
    return [(pltpu.make_async_remote_copy(
        src_ref=g_refs[t].at[2 * k + 1 - c], dst_ref=land_refs[t].at[k], send_sem=send_sems.at[4 * t + k],
        recv_sem=recv_sems.at[4 * t + k], device_id=(x, y, 1 - c), device_id_type=_MESH), land_refs[t].at[k], 4 * t + k)
        for t in range(nt) for k in range(4)]


def exchange_start(name, copies_fn, n_land, parts):
    nt = len(parts)

    def body(*refs):
        p_refs, land_refs = refs[:nt], refs[nt:2 * nt]
        send_sems, recv_sems = refs[2 * nt:2 * nt + 2]
        token = refs[-1]
        for cp, _, _ in copies_fn(p_refs, land_refs, send_sems, recv_sems):
            cp.start()
        token[...] = jnp.zeros_like(token)

    lands = [lax.empty((n_land,) + p.shape[1:], p.dtype) for p in parts]
    outs = pl.pallas_call(
        body, name=name,
        out_shape=[pltpu.SemaphoreType.DMA((n_land * nt,)), pltpu.SemaphoreType.DMA((n_land * nt,))]
        + [pltpu.HBM(p.shape, p.dtype) for p in parts] + [pltpu.HBM(a.shape, a.dtype) for a in lands]
        + [jax.ShapeDtypeStruct((8, LANE), F32)],
        in_specs=[_HBM] * (2 * nt), out_specs=[_SEM] * 2 + [_HBM] * (2 * nt) + [pl.BlockSpec(memory_space=pltpu.VMEM)],
        input_output_aliases={i: 2 + i for i in range(2 * nt)},
        compiler_params=pltpu.CompilerParams(has_side_effects=_EFFECT),
    )(*[_in_hbm(p) for p in parts], *[_in_hbm(a) for a in lands])
    return outs[:2], outs[2:2 + nt], outs[2 + nt:2 + 2 * nt], outs[-1]


def exchange_wait(name, copies_fn, handle, after):
    sems, parts, lands, _ = handle
    nt = len(parts)

    def body(*refs):
        p_refs, land_refs = refs[:nt], refs[nt:2 * nt]
        send_sems, recv_sems = refs[2 * nt:2 * nt + 2]
        for cp, landed, k in copies_fn(p_refs, land_refs, send_sems, recv_sems):
            cp.wait_send()
            pltpu.make_async_remote_copy(src_ref=landed, dst_ref=landed, send_sem=send_sems.at[k], recv_sem=recv_sems.at[k],
                                         device_id=_place(), device_id_type=_MESH).wait_recv()

    outs = pl.pallas_call(
        body, name=name, out_shape=[pltpu.HBM(p.shape, p.dtype) for p in parts] + [pltpu.HBM(a.shape, a.dtype) for a in lands],
        in_specs=[_HBM] * (2 * nt) + [_SEM] * 2 + [_ANY], out_specs=[_HBM] * (2 * nt),
        input_output_aliases={i: i for i in range(2 * nt)},
        compiler_params=pltpu.CompilerParams(has_side_effects=_EFFECT),
    )(*parts, *lands, *sems, after)
    return outs[:nt], outs[nt:]


def _row_tile(rows):
    return rows if rows <= 1024 else 1024


def pair_add(name, g, recv, core):
    _, R, C = g.shape
    tr = _row_tile(R)

    def body(core_ref, g_ref, r_ref, o_ref):
        o_ref[...] = (g_ref[...].astype(F32) + r_ref[...].astype(F32)).astype(o_ref.dtype)

    return pl.pallas_call(
        body, name=name, out_shape=jax.ShapeDtypeStruct((4, R, C), BF),
        grid_spec=pltpu.PrefetchScalarGridSpec(
            num_scalar_prefetch=1, grid=(4, R // tr),
            in_specs=[pl.BlockSpec((None, tr, C), lambda k, i, core_ref: (2 * k + core_ref[0], i, 0)),
                      pl.BlockSpec((None, tr, C), lambda k, i, core_ref: (k, i, 0))],
            out_specs=pl.BlockSpec((None, tr, C), lambda k, i, core_ref: (k, i, 0))),
        compiler_params=_cp("parallel", "parallel"),
    )(core, g, recv)


def chip_add(name, part, recv, chip):
    _, R, C = part.shape
    tr = _row_tile(R)

    def body(chip_ref, p_ref, r_ref, o_ref):
        o_ref[...] = ((p_ref[...].astype(F32) + r_ref[0].astype(F32)) + r_ref[1].astype(F32)) + r_ref[2].astype(F32)

    return pl.pallas_call(
        body, name=name, out_shape=jax.ShapeDtypeStruct((R, C), F32),
        grid_spec=pltpu.PrefetchScalarGridSpec(
            num_scalar_prefetch=1, grid=(R // tr,),
            in_specs=[pl.BlockSpec((None, tr, C), lambda i, chip_ref: (chip_ref[0], i, 0)),
                      pl.BlockSpec((3, tr, C), lambda i, chip_ref: (0, i, 0))],
            out_specs=pl.BlockSpec((tr, C), lambda i, chip_ref: (i, 0))),
        compiler_params=_cp("parallel"),
    )(chip, part, recv)


def reduce_scatter_pair(tag, gs):
    return exchange_start("pair_exchange_start_" + tag, _pair_copies, 4, gs)


def reduce_scatter_chips(tag, handle, core, after):
    gs, from_pair = exchange_wait("pair_exchange_wait_" + tag, _pair_copies, handle, after)
    parts = [pair_add("pair_add_%s_%d" % (tag, t), g, r, core) for t, (g, r) in enumerate(zip(gs, from_pair))]
    return exchange_start("chip_exchange_start_" + tag, _chip_copies, 3, parts)


def reduce_scatter_finish(tag, handle, chip, after):
    parts, landed = exchange_wait("chip_exchange_wait_" + tag, _chip_copies, handle, after)
    return [chip_add("chip_add_%s_%d" % (tag, t), p, r, chip) for t, (p, r) in enumerate(zip(parts, landed))]


def sum_devices(name, g):
    _, R, C = g.shape

    def body(g_ref, o_ref):
        acc = g_ref[0]
        for d in range(1, N_DEV):
            acc = acc + g_ref[d]
        o_ref[...] = acc

    return pl.pallas_call(body, name=name, out_shape=jax.ShapeDtypeStruct((R, C), F32),
                          compiler_params=pltpu.CompilerParams(vmem_limit_bytes=VMEM_LIMIT_BYTES))(g)


_WEIGHTS = ["ada_w", "ada_b", "ln1_g", "ln1_b", "ln2_g", "ln2_b", "ffn_w1", "ffn_w3", "ffn_w2", "mla_w_down", "mla_q_norm",
            "mla_w_uq", "mla_kv_norm", "mla_w_uk", "mla_w_uv", "mla_w_o", "fox_w_in", "fox_b_f", "fox_w_o"]
_SMALL = ["ada_b", "ln1_g", "ln1_b", "ln2_g", "ln2_b", "mla_q_norm", "mla_kv_norm", "fox_b_f"]


def _pad_cols(a, width):
    return jnp.pad(a, ((0, 0), (0, width - a.shape[1])))


def _small_pack(parts, cols):
    rows = [jnp.pad(p.reshape(-1), (0, _round_up(p.size, cols) - p.size)).reshape(-1, cols) for p in parts]
    out = jnp.concatenate(rows, axis=0)
    return jnp.pad(out, ((0, _round_up(out.shape[0], 8) - out.shape[0]), (0, 0)))


def kernel(x, c, positions, ada_w, ada_b, ln1_g, ln1_b, ln2_g, ln2_b, ffn_w1, ffn_w3, ffn_w2, mla_w_down, mla_q_norm, mla_w_uq, mla_kv_norm, mla_w_uk, mla_w_uv, mla_w_o, fox_w_in, fox_b_f, fox_w_o, loss_target, m_ada_w, m_ada_b, m_ln1_g, m_ln1_b, m_ln2_g, m_ln2_b, m_ffn_w1, m_ffn_w3, m_ffn_w2, m_mla_w_down, m_mla_q_norm, m_mla_w_uq, m_mla_kv_norm, m_mla_w_uk, m_mla_w_uv, m_mla_w_o, m_fox_w_in, m_fox_b_f, m_fox_w_o, v_ada_w, v_ada_b, v_ln1_g, v_ln1_b, v_ln2_g, v_ln2_b, v_ffn_w1, v_ffn_w3, v_ffn_w2, v_mla_w_down, v_mla_q_norm, v_mla_w_uq, v_mla_kv_norm, v_mla_w_uk, v_mla_w_uv, v_mla_w_o, v_fox_w_in, v_fox_b_f, v_fox_w_o):
    env = dict(locals())
    W = {n: env[n] for n in _WEIGHTS}
    M = {n: env["m_" + n] for n in _WEIGHTS}
    V = {n: env["v_" + n] for n in _WEIGHTS}

    T, D = x.shape[1], x.shape[2]
    L = ada_w.shape[0]
    QL, KVL = mla_q_norm.shape[1], mla_kv_norm.shape[1]
    HA = mla_w_uq.shape[2] * N_DEV // (NOPE + ROPE)
    HB = fox_b_f.shape[1]
    f_loc = ffn_w1.shape[2]
    FP = _round_up(f_loc, LANE)
    n_ada = ada_w.shape[2]
    n_in = fox_w_in.shape[2]
    n_in_pad = _round_up(n_in, LANE)
    k_loc = mla_w_down.shape[1]
    n_uq, n_uk = mla_w_uq.shape[2], mla_w_uk.shape[2]
    assert D == HB * HEAD == HA * HEAD and n_ada * N_DEV == 6 * D
    alpha = (2 * L) ** 0.25
    xi, yi, ci = _place()
    me = 4 * xi + 2 * yi + ci
    core = jnp.reshape(ci, (1,)).astype(jnp.int32)
    chip = jnp.reshape(2 * xi + yi, (1,)).astype(jnp.int32)

    x2 = x[0]
    tgt = loss_target[0]

    c_all = all_gather("gather_c", [jnp.pad(c, ((0, 7), (0, 0)))], True)[0].reshape(N_DEV, 8, D)[:, 0]
    ada_b_cols = lax.dynamic_slice_in_dim(ada_b, me * n_ada, n_ada, axis=1)
    mod_part = ada_fwd("ada_fwd", c_all, ada_w, ada_b_cols)
    mod_all = all_gather("gather_mod", [mod_part.reshape(L * N_DEV, n_ada)], True)[0].reshape(N_DEV, L, N_DEV, n_ada)
    mod = lax.dynamic_index_in_dim(mod_all, me, axis=2, keepdims=False)
    mod = jnp.moveaxis(mod, 0, 1).reshape(L, 6, D)

    fox_cols = 3 * D + LANE

    def bf(a, pad_rows=0, pad_cols=0):
        return jnp.pad(a.astype(BF), ((0, pad_rows), (0, pad_cols)))

    groups = []
    for i in range(L):
        j = i // 2
        if i % 2 == 0:
            groups.append(("mla%d" % i, [bf(mla_w_down[j]), bf(mla_w_uq[j]), bf(mla_w_uk[j]), bf(mla_w_uv[j]), bf(mla_w_o[j])]))
        else:
            groups.append(("fox%d" % i, [bf(fox_w_in[j], 0, n_in_pad - n_in), bf(fox_w_o[j])]))
        groups.append(("ffn%d" % i, [bf(ffn_w1[i], 0, FP - f_loc), bf(ffn_w3[i], 0, FP - f_loc), bf(ffn_w2[i], FP - f_loc, 0)]))
    handles = {}
    tok = mod
    for gname, shards in groups:
        handles[gname] = gather_start("gather_start_" + gname, shards, tok)
        tok = handles[gname][3]
    mod = mod + tok[0, 0]

    def gathered(gname, after):
        lands = gather_wait("gather_wait_" + gname, handles[gname], after)
        lands = gather_forward("gather_forward_" + gname, lands)
        return [a.reshape((N_DEV,) + s.shape) for a, s in zip(lands, dict(groups)[gname])]

    def cols_whole(g, keep=None):
        return jnp.moveaxis(g[:, :, :keep], 0, 1).reshape(g.shape[1], -1)

    def modv(i, k):
        return mod[i, k][None, :]

    b_f = _pad_cols(fox_b_f, LANE)

    inv_freq = ROPE_THETA ** (-jnp.arange(0, ROPE, 2, dtype=F32) / ROPE)
    ang = jnp.repeat(positions[0].astype(F32)[:, None] * inv_freq, 2, axis=1)
    cos_k, sin_k = jnp.cos(ang), jnp.sin(ang)
    cos_q, sin_q = jnp.tile(cos_k, (1, HA)), jnp.tile(sin_k, (1, HA))
    rot_k, rot_q = _pair_rotation(ROPE), _pair_rotation(HA * ROPE)
    tk = min(ATT_TILE, T)
    mla_scale = (NOPE + ROPE) ** -0.5
    fox_scale = HEAD ** -0.5

    def modulate(tag, xin, sc, sh):
        return rowwise("modulate_" + tag, lambda xv, scv, shv: xv * (1.0 + scv) + shv, [xin], [sc, sh], [(D, BF)])[0]

    def post(tag, xin, y, g, gam, bet, nsc, nsh):
        def f(xv, yv, gv, gamv, betv, scv, shv):
            xhat, _ = _ln_stats(alpha, xv, yv, gv)
            out = xhat * gamv + betv
            return out, out * (1.0 + scv) + shv
        return rowwise("post_" + tag, f, [xin, y], [g, gam, bet, nsc, nsh], [(D, F32), (D, BF)])

    def post_bwd(tag, dxo, xin, y, g, gam):
        def f(dv, xv, yv, gv, gamv):
            xhat, rstd = _ln_stats(alpha, xv, yv, gv)
            dxh = dv * gamv
            dz = rstd * (dxh - _mean1(dxh) - xhat * _mean1(dxh * xhat))
            return alpha * dz, (1.0 + gv) * dz, _sum0(dv * xhat), _sum0(dv), _sum0(dz * yv)
        return rowwise("post_bwd_" + tag, f, [dxo, xin, y], [g, gam], [(D, F32), (D, BF)], [D, D, D])

    def pre_bwd(tag, dxa, dh, xin, sc):
        def f(dxav, dhv, xv, scv):
            return dxav + dhv * (1.0 + scv), _sum0(dhv * xv), _sum0(dhv)
        return rowwise("pre_bwd_" + tag, f, [dxa, dh, xin], [sc], [(D, F32)], [D, D])

    saved = []
    xc = x2
    h = modulate("l0a", xc, modv(0, 1), modv(0, 0))
    for i in range(L):
        j = i // 2
        tag = "l%d" % i
        s = {"x_a": xc}
        s["h_a"] = h
        if i % 2 == 0:
            wdn, wuq, wuk, wuv, wo = gathered("mla%d" % i, xc)
            w_down, w_o = wdn.reshape(D, -1), wo.reshape(D, D)
            q3 = cols_whole(wuq).reshape(QL, HA, NOPE + ROPE)
            w_uq = jnp.concatenate([q3[..., :NOPE].reshape(QL, HA * NOPE), q3[..., NOPE:].reshape(QL, HA * ROPE)], axis=-1)
            w_ukv = jnp.concatenate([cols_whole(wuk), cols_whole(wuv)], axis=-1)
            s.update(w_down=w_down, w_o=w_o, w_uq=w_uq, w_ukv=w_ukv)
            lat = mm("mla_down_" + tag, h, w_down)

            def mid(latv, ck, sk, qg, kg, p):
                ql, kl, kp = latv[:, :QL], latv[:, QL:QL + KVL], latv[:, QL + KVL:]
                qn = ql * lax.rsqrt(_mean1(ql * ql) + RMS_EPS) * qg
                cn = kl * lax.rsqrt(_mean1(kl * kl) + RMS_EPS) * kg
                return qn, cn, kp * ck + _rot(kp, p) * sk

            qn, ckv, kpe = rowwise("mla_mid_" + tag, mid, [lat, cos_k, sin_k],
                                   [mla_q_norm[j][None], mla_kv_norm[j][None], rot_k], [(QL, BF), (KVL, BF), (ROPE, BF)])
            qf = mm("mla_uq_" + tag, qn, w_uq)

            def qrope(qv, cq, sq, p):
                qp = qv[:, HA * NOPE:]
                return (qp * cq + _rot(qp, p) * sq) * mla_scale

            qpe = rowwise("mla_qrope_" + tag, qrope, [qf, cos_q, sin_q], [rot_q], [(HA * ROPE, BF)])[0]
            qpe_h = jnp.moveaxis(qpe.reshape(T, HA, ROPE), 1, 0)
            kv = mm("mla_ukv_" + tag, ckv, w_ukv)
            o, lse = attn_fwd("mla_attn_" + tag, "mla", mla_scale, HA, T, qf, 0, kv, 0, kv, HA, qpe_h, kpe)
            y = mm("mla_o_" + tag, o, w_o)
            s.update(lat=lat, qn=qn, ckv=ckv, kpe=kpe, qf=qf, qpe_h=qpe_h, kv=kv, o=o, lse=lse)
        else:
            win, wo = gathered("fox%d" % i, xc)
            w_in, w_o = _pad_cols(cols_whole(win, keep=n_in), fox_cols), wo.reshape(D, D)
            s.update(w_in=w_in, w_o=w_o)
            proj = mm("fox_in_" + tag, h, w_in, bn=896)
            fz = proj[:, 3 * D:]
            cum = fox_cum("fox_cum_" + tag, fz, b_f[j][None])
            cum_h = cum[:, :HB].T
            o, lse = attn_fwd("fox_attn_" + tag, "fox", fox_scale, HB, T, proj, 0, proj, HB, proj, 2 * HB,
                              cum_h.reshape(HB, T // tk, 1, tk), cum_h.reshape(HB, T, 1))
            y = mm("fox_o_" + tag, o, w_o)
            s.update(proj=proj, fz=fz, cum_h=cum_h, o=o, lse=lse)
        s["y_a"] = y
        xc, h = post(tag + "a", xc, y, modv(i, 2), ln1_g[i][None], ln1_b[i][None], modv(i, 4), modv(i, 3))
        s["x_f"] = xc
        w1g, w3g, w2g = [w[:, None] for w in gathered("ffn%d" % i, xc)]
        a, b, gte = ffn_up("ffn_up_" + tag, h, w1g, w3g, 0)
        y = ffn_down("ffn_down_" + tag, gte, w2g, 0)
        s.update(h_f=h, a=a, b=b, gte=gte, y_f=y, w1g=w1g, w3g=w3g, w2g=w2g)
        nxt = min(i + 1, L - 1)
        xc, h = post(tag + "f", xc, y, modv(i, 5), ln2_g[i][None], ln2_b[i][None], modv(nxt, 1), modv(nxt, 0))
        saved.append(s)

    def loss_fn(yv, tv):
        e = yv - tv
        return e * (1.0 / D), jnp.zeros((1, LANE), F32) + _sum0(jnp.sum(e * e, axis=1, keepdims=True)) * (0.5 / D)

    dx, loss_part = rowwise("loss", loss_fn, [xc, tgt], [], [(D, F32)], [LANE])

    G = {n: [None] * W[n].shape[0] for n in _WEIGHTS if n != "ada_w"}
    pending, pair_open = [], []

    def scatter(kind, idx, tag, gs):
        handle = reduce_scatter_pair(tag, gs)
        pair_open.append((kind, idx, tag, handle))
        return handle[3][0, 0]

    def scatter_on(after):
        pkind, pidx, ptag, phandle = pair_open.pop()
        chips = reduce_scatter_chips(ptag, phandle, core, after)
        pending.append((pkind, pidx, chips))
        return chips[3][0, 0]
    dmod = [[None] * 6 for _ in range(L)]
    for i in reversed(range(L)):
        j = i // 2
        tag = "l%d" % i
        s = saved[i]
        dxa, dy, G["ln2_g"][i], G["ln2_b"][i], dmod[i][5] = post_bwd(tag + "f", dx, s["x_f"], s["y_f"], modv(i, 5), ln2_g[i][None])
        g_w2 = ffn_out_grad("ffn_w2_grad_" + tag, s["gte"], dy)
        da, db = ffn_down_bwd("ffn_down_bwd_" + tag, dy, s["w2g"], s["a"], s["b"], 0)
        order = scatter_on(da) if pair_open else 0.0
        g_w1 = ffn_in_grad("ffn_w1_grad_" + tag, s["h_f"], da)
        g_w3 = ffn_in_grad("ffn_w3_grad_" + tag, s["h_f"], db)
        dh = ffn_dh("ffn_dh_" + tag, da, db, s["w1g"], s["w3g"], 0)
        order = order + scatter("ffn", i, "ffn_" + tag, [g_w1, g_w3, g_w2])
        if i == 0:
            order = order + scatter_on(dh)
        dx, dmod[i][4], dmod[i][3] = pre_bwd(tag + "f", dxa, dh, s["x_f"], modv(i, 4) + order)
        dxa, dy, G["ln1_g"][i], G["ln1_b"][i], dmod[i][2] = post_bwd(tag + "a", dx, s["x_a"], s["y_a"], modv(i, 2), ln1_g[i][None])
        if i % 2 == 0:
            g_o = mm("mla_o_grad_" + tag, s["o"], dy, ta=True, out_dtype=BF)
            do = mm("mla_do_" + tag, dy, s["w_o"], tb=True, out_dtype=BF)
            delta = attn_delta("mla_delta_" + tag, do, s["o"], HA, tk)
            dqn_, dkn_, dv_, dqpe_h, dkpe = attn_bwd("mla_attn_bwd_" + tag, "mla", mla_scale, HA, T, s["qf"], 0, s["kv"], 0,
                                                     s["kv"], HA, s["qpe_h"], s["kpe"], do, s["lse"], delta)
            order = scatter_on(dkpe) if pair_open else 0.0
            dkv = jnp.concatenate([dkn_, dv_], axis=1)
            g_kv = mm("mla_ukv_grad_" + tag, s["ckv"], dkv, ta=True, out_dtype=BF)
            dckv = mm("mla_dckv_" + tag, dkv, s["w_ukv"], tb=True)

            def qrope_bwd(dv, cq, sq, p):
                dv = dv * mla_scale
                return dv * cq - _rot(dv * sq, p)

            dqpe = rowwise("mla_qrope_bwd_" + tag, qrope_bwd, [jnp.moveaxis(dqpe_h, 0, 1).reshape(T, HA * ROPE), cos_q, sin_q],
                           [rot_q], [(HA * ROPE, BF)])[0]
            dq = jnp.concatenate([dqn_, dqpe], axis=1)
            g_uq = mm("mla_uq_grad_" + tag, s["qn"], dq, ta=True, out_dtype=BF)
            g_uq = jnp.concatenate([g_uq[:, :HA * NOPE].reshape(QL, HA, NOPE), g_uq[:, HA * NOPE:].reshape(QL, HA, ROPE)],
                                   axis=-1).reshape(QL, N_DEV, n_uq)
            dqn = mm("mla_dqn_" + tag, dq, s["w_uq"], tb=True)

            def mid_bwd(latv, dqv, dcv, dkp, ck, sk, qg, kg, p):
                ql, kl = latv[:, :QL], latv[:, QL:QL + KVL]
                rq = lax.rsqrt(_mean1(ql * ql) + RMS_EPS)
                rk = lax.rsqrt(_mean1(kl * kl) + RMS_EPS)
                uq, uk = dqv * qg, dcv * kg
                dql = rq * uq - ql * (rq * rq * rq) * _mean1(uq * ql)
                dkl = rk * uk - kl * (rk * rk * rk) * _mean1(uk * kl)
                return dql, dkl, dkp * ck - _rot(dkp * sk, p), _sum0(dqv * ql * rq), _sum0(dcv * kl * rk)

            dql, dkl, dkp, G["mla_q_norm"][j], G["mla_kv_norm"][j] = rowwise(
                "mla_mid_bwd_" + tag, mid_bwd, [s["lat"], dqn, dckv, dkpe, cos_k, sin_k],
                [mla_q_norm[j][None], mla_kv_norm[j][None], rot_k], [(QL, BF), (KVL, BF), (ROPE, BF)], [QL, KVL])
            dlat = jnp.concatenate([dql, dkl, dkp], axis=1)
            g_dn = mm("mla_down_grad_" + tag, s["h_a"], dlat, ta=True, out_dtype=BF)
            dh = mm("mla_dh_" + tag, dlat, s["w_down"], tb=True)
            g_kv = jnp.moveaxis(g_kv.reshape(KVL, 2, N_DEV, n_uk), 2, 0).reshape(N_DEV, KVL, 2 * n_uk)
            order = order + scatter("mla", j, "mla_" + tag, [
                g_dn.reshape(N_DEV, k_loc, -1), jnp.moveaxis(g_uq, 1, 0), g_kv, g_o.reshape(N_DEV, k_loc, D)])
        else:
            g_o = mm("fox_o_grad_" + tag, s["o"], dy, ta=True, out_dtype=BF)
            do = mm("fox_do_" + tag, dy, s["w_o"], tb=True, out_dtype=BF)
            cum_h = s["cum_h"]
            delta = attn_delta("fox_delta_" + tag, do, s["o"], HB, tk)
            dq_, dk_, dv_, dcq, dck = attn_bwd("fox_attn_bwd_" + tag, "fox", fox_scale, HB, T, s["proj"], 0, s["proj"], HB,
                                               s["proj"], 2 * HB, cum_h.reshape(HB, T // tk, 1, tk), cum_h.reshape(HB, T, 1),
                                               do, s["lse"], delta)
            order = scatter_on(dck) if pair_open else 0.0
            dcum = _pad_cols((dcq.reshape(HB, T) + dck.reshape(HB, T)).T, LANE)
            df, db_f = fox_cum_bwd("fox_cum_bwd_" + tag, s["fz"], b_f[j][None], dcum)
            G["fox_b_f"][j] = db_f[:, :HB]
            dproj = jnp.concatenate([dq_, dk_, dv_, df], axis=1)
            g_in = mm("fox_in_grad_" + tag, s["h_a"], dproj, ta=True, out_dtype=BF, bn=896)[:, :N_DEV * n_in]
            dh = mm("fox_dh_" + tag, dproj, s["w_in"], tb=True, bk=896)
            g_in = jnp.pad(jnp.moveaxis(g_in.reshape(D, N_DEV, n_in), 1, 0), ((0, 0), (0, 0), (0, n_in_pad - n_in)))
            order = order + scatter("fox", j, "fox_" + tag, [g_in, g_o.reshape(N_DEV, k_loc, D)])
        dx, dmod[i][1], dmod[i][0] = pre_bwd(tag + "a", dxa, dh, s["x_a"], modv(i, 1) + order)

    scatter_on(dx)
    for kind, idx, handle in pending:
        res = reduce_scatter_finish("%s_%d" % (kind, idx), handle, chip, dx)
        if kind == "ffn":
            G["ffn_w1"][idx], G["ffn_w3"][idx], G["ffn_w2"][idx] = res[0][:, :f_loc], res[1][:, :f_loc], res[2][:f_loc]
        elif kind == "mla":
            G["mla_w_down"][idx], G["mla_w_uq"][idx], G["mla_w_o"][idx] = res[0], res[1], res[3]
            G["mla_w_uk"][idx], G["mla_w_uv"][idx] = res[2][:, :n_uk], res[2][:, n_uk:]
        else:
            G["fox_w_in"][idx], G["fox_w_o"][idx] = res[0][:, :n_in], res[1]

    dmod_mine = jnp.concatenate([jnp.concatenate(r, axis=1) for r in dmod], axis=0)
    small_parts = [dmod_mine] + [jnp.concatenate(G[n], axis=0) for n in _SMALL[1:]] + [loss_part]
    small = _small_pack(small_parts, 512)
    small_all = all_gather("gather_small", [small], True)[0].reshape(N_DEV, -1, 512)
    small_sum = sum_devices("sum_small", small_all)
    sums, off = [], 0
    for p in small_parts:
        nrow = _round_up(p.size, 512) // 512
        sums.append(small_sum[off:off + nrow].reshape(-1)[:p.size].reshape(p.shape))
        off += nrow
    grads = {n: gsum for n, gsum in zip(_SMALL, sums[:-1])}
    loss = sums[-1][0, 0]
    nrow = L * 6 * D // 512
    dmod_all = small_all[:, :nrow].reshape(N_DEV, L, 6 * D)
    dmod_cols = jnp.moveaxis(lax.dynamic_slice_in_dim(dmod_all, me * n_ada, n_ada, axis=2), 0, 1)

    for n in ("ffn_w1", "ffn_w3", "ffn_w2", "mla_w_down", "mla_w_uq", "mla_w_uk", "mla_w_uv", "mla_w_o", "fox_w_in", "fox_w_o"):
        grads[n] = jnp.stack(G[n])

    delta, new_m, new_v = {}, {}, {}
    grads["ada_w"], delta["ada_w"], new_m["ada_w"], new_v["ada_w"] = ada_bwd_adamw(
        "ada_grad_adamw", c_all, dmod_cols, ada_w, m_ada_w, v_ada_w)
    for n in _WEIGHTS:
        if n != "ada_w":
            delta[n], new_m[n], new_v[n] = adamw("adamw_" + n, W[n], grads[n], M[n], V[n])

    return (loss, dx[None], *[grads[n] for n in _WEIGHTS], *[delta[n] for n in _WEIGHTS],
            *[new_m[n] for n in _WEIGHTS], *[new_v[n] for n in _WEIGHTS])
```

```python
import math

import numpy as np
import jax
import jax.numpy as jnp
from jax import lax
from jax.experimental import pallas as pl
from jax.experimental.pallas import tpu as pltpu

F32 = jnp.float32
BF = jnp.bfloat16
N_DEV = 8

CHUNK_SHIFT = 6
NOPE = 128
ROPE = 64
HEAD = 128
ROPE_THETA = 10000.0
LN_EPS = 1e-5
RMS_EPS = 1e-6
ADAM_LR = 0.001
ADAM_B1 = 0.9
ADAM_B2 = 0.999
ADAM_EPS = 1e-08
ADAM_WD = 0.01
ADAM_STEP = 10

VMEM_LIMIT_BYTES = 48 * 1024 * 1024
LANE = 128
NEG = -1e30


def _cp(*sem):
    return pltpu.CompilerParams(dimension_semantics=sem, vmem_limit_bytes=VMEM_LIMIT_BYTES)


def _pick(dim, pref):
    if dim <= pref:
        return dim
    for t in range(pref - pref % LANE, LANE - 1, -LANE):
        if dim % t == 0:
            return t
    return dim


def _round_up(v, m):
    return (v + m - 1) // m * m


def _sum0(v):
    return jnp.sum(v, axis=0, keepdims=True)


def _mean1(v):
    return jnp.mean(v, axis=-1, keepdims=True)


_NN = (((1,), (0,)), ((), ()))
_NT = (((1,), (1,)), ((), ()))
_TN = (((0,), (0,)), ((), ()))


def _mm_call(name, grid, ins, in_specs, dns, out_shape, out_spec, add=None, add_spec=None):
    npairs = len(dns)
    nk = grid[-1]
    has_add = add is not None

    def body(*refs):
        add_ref = refs[2 * npairs] if has_add else None
        o_ref = refs[2 * npairs + (1 if has_add else 0)]
        r = None
        for p in range(npairs):
            t = lax.dot_general(refs[2 * p][...].astype(BF), refs[2 * p + 1][...].astype(BF), dns[p], preferred_element_type=F32)
            r = t if r is None else r + t

        def fin(val):
            if has_add:
                val = val + add_ref[...]
            o_ref[...] = val.astype(o_ref.dtype)

        if nk == 1:
            fin(r)
        else:
            acc = refs[-1]
            k = pl.program_id(len(grid) - 1)

            @pl.when(k == 0)
            def _():
                acc[...] = r

            @pl.when(k > 0)
            def _():
                acc[...] += r

            @pl.when(k == nk - 1)
            def _():
                fin(acc[...])

    acc_shape = tuple(b for b in out_spec.block_shape if b is not None)
    args, specs = list(ins), list(in_specs)
    if has_add:
        args.append(add)
        specs.append(add_spec)
    return pl.pallas_call(
        body, name=name, grid=grid, in_specs=specs, out_specs=out_spec, out_shape=out_shape,
        scratch_shapes=[pltpu.VMEM(acc_shape, F32)] if nk > 1 else [],
        compiler_params=_cp(*(["parallel"] * (len(grid) - 1) + ["arbitrary"])),
    )(*args)


def mm(name, a, b, ta=False, tb=False, out_dtype=F32, add=None, bm=1024, bn=1024, bk=2048):
    M, K = (a.shape[1], a.shape[0]) if ta else a.shape
    N = b.shape[0] if tb else b.shape[1]
    bm, bn, bk = _pick(M, bm), _pick(N, bn), _pick(K, bk)
    a_spec = pl.BlockSpec((bk, bm), lambda i, j, k: (k, i)) if ta else pl.BlockSpec((bm, bk), lambda i, j, k: (i, k))
    b_spec = pl.BlockSpec((bn, bk), lambda i, j, k: (j, k)) if tb else pl.BlockSpec((bk, bn), lambda i, j, k: (k, j))
    dn = (((0 if ta else 1,), (1 if tb else 0,)), ((), ()))
    tile = pl.BlockSpec((bm, bn), lambda i, j, k: (i, j))
    return _mm_call(name, (M // bm, N // bn, K // bk), [a, b], [a_spec, b_spec], [dn],
                    jax.ShapeDtypeStruct((M, N), out_dtype), tile, add, tile)


def rowwise(name, fn, rows, vecs, out_rows, out_accs=(), tm=256):
    T = rows[0].shape[0]
    tm = min(tm, T)
    nr, nv, no = len(rows), len(vecs), len(out_rows)

    def body(*refs):
        outs = fn(*[r[...] for r in refs[:nr + nv]])
        if not isinstance(outs, (tuple, list)):
            outs = (outs,)
        o_refs = refs[nr + nv:nr + nv + no]
        a_refs = refs[nr + nv + no:]
        for r, o in zip(o_refs, outs[:no]):
            r[...] = o.astype(r.dtype)
        if a_refs:
            @pl.when(pl.program_id(0) == 0)
            def _():
                for r in a_refs:
                    r[...] = jnp.zeros_like(r)

            for r, o in zip(a_refs, outs[no:]):
                r[...] += o

    in_specs = [pl.BlockSpec((tm, r.shape[1]), lambda i: (i, 0)) for r in rows]
    in_specs += [pl.BlockSpec(v.shape, lambda i: (0, 0)) for v in vecs]
    out_specs = [pl.BlockSpec((tm, w), lambda i: (i, 0)) for w, _ in out_rows]
    out_specs += [pl.BlockSpec((1, w), lambda i: (0, 0)) for w in out_accs]
    out_shape = [jax.ShapeDtypeStruct((T, w), dt) for w, dt in out_rows]
    out_shape += [jax.ShapeDtypeStruct((1, w), F32) for w in out_accs]
    return pl.pallas_call(
        body, name=name, grid=(T // tm,), in_specs=in_specs, out_specs=out_specs, out_shape=out_shape,
        compiler_params=_cp("arbitrary"),
    )(*rows, *vecs)


def _rot(v, p):
    hi = v.astype(BF)
    lo = (v - hi.astype(F32)).astype(BF)
    return jnp.dot(hi, p, preferred_element_type=F32) + jnp.dot(lo, p, preferred_element_type=F32)


def _pair_rotation(width):
    p = np.zeros((width, width), np.float32)
    idx = np.arange(0, width, 2)
    p[idx + 1, idx] = -1.0
    p[idx, idx + 1] = 1.0
    return jnp.asarray(p, BF)


def _ln_stats(alpha, x, y, g):
    z = alpha * x + (1.0 + g) * y
    zc = z - _mean1(z)
    rstd = lax.rsqrt(_mean1(zc * zc) + LN_EPS)
    return zc * rstd, rstd


ATT_G_FWD = 4
ATT_G_BWD = 2
ATT_TILE = 256


def _visible(mode, ks, qs, t):
    krow = ks + lax.broadcasted_iota(jnp.int32, (t, t), 0)
    qcol = qs + lax.broadcasted_iota(jnp.int32, (t, t), 1)
    if mode == "mla":
        return lax.shift_right_logical(qcol, CHUNK_SHIFT) >= lax.shift_right_logical(krow, CHUNK_SHIFT)
    return qcol >= krow


def _scores(mode, k, q, e_k, e_q, ks, qs, t, masked):
    s = lax.dot_general(k, q, _NT, preferred_element_type=F32)
    if mode == "mla":
        s = s + lax.dot_general(e_k, e_q, _NT, preferred_element_type=F32)
    else:
        s = s + e_q - e_k
    if masked:
        s = jnp.where(_visible(mode, ks, qs, t), s, NEG)
    return s


def _attn_specs(mode, G, T, t, q_off, k_off, v_off):
    def col(off):
        return pl.BlockSpec((T, G * HEAD), lambda h: (0, off // G + h))

    stat = pl.BlockSpec((G, T // t, 1, t), lambda h: (h, 0, 0, 0))
    if mode == "mla":
        e_specs = [pl.BlockSpec((G, T, ROPE), lambda h: (h, 0, 0)), pl.BlockSpec((T, ROPE), lambda h: (0, 0))]
    else:
        e_specs = [stat, pl.BlockSpec((G, T, 1), lambda h: (h, 0, 0))]
    return [col(q_off), col(k_off), col(v_off)], e_specs, stat


def _head(ref, rows, g):
    return ref[rows, g * HEAD:(g + 1) * HEAD]


def attn_fwd(name, mode, scale, H, T, q_src, q_off, k_src, k_off, v_src, v_off, e_q, e_k):
    G, t = ATT_G_FWD, min(ATT_TILE, T)
    nq = T // t
    mla = mode == "mla"

    def body(q_ref, k_ref, v_ref, eq_ref, ek_ref, o_ref, lse_ref, acc_ref):
        def q_loop(qi, _):
            qs = pl.multiple_of(qi * t, t)
            qrows = pl.ds(qs, t)
            qh = [(_head(q_ref, qrows, g).astype(F32) * scale).astype(BF) for g in range(G)]
            eqh = [eq_ref[g, qrows, :] if mla else eq_ref[g, qi] for g in range(G)]
            acc_ref[...] = jnp.zeros_like(acc_ref)

            def step(j, carry, masked):
                ks = pl.multiple_of(j * t, t)
                krows = pl.ds(ks, t)
                out, ps, corrs = [], [], []
                ss = [_scores(mode, _head(k_ref, krows, g).astype(BF), qh[g], ek_ref[krows, :] if mla else ek_ref[g, krows, :],
                              eqh[g], ks, qs, t, masked) for g in range(G)]
                for g in range(G):
                    m, l = carry[g]
                    m_new = jnp.maximum(m, jnp.max(ss[g], axis=0, keepdims=True))
                    p = jnp.exp(ss[g] - m_new)
                    corr = jnp.exp(m - m_new)
                    out.append((m_new, corr * l + jnp.sum(p, axis=0, keepdims=True)))
                    ps.append(p.astype(BF))
                    corrs.append(corr)
                pvs = [lax.dot_general(_head(v_ref, krows, g).astype(BF), ps[g], _TN, preferred_element_type=F32) for g in range(G)]
                for g in range(G):
                    acc_ref[g] = corrs[g] * acc_ref[g] + pvs[g]
                return tuple(out)

            init = tuple((jnp.full((1, t), NEG, F32), jnp.zeros((1, t), F32)) for _ in range(G))
            carry = lax.fori_loop(0, qi, lambda j, cr: step(j, cr, False), init)
            carry = step(qi, carry, True)
            for g in range(G):
                m, l = carry[g]
                o_ref[qrows, g * HEAD:(g + 1) * HEAD] = (acc_ref[g] / l).T.astype(o_ref.dtype)
                lse_ref[g, qi] = m + jnp.log(l)
            return 0

        lax.fori_loop(0, nq, q_loop, 0)

    cols, e_specs, stat = _attn_specs(mode, G, T, t, q_off, k_off, v_off)
    return pl.pallas_call(
        body, name=name, grid=(H // G,), in_specs=cols + e_specs,
        out_specs=[pl.BlockSpec((T, G * HEAD), lambda h: (0, h)), stat],
        out_shape=[jax.ShapeDtypeStruct((T, H * HEAD), BF), jax.ShapeDtypeStruct((H, nq, 1, t), F32)],
        scratch_shapes=[pltpu.VMEM((G, HEAD, t), F32)],
        compiler_params=_cp("parallel"),
    )(q_src, k_src, v_src, e_q, e_k)


def attn_bwd(name, mode, scale, H, T, q_src, q_off, k_src, k_off, v_src, v_off, e_q, e_k, do, lse, delta):
    G, t = ATT_G_BWD, min(ATT_TILE, T)
    nq = T // t
    mla = mode == "mla"

    def body(q_ref, k_ref, v_ref, eq_ref, ek_ref, do_ref, lse_ref, dl_ref,
             dq_ref, dk_ref, dv_ref, deq_ref, dek_ref, dk_acc, dv_acc, dq_acc, deq_acc):
        dk_acc[...] = jnp.zeros_like(dk_acc)
        dv_acc[...] = jnp.zeros_like(dv_acc)
        if mla:
            @pl.when(pl.program_id(0) == 0)
            def _():
                dek_ref[...] = jnp.zeros_like(dek_ref)
        else:
            dek_ref[...] = jnp.zeros_like(dek_ref)

        def q_loop(qi, _):
            qs = pl.multiple_of(qi * t, t)
            qrows = pl.ds(qs, t)
            qh = [(_head(q_ref, qrows, g).astype(F32) * scale).astype(BF) for g in range(G)]
            eqh = [eq_ref[g, qrows, :] if mla else eq_ref[g, qi] for g in range(G)]
            doh = [_head(do_ref, qrows, g) for g in range(G)]
            lse_q = [lse_ref[g, qi] for g in range(G)]
            dl_q = [dl_ref[g, qi] for g in range(G)]
            dq_acc[...] = jnp.zeros_like(dq_acc)
            deq_acc[...] = jnp.zeros_like(deq_acc)

            def step(j, carry, masked):
                ks = pl.multiple_of(j * t, t)
                krows = pl.ds(ks, t)
                kh = [_head(k_ref, krows, g).astype(BF) for g in range(G)]
                ekh = [ek_ref[krows, :] if mla else ek_ref[g, krows, :] for g in range(G)]
                ss = [_scores(mode, kh[g], qh[g], ekh[g], eqh[g], ks, qs, t, masked) for g in range(G)]
                dps = [lax.dot_general(_head(v_ref, krows, g).astype(BF), doh[g], _NT, preferred_element_type=F32) for g in range(G)]
                pbs, dss, dsbs = [], [], []
                for g in range(G):
                    p = jnp.exp(ss[g] - lse_q[g])
                    ds = p * (dps[g] - dl_q[g])
                    pbs.append(p.astype(BF))
                    dss.append(ds)
                    dsbs.append(ds.astype(BF))
                for g in range(G):
                    dv_acc[g, krows, :] += jnp.dot(pbs[g], doh[g], preferred_element_type=F32)
                    dk_acc[g, krows, :] += jnp.dot(dsbs[g], qh[g], preferred_element_type=F32)
                    dq_acc[g] += lax.dot_general(dsbs[g], kh[g], _TN, preferred_element_type=F32)
                    if mla:
                        deq_acc[g] += lax.dot_general(dsbs[g], ekh[g], _TN, preferred_element_type=F32)
                        dek_ref[krows, :] += jnp.dot(dsbs[g], eqh[g], preferred_element_type=F32)
                    else:
                        deq_acc[g] += jnp.sum(dss[g], axis=0, keepdims=True)
                        dek_ref[g, krows, :] -= jnp.sum(dss[g], axis=1, keepdims=True)
                return carry

            lax.fori_loop(0, qi, lambda j, cr: step(j, cr, False), 0)
            step(qi, 0, True)
            for g in range(G):
                dq_ref[qrows, g * HEAD:(g + 1) * HEAD] = (dq_acc[g] * scale).astype(dq_ref.dtype)
                if mla:
                    deq_ref[g, qrows, :] = deq_acc[g]
                else:
                    deq_ref[g, qi] = deq_acc[g]
            return 0

        lax.fori_loop(0, nq, q_loop, 0)
        for g in range(G):
            dk_ref[:, g * HEAD:(g + 1) * HEAD] = dk_acc[g].astype(dk_ref.dtype)
            dv_ref[:, g * HEAD:(g + 1) * HEAD] = dv_acc[g].astype(dv_ref.dtype)

    cols, e_specs, stat = _attn_specs(mode, G, T, t, q_off, k_off, v_off)
    heads = pl.BlockSpec((T, G * HEAD), lambda h: (0, h))
    if mla:
        de_shapes = [jax.ShapeDtypeStruct((H, T, ROPE), F32), jax.ShapeDtypeStruct((T, ROPE), F32)]
        deq_scratch = pltpu.VMEM((G, t, ROPE), F32)
    else:
        de_shapes = [jax.ShapeDtypeStruct((H, nq, 1, t), F32), jax.ShapeDtypeStruct((H, T, 1), F32)]
        deq_scratch = pltpu.VMEM((G, 1, t), F32)
    return pl.pallas_call(
        body, name=name, grid=(H // G,),
        in_specs=cols + e_specs + [heads, stat, stat],
        out_specs=[heads, heads, heads] + e_specs,
        out_shape=[jax.ShapeDtypeStruct((T, H * HEAD), BF)] * 3 + de_shapes,
        scratch_shapes=[pltpu.VMEM((G, T, HEAD), F32), pltpu.VMEM((G, T, HEAD), F32), pltpu.VMEM((G, t, HEAD), F32), deq_scratch],
        compiler_params=_cp("arbitrary"),
    )(q_src, k_src, v_src, e_q, e_k, do, lse, delta)


def attn_delta(name, do, o, H, t):
    T, W = do.shape
    ind = np.zeros((W, LANE), np.float32)
    ind[np.arange(W), np.arange(W) // HEAD] = 1.0

    def f(dv, ov, e):
        return _rot(dv.astype(F32) * ov.astype(F32), e)

    d = rowwise(name, f, [do, o], [jnp.asarray(ind, BF)], [(LANE, F32)])[0]
    return d[:, :H].T.reshape(H, T // t, 1, t)


def fox_cum(name, f, b):
    T = f.shape[0]

    def body(f_ref, b_ref, cum_ref):
        z = f_ref[...] + b_ref[...]
        v = jnp.minimum(z, 0.0) - jnp.log(1.0 + jnp.exp(-jnp.abs(z)))
        row = lax.broadcasted_iota(jnp.int32, v.shape, 0)
        s = 1
        while s < T:
            v = v + jnp.where(row >= s, pltpu.roll(v, s, 0), 0.0)
            s *= 2
        cum_ref[...] = v

    return pl.pallas_call(body, name=name, out_shape=jax.ShapeDtypeStruct(f.shape, F32),
                          compiler_params=pltpu.CompilerParams(vmem_limit_bytes=VMEM_LIMIT_BYTES))(f, b)


def fox_cum_bwd(name, f, b, dcum):
    T = f.shape[0]

    def body(f_ref, b_ref, dc_ref, df_ref, db_ref):
        v = dc_ref[...]
        row = lax.broadcasted_iota(jnp.int32, v.shape, 0)
        s = 1
        while s < T:
            v = v + jnp.where(row < T - s, pltpu.roll(v, T - s, 0), 0.0)
            s *= 2
        z = f_ref[...] + b_ref[...]
        df = v / (1.0 + jnp.exp(z))
        df_ref[...] = df.astype(df_ref.dtype)
        db_ref[...] = _sum0(df)

    return pl.pallas_call(body, name=name,
                          out_shape=[jax.ShapeDtypeStruct(f.shape, BF), jax.ShapeDtypeStruct((1, f.shape[1]), F32)],
                          compiler_params=pltpu.CompilerParams(vmem_limit_bytes=VMEM_LIMIT_BYTES))(f, b, dcum)


def ffn_up(name, h, w1g, w3g, l):
    T, D = h.shape
    FP = w1g.shape[3]
    bm = _pick(T, 1024)

    def body(h_ref, w1_ref, w3_ref, a_ref, b_ref, g_ref):
        hv = h_ref[...]
        a = jnp.dot(hv, w1_ref[...], preferred_element_type=F32)
        b = jnp.dot(hv, w3_ref[...], preferred_element_type=F32)
        a_ref[...] = a.astype(BF)
        b_ref[...] = b.astype(BF)
        g_ref[...] = (a * (1.0 / (1.0 + jnp.exp(-a))) * b).astype(BF)

    w_spec = pl.BlockSpec((None, None, D, FP), lambda i, d: (d, l, 0, 0))
    tile = pl.BlockSpec((None, bm, FP), lambda i, d: (d, i, 0))
    return pl.pallas_call(
        body, name=name, grid=(T // bm, N_DEV), in_specs=[pl.BlockSpec((bm, D), lambda i, d: (i, 0)), w_spec, w_spec],
        out_specs=[tile, tile, tile], out_shape=[jax.ShapeDtypeStruct((N_DEV, T, FP), BF)] * 3,
        compiler_params=_cp("parallel", "parallel"),
    )(h, w1g, w3g)


def ffn_down(name, g, w2g, l):
    _, T, FP = g.shape
    D = w2g.shape[3]
    bm, bn = _pick(T, 1024), _pick(D, 1024)
    return _mm_call(
        name, (T // bm, D // bn, N_DEV), [g, w2g],
        [pl.BlockSpec((None, bm, FP), lambda i, j, d: (d, i, 0)), pl.BlockSpec((None, None, FP, bn), lambda i, j, d: (d, l, 0, j))],
        [_NN], jax.ShapeDtypeStruct((T, D), F32), pl.BlockSpec((bm, bn), lambda i, j, d: (i, j)))


def ffn_down_bwd(name, dy, w2g, a, b, l):
    T, D = dy.shape
    FP = w2g.shape[2]
    bm = _pick(T, 1024)

    def body(dy_ref, w2_ref, a_ref, b_ref, da_ref, db_ref):
        dg = lax.dot_general(dy_ref[...], w2_ref[...], _NT, preferred_element_type=F32)
        av = a_ref[...].astype(F32)
        bv = b_ref[...].astype(F32)
        sig = 1.0 / (1.0 + jnp.exp(-av))
        da_ref[...] = (dg * bv * sig * (1.0 + av * (1.0 - sig))).astype(BF)
        db_ref[...] = (dg * av * sig).astype(BF)

    tile = pl.BlockSpec((None, bm, FP), lambda i, d: (d, i, 0))
    return pl.pallas_call(
        body, name=name, grid=(T // bm, N_DEV),
        in_specs=[pl.BlockSpec((bm, D), lambda i, d: (i, 0)), pl.BlockSpec((None, None, FP, D), lambda i, d: (d, l, 0, 0)), tile, tile],
        out_specs=[tile, tile], out_shape=[jax.ShapeDtypeStruct((N_DEV, T, FP), BF)] * 2,
        compiler_params=_cp("parallel", "parallel"),
    )(dy, w2g, a, b)


def ffn_dh(name, da, db, w1g, w3g, l):
    _, T, FP = da.shape
    D = w1g.shape[2]
    bm, bn = _pick(T, 1024), _pick(D, 1024)
    act = pl.BlockSpec((None, bm, FP), lambda i, j, d: (d, i, 0))
    wgt = pl.BlockSpec((None, None, bn, FP), lambda i, j, d: (d, l, j, 0))
    return _mm_call(name, (T // bm, D // bn, N_DEV), [da, w1g, db, w3g], [act, wgt, act, wgt], [_NT, _NT],
                    jax.ShapeDtypeStruct((T, D), F32), pl.BlockSpec((bm, bn), lambda i, j, d: (i, j)))


def ffn_in_grad(name, h, dact):
    T, D = h.shape
    FP = dact.shape[2]
    bm = _pick(D, 1024)
    return _mm_call(
        name, (D // bm, N_DEV, 1), [h, dact],
        [pl.BlockSpec((T, bm), lambda i, d, k: (0, i)), pl.BlockSpec((None, T, FP), lambda i, d, k: (d, 0, 0))],
        [_TN], jax.ShapeDtypeStruct((N_DEV, D, FP), BF), pl.BlockSpec((None, bm, FP), lambda i, d, k: (d, i, 0)))


def ffn_out_grad(name, g, dy):
    _, T, FP = g.shape
    D = dy.shape[1]
    bn = _pick(D, 1024)
    return _mm_call(
        name, (N_DEV, D // bn, 1), [g, dy],
        [pl.BlockSpec((None, T, FP), lambda d, j, k: (d, 0, 0)), pl.BlockSpec((T, bn), lambda d, j, k: (0, j))],
        [_TN], jax.ShapeDtypeStruct((N_DEV, FP, D), BF), pl.BlockSpec((None, FP, bn), lambda d, j, k: (d, 0, j)))


def _adamw_math(w, g, m, v):
    m = ADAM_B1 * m + (1.0 - ADAM_B1) * g
    v = ADAM_B2 * v + (1.0 - ADAM_B2) * (g * g)
    m_hat = m / (1.0 - ADAM_B1 ** ADAM_STEP)
    v_hat = v / (1.0 - ADAM_B2 ** ADAM_STEP)
    delta = -ADAM_LR * (m_hat / (jnp.sqrt(v_hat) + ADAM_EPS) + ADAM_WD * w)
    return delta, m, v


def adamw(name, w, g, m, v):
    shape = w.shape
    n = shape[-1]
    rows = math.prod(shape[:-1])
    tr = rows if rows <= 512 else 256
    assert rows % tr == 0

    def body(w_ref, g_ref, m_ref, v_ref, d_ref, mo_ref, vo_ref):
        d, mn, vn = _adamw_math(w_ref[...], g_ref[...], m_ref[...], v_ref[...])
        d_ref[...] = d
        mo_ref[...] = mn
        vo_ref[...] = vn

    blk = pl.BlockSpec((tr, n), lambda i: (i, 0))
    outs = pl.pallas_call(
        body, name=name, grid=(rows // tr,), in_specs=[blk] * 4, out_specs=[blk] * 3,
        out_shape=[jax.ShapeDtypeStruct((rows, n), F32)] * 3, compiler_params=_cp("parallel"),
    )(*[t.reshape(rows, n) for t in (w, g, m, v)])
    return [t.reshape(shape) for t in outs]


def _silu(v):
    return v * (1.0 / (1.0 + jnp.exp(-v)))


def ada_fwd(name, c_all, ada_w, ada_b_cols):
    L, D, n = ada_w.shape
    bn = _pick(n, 512)

    def body(c_ref, w_ref, b_ref, o_ref):
        act = _silu(c_ref[...]).astype(BF)
        o_ref[...] = jnp.dot(act, w_ref[...].astype(BF), preferred_element_type=F32) + b_ref[...]

    return pl.pallas_call(
        body, name=name, grid=(L, n // bn),
        in_specs=[pl.BlockSpec((N_DEV, D), lambda l, j: (0, 0)), pl.BlockSpec((None, D, bn), lambda l, j: (l, 0, j)),
                  pl.BlockSpec((None, 1, bn), lambda l, j: (l, 0, j))],
        out_specs=pl.BlockSpec((None, N_DEV, bn), lambda l, j: (l, 0, j)),
        out_shape=jax.ShapeDtypeStruct((L, N_DEV, n), F32), compiler_params=_cp("parallel", "parallel"),
    )(c_all, ada_w, ada_b_cols.reshape(L, 1, n))


def ada_bwd_adamw(name, c_all, dmod_cols, w, m, v):
    L, D, n = w.shape
    tr = _pick(D, 256)

    def body(c_ref, dm_ref, w_ref, m_ref, v_ref, g_ref, d_ref, mo_ref, vo_ref):
        act = _silu(c_ref[...]).astype(BF)
        g = lax.dot_general(act, dm_ref[...].astype(BF), _TN, preferred_element_type=F32)
        d, mn, vn = _adamw_math(w_ref[...], g, m_ref[...], v_ref[...])
        g_ref[...] = g
        d_ref[...] = d
        mo_ref[...] = mn
        vo_ref[...] = vn

    blk = pl.BlockSpec((None, tr, n), lambda l, i: (l, i, 0))
    return pl.pallas_call(
        body, name=name, grid=(L, D // tr),
        in_specs=[pl.BlockSpec((N_DEV, tr), lambda l, i: (0, i)), pl.BlockSpec((None, N_DEV, n), lambda l, i: (l, 0, 0)),
                  blk, blk, blk],
        out_specs=[blk] * 4, out_shape=[jax.ShapeDtypeStruct((L, D, n), F32)] * 4,
        compiler_params=_cp("parallel", "parallel"),
    )(c_all, dmod_cols, w, m, v)


_MESH = pl.DeviceIdType.MESH
_ANY = pl.BlockSpec(memory_space=pl.ANY)
_CHIP_FLIPS = ((1, 0), (0, 1), (1, 1))


def _place():
    return lax.axis_index("x"), lax.axis_index("y"), lax.axis_index("c")


def _flip(v, f):
    return 1 - v if f else v


def all_gather(name, shards, in_vmem=False):
    nt = len(shards)

    def body(*refs):
        x_refs, out_refs = refs[:nt], refs[nt:2 * nt]
        send_sems, recv_sems, local_sems = refs[2 * nt:]
        x, y, c = _place()
        me, sibling = (x, y, c), (x, y, 1 - c)
        chips = [(_flip(x, fx), _flip(y, fy)) for fx, fy in _CHIP_FLIPS]

        def rows(t, px, py, pc):
            m_per = shards[t].shape[0]
            return out_refs[t].at[pl.ds((4 * px + 2 * py + pc) * m_per, m_per), :]

        def copy(t, k, block, to, src=None):
            return pltpu.make_async_remote_copy(
                src_ref=rows(t, *block) if src is None else src, dst_ref=rows(t, *block),
                send_sem=send_sems.at[7 * t + k], recv_sem=recv_sems.at[7 * t + k], device_id=to, device_id_type=_MESH)

        mine = [pltpu.make_async_copy(x_refs[t], rows(t, *me), local_sems.at[t]) for t in range(nt)]
        for cp in mine:
            cp.start()
        first = []
        for j, chip in enumerate(chips):
            first += [copy(t, 1 + j, me, (*chip, c), src=x_refs[t]) for t in range(nt)]
        first += [copy(t, 0, me, sibling, src=x_refs[t]) for t in range(nt)]
        for cp in first:
            cp.start()
        passed = []
        for j, chip in enumerate(chips):
            for t in range(nt):
                copy(t, 1 + j, (*chip, c), me).wait_recv()
                passed.append(copy(t, 4 + j, (*chip, c), sibling))
                passed[-1].start()
        for t in range(nt):
            copy(t, 0, sibling, me).wait_recv()
            for j, chip in enumerate(chips):
                copy(t, 4 + j, (*chip, 1 - c), me).wait_recv()
        for cp in first + passed:
            cp.wait_send()
        for cp in mine:
            cp.wait()

    space = pl.BlockSpec(memory_space=pltpu.VMEM) if in_vmem else _ANY
    return pl.pallas_call(
        body, name=name, out_shape=[jax.ShapeDtypeStruct((N_DEV * s.shape[0], s.shape[1]), s.dtype) for s in shards],
        in_specs=[space] * nt, out_specs=[space] * nt,
        scratch_shapes=[pltpu.SemaphoreType.DMA((7 * nt,)), pltpu.SemaphoreType.DMA((7 * nt,)), pltpu.SemaphoreType.DMA((nt,))],
    )(*shards)


_HBM = pl.BlockSpec(memory_space=pltpu.HBM)
_SEM = pl.BlockSpec(memory_space=pltpu.SEMAPHORE)
_EFFECT = pltpu.SideEffectType.DATAFLOW_SIDE_EFFECTING


def _in_hbm(a):
    return pltpu.with_memory_space_constraint(a, pltpu.HBM)


def _gather_first_copies(x_refs, land_refs, send_sems, recv_sems, local_sems):
    nt = len(x_refs)
    x, y, c = _place()
    me = (x, y, c)

    def rows(t, px, py, pc):
        m_per = x_refs[t].shape[0]
        return land_refs[t].at[pl.ds((4 * px + 2 * py + pc) * m_per, m_per), :]

    local = [pltpu.make_async_copy(x_refs[t], rows(t, *me), local_sems.at[t]) for t in range(nt)]
    remote = []
    peers = [(_flip(x, fx), _flip(y, fy), c) for fx, fy in _CHIP_FLIPS] + [(x, y, 1 - c)]
    for k, peer in enumerate(peers):
        remote += [(pltpu.make_async_remote_copy(
            src_ref=x_refs[t], dst_ref=rows(t, *me), send_sem=send_sems.at[4 * t + k], recv_sem=recv_sems.at[4 * t + k],
            device_id=peer, device_id_type=_MESH), rows(t, *peer), 4 * t + k) for t in range(nt)]
    return local, remote


def gather_start(name, shards, after):
    nt = len(shards)

    def body(*refs):
        x_refs, land_refs = refs[:nt], refs[nt:2 * nt]
        send_sems, recv_sems, local_sems = refs[2 * nt + 1:2 * nt + 4]
        token = refs[-1]
        local, remote = _gather_first_copies(x_refs, land_refs, send_sems, recv_sems, local_sems)
        for cp in local:
            cp.start()
        for cp, _, _ in remote:
            cp.start()
        token[...] = jnp.zeros_like(token)

    lands = [lax.empty((N_DEV * s.shape[0], s.shape[1]), s.dtype) for s in shards]
    outs = pl.pallas_call(
        body, name=name,
        out_shape=[pltpu.SemaphoreType.DMA((4 * nt,)), pltpu.SemaphoreType.DMA((4 * nt,)), pltpu.SemaphoreType.DMA((nt,))]
        + [pltpu.HBM(s.shape, s.dtype) for s in shards] + [pltpu.HBM(a.shape, a.dtype) for a in lands]
        + [jax.ShapeDtypeStruct((8, LANE), F32)],
        in_specs=[_HBM] * (2 * nt) + [_ANY], out_specs=[_SEM] * 3 + [_HBM] * (2 * nt) + [pl.BlockSpec(memory_space=pltpu.VMEM)],
        input_output_aliases={i: 3 + i for i in range(2 * nt)},
        compiler_params=pltpu.CompilerParams(has_side_effects=_EFFECT),
    )(*[_in_hbm(s) for s in shards], *[_in_hbm(a) for a in lands], after)
    return outs[:3], outs[3:3 + nt], outs[3 + nt:3 + 2 * nt], outs[-1]


def gather_wait(name, handle, after):
    sems, shards, lands, _ = handle
    nt = len(shards)

    def body(*refs):
        x_refs, land_refs = refs[:nt], refs[nt:2 * nt]
        send_sems, recv_sems, local_sems = refs[2 * nt:2 * nt + 3]
        local, remote = _gather_first_copies(x_refs, land_refs, send_sems, recv_sems, local_sems)
        for cp, landed, k in remote:
            cp.wait_send()
            pltpu.make_async_remote_copy(src_ref=landed, dst_ref=landed, send_sem=send_sems.at[k], recv_sem=recv_sems.at[k],
                                         device_id=_place(), device_id_type=_MESH).wait_recv()
        for cp in local:
            cp.wait()

    outs = pl.pallas_call(
        body, name=name, out_shape=[pltpu.HBM(s.shape, s.dtype) for s in shards] + [pltpu.HBM(a.shape, a.dtype) for a in lands],
        in_specs=[_HBM] * (2 * nt) + [_SEM] * 3 + [_ANY], out_specs=[_HBM] * (2 * nt),
        input_output_aliases={i: i for i in range(2 * nt)},
        compiler_params=pltpu.CompilerParams(has_side_effects=_EFFECT),
    )(*shards, *lands, *sems, after)
    return outs[nt:]


def gather_forward(name, lands):
    nt = len(lands)

    def body(*refs):
        land_refs = refs[:nt]
        send_sems, recv_sems = refs[2 * nt:]
        x, y, c = _place()
        copies = []
        for j, (fx, fy) in enumerate(_CHIP_FLIPS):
            px, py = _flip(x, fx), _flip(y, fy)
            for t in range(nt):
                m_per = land_refs[t].shape[0] // N_DEV
                mine = land_refs[t].at[pl.ds((4 * px + 2 * py + c) * m_per, m_per), :]
                theirs = land_refs[t].at[pl.ds((4 * px + 2 * py + 1 - c) * m_per, m_per), :]
                copies.append((pltpu.make_async_remote_copy(
                    src_ref=mine, dst_ref=mine, send_sem=send_sems.at[3 * t + j], recv_sem=recv_sems.at[3 * t + j],
                    device_id=(x, y, 1 - c), device_id_type=_MESH), theirs, 3 * t + j))
        for cp, _, _ in copies:
            cp.start()
        for cp, theirs, k in copies:
            cp.wait_send()
            pltpu.make_async_remote_copy(src_ref=theirs, dst_ref=theirs, send_sem=send_sems.at[k], recv_sem=recv_sems.at[k],
                                         device_id=(x, y, 1 - c), device_id_type=_MESH).wait_recv()

    return pl.pallas_call(
        body, name=name, out_shape=[jax.ShapeDtypeStruct(a.shape, a.dtype) for a in lands],
        in_specs=[_ANY] * nt, out_specs=[_ANY] * nt, input_output_aliases={i: i for i in range(nt)},
        scratch_shapes=[pltpu.SemaphoreType.DMA((3 * nt,)), pltpu.SemaphoreType.DMA((3 * nt,))],
    )(*lands)


def _chip_copies(p_refs, land_refs, send_sems, recv_sems):
    nt = len(p_refs)
    x, y, c = _place()
    copies = []
    for j, (fx, fy) in enumerate(_CHIP_FLIPS):
        px, py = _flip(x, fx), _flip(y, fy)
        copies += [(pltpu.make_async_remote_copy(
            src_ref=p_refs[t].at[2 * px + py], dst_ref=land_refs[t].at[j], send_sem=send_sems.at[3 * t + j],
            recv_sem=recv_sems.at[3 * t + j], device_id=(px, py, c), device_id_type=_MESH), land_refs[t].at[j], 3 * t + j)
            for t in range(nt)]
    return copies


def _pair_copies(g_refs, land_refs, send_sems, recv_sems):
    nt = len(g_refs)
    x, y, c = _place()
    return [(pltpu.make_async_remote_copy(
        src_ref=g_refs[t].at[2 * k + 1 - c], dst_ref=land_refs[t].at[k], send_sem=send_sems.at[4 * t + k],
        recv_sem=recv_sems.at[4 * t + k], device_id=(x, y, 1 - c), device_id_type=_MESH), land_refs[t].at[k], 4 * t + k)
        for t in range(nt) for k in range(4)]


def exchange_start(name, copies_fn, n_land, parts):
    nt = len(parts)

    def body(*refs):
        p_refs, land_refs = refs[:nt], refs[nt:2 * nt]
        send_sems, recv_sems = refs[2 * nt:2 * nt + 2]
        token = refs[-1]
        for cp, _, _ in copies_fn(p_refs, land_refs, send_sems, recv_sems):
            cp.start()
        token[...] = jnp.zeros_like(token)

    lands = [lax.empty((n_land,) + p.shape[1:], p.dtype) for p in parts]
    outs = pl.pallas_call(
        body, name=name,
        out_shape=[pltpu.SemaphoreType.DMA((n_land * nt,)), pltpu.SemaphoreType.DMA((n_land * nt,))]
        + [pltpu.HBM(p.shape, p.dtype) for p in parts] + [pltpu.HBM(a.shape, a.dtype) for a in lands]
        + [jax.ShapeDtypeStruct((8, LANE), F32)],
        in_specs=[_HBM] * (2 * nt), out_specs=[_SEM] * 2 + [_HBM] * (2 * nt) + [pl.BlockSpec(memory_space=pltpu.VMEM)],
        input_output_aliases={i: 2 + i for i in range(2 * nt)},
        compiler_params=pltpu.CompilerParams(has_side_effects=_EFFECT),
    )(*[_in_hbm(p) for p in parts], *[_in_hbm(a) for a in lands])
    return outs[:2], outs[2:2 + nt], outs[2 + nt:2 + 2 * nt], outs[-1]


def exchange_wait(name, copies_fn, handle, after):
    sems, parts, lands, _ = handle
    nt = len(parts)

    def body(*refs):
        p_refs, land_refs = refs[:nt], refs[nt:2 * nt]
        send_sems, recv_sems = refs[2 * nt:2 * nt + 2]
        for cp, landed, k in copies_fn(p_refs, land_refs, send_sems, recv_sems):
            cp.wait_send()
            pltpu.make_async_remote_copy(src_ref=landed, dst_ref=landed, send_sem=send_sems.at[k], recv_sem=recv_sems.at[k],
                                         device_id=_place(), device_id_type=_MESH).wait_recv()

    outs = pl.pallas_call(
        body, name=name, out_shape=[pltpu.HBM(p.shape, p.dtype) for p in parts] + [pltpu.HBM(a.shape, a.dtype) for a in lands],
        in_specs=[_HBM] * (2 * nt) + [_SEM] * 2 + [_ANY], out_specs=[_HBM] * (2 * nt),
        input_output_aliases={i: i for i in range(2 * nt)},
        compiler_params=pltpu.CompilerParams(has_side_effects=_EFFECT),
    )(*parts, *lands, *sems, after)
    return outs[:nt], outs[nt:]


def _row_tile(rows):
    return rows if rows <= 1024 else 1024


def pair_add(name, g, recv, core):
    _, R, C = g.shape
    tr = _row_tile(R)

    def body(core_ref, g_ref, r_ref, o_ref):
        o_ref[...] = (g_ref[...].astype(F32) + r_ref[...].astype(F32)).astype(o_ref.dtype)

    return pl.pallas_call(
        body, name=name, out_shape=jax.ShapeDtypeStruct((4, R, C), BF),
        grid_spec=pltpu.PrefetchScalarGridSpec(
            num_scalar_prefetch=1, grid=(4, R // tr),
            in_specs=[pl.BlockSpec((None, tr, C), lambda k, i, core_ref: (2 * k + core_ref[0], i, 0)),
                      pl.BlockSpec((None, tr, C), lambda k, i, core_ref: (k, i, 0))],
            out_specs=pl.BlockSpec((None, tr, C), lambda k, i, core_ref: (k, i, 0))),
        compiler_params=_cp("parallel", "parallel"),
    )(core, g, recv)


def chip_add(name, part, recv, chip):
    _, R, C = part.shape
    tr = _row_tile(R)

    def body(chip_ref, p_ref, r_ref, o_ref):
        o_ref[...] = ((p_ref[...].astype(F32) + r_ref[0].astype(F32)) + r_ref[1].astype(F32)) + r_ref[2].astype(F32)

    return pl.pallas_call(
        body, name=name, out_shape=jax.ShapeDtypeStruct((R, C), F32),
        grid_spec=pltpu.PrefetchScalarGridSpec(
            num_scalar_prefetch=1, grid=(R // tr,),
            in_specs=[pl.BlockSpec((None, tr, C), lambda i, chip_ref: (chip_ref[0], i, 0)),
                      pl.BlockSpec((3, tr, C), lambda i, chip_ref: (0, i, 0))],
            out_specs=pl.BlockSpec((tr, C), lambda i, chip_ref: (i, 0))),
        compiler_params=_cp("parallel"),
    )(chip, part, recv)


def reduce_scatter_pair(tag, gs):
    return exchange_start("pair_exchange_start_" + tag, _pair_copies, 4, gs)


def reduce_scatter_chips(tag, handle, core, after):
    gs, from_pair = exchange_wait("pair_exchange_wait_" + tag, _pair_copies, handle, after)
    parts = [pair_add("pair_add_%s_%d" % (tag, t), g, r, core) for t, (g, r) in enumerate(zip(gs, from_pair))]
    return exchange_start("chip_exchange_start_" + tag, _chip_copies, 3, parts)


def reduce_scatter_finish(tag, handle, chip, after):
    parts, landed = exchange_wait("chip_exchange_wait_" + tag, _chip_copies, handle, after)
    return [chip_add("chip_add_%s_%d" % (tag, t), p, r, chip) for t, (p, r) in enumerate(zip(parts, landed))]


def sum_devices(name, g):
    _, R, C = g.shape

    def body(g_ref, o_ref):
        acc = g_ref[0]
        for d in range(1, N_DEV):
            acc = acc + g_ref[d]
        o_ref[...] = acc

    return pl.pallas_call(body, name=name, out_shape=jax.ShapeDtypeStruct((R, C), F32),
                          compiler_params=pltpu.CompilerParams(vmem_limit_bytes=VMEM_LIMIT_BYTES))(g)


_WEIGHTS = ["ada_w", "ada_b", "ln1_g", "ln1_b", "ln2_g", "ln2_b", "ffn_w1", "ffn_w3", "ffn_w2", "mla_w_down", "mla_q_norm",
            "mla_w_uq", "mla_kv_norm", "mla_w_uk", "mla_w_uv", "mla_w_o", "fox_w_in", "fox_b_f", "fox_w_o"]
_SMALL = ["ada_b", "ln1_g", "ln1_b", "ln2_g", "ln2_b", "mla_q_norm", "mla_kv_norm", "fox_b_f"]


def _pad_cols(a, width):
    return jnp.pad(a, ((0, 0), (0, width - a.shape[1])))


def _small_pack(parts, cols):
    rows = [jnp.pad(p.reshape(-1), (0, _round_up(p.size, cols) - p.size)).reshape(-1, cols) for p in parts]
    out = jnp.concatenate(rows, axis=0)
    return jnp.pad(out, ((0, _round_up(out.shape[0], 8) - out.shape[0]), (0, 0)))


def kernel(x, c, positions, ada_w, ada_b, ln1_g, ln1_b, ln2_g, ln2_b, ffn_w1, ffn_w3, ffn_w2, mla_w_down, mla_q_norm, mla_w_uq, mla_kv_norm, mla_w_uk, mla_w_uv, mla_w_o, fox_w_in, fox_b_f, fox_w_o, loss_target, m_ada_w, m_ada_b, m_ln1_g, m_ln1_b, m_ln2_g, m_ln2_b, m_ffn_w1, m_ffn_w3, m_ffn_w2, m_mla_w_down, m_mla_q_norm, m_mla_w_uq, m_mla_kv_norm, m_mla_w_uk, m_mla_w_uv, m_mla_w_o, m_fox_w_in, m_fox_b_f, m_fox_w_o, v_ada_w, v_ada_b, v_ln1_g, v_ln1_b, v_ln2_g, v_ln2_b, v_ffn_w1, v_ffn_w3, v_ffn_w2, v_mla_w_down, v_mla_q_norm, v_mla_w_uq, v_mla_kv_norm, v_mla_w_uk, v_mla_w_uv, v_mla_w_o, v_fox_w_in, v_fox_b_f, v_fox_w_o):
    env = dict(locals())
    W = {n: env[n] for n in _WEIGHTS}
    M = {n: env["m_" + n] for n in _WEIGHTS}
    V = {n: env["v_" + n] for n in _WEIGHTS}

    T, D = x.shape[1], x.shape[2]
    L = ada_w.shape[0]
    QL, KVL = mla_q_norm.shape[1], mla_kv_norm.shape[1]
    HA = mla_w_uq.shape[2] * N_DEV // (NOPE + ROPE)
    HB = fox_b_f.shape[1]
    f_loc = ffn_w1.shape[2]
    FP = _round_up(f_loc, LANE)
    n_ada = ada_w.shape[2]
    n_in = fox_w_in.shape[2]
    n_in_pad = _round_up(n_in, LANE)
    k_loc = mla_w_down.shape[1]
    n_uq, n_uk = mla_w_uq.shape[2], mla_w_uk.shape[2]
    assert D == HB * HEAD == HA * HEAD and n_ada * N_DEV == 6 * D
    alpha = (2 * L) ** 0.25
    xi, yi, ci = _place()
    me = 4 * xi + 2 * yi + ci
    core = jnp.reshape(ci, (1,)).astype(jnp.int32)
    chip = jnp.reshape(2 * xi + yi, (1,)).astype(jnp.int32)

    x2 = x[0]
    tgt = loss_target[0]

    c_all = all_gather("gather_c", [jnp.pad(c, ((0, 7), (0, 0)))], True)[0].reshape(N_DEV, 8, D)[:, 0]
    ada_b_cols = lax.dynamic_slice_in_dim(ada_b, me * n_ada, n_ada, axis=1)
    mod_part = ada_fwd("ada_fwd", c_all, ada_w, ada_b_cols)
    mod_all = all_gather("gather_mod", [mod_part.reshape(L * N_DEV, n_ada)], True)[0].reshape(N_DEV, L, N_DEV, n_ada)
    mod = lax.dynamic_index_in_dim(mod_all, me, axis=2, keepdims=False)
    mod = jnp.moveaxis(mod, 0, 1).reshape(L, 6, D)

    fox_cols = 3 * D + LANE

    def bf(a, pad_rows=0, pad_cols=0):
        return jnp.pad(a.astype(BF), ((0, pad_rows), (0, pad_cols)))

    groups = []
    for i in range(L):
        j = i // 2
        if i % 2 == 0:
            groups.append(("mla%d" % i, [bf(mla_w_down[j]), bf(mla_w_uq[j]), bf(mla_w_uk[j]), bf(mla_w_uv[j]), bf(mla_w_o[j])]))
        else:
            groups.append(("fox%d" % i, [bf(fox_w_in[j], 0, n_in_pad - n_in), bf(fox_w_o[j])]))
        groups.append(("ffn%d" % i, [bf(ffn_w1[i], 0, FP - f_loc), bf(ffn_w3[i], 0, FP - f_loc), bf(ffn_w2[i], FP - f_loc, 0)]))
    handles = {}
    tok = mod
    for gname, shards in groups:
        handles[gname] = gather_start("gather_start_" + gname, shards, tok)
        tok = handles[gname][3]
    mod = mod + tok[0, 0]

    def gathered(gname, after):
        lands = gather_wait("gather_wait_" + gname, handles[gname], after)
        lands = gather_forward("gather_forward_" + gname, lands)
        return [a.reshape((N_DEV,) + s.shape) for a, s in zip(lands, dict(groups)[gname])]

    def cols_whole(g, keep=None):
        return jnp.moveaxis(g[:, :, :keep], 0, 1).reshape(g.shape[1], -1)

    def modv(i, k):
        return mod[i, k][None, :]

    b_f = _pad_cols(fox_b_f, LANE)

    inv_freq = ROPE_THETA ** (-jnp.arange(0, ROPE, 2, dtype=F32) / ROPE)
    ang = jnp.repeat(positions[0].astype(F32)[:, None] * inv_freq, 2, axis=1)
    cos_k, sin_k = jnp.cos(ang), jnp.sin(ang)
    cos_q, sin_q = jnp.tile(cos_k, (1, HA)), jnp.tile(sin_k, (1, HA))
    rot_k, rot_q = _pair_rotation(ROPE), _pair_rotation(HA * ROPE)
    tk = min(ATT_TILE, T)
    mla_scale = (NOPE + ROPE) ** -0.5
    fox_scale = HEAD ** -0.5

    def modulate(tag, xin, sc, sh):
        return rowwise("modulate_" + tag, lambda xv, scv, shv: xv * (1.0 + scv) + shv, [xin], [sc, sh], [(D, BF)])[0]

    def post(tag, xin, y, g, gam, bet, nsc, nsh):
        def f(xv, yv, gv, gamv, betv, scv, shv):
            xhat, _ = _ln_stats(alpha, xv, yv, gv)
            out = xhat * gamv + betv
            return out, out * (1.0 + scv) + shv
        return rowwise("post_" + tag, f, [xin, y], [g, gam, bet, nsc, nsh], [(D, F32), (D, BF)])

    def post_bwd(tag, dxo, xin, y, g, gam):
        def f(dv, xv, yv, gv, gamv):
            xhat, rstd = _ln_stats(alpha, xv, yv, gv)
            dxh = dv * gamv
            dz = rstd * (dxh - _mean1(dxh) - xhat * _mean1(dxh * xhat))
            return alpha * dz, (1.0 + gv) * dz, _sum0(dv * xhat), _sum0(dv), _sum0(dz * yv)
        return rowwise("post_bwd_" + tag, f, [dxo, xin, y], [g, gam], [(D, F32), (D, BF)], [D, D, D])

    def join_bwd(tag, dxa, dh, sc, xprev, yprev, g, gam, bet):
        def f(dxav, dhv, xpv, ypv, scv, gv, gamv, betv):
            xhat, rstd = _ln_stats(alpha, xpv, ypv, gv)
            dv = dxav + dhv * (1.0 + scv)
            dxh = dv * gamv
            dz = rstd * (dxh - _mean1(dxh) - xhat * _mean1(dxh * xhat))
            return (alpha * dz, (1.0 + gv) * dz, _sum0(dhv * (xhat * gamv + betv)), _sum0(dhv),
                    _sum0(dv * xhat), _sum0(dv), _sum0(dz * ypv))
        return rowwise("join_bwd_" + tag, f, [dxa, dh, xprev, yprev], [sc, g, gam, bet], [(D, F32), (D, BF)], [D] * 5)

    def pre_bwd(tag, dxa, dh, xin, sc):
        def f(dxav, dhv, xv, scv):
            return dxav + dhv * (1.0 + scv), _sum0(dhv * xv), _sum0(dhv)
        return rowwise("pre_bwd_" + tag, f, [dxa, dh, xin], [sc], [(D, F32)], [D, D])

    saved = []
    xc = x2
    h = modulate("l0a", xc, modv(0, 1), modv(0, 0))
    for i in range(L):
        j = i // 2
        tag = "l%d" % i
        s = {"x_a": xc}
        s["h_a"] = h
        if i % 2 == 0:
            wdn, wuq, wuk, wuv, wo = gathered("mla%d" % i, xc)
            w_down, w_o = wdn.reshape(D, -1), wo.reshape(D, D)
            q3 = cols_whole(wuq).reshape(QL, HA, NOPE + ROPE)
            w_uq = jnp.concatenate([q3[..., :NOPE].reshape(QL, HA * NOPE), q3[..., NOPE:].reshape(QL, HA * ROPE)], axis=-1)
            w_ukv = jnp.concatenate([cols_whole(wuk), cols_whole(wuv)], axis=-1)
            s.update(w_down=w_down, w_o=w_o, w_uq=w_uq, w_ukv=w_ukv)
            lat = mm("mla_down_" + tag, h, w_down)

            def mid(latv, ck, sk, qg, kg, p):
                ql, kl, kp = latv[:, :QL], latv[:, QL:QL + KVL], latv[:, QL + KVL:]
                qn = ql * lax.rsqrt(_mean1(ql * ql) + RMS_EPS) * qg
                cn = kl * lax.rsqrt(_mean1(kl * kl) + RMS_EPS) * kg
                return qn, cn, kp * ck + _rot(kp, p) * sk

            qn, ckv, kpe = rowwise("mla_mid_" + tag, mid, [lat, cos_k, sin_k],
                                   [mla_q_norm[j][None], mla_kv_norm[j][None], rot_k], [(QL, BF), (KVL, BF), (ROPE, BF)])
            qf = mm("mla_uq_" + tag, qn, w_uq)

            def qrope(qv, cq, sq, p):
                qp = qv[:, HA * NOPE:]
                return (qp * cq + _rot(qp, p) * sq) * mla_scale

            qpe = rowwise("mla_qrope_" + tag, qrope, [qf, cos_q, sin_q], [rot_q], [(HA * ROPE, BF)])[0]
            qpe_h = jnp.moveaxis(qpe.reshape(T, HA, ROPE), 1, 0)
            kv = mm("mla_ukv_" + tag, ckv, w_ukv)
            o, lse = attn_fwd("mla_attn_" + tag, "mla", mla_scale, HA, T, qf, 0, kv, 0, kv, HA, qpe_h, kpe)
            y = mm("mla_o_" + tag, o, w_o)
            s.update(lat=lat, qn=qn, ckv=ckv, kpe=kpe, qf=qf, qpe_h=qpe_h, kv=kv, o=o, lse=lse)
        else:
            win, wo = gathered("fox%d" % i, xc)
            w_in, w_o = _pad_cols(cols_whole(win, keep=n_in), fox_cols), wo.reshape(D, D)
            s.update(w_in=w_in, w_o=w_o)
            proj = mm("fox_in_" + tag, h, w_in, bn=896)
            fz = proj[:, 3 * D:]
            cum = fox_cum("fox_cum_" + tag, fz, b_f[j][None])
            cum_h = cum[:, :HB].T
            o, lse = attn_fwd("fox_attn_" + tag, "fox", fox_scale, HB, T, proj, 0, proj, HB, proj, 2 * HB,
                              cum_h.reshape(HB, T // tk, 1, tk), cum_h.reshape(HB, T, 1))
            y = mm("fox_o_" + tag, o, w_o)
            s.update(proj=proj, fz=fz, cum_h=cum_h, o=o, lse=lse)
        s["y_a"] = y
        xc, h = post(tag + "a", xc, y, modv(i, 2), ln1_g[i][None], ln1_b[i][None], modv(i, 4), modv(i, 3))
        s["x_f"] = xc
        w1g, w3g, w2g = [w[:, None] for w in gathered("ffn%d" % i, xc)]
        a, b, gte = ffn_up("ffn_up_" + tag, h, w1g, w3g, 0)
        y = ffn_down("ffn_down_" + tag, gte, w2g, 0)
        s.update(h_f=h, a=a, b=b, gte=gte, y_f=y, w1g=w1g, w3g=w3g, w2g=w2g)
        nxt = min(i + 1, L - 1)
        xc, h = post(tag + "f", xc, y, modv(i, 5), ln2_g[i][None], ln2_b[i][None], modv(nxt, 1), modv(nxt, 0))
        saved.append(s)

    def loss_fn(yv, tv):
        e = yv - tv
        return e * (1.0 / D), jnp.zeros((1, LANE), F32) + _sum0(jnp.sum(e * e, axis=1, keepdims=True)) * (0.5 / D)

    dx, loss_part = rowwise("loss", loss_fn, [xc, tgt], [], [(D, F32)], [LANE])

    G = {n: [None] * W[n].shape[0] for n in _WEIGHTS if n != "ada_w"}
    pending, pair_open = [], []

    def scatter(kind, idx, tag, gs):
        handle = reduce_scatter_pair(tag, gs)
        pair_open.append((kind, idx, tag, handle))
        return handle[3][0, 0]

    def scatter_on(after):
        pkind, pidx, ptag, phandle = pair_open.pop()
        chips = reduce_scatter_chips(ptag, phandle, core, after)
        pending.append((pkind, pidx, chips))
        return chips[3][0, 0]
    dmod = [[None] * 6 for _ in range(L)]
    s = saved[L - 1]
    dxa, dy, G["ln2_g"][L - 1], G["ln2_b"][L - 1], dmod[L - 1][5] = post_bwd(
        "l%df" % (L - 1), dx, s["x_f"], s["y_f"], modv(L - 1, 5), ln2_g[L - 1][None])
    for i in reversed(range(L)):
        j = i // 2
        tag = "l%d" % i
        s = saved[i]
        g_w2 = ffn_out_grad("ffn_w2_grad_" + tag, s["gte"], dy)
        da, db = ffn_down_bwd("ffn_down_bwd_" + tag, dy, s["w2g"], s["a"], s["b"], 0)
        order = scatter_on(da) if pair_open else 0.0
        g_w1 = ffn_in_grad("ffn_w1_grad_" + tag, s["h_f"], da)
        g_w3 = ffn_in_grad("ffn_w3_grad_" + tag, s["h_f"], db)
        dh = ffn_dh("ffn_dh_" + tag, da, db, s["w1g"], s["w3g"], 0)
        order = order + scatter("ffn", i, "ffn_" + tag, [g_w1, g_w3, g_w2])
        if i == 0:
            order = order + scatter_on(dh)
        dxa, dy, dmod[i][4], dmod[i][3], G["ln1_g"][i], G["ln1_b"][i], dmod[i][2] = join_bwd(
            tag + "f", dxa, dh, modv(i, 4) + order, s["x_a"], s["y_a"], modv(i, 2), ln1_g[i][None], ln1_b[i][None])
        if i % 2 == 0:
            g_o = mm("mla_o_grad_" + tag, s["o"], dy, ta=True, out_dtype=BF)
            do = mm("mla_do_" + tag, dy, s["w_o"], tb=True, out_dtype=BF)
            delta = attn_delta("mla_delta_" + tag, do, s["o"], HA, tk)
            dqn_, dkn_, dv_, dqpe_h, dkpe = attn_bwd("mla_attn_bwd_" + tag, "mla", mla_scale, HA, T, s["qf"], 0, s["kv"], 0,
                                                     s["kv"], HA, s["qpe_h"], s["kpe"], do, s["lse"], delta)
            order = scatter_on(dkpe) if pair_open else 0.0
            dkv = jnp.concatenate([dkn_, dv_], axis=1)
            g_kv = mm("mla_ukv_grad_" + tag, s["ckv"], dkv, ta=True, out_dtype=BF)
            dckv = mm("mla_dckv_" + tag, dkv, s["w_ukv"], tb=True)

            def qrope_bwd(dv, cq, sq, p):
                dv = dv * mla_scale
                return dv * cq - _rot(dv * sq, p)

            dqpe = rowwise("mla_qrope_bwd_" + tag, qrope_bwd, [jnp.moveaxis(dqpe_h, 0, 1).reshape(T, HA * ROPE), cos_q, sin_q],
                           [rot_q], [(HA * ROPE, BF)])[0]
            dq = jnp.concatenate([dqn_, dqpe], axis=1)
            g_uq = mm("mla_uq_grad_" + tag, s["qn"], dq, ta=True, out_dtype=BF)
            g_uq = jnp.concatenate([g_uq[:, :HA * NOPE].reshape(QL, HA, NOPE), g_uq[:, HA * NOPE:].reshape(QL, HA, ROPE)],
                                   axis=-1).reshape(QL, N_DEV, n_uq)
            dqn = mm("mla_dqn_" + tag, dq, s["w_uq"], tb=True)

            def mid_bwd(latv, dqv, dcv, dkp, ck, sk, qg, kg, p):
                ql, kl = latv[:, :QL], latv[:, QL:QL + KVL]
                rq = lax.rsqrt(_mean1(ql * ql) + RMS_EPS)
                rk = lax.rsqrt(_mean1(kl * kl) + RMS_EPS)
                uq, uk = dqv * qg, dcv * kg
                dql = rq * uq - ql * (rq * rq * rq) * _mean1(uq * ql)
                dkl = rk * uk - kl * (rk * rk * rk) * _mean1(uk * kl)
                return dql, dkl, dkp * ck - _rot(dkp * sk, p), _sum0(dqv * ql * rq), _sum0(dcv * kl * rk)

            dql, dkl, dkp, G["mla_q_norm"][j], G["mla_kv_norm"][j] = rowwise(
                "mla_mid_bwd_" + tag, mid_bwd, [s["lat"], dqn, dckv, dkpe, cos_k, sin_k],
                [mla_q_norm[j][None], mla_kv_norm[j][None], rot_k], [(QL, BF), (KVL, BF), (ROPE, BF)], [QL, KVL])
            dlat = jnp.concatenate([dql, dkl, dkp], axis=1)
            g_dn = mm("mla_down_grad_" + tag, s["h_a"], dlat, ta=True, out_dtype=BF)
            dh = mm("mla_dh_" + tag, dlat, s["w_down"], tb=True)
            g_kv = jnp.moveaxis(g_kv.reshape(KVL, 2, N_DEV, n_uk), 2, 0).reshape(N_DEV, KVL, 2 * n_uk)
            order = order + scatter("mla", j, "mla_" + tag, [
                g_dn.reshape(N_DEV, k_loc, -1), jnp.moveaxis(g_uq, 1, 0), g_kv, g_o.reshape(N_DEV, k_loc, D)])
        else:
            g_o = mm("fox_o_grad_" + tag, s["o"], dy, ta=True, out_dtype=BF)
            do = mm("fox_do_" + tag, dy, s["w_o"], tb=True, out_dtype=BF)
            cum_h = s["cum_h"]
            delta = attn_delta("fox_delta_" + tag, do, s["o"], HB, tk)
            dq_, dk_, dv_, dcq, dck = attn_bwd("fox_attn_bwd_" + tag, "fox", fox_scale, HB, T, s["proj"], 0, s["proj"], HB,
                                               s["proj"], 2 * HB, cum_h.reshape(HB, T // tk, 1, tk), cum_h.reshape(HB, T, 1),
                                               do, s["lse"], delta)
            order = scatter_on(dck) if pair_open else 0.0
            dcum = _pad_cols((dcq.reshape(HB, T) + dck.reshape(HB, T)).T, LANE)
            df, db_f = fox_cum_bwd("fox_cum_bwd_" + tag, s["fz"], b_f[j][None], dcum)
            G["fox_b_f"][j] = db_f[:, :HB]
            dproj = jnp.concatenate([dq_, dk_, dv_, df], axis=1)
            g_in = mm("fox_in_grad_" + tag, s["h_a"], dproj, ta=True, out_dtype=BF, bn=896)[:, :N_DEV * n_in]
            dh = mm("fox_dh_" + tag, dproj, s["w_in"], tb=True, bk=896)
            g_in = jnp.pad(jnp.moveaxis(g_in.reshape(D, N_DEV, n_in), 1, 0), ((0, 0), (0, 0), (0, n_in_pad - n_in)))
            order = order + scatter("fox", j, "fox_" + tag, [g_in, g_o.reshape(N_DEV, k_loc, D)])
        if i > 0:
            p = saved[i - 1]
            dxa, dy, dmod[i][1], dmod[i][0], G["ln2_g"][i - 1], G["ln2_b"][i - 1], dmod[i - 1][5] = join_bwd(
                tag + "a", dxa, dh, modv(i, 1) + order, p["x_f"], p["y_f"], modv(i - 1, 5), ln2_g[i - 1][None], ln2_b[i - 1][None])
        else:
            dx, dmod[i][1], dmod[i][0] = pre_bwd(tag + "a", dxa, dh, s["x_a"], modv(i, 1) + order)

    scatter_on(dx)

    def finish(kind, idx, handle, after):
        res = reduce_scatter_finish("%s_%d" % (kind, idx), handle, chip, after)
        if kind == "ffn":
            G["ffn_w1"][idx], G["ffn_w3"][idx], G["ffn_w2"][idx] = res[0][:, :f_loc], res[1][:, :f_loc], res[2][:f_loc]
        elif kind == "mla":
            G["mla_w_down"][idx], G["mla_w_uq"][idx], G["mla_w_o"][idx] = res[0], res[1], res[3]
            G["mla_w_uk"][idx], G["mla_w_uv"][idx] = res[2][:, :n_uk], res[2][:, n_uk:]
        else:
            G["fox_w_in"][idx], G["fox_w_o"][idx] = res[0][:, :n_in], res[1]

    for kind, idx, handle in pending[:-1]:
        finish(kind, idx, handle, dx)

    dmod_mine = jnp.concatenate([jnp.concatenate(r, axis=1) for r in dmod], axis=0)
    small_parts = [dmod_mine] + [jnp.concatenate(G[n], axis=0) for n in _SMALL[1:]] + [loss_part]
    small = _small_pack(small_parts, 512)
    small_all = all_gather("gather_small", [small], True)[0].reshape(N_DEV, -1, 512)
    small_sum = sum_devices("sum_small", small_all)
    sums, off = [], 0
    for p in small_parts:
        nrow = _round_up(p.size, 512) // 512
        sums.append(small_sum[off:off + nrow].reshape(-1)[:p.size].reshape(p.shape))
        off += nrow
    grads = {n: gsum for n, gsum in zip(_SMALL, sums[:-1])}
    loss = sums[-1][0, 0]
    nrow = L * 6 * D // 512
    dmod_all = small_all[:, :nrow].reshape(N_DEV, L, 6 * D)
    dmod_cols = jnp.moveaxis(lax.dynamic_slice_in_dim(dmod_all, me * n_ada, n_ada, axis=2), 0, 1)

    delta, new_m, new_v = {}, {}, {}
    grads["ada_w"], delta["ada_w"], new_m["ada_w"], new_v["ada_w"] = ada_bwd_adamw(
        "ada_grad_adamw", c_all, dmod_cols, ada_w, m_ada_w, v_ada_w)
    last = ("mla_w_down", "mla_w_uq", "mla_w_uk", "mla_w_uv", "mla_w_o")
    for n in ("ffn_w1", "ffn_w3", "ffn_w2", "fox_w_in", "fox_w_o"):
        grads[n] = jnp.stack(G[n])
    for n in _WEIGHTS:
        if n != "ada_w" and n not in last:
            delta[n], new_m[n], new_v[n] = adamw("adamw_" + n, W[n], grads[n], M[n], V[n])
    finish(*pending[-1], delta["ffn_w2"])
    for n in last:
        grads[n] = jnp.stack(G[n])
        delta[n], new_m[n], new_v[n] = adamw("adamw_" + n, W[n], grads[n], M[n], V[n])

    return (loss, dx[None], *[grads[n] for n in _WEIGHTS], *[delta[n] for n in _WEIGHTS],
            *[new_m[n] for n in _WEIGHTS], *[new_v[n] for n in _WEIGHTS])
```

```python
import math

import numpy as np
import jax
import jax.numpy as jnp
from jax import lax
from jax.experimental import pallas as pl
from jax.experimental.pallas import tpu as pltpu

F32 = jnp.float32
BF = jnp.bfloat16
N_DEV = 8

CHUNK_SHIFT = 6
NOPE = 128
ROPE = 64
HEAD = 128
ROPE_THETA = 10000.0
LN_EPS = 1e-5
RMS_EPS = 1e-6
ADAM_LR = 0.001
ADAM_B1 = 0.9
ADAM_B2 = 0.999
ADAM_EPS = 1e-08
ADAM_WD = 0.01
ADAM_STEP = 10

VMEM_LIMIT_BYTES = 48 * 1024 * 1024
LANE = 128
NEG = -1e30


def _cp(*sem):
    return pltpu.CompilerParams(dimension_semantics=sem, vmem_limit_bytes=VMEM_LIMIT_BYTES)


def _pick(dim, pref):
    if dim <= pref:
        return dim
    for t in range(pref - pref % LANE, LANE - 1, -LANE):
        if dim % t == 0:
            return t
    return dim


def _round_up(v, m):
    return (v + m - 1) // m * m


def _sum0(v):
    return jnp.sum(v, axis=0, keepdims=True)


def _mean1(v):
    return jnp.mean(v, axis=-1, keepdims=True)


_NN = (((1,), (0,)), ((), ()))
_NT = (((1,), (1,)), ((), ()))
_TN = (((0,), (0,)), ((), ()))


def _mm_call(name, grid, ins, in_specs, dns, out_shape, out_spec, add=None, add_spec=None):
    npairs = len(dns)
    nk = grid[-1]
    has_add = add is not None

    def body(*refs):
        add_ref = refs[2 * npairs] if has_add else None
        o_ref = refs[2 * npairs + (1 if has_add else 0)]
        r = None
        for p in range(npairs):
            t = lax.dot_general(refs[2 * p][...].astype(BF), refs[2 * p + 1][...].astype(BF), dns[p], preferred_element_type=F32)
            r = t if r is None else r + t

        def fin(val):
            if has_add:
                val = val + add_ref[...]
            o_ref[...] = val.astype(o_ref.dtype)

        if nk == 1:
            fin(r)
        else:
            acc = refs[-1]
            k = pl.program_id(len(grid) - 1)

            @pl.when(k == 0)
            def _():
                acc[...] = r

            @pl.when(k > 0)
            def _():
                acc[...] += r

            @pl.when(k == nk - 1)
            def _():
                fin(acc[...])

    acc_shape = tuple(b for b in out_spec.block_shape if b is not None)
    args, specs = list(ins), list(in_specs)
    if has_add:
        args.append(add)
        specs.append(add_spec)
    return pl.pallas_call(
        body, name=name, grid=grid, in_specs=specs, out_specs=out_spec, out_shape=out_shape,
        scratch_shapes=[pltpu.VMEM(acc_shape, F32)] if nk > 1 else [],
        compiler_params=_cp(*(["parallel"] * (len(grid) - 1) + ["arbitrary"])),
    )(*args)


def mm(name, a, b, ta=False, tb=False, out_dtype=F32, add=None, bm=1024, bn=1024, bk=2048):
    M, K = (a.shape[1], a.shape[0]) if ta else a.shape
    N = b.shape[0] if tb else b.shape[1]
    bm, bn, bk = _pick(M, bm), _pick(N, bn), _pick(K, bk)
    a_spec = pl.BlockSpec((bk, bm), lambda i, j, k: (k, i)) if ta else pl.BlockSpec((bm, bk), lambda i, j, k: (i, k))
    b_spec = pl.BlockSpec((bn, bk), lambda i, j, k: (j, k)) if tb else pl.BlockSpec((bk, bn), lambda i, j, k: (k, j))
    dn = (((0 if ta else 1,), (1 if tb else 0,)), ((), ()))
    tile = pl.BlockSpec((bm, bn), lambda i, j, k: (i, j))
    return _mm_call(name, (M // bm, N // bn, K // bk), [a, b], [a_spec, b_spec], [dn],
                    jax.ShapeDtypeStruct((M, N), out_dtype), tile, add, tile)


def rowwise(name, fn, rows, vecs, out_rows, out_accs=(), tm=256):
    T = rows[0].shape[0]
    tm = min(tm, T)
    nr, nv, no = len(rows), len(vecs), len(out_rows)

    def body(*refs):
        outs = fn(*[r[...] for r in refs[:nr + nv]])
        if not isinstance(outs, (tuple, list)):
            outs = (outs,)
        o_refs = refs[nr + nv:nr + nv + no]
        a_refs = refs[nr + nv + no:]
        for r, o in zip(o_refs, outs[:no]):
            r[...] = o.astype(r.dtype)
        if a_refs:
            @pl.when(pl.program_id(0) == 0)
            def _():
                for r in a_refs:
                    r[...] = jnp.zeros_like(r)

            for r, o in zip(a_refs, outs[no:]):
                r[...] += o

    in_specs = [pl.BlockSpec((tm, r.shape[1]), lambda i: (i, 0)) for r in rows]
    in_specs += [pl.BlockSpec(v.shape, lambda i: (0, 0)) for v in vecs]
    out_specs = [pl.BlockSpec((tm, w), lambda i: (i, 0)) for w, _ in out_rows]
    out_specs += [pl.BlockSpec((1, w), lambda i: (0, 0)) for w in out_accs]
    out_shape = [jax.ShapeDtypeStruct((T, w), dt) for w, dt in out_rows]
    out_shape += [jax.ShapeDtypeStruct((1, w), F32) for w in out_accs]
    return pl.pallas_call(
        body, name=name, grid=(T // tm,), in_specs=in_specs, out_specs=out_specs, out_shape=out_shape,
        compiler_params=_cp("arbitrary"),
    )(*rows, *vecs)


def _rot(v, p):
    hi = v.astype(BF)
    lo = (v - hi.astype(F32)).astype(BF)
    return jnp.dot(hi, p, preferred_element_type=F32) + jnp.dot(lo, p, preferred_element_type=F32)


def _pair_rotation(width):
    p = np.zeros((width, width), np.float32)
    idx = np.arange(0, width, 2)
    p[idx + 1, idx] = -1.0
    p[idx, idx + 1] = 1.0
    return jnp.asarray(p, BF)


def _ln_stats(alpha, x, y, g):
    z = alpha * x + (1.0 + g) * y
    zc = z - _mean1(z)
    rstd = lax.rsqrt(_mean1(zc * zc) + LN_EPS)
    return zc * rstd, rstd


ATT_G_FWD = 4
ATT_G_BWD = 2
ATT_TILE = 256


def _visible(mode, ks, qs, t):
    krow = ks + lax.broadcasted_iota(jnp.int32, (t, t), 0)
    qcol = qs + lax.broadcasted_iota(jnp.int32, (t, t), 1)
    if mode == "mla":
        return lax.shift_right_logical(qcol, CHUNK_SHIFT) >= lax.shift_right_logical(krow, CHUNK_SHIFT)
    return qcol >= krow


def _scores(mode, k, q, e_k, e_q, ks, qs, t, masked):
    s = lax.dot_general(k, q, _NT, preferred_element_type=F32)
    if mode == "mla":
        s = s + lax.dot_general(e_k, e_q, _NT, preferred_element_type=F32)
    else:
        s = s + e_q - e_k
    if masked:
        s = jnp.where(_visible(mode, ks, qs, t), s, NEG)
    return s


def _attn_specs(mode, G, T, t, q_off, k_off, v_off):
    def col(off):
        return pl.BlockSpec((T, G * HEAD), lambda h: (0, off // G + h))

    stat = pl.BlockSpec((G, T // t, 1, t), lambda h: (h, 0, 0, 0))
    if mode == "mla":
        e_specs = [pl.BlockSpec((G, T, ROPE), lambda h: (h, 0, 0)), pl.BlockSpec((T, ROPE), lambda h: (0, 0))]
    else:
        e_specs = [stat, pl.BlockSpec((G, T, 1), lambda h: (h, 0, 0))]
    return [col(q_off), col(k_off), col(v_off)], e_specs, stat


def _head(ref, rows, g):
    return ref[rows, g * HEAD:(g + 1) * HEAD]


def attn_fwd(name, mode, scale, H, T, q_src, q_off, k_src, k_off, v_src, v_off, e_q, e_k):
    G, t = ATT_G_FWD, min(ATT_TILE, T)
    nq = T // t
    mla = mode == "mla"

    def body(q_ref, k_ref, v_ref, eq_ref, ek_ref, o_ref, lse_ref, acc_ref):
        def q_loop(qi, _):
            qs = pl.multiple_of(qi * t, t)
            qrows = pl.ds(qs, t)
            qh = [(_head(q_ref, qrows, g).astype(F32) * scale).astype(BF) for g in range(G)]
            eqh = [eq_ref[g, qrows, :] if mla else eq_ref[g, qi] for g in range(G)]
            acc_ref[...] = jnp.zeros_like(acc_ref)

            def step(j, carry, masked):
                ks = pl.multiple_of(j * t, t)
                krows = pl.ds(ks, t)
                out, ps, corrs = [], [], []
                ss = [_scores(mode, _head(k_ref, krows, g).astype(BF), qh[g], ek_ref[krows, :] if mla else ek_ref[g, krows, :],
                              eqh[g], ks, qs, t, masked) for g in range(G)]
                for g in range(G):
                    m, l = carry[g]
                    m_new = jnp.maximum(m, jnp.max(ss[g], axis=0, keepdims=True))
                    p = jnp.exp(ss[g] - m_new)
                    corr = jnp.exp(m - m_new)
                    out.append((m_new, corr * l + jnp.sum(p, axis=0, keepdims=True)))
                    ps.append(p.astype(BF))
                    corrs.append(corr)
                pvs = [lax.dot_general(_head(v_ref, krows, g).astype(BF), ps[g], _TN, preferred_element_type=F32) for g in range(G)]
                for g in range(G):
                    acc_ref[g] = corrs[g] * acc_ref[g] + pvs[g]
                return tuple(out)

            init = tuple((jnp.full((1, t), NEG, F32), jnp.zeros((1, t), F32)) for _ in range(G))
            carry = lax.fori_loop(0, qi, lambda j, cr: step(j, cr, False), init)
            carry = step(qi, carry, True)
            for g in range(G):
                m, l = carry[g]
                o_ref[qrows, g * HEAD:(g + 1) * HEAD] = (acc_ref[g] / l).T.astype(o_ref.dtype)
                lse_ref[g, qi] = m + jnp.log(l)
            return 0

        lax.fori_loop(0, nq, q_loop, 0)

    cols, e_specs, stat = _attn_specs(mode, G, T, t, q_off, k_off, v_off)
    return pl.pallas_call(
        body, name=name, grid=(H // G,), in_specs=cols + e_specs,
        out_specs=[pl.BlockSpec((T, G * HEAD), lambda h: (0, h)), stat],
        out_shape=[jax.ShapeDtypeStruct((T, H * HEAD), BF), jax.ShapeDtypeStruct((H, nq, 1, t), F32)],
        scratch_shapes=[pltpu.VMEM((G, HEAD, t), F32)],
        compiler_params=_cp("parallel"),
    )(q_src, k_src, v_src, e_q, e_k)


def attn_bwd(name, mode, scale, H, T, q_src, q_off, k_src, k_off, v_src, v_off, e_q, e_k, do, lse, delta):
    G, t = ATT_G_BWD, min(ATT_TILE, T)
    nq = T // t
    mla = mode == "mla"

    def body(q_ref, k_ref, v_ref, eq_ref, ek_ref, do_ref, lse_ref, dl_ref,
             dq_ref, dk_ref, dv_ref, deq_ref, dek_ref, dk_acc, dv_acc, dq_acc, deq_acc):
        dk_acc[...] = jnp.zeros_like(dk_acc)
        dv_acc[...] = jnp.zeros_like(dv_acc)
        if mla:
            @pl.when(pl.program_id(0) == 0)
            def _():
                dek_ref[...] = jnp.zeros_like(dek_ref)
        else:
            dek_ref[...] = jnp.zeros_like(dek_ref)

        def q_loop(qi, _):
            qs = pl.multiple_of(qi * t, t)
            qrows = pl.ds(qs, t)
            qh = [(_head(q_ref, qrows, g).astype(F32) * scale).astype(BF) for g in range(G)]
            eqh = [eq_ref[g, qrows, :] if mla else eq_ref[g, qi] for g in range(G)]
            doh = [_head(do_ref, qrows, g) for g in range(G)]
            lse_q = [lse_ref[g, qi] for g in range(G)]
            dl_q = [dl_ref[g, qi] for g in range(G)]
            dq_acc[...] = jnp.zeros_like(dq_acc)
            deq_acc[...] = jnp.zeros_like(deq_acc)

            def step(j, carry, masked):
                ks = pl.multiple_of(j * t, t)
                krows = pl.ds(ks, t)
                kh = [_head(k_ref, krows, g).astype(BF) for g in range(G)]
                ekh = [ek_ref[krows, :] if mla else ek_ref[g, krows, :] for g in range(G)]
                ss = [_scores(mode, kh[g], qh[g], ekh[g], eqh[g], ks, qs, t, masked) for g in range(G)]
                dps = [lax.dot_general(_head(v_ref, krows, g).astype(BF), doh[g], _NT, preferred_element_type=F32) for g in range(G)]
                pbs, dss, dsbs = [], [], []
                for g in range(G):
                    p = jnp.exp(ss[g] - lse_q[g])
                    ds = p * (dps[g] - dl_q[g])
                    pbs.append(p.astype(BF))
                    dss.append(ds)
                    dsbs.append(ds.astype(BF))
                for g in range(G):
                    dv_acc[g, krows, :] += jnp.dot(pbs[g], doh[g], preferred_element_type=F32)
                    dk_acc[g, krows, :] += jnp.dot(dsbs[g], qh[g], preferred_element_type=F32)
                    dq_acc[g] += lax.dot_general(dsbs[g], kh[g], _TN, preferred_element_type=F32)
                    if mla:
                        deq_acc[g] += lax.dot_general(dsbs[g], ekh[g], _TN, preferred_element_type=F32)
                        dek_ref[krows, :] += jnp.dot(dsbs[g], eqh[g], preferred_element_type=F32)
                    else:
                        deq_acc[g] += jnp.sum(dss[g], axis=0, keepdims=True)
                        dek_ref[g, krows, :] -= jnp.sum(dss[g], axis=1, keepdims=True)
                return carry

            lax.fori_loop(0, qi, lambda j, cr: step(j, cr, False), 0)
            step(qi, 0, True)
            for g in range(G):
                dq_ref[qrows, g * HEAD:(g + 1) * HEAD] = (dq_acc[g] * scale).astype(dq_ref.dtype)
                if mla:
                    deq_ref[g, qrows, :] = deq_acc[g]
                else:
                    deq_ref[g, qi] = deq_acc[g]
            return 0

        lax.fori_loop(0, nq, q_loop, 0)
        for g in range(G):
            dk_ref[:, g * HEAD:(g + 1) * HEAD] = dk_acc[g].astype(dk_ref.dtype)
            dv_ref[:, g * HEAD:(g + 1) * HEAD] = dv_acc[g].astype(dv_ref.dtype)

    cols, e_specs, stat = _attn_specs(mode, G, T, t, q_off, k_off, v_off)
    heads = pl.BlockSpec((T, G * HEAD), lambda h: (0, h))
    if mla:
        de_shapes = [jax.ShapeDtypeStruct((H, T, ROPE), F32), jax.ShapeDtypeStruct((T, ROPE), F32)]
        deq_scratch = pltpu.VMEM((G, t, ROPE), F32)
    else:
        de_shapes = [jax.ShapeDtypeStruct((H, nq, 1, t), F32), jax.ShapeDtypeStruct((H, T, 1), F32)]
        deq_scratch = pltpu.VMEM((G, 1, t), F32)
    return pl.pallas_call(
        body, name=name, grid=(H // G,),
        in_specs=cols + e_specs + [heads, stat, stat],
        out_specs=[heads, heads, heads] + e_specs,
        out_shape=[jax.ShapeDtypeStruct((T, H * HEAD), BF)] * 3 + de_shapes,
        scratch_shapes=[pltpu.VMEM((G, T, HEAD), F32), pltpu.VMEM((G, T, HEAD), F32), pltpu.VMEM((G, t, HEAD), F32), deq_scratch],
        compiler_params=_cp("arbitrary"),
    )(q_src, k_src, v_src, e_q, e_k, do, lse, delta)


def attn_delta(name, do, o, H, t):
    T, W = do.shape
    ind = np.zeros((W, LANE), np.float32)
    ind[np.arange(W), np.arange(W) // HEAD] = 1.0

    def f(dv, ov, e):
        return _rot(dv.astype(F32) * ov.astype(F32), e)

    d = rowwise(name, f, [do, o], [jnp.asarray(ind, BF)], [(LANE, F32)])[0]
    return d[:, :H].T.reshape(H, T // t, 1, t)


def fox_cum(name, f, b):
    T = f.shape[0]

    def body(f_ref, b_ref, cum_ref):
        z = f_ref[...] + b_ref[...]
        v = jnp.minimum(z, 0.0) - jnp.log(1.0 + jnp.exp(-jnp.abs(z)))
        row = lax.broadcasted_iota(jnp.int32, v.shape, 0)
        s = 1
        while s < T:
            v = v + jnp.where(row >= s, pltpu.roll(v, s, 0), 0.0)
            s *= 2
        cum_ref[...] = v

    return pl.pallas_call(body, name=name, out_shape=jax.ShapeDtypeStruct(f.shape, F32),
                          compiler_params=pltpu.CompilerParams(vmem_limit_bytes=VMEM_LIMIT_BYTES))(f, b)


def fox_cum_bwd(name, f, b, dcum):
    T = f.shape[0]

    def body(f_ref, b_ref, dc_ref, df_ref, db_ref):
        v = dc_ref[...]
        row = lax.broadcasted_iota(jnp.int32, v.shape, 0)
        s = 1
        while s < T:
            v = v + jnp.where(row < T - s, pltpu.roll(v, T - s, 0), 0.0)
            s *= 2
        z = f_ref[...] + b_ref[...]
        df = v / (1.0 + jnp.exp(z))
        df_ref[...] = df.astype(df_ref.dtype)
        db_ref[...] = _sum0(df)

    return pl.pallas_call(body, name=name,
                          out_shape=[jax.ShapeDtypeStruct(f.shape, BF), jax.ShapeDtypeStruct((1, f.shape[1]), F32)],
                          compiler_params=pltpu.CompilerParams(vmem_limit_bytes=VMEM_LIMIT_BYTES))(f, b, dcum)


def ffn_up(name, h, w1g, w3g, l):
    T, D = h.shape
    FP = w1g.shape[3]
    bm = _pick(T, 1024)

    def body(h_ref, w1_ref, w3_ref, a_ref, b_ref, g_ref):
        hv = h_ref[...]
        a = jnp.dot(hv, w1_ref[...], preferred_element_type=F32)
        b = jnp.dot(hv, w3_ref[...], preferred_element_type=F32)
        a_ref[...] = a.astype(BF)
        b_ref[...] = b.astype(BF)
        g_ref[...] = (a * (1.0 / (1.0 + jnp.exp(-a))) * b).astype(BF)

    w_spec = pl.BlockSpec((None, None, D, FP), lambda i, d: (d, l, 0, 0))
    tile = pl.BlockSpec((None, bm, FP), lambda i, d: (d, i, 0))
    return pl.pallas_call(
        body, name=name, grid=(T // bm, N_DEV), in_specs=[pl.BlockSpec((bm, D), lambda i, d: (i, 0)), w_spec, w_spec],
        out_specs=[tile, tile, tile], out_shape=[jax.ShapeDtypeStruct((N_DEV, T, FP), BF)] * 3,
        compiler_params=_cp("parallel", "parallel"),
    )(h, w1g, w3g)


def ffn_down(name, g, w2g, l):
    _, T, FP = g.shape
    D = w2g.shape[3]
    bm, bn = _pick(T, 1024), _pick(D, 1024)
    return _mm_call(
        name, (T // bm, D // bn, N_DEV), [g, w2g],
        [pl.BlockSpec((None, bm, FP), lambda i, j, d: (d, i, 0)), pl.BlockSpec((None, None, FP, bn), lambda i, j, d: (d, l, 0, j))],
        [_NN], jax.ShapeDtypeStruct((T, D), F32), pl.BlockSpec((bm, bn), lambda i, j, d: (i, j)))


def ffn_down_bwd(name, dy, w2g, a, b, l):
    T, D = dy.shape
    FP = w2g.shape[2]
    bm = _pick(T, 1024)

    def body(dy_ref, w2_ref, a_ref, b_ref, da_ref, db_ref):
        dg = lax.dot_general(dy_ref[...], w2_ref[...], _NT, preferred_element_type=F32)
        av = a_ref[...].astype(F32)
        bv = b_ref[...].astype(F32)
        sig = 1.0 / (1.0 + jnp.exp(-av))
        da_ref[...] = (dg * bv * sig * (1.0 + av * (1.0 - sig))).astype(BF)
        db_ref[...] = (dg * av * sig).astype(BF)

    tile = pl.BlockSpec((None, bm, FP), lambda i, d: (d, i, 0))
    return pl.pallas_call(
        body, name=name, grid=(T // bm, N_DEV),
        in_specs=[pl.BlockSpec((bm, D), lambda i, d: (i, 0)), pl.BlockSpec((None, None, FP, D), lambda i, d: (d, l, 0, 0)), tile, tile],
        out_specs=[tile, tile], out_shape=[jax.ShapeDtypeStruct((N_DEV, T, FP), BF)] * 2,
        compiler_params=_cp("parallel", "parallel"),
    )(dy, w2g, a, b)


def ffn_dh(name, da, db, w1g, w3g, l):
    _, T, FP = da.shape
    D = w1g.shape[2]
    bm, bn = _pick(T, 1024), _pick(D, 1024)
    act = pl.BlockSpec((None, bm, FP), lambda i, j, d: (d, i, 0))
    wgt = pl.BlockSpec((None, None, bn, FP), lambda i, j, d: (d, l, j, 0))
    return _mm_call(name, (T // bm, D // bn, N_DEV), [da, w1g, db, w3g], [act, wgt, act, wgt], [_NT, _NT],
                    jax.ShapeDtypeStruct((T, D), F32), pl.BlockSpec((bm, bn), lambda i, j, d: (i, j)))


def ffn_in_grad(name, h, dact):
    T, D = h.shape
    FP = dact.shape[2]
    bm = _pick(D, 1024)
    return _mm_call(
        name, (D // bm, N_DEV, 1), [h, dact],
        [pl.BlockSpec((T, bm), lambda i, d, k: (0, i)), pl.BlockSpec((None, T, FP), lambda i, d, k: (d, 0, 0))],
        [_TN], jax.ShapeDtypeStruct((N_DEV, D, FP), BF), pl.BlockSpec((None, bm, FP), lambda i, d, k: (d, i, 0)))


def ffn_out_grad(name, g, dy):
    _, T, FP = g.shape
    D = dy.shape[1]
    bn = _pick(D, 1024)
    return _mm_call(
        name, (N_DEV, D // bn, 1), [g, dy],
        [pl.BlockSpec((None, T, FP), lambda d, j, k: (d, 0, 0)), pl.BlockSpec((T, bn), lambda d, j, k: (0, j))],
        [_TN], jax.ShapeDtypeStruct((N_DEV, FP, D), BF), pl.BlockSpec((None, FP, bn), lambda d, j, k: (d, 0, j)))


def _adamw_math(w, g, m, v):
    m = ADAM_B1 * m + (1.0 - ADAM_B1) * g
    v = ADAM_B2 * v + (1.0 - ADAM_B2) * (g * g)
    m_hat = m / (1.0 - ADAM_B1 ** ADAM_STEP)
    v_hat = v / (1.0 - ADAM_B2 ** ADAM_STEP)
    delta = -ADAM_LR * (m_hat / (jnp.sqrt(v_hat) + ADAM_EPS) + ADAM_WD * w)
    return delta, m, v


def adamw(name, w, g, m, v):
    shape = w.shape
    n = shape[-1]
    rows = math.prod(shape[:-1])
    tr = rows if rows <= 512 else 256
    assert rows % tr == 0

    def body(w_ref, g_ref, m_ref, v_ref, d_ref, mo_ref, vo_ref):
        d, mn, vn = _adamw_math(w_ref[...], g_ref[...], m_ref[...], v_ref[...])
        d_ref[...] = d
        mo_ref[...] = mn
        vo_ref[...] = vn

    blk = pl.BlockSpec((tr, n), lambda i: (i, 0))
    outs = pl.pallas_call(
        body, name=name, grid=(rows // tr,), in_specs=[blk] * 4, out_specs=[blk] * 3,
        out_shape=[jax.ShapeDtypeStruct((rows, n), F32)] * 3, compiler_params=_cp("parallel"),
    )(*[t.reshape(rows, n) for t in (w, g, m, v)])
    return [t.reshape(shape) for t in outs]


def _silu(v):
    return v * (1.0 / (1.0 + jnp.exp(-v)))


def ada_fwd(name, c_all, ada_w, ada_b_cols):
    L, D, n = ada_w.shape
    bn = _pick(n, 512)

    def body(c_ref, w_ref, b_ref, o_ref):
        act = _silu(c_ref[...]).astype(BF)
        o_ref[...] = jnp.dot(act, w_ref[...].astype(BF), preferred_element_type=F32) + b_ref[...]

    return pl.pallas_call(
        body, name=name, grid=(L, n // bn),
        in_specs=[pl.BlockSpec((N_DEV, D), lambda l, j: (0, 0)), pl.BlockSpec((None, D, bn), lambda l, j: (l, 0, j)),
                  pl.BlockSpec((None, 1, bn), lambda l, j: (l, 0, j))],
        out_specs=pl.BlockSpec((None, N_DEV, bn), lambda l, j: (l, 0, j)),
        out_shape=jax.ShapeDtypeStruct((L, N_DEV, n), F32), compiler_params=_cp("parallel", "parallel"),
    )(c_all, ada_w, ada_b_cols.reshape(L, 1, n))


def ada_bwd_adamw(name, c_all, dmod_cols, w, m, v):
    L, D, n = w.shape
    tr = _pick(D, 256)

    def body(c_ref, dm_ref, w_ref, m_ref, v_ref, g_ref, d_ref, mo_ref, vo_ref):
        act = _silu(c_ref[...]).astype(BF)
        g = lax.dot_general(act, dm_ref[...].astype(BF), _TN, preferred_element_type=F32)
        d, mn, vn = _adamw_math(w_ref[...], g, m_ref[...], v_ref[...])
        g_ref[...] = g
        d_ref[...] = d
        mo_ref[...] = mn
        vo_ref[...] = vn

    blk = pl.BlockSpec((None, tr, n), lambda l, i: (l, i, 0))
    return pl.pallas_call(
        body, name=name, grid=(L, D // tr),
        in_specs=[pl.BlockSpec((N_DEV, tr), lambda l, i: (0, i)), pl.BlockSpec((None, N_DEV, n), lambda l, i: (l, 0, 0)),
                  blk, blk, blk],
        out_specs=[blk] * 4, out_shape=[jax.ShapeDtypeStruct((L, D, n), F32)] * 4,
        compiler_params=_cp("parallel", "parallel"),
    )(c_all, dmod_cols, w, m, v)


_MESH = pl.DeviceIdType.MESH
_ANY = pl.BlockSpec(memory_space=pl.ANY)
_CHIP_FLIPS = ((1, 0), (0, 1), (1, 1))


def _place():
    return lax.axis_index("x"), lax.axis_index("y"), lax.axis_index("c")


def _flip(v, f):
    return 1 - v if f else v


def all_gather(name, shards, in_vmem=False):
    nt = len(shards)

    def body(*refs):
        x_refs, out_refs = refs[:nt], refs[nt:2 * nt]
        send_sems, recv_sems, local_sems = refs[2 * nt:]
        x, y, c = _place()
        me, sibling = (x, y, c), (x, y, 1 - c)
        chips = [(_flip(x, fx), _flip(y, fy)) for fx, fy in _CHIP_FLIPS]

        def rows(t, px, py, pc):
            m_per = shards[t].shape[0]
            return out_refs[t].at[pl.ds((4 * px + 2 * py + pc) * m_per, m_per), :]

        def copy(t, k, block, to, src=None):
            return pltpu.make_async_remote_copy(
                src_ref=rows(t, *block) if src is None else src, dst_ref=rows(t, *block),
                send_sem=send_sems.at[7 * t + k], recv_sem=recv_sems.at[7 * t + k], device_id=to, device_id_type=_MESH)

        mine = [pltpu.make_async_copy(x_refs[t], rows(t, *me), local_sems.at[t]) for t in range(nt)]
        for cp in mine:
            cp.start()
        first = []
        for j, chip in enumerate(chips):
            first += [copy(t, 1 + j, me, (*chip, c), src=x_refs[t]) for t in range(nt)]
        first += [copy(t, 0, me, sibling, src=x_refs[t]) for t in range(nt)]
        for cp in first:
            cp.start()
        passed = []
        for j, chip in enumerate(chips):
            for t in range(nt):
                copy(t, 1 + j, (*chip, c), me).wait_recv()
                passed.append(copy(t, 4 + j, (*chip, c), sibling))
                passed[-1].start()
        for t in range(nt):
            copy(t, 0, sibling, me).wait_recv()
            for j, chip in enumerate(chips):
                copy(t, 4 + j, (*chip, 1 - c), me).wait_recv()
        for cp in first + passed:
            cp.wait_send()
        for cp in mine:
            cp.wait()

    space = pl.BlockSpec(memory_space=pltpu.VMEM) if in_vmem else _ANY
    return pl.pallas_call(
        body, name=name, out_shape=[jax.ShapeDtypeStruct((N_DEV * s.shape[0], s.shape[1]), s.dtype) for s in shards],
        in_specs=[space] * nt, out_specs=[space] * nt,
        scratch_shapes=[pltpu.SemaphoreType.DMA((7 * nt,)), pltpu.SemaphoreType.DMA((7 * nt,)), pltpu.SemaphoreType.DMA((nt,))],
    )(*shards)


_HBM = pl.BlockSpec(memory_space=pltpu.HBM)
_SEM = pl.BlockSpec(memory_space=pltpu.SEMAPHORE)
_EFFECT = pltpu.SideEffectType.DATAFLOW_SIDE_EFFECTING


def _in_hbm(a):
    return pltpu.with_memory_space_constraint(a, pltpu.HBM)


def _gather_first_copies(x_refs, land_refs, send_sems, recv_sems, local_sems):
    nt = len(x_refs)
    x, y, c = _place()
    me = (x, y, c)

    def rows(t, px, py, pc):
        m_per = x_refs[t].shape[0]
        return land_refs[t].at[pl.ds((4 * px + 2 * py + pc) * m_per, m_per), :]

    local = [pltpu.make_async_copy(x_refs[t], rows(t, *me), local_sems.at[t]) for t in range(nt)]
    remote = []
    peers = [(_flip(x, fx), _flip(y, fy), c) for fx, fy in _CHIP_FLIPS] + [(x, y, 1 - c)]
    for k, peer in enumerate(peers):
        remote += [(pltpu.make_async_remote_copy(
            src_ref=x_refs[t], dst_ref=rows(t, *me), send_sem=send_sems.at[4 * t + k], recv_sem=recv_sems.at[4 * t + k],
            device_id=peer, device_id_type=_MESH), rows(t, *peer), 4 * t + k) for t in range(nt)]
    return local, remote


def gather_start(name, shards, after):
    nt = len(shards)

    def body(*refs):
        x_refs, land_refs = refs[:nt], refs[nt:2 * nt]
        send_sems, recv_sems, local_sems = refs[2 * nt + 1:2 * nt + 4]
        token = refs[-1]
        local, remote = _gather_first_copies(x_refs, land_refs, send_sems, recv_sems, local_sems)
        for cp in local:
            cp.start()
        for cp, _, _ in remote:
            cp.start()
        token[...] = jnp.zeros_like(token)

    lands = [lax.empty((N_DEV * s.shape[0], s.shape[1]), s.dtype) for s in shards]
    outs = pl.pallas_call(
        body, name=name,
        out_shape=[pltpu.SemaphoreType.DMA((4 * nt,)), pltpu.SemaphoreType.DMA((4 * nt,)), pltpu.SemaphoreType.DMA((nt,))]
        + [pltpu.HBM(s.shape, s.dtype) for s in shards] + [pltpu.HBM(a.shape, a.dtype) for a in lands]
        + [jax.ShapeDtypeStruct((8, LANE), F32)],
        in_specs=[_HBM] * (2 * nt) + [_ANY], out_specs=[_SEM] * 3 + [_HBM] * (2 * nt) + [pl.BlockSpec(memory_space=pltpu.VMEM)],
        input_output_aliases={i: 3 + i for i in range(2 * nt)},
        compiler_params=pltpu.CompilerParams(has_side_effects=_EFFECT),
    )(*[_in_hbm(s) for s in shards], *[_in_hbm(a) for a in lands], after)
    return outs[:3], outs[3:3 + nt], outs[3 + nt:3 + 2 * nt], outs[-1]


def gather_wait(name, handle, after):
    sems, shards, lands, _ = handle
    nt = len(shards)

    def body(*refs):
        x_refs, land_refs = refs[:nt], refs[nt:2 * nt]
        send_sems, recv_sems, local_sems = refs[2 * nt:2 * nt + 3]
        local, remote = _gather_first_copies(x_refs, land_refs, send_sems, recv_sems, local_sems)
        for cp, landed, k in remote:
            cp.wait_send()
            pltpu.make_async_remote_copy(src_ref=landed, dst_ref=landed, send_sem=send_sems.at[k], recv_sem=recv_sems.at[k],
                                         device_id=_place(), device_id_type=_MESH).wait_recv()
        for cp in local:
            cp.wait()

    outs = pl.pallas_call(
        body, name=name, out_shape=[pltpu.HBM(s.shape, s.dtype) for s in shards] + [pltpu.HBM(a.shape, a.dtype) for a in lands],
        in_specs=[_HBM] * (2 * nt) + [_SEM] * 3 + [_ANY], out_specs=[_HBM] * (2 * nt),
        input_output_aliases={i: i for i in range(2 * nt)},
        compiler_params=pltpu.CompilerParams(has_side_effects=_EFFECT),
    )(*shards, *lands, *sems, after)
    return outs[nt:]


def _forward_copies(land_refs, send_sems, recv_sems):
    nt = len(land_refs)
    x, y, c = _place()
    copies = []
    for j, (fx, fy) in enumerate(_CHIP_FLIPS):
        px, py = _flip(x, fx), _flip(y, fy)
        for t in range(nt):
            m_per = land_refs[t].shape[0] // N_DEV
            mine = land_refs[t].at[pl.ds((4 * px + 2 * py + c) * m_per, m_per), :]
            theirs = land_refs[t].at[pl.ds((4 * px + 2 * py + 1 - c) * m_per, m_per), :]
            copies.append((pltpu.make_async_remote_copy(
                src_ref=mine, dst_ref=mine, send_sem=send_sems.at[3 * t + j], recv_sem=recv_sems.at[3 * t + j],
                device_id=(x, y, 1 - c), device_id_type=_MESH), theirs, 3 * t + j))
    return copies


def _wait_all(copies, send_sems, recv_sems):
    for cp, landed, k in copies:
        cp.wait_send()
        pltpu.make_async_remote_copy(src_ref=landed, dst_ref=landed, send_sem=send_sems.at[k], recv_sem=recv_sems.at[k],
                                     device_id=_place(), device_id_type=_MESH).wait_recv()


def gather_forward(name, lands):
    nt = len(lands)

    def body(*refs):
        send_sems, recv_sems = refs[2 * nt:]
        copies = _forward_copies(refs[:nt], send_sems, recv_sems)
        for cp, _, _ in copies:
            cp.start()
        _wait_all(copies, send_sems, recv_sems)

    return pl.pallas_call(
        body, name=name, out_shape=[jax.ShapeDtypeStruct(a.shape, a.dtype) for a in lands],
        in_specs=[_ANY] * nt, out_specs=[_ANY] * nt, input_output_aliases={i: i for i in range(nt)},
        scratch_shapes=[pltpu.SemaphoreType.DMA((3 * nt,)), pltpu.SemaphoreType.DMA((3 * nt,))],
    )(*lands)


def forward_start(name, lands):
    nt = len(lands)

    def body(*refs):
        for cp, _, _ in _forward_copies(refs[:nt], refs[nt], refs[nt + 1]):
            cp.start()
        refs[-1][...] = jnp.zeros_like(refs[-1])

    outs = pl.pallas_call(
        body, name=name,
        out_shape=[pltpu.SemaphoreType.DMA((3 * nt,)), pltpu.SemaphoreType.DMA((3 * nt,))]
        + [pltpu.HBM(a.shape, a.dtype) for a in lands] + [jax.ShapeDtypeStruct((8, LANE), F32)],
        in_specs=[_HBM] * nt, out_specs=[_SEM] * 2 + [_HBM] * nt + [pl.BlockSpec(memory_space=pltpu.VMEM)],
        input_output_aliases={i: 2 + i for i in range(nt)},
        compiler_params=pltpu.CompilerParams(has_side_effects=_EFFECT),
    )(*[_in_hbm(a) for a in lands])
    return outs[:2], outs[2:2 + nt], outs[-1]


def forward_wait(name, handle, after):
    sems, lands, _ = handle
    nt = len(lands)

    def body(*refs):
        send_sems, recv_sems = refs[nt], refs[nt + 1]
        _wait_all(_forward_copies(refs[:nt], send_sems, recv_sems), send_sems, recv_sems)

    return pl.pallas_call(
        body, name=name, out_shape=[pltpu.HBM(a.shape, a.dtype) for a in lands],
        in_specs=[_HBM] * nt + [_SEM] * 2 + [_ANY], out_specs=[_HBM] * nt,
        input_output_aliases={i: i for i in range(nt)},
        compiler_params=pltpu.CompilerParams(has_side_effects=_EFFECT),
    )(*lands, *sems, after)


def _chip_copies(p_refs, land_refs, send_sems, recv_sems):
    nt = len(p_refs)
    x, y, c = _place()
    copies = []
    for j, (fx, fy) in enumerate(_CHIP_FLIPS):
        px, py = _flip(x, fx), _flip(y, fy)
        copies += [(pltpu.make_async_remote_copy(
            src_ref=p_refs[t].at[2 * px + py], dst_ref=land_refs[t].at[j], send_sem=send_sems.at[3 * t + j],
            recv_sem=recv_sems.at[3 * t + j], device_id=(px, py, c), device_id_type=_MESH), land_refs[t].at[j], 3 * t + j)
            for t in range(nt)]
    return copies


def _pair_copies(g_refs, land_refs, send_sems, recv_sems):
    nt = len(g_refs)
    x, y, c = _place()
    return [(pltpu.make_async_remote_copy(
        src_ref=g_refs[t].at[2 * k + 1 - c], dst_ref=land_refs[t].at[k], send_sem=send_sems.at[4 * t + k],
        recv_sem=recv_sems.at[4 * t + k], device_id=(x, y, 1 - c), device_id_type=_MESH), land_refs[t].at[k], 4 * t + k)
        for t in range(nt) for k in range(4)]


def exchange_start(name, copies_fn, n_land, parts):
    nt = len(parts)

    def body(*refs):
        p_refs, land_refs = refs[:nt], refs[nt:2 * nt]
        send_sems, recv_sems = refs[2 * nt:2 * nt + 2]
        token = refs[-1]
        for cp, _, _ in copies_fn(p_refs, land_refs, send_sems, recv_sems):
            cp.start()
        token[...] = jnp.zeros_like(token)

    lands = [lax.empty((n_land,) + p.shape[1:], p.dtype) for p in parts]
    outs = pl.pallas_call(
        body, name=name,
        out_shape=[pltpu.SemaphoreType.DMA((n_land * nt,)), pltpu.SemaphoreType.DMA((n_land * nt,))]
        + [pltpu.HBM(p.shape, p.dtype) for p in parts] + [pltpu.HBM(a.shape, a.dtype) for a in lands]
        + [jax.ShapeDtypeStruct((8, LANE), F32)],
        in_specs=[_HBM] * (2 * nt), out_specs=[_SEM] * 2 + [_HBM] * (2 * nt) + [pl.BlockSpec(memory_space=pltpu.VMEM)],
        input_output_aliases={i: 2 + i for i in range(2 * nt)},
        compiler_params=pltpu.CompilerParams(has_side_effects=_EFFECT),
    )(*[_in_hbm(p) for p in parts], *[_in_hbm(a) for a in lands])
    return outs[:2], outs[2:2 + nt], outs[2 + nt:2 + 2 * nt], outs[-1]


def exchange_wait(name, copies_fn, handle, after):
    sems, parts, lands, _ = handle
    nt = len(parts)

    def body(*refs):
        p_refs, land_refs = refs[:nt], refs[nt:2 * nt]
        send_sems, recv_sems = refs[2 * nt:2 * nt + 2]
        for cp, landed, k in copies_fn(p_refs, land_refs, send_sems, recv_sems):
            cp.wait_send()
            pltpu.make_async_remote_copy(src_ref=landed, dst_ref=landed, send_sem=send_sems.at[k], recv_sem=recv_sems.at[k],
                                         device_id=_place(), device_id_type=_MESH).wait_recv()

    outs = pl.pallas_call(
        body, name=name, out_shape=[pltpu.HBM(p.shape, p.dtype) for p in parts] + [pltpu.HBM(a.shape, a.dtype) for a in lands],
        in_specs=[_HBM] * (2 * nt) + [_SEM] * 2 + [_ANY], out_specs=[_HBM] * (2 * nt),
        input_output_aliases={i: i for i in range(2 * nt)},
        compiler_params=pltpu.CompilerParams(has_side_effects=_EFFECT),
    )(*parts, *lands, *sems, after)
    return outs[:nt], outs[nt:]


def _row_tile(rows):
    return rows if rows <= 1024 else 1024


def pair_add(name, g, recv, core):
    _, R, C = g.shape
    tr = _row_tile(R)

    def body(core_ref, g_ref, r_ref, o_ref):
        o_ref[...] = (g_ref[...].astype(F32) + r_ref[...].astype(F32)).astype(o_ref.dtype)

    return pl.pallas_call(
        body, name=name, out_shape=jax.ShapeDtypeStruct((4, R, C), BF),
        grid_spec=pltpu.PrefetchScalarGridSpec(
            num_scalar_prefetch=1, grid=(4, R // tr),
            in_specs=[pl.BlockSpec((None, tr, C), lambda k, i, core_ref: (2 * k + core_ref[0], i, 0)),
                      pl.BlockSpec((None, tr, C), lambda k, i, core_ref: (k, i, 0))],
            out_specs=pl.BlockSpec((None, tr, C), lambda k, i, core_ref: (k, i, 0))),
        compiler_params=_cp("parallel", "parallel"),
    )(core, g, recv)


def chip_add(name, part, recv, chip):
    _, R, C = part.shape
    tr = _row_tile(R)

    def body(chip_ref, p_ref, r_ref, o_ref):
        o_ref[...] = ((p_ref[...].astype(F32) + r_ref[0].astype(F32)) + r_ref[1].astype(F32)) + r_ref[2].astype(F32)

    return pl.pallas_call(
        body, name=name, out_shape=jax.ShapeDtypeStruct((R, C), F32),
        grid_spec=pltpu.PrefetchScalarGridSpec(
            num_scalar_prefetch=1, grid=(R // tr,),
            in_specs=[pl.BlockSpec((None, tr, C), lambda i, chip_ref: (chip_ref[0], i, 0)),
                      pl.BlockSpec((3, tr, C), lambda i, chip_ref: (0, i, 0))],
            out_specs=pl.BlockSpec((tr, C), lambda i, chip_ref: (i, 0))),
        compiler_params=_cp("parallel"),
    )(chip, part, recv)


def reduce_scatter_pair(tag, gs):
    return exchange_start("pair_exchange_start_" + tag, _pair_copies, 4, gs)


def reduce_scatter_chips(tag, handle, core, after):
    gs, from_pair = exchange_wait("pair_exchange_wait_" + tag, _pair_copies, handle, after)
    parts = [pair_add("pair_add_%s_%d" % (tag, t), g, r, core) for t, (g, r) in enumerate(zip(gs, from_pair))]
    return exchange_start("chip_exchange_start_" + tag, _chip_copies, 3, parts)


def reduce_scatter_finish(tag, handle, chip, after):
    parts, landed = exchange_wait("chip_exchange_wait_" + tag, _chip_copies, handle, after)
    return [chip_add("chip_add_%s_%d" % (tag, t), p, r, chip) for t, (p, r) in enumerate(zip(parts, landed))]


def sum_devices(name, g):
    _, R, C = g.shape

    def body(g_ref, o_ref):
        acc = g_ref[0]
        for d in range(1, N_DEV):
            acc = acc + g_ref[d]
        o_ref[...] = acc

    return pl.pallas_call(body, name=name, out_shape=jax.ShapeDtypeStruct((R, C), F32),
                          compiler_params=pltpu.CompilerParams(vmem_limit_bytes=VMEM_LIMIT_BYTES))(g)


_WEIGHTS = ["ada_w", "ada_b", "ln1_g", "ln1_b", "ln2_g", "ln2_b", "ffn_w1", "ffn_w3", "ffn_w2", "mla_w_down", "mla_q_norm",
            "mla_w_uq", "mla_kv_norm", "mla_w_uk", "mla_w_uv", "mla_w_o", "fox_w_in", "fox_b_f", "fox_w_o"]
_SMALL = ["ada_b", "ln1_g", "ln1_b", "ln2_g", "ln2_b", "mla_q_norm", "mla_kv_norm", "fox_b_f"]


def _pad_cols(a, width):
    return jnp.pad(a, ((0, 0), (0, width - a.shape[1])))


def _small_pack(parts, cols):
    rows = [jnp.pad(p.reshape(-1), (0, _round_up(p.size, cols) - p.size)).reshape(-1, cols) for p in parts]
    out = jnp.concatenate(rows, axis=0)
    return jnp.pad(out, ((0, _round_up(out.shape[0], 8) - out.shape[0]), (0, 0)))


def kernel(x, c, positions, ada_w, ada_b, ln1_g, ln1_b, ln2_g, ln2_b, ffn_w1, ffn_w3, ffn_w2, mla_w_down, mla_q_norm, mla_w_uq, mla_kv_norm, mla_w_uk, mla_w_uv, mla_w_o, fox_w_in, fox_b_f, fox_w_o, loss_target, m_ada_w, m_ada_b, m_ln1_g, m_ln1_b, m_ln2_g, m_ln2_b, m_ffn_w1, m_ffn_w3, m_ffn_w2, m_mla_w_down, m_mla_q_norm, m_mla_w_uq, m_mla_kv_norm, m_mla_w_uk, m_mla_w_uv, m_mla_w_o, m_fox_w_in, m_fox_b_f, m_fox_w_o, v_ada_w, v_ada_b, v_ln1_g, v_ln1_b, v_ln2_g, v_ln2_b, v_ffn_w1, v_ffn_w3, v_ffn_w2, v_mla_w_down, v_mla_q_norm, v_mla_w_uq, v_mla_kv_norm, v_mla_w_uk, v_mla_w_uv, v_mla_w_o, v_fox_w_in, v_fox_b_f, v_fox_w_o):
    env = dict(locals())
    W = {n: env[n] for n in _WEIGHTS}
    M = {n: env["m_" + n] for n in _WEIGHTS}
    V = {n: env["v_" + n] for n in _WEIGHTS}

    T, D = x.shape[1], x.shape[2]
    L = ada_w.shape[0]
    QL, KVL = mla_q_norm.shape[1], mla_kv_norm.shape[1]
    HA = mla_w_uq.shape[2] * N_DEV // (NOPE + ROPE)
    HB = fox_b_f.shape[1]
    f_loc = ffn_w1.shape[2]
    FP = _round_up(f_loc, LANE)
    n_ada = ada_w.shape[2]
    n_in = fox_w_in.shape[2]
    n_in_pad = _round_up(n_in, LANE)
    k_loc = mla_w_down.shape[1]
    n_uq, n_uk = mla_w_uq.shape[2], mla_w_uk.shape[2]
    assert D == HB * HEAD == HA * HEAD and n_ada * N_DEV == 6 * D
    alpha = (2 * L) ** 0.25
    xi, yi, ci = _place()
    me = 4 * xi + 2 * yi + ci
    core = jnp.reshape(ci, (1,)).astype(jnp.int32)
    chip = jnp.reshape(2 * xi + yi, (1,)).astype(jnp.int32)

    x2 = x[0]
    tgt = loss_target[0]

    c_all = all_gather("gather_c", [jnp.pad(c, ((0, 7), (0, 0)))], True)[0].reshape(N_DEV, 8, D)[:, 0]
    ada_b_cols = lax.dynamic_slice_in_dim(ada_b, me * n_ada, n_ada, axis=1)
    mod_part = ada_fwd("ada_fwd", c_all, ada_w, ada_b_cols)
    mod_all = all_gather("gather_mod", [mod_part.reshape(L * N_DEV, n_ada)], True)[0].reshape(N_DEV, L, N_DEV, n_ada)
    mod = lax.dynamic_index_in_dim(mod_all, me, axis=2, keepdims=False)
    mod = jnp.moveaxis(mod, 0, 1).reshape(L, 6, D)

    fox_cols = 3 * D + LANE

    def bf(a, pad_rows=0, pad_cols=0):
        return jnp.pad(a.astype(BF), ((0, pad_rows), (0, pad_cols)))

    groups = []
    for i in range(L):
        j = i // 2
        if i % 2 == 0:
            groups.append(("mla%d" % i, [bf(mla_w_down[j]), bf(mla_w_uq[j]), bf(mla_w_uk[j]), bf(mla_w_uv[j]), bf(mla_w_o[j])]))
        else:
            groups.append(("fox%d" % i, [bf(fox_w_in[j], 0, n_in_pad - n_in), bf(fox_w_o[j])]))
        groups.append(("ffn%d" % i, [bf(ffn_w1[i], 0, FP - f_loc), bf(ffn_w3[i], 0, FP - f_loc), bf(ffn_w2[i], FP - f_loc, 0)]))
    handles = {}
    tok = mod
    for gname, shards in groups:
        handles[gname] = gather_start("gather_start_" + gname, shards, tok)
        tok = handles[gname][3]
    mod = mod + tok[0, 0]

    forwarding = {}
    order_of = [gname for gname, _ in groups]

    def prefetch(gname, after):
        if gname is None or order_of.index(gname) < 2:
            return 0.0
        lands = gather_wait("gather_wait_" + gname, handles[gname], after)
        forwarding[gname] = forward_start("gather_forward_start_" + gname, lands)
        return forwarding[gname][2][0, 0]

    def gathered(gname, after):
        if gname in forwarding:
            lands = forward_wait("gather_forward_wait_" + gname, forwarding[gname], after)
        else:
            lands = gather_wait("gather_wait_" + gname, handles[gname], after)
            lands = gather_forward("gather_forward_" + gname, lands)
        return [a.reshape((N_DEV,) + s.shape) for a, s in zip(lands, dict(groups)[gname])]

    def cols_whole(g, keep=None):
        return jnp.moveaxis(g[:, :, :keep], 0, 1).reshape(g.shape[1], -1)

    def modv(i, k):
        return mod[i, k][None, :]

    b_f = _pad_cols(fox_b_f, LANE)

    inv_freq = ROPE_THETA ** (-jnp.arange(0, ROPE, 2, dtype=F32) / ROPE)
    ang = jnp.repeat(positions[0].astype(F32)[:, None] * inv_freq, 2, axis=1)
    cos_k, sin_k = jnp.cos(ang), jnp.sin(ang)
    cos_q, sin_q = jnp.tile(cos_k, (1, HA)), jnp.tile(sin_k, (1, HA))
    rot_k, rot_q = _pair_rotation(ROPE), _pair_rotation(HA * ROPE)
    tk = min(ATT_TILE, T)
    mla_scale = (NOPE + ROPE) ** -0.5
    fox_scale = HEAD ** -0.5

    def modulate(tag, xin, sc, sh):
        return rowwise("modulate_" + tag, lambda xv, scv, shv: xv * (1.0 + scv) + shv, [xin], [sc, sh], [(D, BF)])[0]

    def post(tag, xin, y, g, gam, bet, nsc, nsh):
        def f(xv, yv, gv, gamv, betv, scv, shv):
            xhat, _ = _ln_stats(alpha, xv, yv, gv)
            out = xhat * gamv + betv
            return out, out * (1.0 + scv) + shv
        return rowwise("post_" + tag, f, [xin, y], [g, gam, bet, nsc, nsh], [(D, F32), (D, BF)])

    def post_bwd(tag, dxo, xin, y, g, gam):
        def f(dv, xv, yv, gv, gamv):
            xhat, rstd = _ln_stats(alpha, xv, yv, gv)
            dxh = dv * gamv
            dz = rstd * (dxh - _mean1(dxh) - xhat * _mean1(dxh * xhat))
            return alpha * dz, (1.0 + gv) * dz, _sum0(dv * xhat), _sum0(dv), _sum0(dz * yv)
        return rowwise("post_bwd_" + tag, f, [dxo, xin, y], [g, gam], [(D, F32), (D, BF)], [D, D, D])

    def join_bwd(tag, dxa, dh, sc, xprev, yprev, g, gam, bet):
        def f(dxav, dhv, xpv, ypv, scv, gv, gamv, betv):
            xhat, rstd = _ln_stats(alpha, xpv, ypv, gv)
            dv = dxav + dhv * (1.0 + scv)
            dxh = dv * gamv
            dz = rstd * (dxh - _mean1(dxh) - xhat * _mean1(dxh * xhat))
            return (alpha * dz, (1.0 + gv) * dz, _sum0(dhv * (xhat * gamv + betv)), _sum0(dhv),
                    _sum0(dv * xhat), _sum0(dv), _sum0(dz * ypv))
        return rowwise("join_bwd_" + tag, f, [dxa, dh, xprev, yprev], [sc, g, gam, bet], [(D, F32), (D, BF)], [D] * 5)

    def pre_bwd(tag, dxa, dh, xin, sc):
        def f(dxav, dhv, xv, scv):
            return dxav + dhv * (1.0 + scv), _sum0(dhv * xv), _sum0(dhv)
        return rowwise("pre_bwd_" + tag, f, [dxa, dh, xin], [sc], [(D, F32)], [D, D])

    saved = []
    xc = x2
    h = modulate("l0a", xc, modv(0, 1), modv(0, 0))
    for i in range(L):
        j = i // 2
        tag = "l%d" % i
        s = {"x_a": xc}
        s["h_a"] = h
        if i % 2 == 0:
            wdn, wuq, wuk, wuv, wo = gathered("mla%d" % i, xc)
            w_down, w_o = wdn.reshape(D, -1), wo.reshape(D, D)
            q3 = cols_whole(wuq).reshape(QL, HA, NOPE + ROPE)
            w_uq = jnp.concatenate([q3[..., :NOPE].reshape(QL, HA * NOPE), q3[..., NOPE:].reshape(QL, HA * ROPE)], axis=-1)
            w_ukv = jnp.concatenate([cols_whole(wuk), cols_whole(wuv)], axis=-1)
            s.update(w_down=w_down, w_o=w_o, w_uq=w_uq, w_ukv=w_ukv)
            lat = mm("mla_down_" + tag, h, w_down)

            def mid(latv, ck, sk, qg, kg, p):
                ql, kl, kp = latv[:, :QL], latv[:, QL:QL + KVL], latv[:, QL + KVL:]
                qn = ql * lax.rsqrt(_mean1(ql * ql) + RMS_EPS) * qg
                cn = kl * lax.rsqrt(_mean1(kl * kl) + RMS_EPS) * kg
                return qn, cn, kp * ck + _rot(kp, p) * sk

            qn, ckv, kpe = rowwise("mla_mid_" + tag, mid, [lat, cos_k, sin_k],
                                   [mla_q_norm[j][None], mla_kv_norm[j][None], rot_k], [(QL, BF), (KVL, BF), (ROPE, BF)])
            qf = mm("mla_uq_" + tag, qn, w_uq)

            def qrope(qv, cq, sq, p):
                qp = qv[:, HA * NOPE:]
                return (qp * cq + _rot(qp, p) * sq) * mla_scale

            qpe = rowwise("mla_qrope_" + tag, qrope, [qf, cos_q, sin_q], [rot_q], [(HA * ROPE, BF)])[0]
            qpe_h = jnp.moveaxis(qpe.reshape(T, HA, ROPE), 1, 0)
            kv = mm("mla_ukv_" + tag, ckv, w_ukv)
            ahead = prefetch("ffn%d" % i, kv)
            o, lse = attn_fwd("mla_attn_" + tag, "mla", mla_scale, HA, T, qf, 0, kv, 0, kv, HA, qpe_h, kpe)
            y = mm("mla_o_" + tag, o, w_o)
            s.update(lat=lat, qn=qn, ckv=ckv, kpe=kpe, qf=qf, qpe_h=qpe_h, kv=kv, o=o, lse=lse)
        else:
            win, wo = gathered("fox%d" % i, xc)
            w_in, w_o = _pad_cols(cols_whole(win, keep=n_in), fox_cols), wo.reshape(D, D)
            s.update(w_in=w_in, w_o=w_o)
            proj = mm("fox_in_" + tag, h, w_in, bn=896)
            fz = proj[:, 3 * D:]
            cum = fox_cum("fox_cum_" + tag, fz, b_f[j][None])
            cum_h = cum[:, :HB].T
            ahead = prefetch("ffn%d" % i, cum)
            o, lse = attn_fwd("fox_attn_" + tag, "fox", fox_scale, HB, T, proj, 0, proj, HB, proj, 2 * HB,
                              cum_h.reshape(HB, T // tk, 1, tk), cum_h.reshape(HB, T, 1))
            y = mm("fox_o_" + tag, o, w_o)
            s.update(proj=proj, fz=fz, cum_h=cum_h, o=o, lse=lse)
        s["y_a"] = y
        xc, h = post(tag + "a", xc, y, modv(i, 2) + ahead, ln1_g[i][None], ln1_b[i][None], modv(i, 4), modv(i, 3))
        s["x_f"] = xc
        w1g, w3g, w2g = [w[:, None] for w in gathered("ffn%d" % i, xc)]
        a, b, gte = ffn_up("ffn_up_" + tag, h, w1g, w3g, 0)
        nxt = min(i + 1, L - 1)
        ahead = prefetch(None if i == L - 1 else ("mla%d" if nxt % 2 == 0 else "fox%d") % nxt, gte)
        y = ffn_down("ffn_down_" + tag, gte, w2g, 0)
        s.update(h_f=h, a=a, b=b, gte=gte, y_f=y, w1g=w1g, w3g=w3g, w2g=w2g)
        xc, h = post(tag + "f", xc, y, modv(i, 5) + ahead, ln2_g[i][None], ln2_b[i][None], modv(nxt, 1), modv(nxt, 0))
        saved.append(s)

    def loss_fn(yv, tv):
        e = yv - tv
        return e * (1.0 / D), jnp.zeros((1, LANE), F32) + _sum0(jnp.sum(e * e, axis=1, keepdims=True)) * (0.5 / D)

    dx, loss_part = rowwise("loss", loss_fn, [xc, tgt], [], [(D, F32)], [LANE])

    G = {n: [None] * W[n].shape[0] for n in _WEIGHTS if n != "ada_w"}
    pending, pair_open = [], []

    def scatter(kind, idx, tag, gs):
        handle = reduce_scatter_pair(tag, gs)
        pair_open.append((kind, idx, tag, handle))
        return handle[3][0, 0]

    def scatter_on(after):
        pkind, pidx, ptag, phandle = pair_open.pop()
        chips = reduce_scatter_chips(ptag, phandle, core, after)
        pending.append((pkind, pidx, chips))
        return chips[3][0, 0]
    dmod = [[None] * 6 for _ in range(L)]
    s = saved[L - 1]
    dxa, dy, G["ln2_g"][L - 1], G["ln2_b"][L - 1], dmod[L - 1][5] = post_bwd(
        "l%df" % (L - 1), dx, s["x_f"], s["y_f"], modv(L - 1, 5), ln2_g[L - 1][None])
    for i in reversed(range(L)):
        j = i // 2
        tag = "l%d" % i
        s = saved[i]
        g_w2 = ffn_out_grad("ffn_w2_grad_" + tag, s["gte"], dy)
        da, db = ffn_down_bwd("ffn_down_bwd_" + tag, dy, s["w2g"], s["a"], s["b"], 0)
        order = scatter_on(da) if pair_open else 0.0
        g_w1 = ffn_in_grad("ffn_w1_grad_" + tag, s["h_f"], da)
        g_w3 = ffn_in_grad("ffn_w3_grad_" + tag, s["h_f"], db)
        dh = ffn_dh("ffn_dh_" + tag, da, db, s["w1g"], s["w3g"], 0)
        order = order + scatter("ffn", i, "ffn_" + tag, [g_w1, g_w3, g_w2])
        if i == 0:
            order = order + scatter_on(dh)
        dxa, dy, dmod[i][4], dmod[i][3], G["ln1_g"][i], G["ln1_b"][i], dmod[i][2] = join_bwd(
            tag + "f", dxa, dh, modv(i, 4) + order, s["x_a"], s["y_a"], modv(i, 2), ln1_g[i][None], ln1_b[i][None])
        if i % 2 == 0:
            g_o = mm("mla_o_grad_" + tag, s["o"], dy, ta=True, out_dtype=BF)
            do = mm("mla_do_" + tag, dy, s["w_o"], tb=True, out_dtype=BF)
            delta = attn_delta("mla_delta_" + tag, do, s["o"], HA, tk)
            dqn_, dkn_, dv_, dqpe_h, dkpe = attn_bwd("mla_attn_bwd_" + tag, "mla", mla_scale, HA, T, s["qf"], 0, s["kv"], 0,
                                                     s["kv"], HA, s["qpe_h"], s["kpe"], do, s["lse"], delta)
            order = scatter_on(dkpe) if pair_open else 0.0
            dkv = jnp.concatenate([dkn_, dv_], axis=1)
            g_kv = mm("mla_ukv_grad_" + tag, s["ckv"], dkv, ta=True, out_dtype=BF)
            dckv = mm("mla_dckv_" + tag, dkv, s["w_ukv"], tb=True)

            def qrope_bwd(dv, cq, sq, p):
                dv = dv * mla_scale
                return dv * cq - _rot(dv * sq, p)

            dqpe = rowwise("mla_qrope_bwd_" + tag, qrope_bwd, [jnp.moveaxis(dqpe_h, 0, 1).reshape(T, HA * ROPE), cos_q, sin_q],
                           [rot_q], [(HA * ROPE, BF)])[0]
            dq = jnp.concatenate([dqn_, dqpe], axis=1)
            g_uq = mm("mla_uq_grad_" + tag, s["qn"], dq, ta=True, out_dtype=BF)
            g_uq = jnp.concatenate([g_uq[:, :HA * NOPE].reshape(QL, HA, NOPE), g_uq[:, HA * NOPE:].reshape(QL, HA, ROPE)],
                                   axis=-1).reshape(QL, N_DEV, n_uq)
            dqn = mm("mla_dqn_" + tag, dq, s["w_uq"], tb=True)

            def mid_bwd(latv, dqv, dcv, dkp, ck, sk, qg, kg, p):
                ql, kl = latv[:, :QL], latv[:, QL:QL + KVL]
                rq = lax.rsqrt(_mean1(ql * ql) + RMS_EPS)
                rk = lax.rsqrt(_mean1(kl * kl) + RMS_EPS)
                uq, uk = dqv * qg, dcv * kg
                dql = rq * uq - ql * (rq * rq * rq) * _mean1(uq * ql)
                dkl = rk * uk - kl * (rk * rk * rk) * _mean1(uk * kl)
                return dql, dkl, dkp * ck - _rot(dkp * sk, p), _sum0(dqv * ql * rq), _sum0(dcv * kl * rk)

            dql, dkl, dkp, G["mla_q_norm"][j], G["mla_kv_norm"][j] = rowwise(
                "mla_mid_bwd_" + tag, mid_bwd, [s["lat"], dqn, dckv, dkpe, cos_k, sin_k],
                [mla_q_norm[j][None], mla_kv_norm[j][None], rot_k], [(QL, BF), (KVL, BF), (ROPE, BF)], [QL, KVL])
            dlat = jnp.concatenate([dql, dkl, dkp], axis=1)
            g_dn = mm("mla_down_grad_" + tag, s["h_a"], dlat, ta=True, out_dtype=BF)
            dh = mm("mla_dh_" + tag, dlat, s["w_down"], tb=True)
            g_kv = jnp.moveaxis(g_kv.reshape(KVL, 2, N_DEV, n_uk), 2, 0).reshape(N_DEV, KVL, 2 * n_uk)
            order = order + scatter("mla", j, "mla_" + tag, [
                g_dn.reshape(N_DEV, k_loc, -1), jnp.moveaxis(g_uq, 1, 0), g_kv, g_o.reshape(N_DEV, k_loc, D)])
        else:
            g_o = mm("fox_o_grad_" + tag, s["o"], dy, ta=True, out_dtype=BF)
            do = mm("fox_do_" + tag, dy, s["w_o"], tb=True, out_dtype=BF)
            cum_h = s["cum_h"]
            delta = attn_delta("fox_delta_" + tag, do, s["o"], HB, tk)
            dq_, dk_, dv_, dcq, dck = attn_bwd("fox_attn_bwd_" + tag, "fox", fox_scale, HB, T, s["proj"], 0, s["proj"], HB,
                                               s["proj"], 2 * HB, cum_h.reshape(HB, T // tk, 1, tk), cum_h.reshape(HB, T, 1),
                                               do, s["lse"], delta)
            order = scatter_on(dck) if pair_open else 0.0
            dcum = _pad_cols((dcq.reshape(HB, T) + dck.reshape(HB, T)).T, LANE)
            df, db_f = fox_cum_bwd("fox_cum_bwd_" + tag, s["fz"], b_f[j][None], dcum)
            G["fox_b_f"][j] = db_f[:, :HB]
            dproj = jnp.concatenate([dq_, dk_, dv_, df], axis=1)
            g_in = mm("fox_in_grad_" + tag, s["h_a"], dproj, ta=True, out_dtype=BF, bn=896)[:, :N_DEV * n_in]
            dh = mm("fox_dh_" + tag, dproj, s["w_in"], tb=True, bk=896)
            g_in = jnp.pad(jnp.moveaxis(g_in.reshape(D, N_DEV, n_in), 1, 0), ((0, 0), (0, 0), (0, n_in_pad - n_in)))
            order = order + scatter("fox", j, "fox_" + tag, [g_in, g_o.reshape(N_DEV, k_loc, D)])
        if i > 0:
            p = saved[i - 1]
            dxa, dy, dmod[i][1], dmod[i][0], G["ln2_g"][i - 1], G["ln2_b"][i - 1], dmod[i - 1][5] = join_bwd(
                tag + "a", dxa, dh, modv(i, 1) + order, p["x_f"], p["y_f"], modv(i - 1, 5), ln2_g[i - 1][None], ln2_b[i - 1][None])
        else:
            dx, dmod[i][1], dmod[i][0] = pre_bwd(tag + "a", dxa, dh, s["x_a"], modv(i, 1) + order)

    scatter_on(dx)

    def finish(kind, idx, handle, after):
        res = reduce_scatter_finish("%s_%d" % (kind, idx), handle, chip, after)
        if kind == "ffn":
            G["ffn_w1"][idx], G["ffn_w3"][idx], G["ffn_w2"][idx] = res[0][:, :f_loc], res[1][:, :f_loc], res[2][:f_loc]
        elif kind == "mla":
            G["mla_w_down"][idx], G["mla_w_uq"][idx], G["mla_w_o"][idx] = res[0], res[1], res[3]
            G["mla_w_uk"][idx], G["mla_w_uv"][idx] = res[2][:, :n_uk], res[2][:, n_uk:]
        else:
            G["fox_w_in"][idx], G["fox_w_o"][idx] = res[0][:, :n_in], res[1]

    for kind, idx, handle in pending[:-1]:
        finish(kind, idx, handle, dx)

    dmod_mine = jnp.concatenate([jnp.concatenate(r, axis=1) for r in dmod], axis=0)
    small_parts = [dmod_mine] + [jnp.concatenate(G[n], axis=0) for n in _SMALL[1:]] + [loss_part]
    small = _small_pack(small_parts, 512)
    small_all = all_gather("gather_small", [small], True)[0].reshape(N_DEV, -1, 512)
    small_sum = sum_devices("sum_small", small_all)
    sums, off = [], 0
    for p in small_parts:
        nrow = _round_up(p.size, 512) // 512
        sums.append(small_sum[off:off + nrow].reshape(-1)[:p.size].reshape(p.shape))
        off += nrow
    grads = {n: gsum for n, gsum in zip(_SMALL, sums[:-1])}
    loss = sums[-1][0, 0]
    nrow = L * 6 * D // 512
    dmod_all = small_all[:, :nrow].reshape(N_DEV, L, 6 * D)
    dmod_cols = jnp.moveaxis(lax.dynamic_slice_in_dim(dmod_all, me * n_ada, n_ada, axis=2), 0, 1)

    delta, new_m, new_v = {}, {}, {}
    grads["ada_w"], delta["ada_w"], new_m["ada_w"], new_v["ada_w"] = ada_bwd_adamw(
        "ada_grad_adamw", c_all, dmod_cols, ada_w, m_ada_w, v_ada_w)
    last = ("mla_w_down", "mla_w_uq", "mla_w_uk", "mla_w_uv", "mla_w_o")
    for n in ("ffn_w1", "ffn_w3", "ffn_w2", "fox_w_in", "fox_w_o"):
        grads[n] = jnp.stack(G[n])
    for n in _WEIGHTS:
        if n != "ada_w" and n not in last:
            delta[n], new_m[n], new_v[n] = adamw("adamw_" + n, W[n], grads[n], M[n], V[n])
    finish(*pending[-1], delta["ffn_w2"])
    for n in last:
        grads[n] = jnp.stack(G[n])
        delta[n], new_m[n], new_v[n] = adamw("adamw_" + n, W[n], grads[n], M[n], V[n])

    return (loss, dx[None], *[grads[n] for n in _WEIGHTS], *[delta[n] for n in _WEIGHTS],
            *[new_m[n] for n in _WEIGHTS], *[new_v[n] for n in _WEIGHTS])
```

```python
import math

import numpy as np
import jax
import jax.numpy as jnp
from jax import lax
from jax.experimental import pallas as pl
from jax.experimental.pallas import tpu as pltpu

F32 = jnp.float32
BF = jnp.bfloat16
N_DEV = 8

CHUNK_SHIFT = 6
NOPE = 128
ROPE = 64
HEAD = 128
ROPE_THETA = 10000.0
LN_EPS = 1e-5
RMS_EPS = 1e-6
ADAM_LR = 0.001
ADAM_B1 = 0.9
ADAM_B2 = 0.999
ADAM_EPS = 1e-08
ADAM_WD = 0.01
ADAM_STEP = 10

VMEM_LIMIT_BYTES = 48 * 1024 * 1024
LANE = 128
NEG = -1e30


def _cp(*sem):
    return pltpu.CompilerParams(dimension_semantics=sem, vmem_limit_bytes=VMEM_LIMIT_BYTES)


def _pick(dim, pref):
    if dim <= pref:
        return dim
    for t in range(pref - pref % LANE, LANE - 1, -LANE):
        if dim % t == 0:
            return t
    return dim


def _round_up(v, m):
    return (v + m - 1) // m * m


def _sum0(v):
    return jnp.sum(v, axis=0, keepdims=True)


def _mean1(v):
    return jnp.mean(v, axis=-1, keepdims=True)


_NN = (((1,), (0,)), ((), ()))
_NT = (((1,), (1,)), ((), ()))
_TN = (((0,), (0,)), ((), ()))


def _mm_call(name, grid, ins, in_specs, dns, out_shape, out_spec, add=None, add_spec=None):
    npairs = len(dns)
    nk = grid[-1]
    has_add = add is not None

    def body(*refs):
        add_ref = refs[2 * npairs] if has_add else None
        o_ref = refs[2 * npairs + (1 if has_add else 0)]
        r = None
        for p in range(npairs):
            t = lax.dot_general(refs[2 * p][...].astype(BF), refs[2 * p + 1][...].astype(BF), dns[p], preferred_element_type=F32)
            r = t if r is None else r + t

        def fin(val):
            if has_add:
                val = val + add_ref[...]
            o_ref[...] = val.astype(o_ref.dtype)

        if nk == 1:
            fin(r)
        else:
            acc = refs[-1]
            k = pl.program_id(len(grid) - 1)

            @pl.when(k == 0)
            def _():
                acc[...] = r

            @pl.when(k > 0)
            def _():
                acc[...] += r

            @pl.when(k == nk - 1)
            def _():
                fin(acc[...])

    acc_shape = tuple(b for b in out_spec.block_shape if b is not None)
    args, specs = list(ins), list(in_specs)
    if has_add:
        args.append(add)
        specs.append(add_spec)
    return pl.pallas_call(
        body, name=name, grid=grid, in_specs=specs, out_specs=out_spec, out_shape=out_shape,
        scratch_shapes=[pltpu.VMEM(acc_shape, F32)] if nk > 1 else [],
        compiler_params=_cp(*(["parallel"] * (len(grid) - 1) + ["arbitrary"])),
    )(*args)


def mm(name, a, b, ta=False, tb=False, out_dtype=F32, add=None, bm=1024, bn=1024, bk=2048):
    M, K = (a.shape[1], a.shape[0]) if ta else a.shape
    N = b.shape[0] if tb else b.shape[1]
    bm, bn, bk = _pick(M, bm), _pick(N, bn), _pick(K, bk)
    a_spec = pl.BlockSpec((bk, bm), lambda i, j, k: (k, i)) if ta else pl.BlockSpec((bm, bk), lambda i, j, k: (i, k))
    b_spec = pl.BlockSpec((bn, bk), lambda i, j, k: (j, k)) if tb else pl.BlockSpec((bk, bn), lambda i, j, k: (k, j))
    dn = (((0 if ta else 1,), (1 if tb else 0,)), ((), ()))
    tile = pl.BlockSpec((bm, bn), lambda i, j, k: (i, j))
    return _mm_call(name, (M // bm, N // bn, K // bk), [a, b], [a_spec, b_spec], [dn],
                    jax.ShapeDtypeStruct((M, N), out_dtype), tile, add, tile)


def rowwise(name, fn, rows, vecs, out_rows, out_accs=(), tm=256):
    T = rows[0].shape[0]
    tm = min(tm, T)
    nr, nv, no = len(rows), len(vecs), len(out_rows)

    def body(*refs):
        outs = fn(*[r[...] for r in refs[:nr + nv]])
        if not isinstance(outs, (tuple, list)):
            outs = (outs,)
        o_refs = refs[nr + nv:nr + nv + no]
        a_refs = refs[nr + nv + no:]
        for r, o in zip(o_refs, outs[:no]):
            r[...] = o.astype(r.dtype)
        if a_refs:
            @pl.when(pl.program_id(0) == 0)
            def _():
                for r in a_refs:
                    r[...] = jnp.zeros_like(r)

            for r, o in zip(a_refs, outs[no:]):
                r[...] += o

    in_specs = [pl.BlockSpec((tm, r.shape[1]), lambda i: (i, 0)) for r in rows]
    in_specs += [pl.BlockSpec(v.shape, lambda i: (0, 0)) for v in vecs]
    out_specs = [pl.BlockSpec((tm, w), lambda i: (i, 0)) for w, _ in out_rows]
    out_specs += [pl.BlockSpec((1, w), lambda i: (0, 0)) for w in out_accs]
    out_shape = [jax.ShapeDtypeStruct((T, w), dt) for w, dt in out_rows]
    out_shape += [jax.ShapeDtypeStruct((1, w), F32) for w in out_accs]
    return pl.pallas_call(
        body, name=name, grid=(T // tm,), in_specs=in_specs, out_specs=out_specs, out_shape=out_shape,
        compiler_params=_cp("arbitrary"),
    )(*rows, *vecs)


def _rot(v, p):
    hi = v.astype(BF)
    lo = (v - hi.astype(F32)).astype(BF)
    return jnp.dot(hi, p, preferred_element_type=F32) + jnp.dot(lo, p, preferred_element_type=F32)


def _pair_rotation(width):
    p = np.zeros((width, width), np.float32)
    idx = np.arange(0, width, 2)
    p[idx + 1, idx] = -1.0
    p[idx, idx + 1] = 1.0
    return jnp.asarray(p, BF)


def _ln_stats(alpha, x, y, g):
    z = alpha * x + (1.0 + g) * y
    zc = z - _mean1(z)
    rstd = lax.rsqrt(_mean1(zc * zc) + LN_EPS)
    return zc * rstd, rstd


ATT_G_FWD = 4
ATT_G_BWD = 2
ATT_TILE = 256


def _visible(mode, ks, qs, t):
    krow = ks + lax.broadcasted_iota(jnp.int32, (t, t), 0)
    qcol = qs + lax.broadcasted_iota(jnp.int32, (t, t), 1)
    if mode == "mla":
        return lax.shift_right_logical(qcol, CHUNK_SHIFT) >= lax.shift_right_logical(krow, CHUNK_SHIFT)
    return qcol >= krow


def _scores(mode, k, q, e_k, e_q, ks, qs, t, masked):
    s = lax.dot_general(k, q, _NT, preferred_element_type=F32)
    if mode == "mla":
        s = s + lax.dot_general(e_k, e_q, _NT, preferred_element_type=F32)
    else:
        s = s + e_q - e_k
    if masked:
        s = jnp.where(_visible(mode, ks, qs, t), s, NEG)
    return s


def _attn_specs(mode, G, T, t, q_off, k_off, v_off):
    def col(off):
        return pl.BlockSpec((T, G * HEAD), lambda h: (0, off // G + h))

    stat = pl.BlockSpec((G, T // t, 1, t), lambda h: (h, 0, 0, 0))
    if mode == "mla":
        e_specs = [pl.BlockSpec((G, T, ROPE), lambda h: (h, 0, 0)), pl.BlockSpec((T, ROPE), lambda h: (0, 0))]
    else:
        e_specs = [stat, pl.BlockSpec((G, T, 1), lambda h: (h, 0, 0))]
    return [col(q_off), col(k_off), col(v_off)], e_specs, stat


def _head(ref, rows, g):
    return ref[rows, g * HEAD:(g + 1) * HEAD]


def attn_fwd(name, mode, scale, H, T, q_src, q_off, k_src, k_off, v_src, v_off, e_q, e_k):
    G, t = ATT_G_FWD, min(ATT_TILE, T)
    nq = T // t
    mla = mode == "mla"

    def body(q_ref, k_ref, v_ref, eq_ref, ek_ref, o_ref, lse_ref, acc_ref):
        def q_loop(qi, _):
            qs = pl.multiple_of(qi * t, t)
            qrows = pl.ds(qs, t)
            qh = [(_head(q_ref, qrows, g).astype(F32) * scale).astype(BF) for g in range(G)]
            eqh = [eq_ref[g, qrows, :] if mla else eq_ref[g, qi] for g in range(G)]
            acc_ref[...] = jnp.zeros_like(acc_ref)

            def step(j, carry, masked):
                ks = pl.multiple_of(j * t, t)
                krows = pl.ds(ks, t)
                out, ps, corrs = [], [], []
                ss = [_scores(mode, _head(k_ref, krows, g).astype(BF), qh[g], ek_ref[krows, :] if mla else ek_ref[g, krows, :],
                              eqh[g], ks, qs, t, masked) for g in range(G)]
                for g in range(G):
                    m, l = carry[g]
                    m_new = jnp.maximum(m, jnp.max(ss[g], axis=0, keepdims=True))
                    p = jnp.exp(ss[g] - m_new)
                    corr = jnp.exp(m - m_new)
                    out.append((m_new, corr * l + jnp.sum(p, axis=0, keepdims=True)))
                    ps.append(p.astype(BF))
                    corrs.append(corr)
                pvs = [lax.dot_general(_head(v_ref, krows, g).astype(BF), ps[g], _TN, preferred_element_type=F32) for g in range(G)]
                for g in range(G):
                    acc_ref[g] = corrs[g] * acc_ref[g] + pvs[g]
                return tuple(out)

            init = tuple((jnp.full((1, t), NEG, F32), jnp.zeros((1, t), F32)) for _ in range(G))
            carry = lax.fori_loop(0, qi, lambda j, cr: step(j, cr, False), init)
            carry = step(qi, carry, True)
            for g in range(G):
                m, l = carry[g]
                o_ref[qrows, g * HEAD:(g + 1) * HEAD] = (acc_ref[g] / l).T.astype(o_ref.dtype)
                lse_ref[g, qi] = m + jnp.log(l)
            return 0

        lax.fori_loop(0, nq, q_loop, 0)

    cols, e_specs, stat = _attn_specs(mode, G, T, t, q_off, k_off, v_off)
    return pl.pallas_call(
        body, name=name, grid=(H // G,), in_specs=cols + e_specs,
        out_specs=[pl.BlockSpec((T, G * HEAD), lambda h: (0, h)), stat],
        out_shape=[jax.ShapeDtypeStruct((T, H * HEAD), BF), jax.ShapeDtypeStruct((H, nq, 1, t), F32)],
        scratch_shapes=[pltpu.VMEM((G, HEAD, t), F32)],
        compiler_params=_cp("parallel"),
    )(q_src, k_src, v_src, e_q, e_k)


def attn_bwd(name, mode, scale, H, T, q_src, q_off, k_src, k_off, v_src, v_off, e_q, e_k, do, lse, delta):
    G, t = ATT_G_BWD, min(ATT_TILE, T)
    nq = T // t
    mla = mode == "mla"

    def body(q_ref, k_ref, v_ref, eq_ref, ek_ref, do_ref, lse_ref, dl_ref,
             dq_ref, dk_ref, dv_ref, deq_ref, dek_ref, dk_acc, dv_acc, dq_acc, deq_acc):
        dk_acc[...] = jnp.zeros_like(dk_acc)
        dv_acc[...] = jnp.zeros_like(dv_acc)
        if mla:
            @pl.when(pl.program_id(0) == 0)
            def _():
                dek_ref[...] = jnp.zeros_like(dek_ref)
        else:
            dek_ref[...] = jnp.zeros_like(dek_ref)

        def q_loop(qi, _):
            qs = pl.multiple_of(qi * t, t)
            qrows = pl.ds(qs, t)
            qh = [(_head(q_ref, qrows, g).astype(F32) * scale).astype(BF) for g in range(G)]
            eqh = [eq_ref[g, qrows, :] if mla else eq_ref[g, qi] for g in range(G)]
            doh = [_head(do_ref, qrows, g) for g in range(G)]
            lse_q = [lse_ref[g, qi] for g in range(G)]
            dl_q = [dl_ref[g, qi] for g in range(G)]
            dq_acc[...] = jnp.zeros_like(dq_acc)
            deq_acc[...] = jnp.zeros_like(deq_acc)

            def step(j, carry, masked):
                ks = pl.multiple_of(j * t, t)
                krows = pl.ds(ks, t)
                kh = [_head(k_ref, krows, g).astype(BF) for g in range(G)]
                ekh = [ek_ref[krows, :] if mla else ek_ref[g, krows, :] for g in range(G)]
                ss = [_scores(mode, kh[g], qh[g], ekh[g], eqh[g], ks, qs, t, masked) for g in range(G)]
                dps = [lax.dot_general(_head(v_ref, krows, g).astype(BF), doh[g], _NT, preferred_element_type=F32) for g in range(G)]
                pbs, dss, dsbs = [], [], []
                for g in range(G):
                    p = jnp.exp(ss[g] - lse_q[g])
                    ds = p * (dps[g] - dl_q[g])
                    pbs.append(p.astype(BF))
                    dss.append(ds)
                    dsbs.append(ds.astype(BF))
                for g in range(G):
                    dv_acc[g, krows, :] += jnp.dot(pbs[g], doh[g], preferred_element_type=F32)
                    dk_acc[g, krows, :] += jnp.dot(dsbs[g], qh[g], preferred_element_type=F32)
                    dq_acc[g] += lax.dot_general(dsbs[g], kh[g], _TN, preferred_element_type=F32)
                    if mla:
                        deq_acc[g] += lax.dot_general(dsbs[g], ekh[g], _TN, preferred_element_type=F32)
                        dek_ref[krows, :] += jnp.dot(dsbs[g], eqh[g], preferred_element_type=F32)
                    else:
                        deq_acc[g] += jnp.sum(dss[g], axis=0, keepdims=True)
                        dek_ref[g, krows, :] -= jnp.sum(dss[g], axis=1, keepdims=True)
                return carry

            lax.fori_loop(0, qi, lambda j, cr: step(j, cr, False), 0)
            step(qi, 0, True)
            for g in range(G):
                dq_ref[qrows, g * HEAD:(g + 1) * HEAD] = (dq_acc[g] * scale).astype(dq_ref.dtype)
                if mla:
                    deq_ref[g, qrows, :] = deq_acc[g]
                else:
                    deq_ref[g, qi] = deq_acc[g]
            return 0

        lax.fori_loop(0, nq, q_loop, 0)
        for g in range(G):
            dk_ref[:, g * HEAD:(g + 1) * HEAD] = dk_acc[g].astype(dk_ref.dtype)
            dv_ref[:, g * HEAD:(g + 1) * HEAD] = dv_acc[g].astype(dv_ref.dtype)

    cols, e_specs, stat = _attn_specs(mode, G, T, t, q_off, k_off, v_off)
    heads = pl.BlockSpec((T, G * HEAD), lambda h: (0, h))
    if mla:
        de_shapes = [jax.ShapeDtypeStruct((H, T, ROPE), F32), jax.ShapeDtypeStruct((T, ROPE), F32)]
        deq_scratch = pltpu.VMEM((G, t, ROPE), F32)
    else:
        de_shapes = [jax.ShapeDtypeStruct((H, nq, 1, t), F32), jax.ShapeDtypeStruct((H, T, 1), F32)]
        deq_scratch = pltpu.VMEM((G, 1, t), F32)
    return pl.pallas_call(
        body, name=name, grid=(H // G,),
        in_specs=cols + e_specs + [heads, stat, stat],
        out_specs=[heads, heads, heads] + e_specs,
        out_shape=[jax.ShapeDtypeStruct((T, H * HEAD), BF)] * 3 + de_shapes,
        scratch_shapes=[pltpu.VMEM((G, T, HEAD), F32), pltpu.VMEM((G, T, HEAD), F32), pltpu.VMEM((G, t, HEAD), F32), deq_scratch],
        compiler_params=_cp("arbitrary"),
    )(q_src, k_src, v_src, e_q, e_k, do, lse, delta)


def attn_delta(name, do, o, H, t):
    T, W = do.shape
    ind = np.zeros((W, LANE), np.float32)
    ind[np.arange(W), np.arange(W) // HEAD] = 1.0

    def f(dv, ov, e):
        return _rot(dv.astype(F32) * ov.astype(F32), e)

    d = rowwise(name, f, [do, o], [jnp.asarray(ind, BF)], [(LANE, F32)])[0]
    return d[:, :H].T.reshape(H, T // t, 1, t)


def fox_cum(name, f, b):
    T = f.shape[0]

    def body(f_ref, b_ref, cum_ref):
        z = f_ref[...] + b_ref[...]
        v = jnp.minimum(z, 0.0) - jnp.log(1.0 + jnp.exp(-jnp.abs(z)))
        row = lax.broadcasted_iota(jnp.int32, v.shape, 0)
        s = 1
        while s < T:
            v = v + jnp.where(row >= s, pltpu.roll(v, s, 0), 0.0)
            s *= 2
        cum_ref[...] = v

    return pl.pallas_call(body, name=name, out_shape=jax.ShapeDtypeStruct(f.shape, F32),
                          compiler_params=pltpu.CompilerParams(vmem_limit_bytes=VMEM_LIMIT_BYTES))(f, b)


def fox_cum_bwd(name, f, b, dcum):
    T = f.shape[0]

    def body(f_ref, b_ref, dc_ref, df_ref, db_ref):
        v = dc_ref[...]
        row = lax.broadcasted_iota(jnp.int32, v.shape, 0)
        s = 1
        while s < T:
            v = v + jnp.where(row < T - s, pltpu.roll(v, T - s, 0), 0.0)
            s *= 2
        z = f_ref[...] + b_ref[...]
        df = v / (1.0 + jnp.exp(z))
        df_ref[...] = df.astype(df_ref.dtype)
        db_ref[...] = _sum0(df)

    return pl.pallas_call(body, name=name,
                          out_shape=[jax.ShapeDtypeStruct(f.shape, BF), jax.ShapeDtypeStruct((1, f.shape[1]), F32)],
                          compiler_params=pltpu.CompilerParams(vmem_limit_bytes=VMEM_LIMIT_BYTES))(f, b, dcum)


def ffn_up(name, h, w1g, w3g, l):
    T, D = h.shape
    FP = w1g.shape[3]
    bm = _pick(T, 1024)

    def body(h_ref, w1_ref, w3_ref, a_ref, b_ref, g_ref):
        hv = h_ref[...]
        a = jnp.dot(hv, w1_ref[...], preferred_element_type=F32)
        b = jnp.dot(hv, w3_ref[...], preferred_element_type=F32)
        a_ref[...] = a.astype(BF)
        b_ref[...] = b.astype(BF)
        g_ref[...] = (a * (1.0 / (1.0 + jnp.exp(-a))) * b).astype(BF)

    w_spec = pl.BlockSpec((None, None, D, FP), lambda i, d: (d, l, 0, 0))
    tile = pl.BlockSpec((None, bm, FP), lambda i, d: (d, i, 0))
    return pl.pallas_call(
        body, name=name, grid=(T // bm, N_DEV), in_specs=[pl.BlockSpec((bm, D), lambda i, d: (i, 0)), w_spec, w_spec],
        out_specs=[tile, tile, tile], out_shape=[jax.ShapeDtypeStruct((N_DEV, T, FP), BF)] * 3,
        compiler_params=_cp("parallel", "parallel"),
    )(h, w1g, w3g)


def ffn_down(name, g, w2g, l):
    _, T, FP = g.shape
    D = w2g.shape[3]
    bm, bn = _pick(T, 1024), _pick(D, 1024)
    return _mm_call(
        name, (T // bm, D // bn, N_DEV), [g, w2g],
        [pl.BlockSpec((None, bm, FP), lambda i, j, d: (d, i, 0)), pl.BlockSpec((None, None, FP, bn), lambda i, j, d: (d, l, 0, j))],
        [_NN], jax.ShapeDtypeStruct((T, D), F32), pl.BlockSpec((bm, bn), lambda i, j, d: (i, j)))


def ffn_down_bwd(name, dy, w2g, a, b, l):
    T, D = dy.shape
    FP = w2g.shape[2]
    bm = _pick(T, 1024)

    def body(dy_ref, w2_ref, a_ref, b_ref, da_ref, db_ref):
        dg = lax.dot_general(dy_ref[...], w2_ref[...], _NT, preferred_element_type=F32)
        av = a_ref[...].astype(F32)
        bv = b_ref[...].astype(F32)
        sig = 1.0 / (1.0 + jnp.exp(-av))
        da_ref[...] = (dg * bv * sig * (1.0 + av * (1.0 - sig))).astype(BF)
        db_ref[...] = (dg * av * sig).astype(BF)

    tile = pl.BlockSpec((None, bm, FP), lambda i, d: (d, i, 0))
    return pl.pallas_call(
        body, name=name, grid=(T // bm, N_DEV),
        in_specs=[pl.BlockSpec((bm, D), lambda i, d: (i, 0)), pl.BlockSpec((None, None, FP, D), lambda i, d: (d, l, 0, 0)), tile, tile],
        out_specs=[tile, tile], out_shape=[jax.ShapeDtypeStruct((N_DEV, T, FP), BF)] * 2,
        compiler_params=_cp("parallel", "parallel"),
    )(dy, w2g, a, b)


def ffn_dh(name, da, db, w1g, w3g, l):
    _, T, FP = da.shape
    D = w1g.shape[2]
    bm, bn = _pick(T, 1024), _pick(D, 1024)
    act = pl.BlockSpec((None, bm, FP), lambda i, j, d: (d, i, 0))
    wgt = pl.BlockSpec((None, None, bn, FP), lambda i, j, d: (d, l, j, 0))
    return _mm_call(name, (T // bm, D // bn, N_DEV), [da, w1g, db, w3g], [act, wgt, act, wgt], [_NT, _NT],
                    jax.ShapeDtypeStruct((T, D), F32), pl.BlockSpec((bm, bn), lambda i, j, d: (i, j)))


def ffn_in_grad(name, h, dact):
    T, D = h.shape
    FP = dact.shape[2]
    bm = _pick(D, 1024)
    return _mm_call(
        name, (D // bm, N_DEV, 1), [h, dact],
        [pl.BlockSpec((T, bm), lambda i, d, k: (0, i)), pl.BlockSpec((None, T, FP), lambda i, d, k: (d, 0, 0))],
        [_TN], jax.ShapeDtypeStruct((N_DEV, D, FP), BF), pl.BlockSpec((None, bm, FP), lambda i, d, k: (d, i, 0)))


def ffn_out_grad(name, g, dy):
    _, T, FP = g.shape
    D = dy.shape[1]
    bn = _pick(D, 1024)
    return _mm_call(
        name, (N_DEV, D // bn, 1), [g, dy],
        [pl.BlockSpec((None, T, FP), lambda d, j, k: (d, 0, 0)), pl.BlockSpec((T, bn), lambda d, j, k: (0, j))],
        [_TN], jax.ShapeDtypeStruct((N_DEV, FP, D), BF), pl.BlockSpec((None, FP, bn), lambda d, j, k: (d, 0, j)))


def _adamw_math(w, g, m, v):
    m = ADAM_B1 * m + (1.0 - ADAM_B1) * g
    v = ADAM_B2 * v + (1.0 - ADAM_B2) * (g * g)
    m_hat = m / (1.0 - ADAM_B1 ** ADAM_STEP)
    v_hat = v / (1.0 - ADAM_B2 ** ADAM_STEP)
    delta = -ADAM_LR * (m_hat / (jnp.sqrt(v_hat) + ADAM_EPS) + ADAM_WD * w)
    return delta, m, v


def adamw(name, w, g, m, v):
    shape = w.shape
    n = shape[-1]
    rows = math.prod(shape[:-1])
    tr = rows if rows <= 512 else 256
    assert rows % tr == 0

    def body(w_ref, g_ref, m_ref, v_ref, d_ref, mo_ref, vo_ref):
        d, mn, vn = _adamw_math(w_ref[...], g_ref[...], m_ref[...], v_ref[...])
        d_ref[...] = d
        mo_ref[...] = mn
        vo_ref[...] = vn

    blk = pl.BlockSpec((tr, n), lambda i: (i, 0))
    outs = pl.pallas_call(
        body, name=name, grid=(rows // tr,), in_specs=[blk] * 4, out_specs=[blk] * 3,
        out_shape=[jax.ShapeDtypeStruct((rows, n), F32)] * 3, compiler_params=_cp("parallel"),
    )(*[t.reshape(rows, n) for t in (w, g, m, v)])
    return [t.reshape(shape) for t in outs]


def _silu(v):
    return v * (1.0 / (1.0 + jnp.exp(-v)))


def ada_fwd(name, c_all, ada_w, ada_b_cols):
    L, D, n = ada_w.shape
    bn = _pick(n, 512)

    def body(c_ref, w_ref, b_ref, o_ref):
        act = _silu(c_ref[...]).astype(BF)
        o_ref[...] = jnp.dot(act, w_ref[...].astype(BF), preferred_element_type=F32) + b_ref[...]

    return pl.pallas_call(
        body, name=name, grid=(L, n // bn),
        in_specs=[pl.BlockSpec((N_DEV, D), lambda l, j: (0, 0)), pl.BlockSpec((None, D, bn), lambda l, j: (l, 0, j)),
                  pl.BlockSpec((None, 1, bn), lambda l, j: (l, 0, j))],
        out_specs=pl.BlockSpec((None, N_DEV, bn), lambda l, j: (l, 0, j)),
        out_shape=jax.ShapeDtypeStruct((L, N_DEV, n), F32), compiler_params=_cp("parallel", "parallel"),
    )(c_all, ada_w, ada_b_cols.reshape(L, 1, n))


def ada_bwd_adamw(name, c_all, dmod_cols, w, m, v):
    L, D, n = w.shape
    tr = _pick(D, 256)

    def body(c_ref, dm_ref, w_ref, m_ref, v_ref, g_ref, d_ref, mo_ref, vo_ref):
        act = _silu(c_ref[...]).astype(BF)
        g = lax.dot_general(act, dm_ref[...].astype(BF), _TN, preferred_element_type=F32)
        d, mn, vn = _adamw_math(w_ref[...], g, m_ref[...], v_ref[...])
        g_ref[...] = g
        d_ref[...] = d
        mo_ref[...] = mn
        vo_ref[...] = vn

    blk = pl.BlockSpec((None, tr, n), lambda l, i: (l, i, 0))
    return pl.pallas_call(
        body, name=name, grid=(L, D // tr),
        in_specs=[pl.BlockSpec((N_DEV, tr), lambda l, i: (0, i)), pl.BlockSpec((None, N_DEV, n), lambda l, i: (l, 0, 0)),
                  blk, blk, blk],
        out_specs=[blk] * 4, out_shape=[jax.ShapeDtypeStruct((L, D, n), F32)] * 4,
        compiler_params=_cp("parallel", "parallel"),
    )(c_all, dmod_cols, w, m, v)


_MESH = pl.DeviceIdType.MESH
_ANY = pl.BlockSpec(memory_space=pl.ANY)
_CHIP_FLIPS = ((1, 0), (0, 1), (1, 1))


def _place():
    return lax.axis_index("x"), lax.axis_index("y"), lax.axis_index("c")


def _flip(v, f):
    return 1 - v if f else v


def all_gather(name, shards, in_vmem=False):
    nt = len(shards)

    def body(*refs):
        x_refs, out_refs = refs[:nt], refs[nt:2 * nt]
        send_sems, recv_sems, local_sems = refs[2 * nt:]
        x, y, c = _place()
        me, sibling = (x, y, c), (x, y, 1 - c)
        chips = [(_flip(x, fx), _flip(y, fy)) for fx, fy in _CHIP_FLIPS]

        def rows(t, px, py, pc):
            m_per = shards[t].shape[0]
            return out_refs[t].at[pl.ds((4 * px + 2 * py + pc) * m_per, m_per), :]

        def copy(t, k, block, to, src=None):
            return pltpu.make_async_remote_copy(
                src_ref=rows(t, *block) if src is None else src, dst_ref=rows(t, *block),
                send_sem=send_sems.at[7 * t + k], recv_sem=recv_sems.at[7 * t + k], device_id=to, device_id_type=_MESH)

        mine = [pltpu.make_async_copy(x_refs[t], rows(t, *me), local_sems.at[t]) for t in range(nt)]
        for cp in mine:
            cp.start()
        first = []
        for j, chip in enumerate(chips):
            first += [copy(t, 1 + j, me, (*chip, c), src=x_refs[t]) for t in range(nt)]
        first += [copy(t, 0, me, sibling, src=x_refs[t]) for t in range(nt)]
        for cp in first:
            cp.start()
        passed = []
        for j, chip in enumerate(chips):
            for t in range(nt):
                copy(t, 1 + j, (*chip, c), me).wait_recv()
                passed.append(copy(t, 4 + j, (*chip, c), sibling))
                passed[-1].start()
        for t in range(nt):
            copy(t, 0, sibling, me).wait_recv()
            for j, chip in enumerate(chips):
                copy(t, 4 + j, (*chip, 1 - c), me).wait_recv()
        for cp in first + passed:
            cp.wait_send()
        for cp in mine:
            cp.wait()

    space = pl.BlockSpec(memory_space=pltpu.VMEM) if in_vmem else _ANY
    return pl.pallas_call(
        body, name=name, out_shape=[jax.ShapeDtypeStruct((N_DEV * s.shape[0], s.shape[1]), s.dtype) for s in shards],
        in_specs=[space] * nt, out_specs=[space] * nt,
        scratch_shapes=[pltpu.SemaphoreType.DMA((7 * nt,)), pltpu.SemaphoreType.DMA((7 * nt,)), pltpu.SemaphoreType.DMA((nt,))],
    )(*shards)


_HBM = pl.BlockSpec(memory_space=pltpu.HBM)
_SEM = pl.BlockSpec(memory_space=pltpu.SEMAPHORE)
_EFFECT = pltpu.SideEffectType.DATAFLOW_SIDE_EFFECTING


def _in_hbm(a):
    return pltpu.with_memory_space_constraint(a, pltpu.HBM)


def _gather_first_copies(x_refs, land_refs, send_sems, recv_sems, local_sems):
    nt = len(x_refs)
    x, y, c = _place()
    me = (x, y, c)

    def rows(t, px, py, pc):
        m_per = x_refs[t].shape[0]
        return land_refs[t].at[pl.ds((4 * px + 2 * py + pc) * m_per, m_per), :]

    local = [pltpu.make_async_copy(x_refs[t], rows(t, *me), local_sems.at[t]) for t in range(nt)]
    remote = []
    peers = [(_flip(x, fx), _flip(y, fy), c) for fx, fy in _CHIP_FLIPS] + [(x, y, 1 - c)]
    for k, peer in enumerate(peers):
        remote += [(pltpu.make_async_remote_copy(
            src_ref=x_refs[t], dst_ref=rows(t, *me), send_sem=send_sems.at[4 * t + k], recv_sem=recv_sems.at[4 * t + k],
            device_id=peer, device_id_type=_MESH), rows(t, *peer), 4 * t + k) for t in range(nt)]
    return local, remote


def gather_start(name, shards, after):
    nt = len(shards)

    def body(*refs):
        x_refs, land_refs = refs[:nt], refs[nt:2 * nt]
        send_sems, recv_sems, local_sems = refs[2 * nt + 1:2 * nt + 4]
        token = refs[-1]
        local, remote = _gather_first_copies(x_refs, land_refs, send_sems, recv_sems, local_sems)
        for cp in local:
            cp.start()
        for cp, _, _ in remote:
            cp.start()
        token[...] = jnp.zeros_like(token)

    lands = [lax.empty((N_DEV * s.shape[0], s.shape[1]), s.dtype) for s in shards]
    outs = pl.pallas_call(
        body, name=name,
        out_shape=[pltpu.SemaphoreType.DMA((4 * nt,)), pltpu.SemaphoreType.DMA((4 * nt,)), pltpu.SemaphoreType.DMA((nt,))]
        + [pltpu.HBM(s.shape, s.dtype) for s in shards] + [pltpu.HBM(a.shape, a.dtype) for a in lands]
        + [jax.ShapeDtypeStruct((8, LANE), F32)],
        in_specs=[_HBM] * (2 * nt) + [_ANY], out_specs=[_SEM] * 3 + [_HBM] * (2 * nt) + [pl.BlockSpec(memory_space=pltpu.VMEM)],
        input_output_aliases={i: 3 + i for i in range(2 * nt)},
        compiler_params=pltpu.CompilerParams(has_side_effects=_EFFECT),
    )(*[_in_hbm(s) for s in shards], *[_in_hbm(a) for a in lands], after)
    return outs[:3], outs[3:3 + nt], outs[3 + nt:3 + 2 * nt], outs[-1]


def gather_wait(name, handle, after):
    sems, shards, lands, _ = handle
    nt = len(shards)

    def body(*refs):
        x_refs, land_refs = refs[:nt], refs[nt:2 * nt]
        send_sems, recv_sems, local_sems = refs[2 * nt:2 * nt + 3]
        local, remote = _gather_first_copies(x_refs, land_refs, send_sems, recv_sems, local_sems)
        for cp, landed, k in remote:
            cp.wait_send()
            pltpu.make_async_remote_copy(src_ref=landed, dst_ref=landed, send_sem=send_sems.at[k], recv_sem=recv_sems.at[k],
                                         device_id=_place(), device_id_type=_MESH).wait_recv()
        for cp in local:
            cp.wait()

    outs = pl.pallas_call(
        body, name=name, out_shape=[pltpu.HBM(s.shape, s.dtype) for s in shards] + [pltpu.HBM(a.shape, a.dtype) for a in lands],
        in_specs=[_HBM] * (2 * nt) + [_SEM] * 3 + [_ANY], out_specs=[_HBM] * (2 * nt),
        input_output_aliases={i: i for i in range(2 * nt)},
        compiler_params=pltpu.CompilerParams(has_side_effects=_EFFECT),
    )(*shards, *lands, *sems, after)
    return outs[nt:]


def _forward_copies(land_refs, send_sems, recv_sems):
    nt = len(land_refs)
    x, y, c = _place()
    copies = []
    for j, (fx, fy) in enumerate(_CHIP_FLIPS):
        px, py = _flip(x, fx), _flip(y, fy)
        for t in range(nt):
            m_per = land_refs[t].shape[0] // N_DEV
            mine = land_refs[t].at[pl.ds((4 * px + 2 * py + c) * m_per, m_per), :]
            theirs = land_refs[t].at[pl.ds((4 * px + 2 * py + 1 - c) * m_per, m_per), :]
            copies.append((pltpu.make_async_remote_copy(
                src_ref=mine, dst_ref=mine, send_sem=send_sems.at[3 * t + j], recv_sem=recv_sems.at[3 * t + j],
                device_id=(x, y, 1 - c), device_id_type=_MESH), theirs, 3 * t + j))
    return copies


def _wait_all(copies, send_sems, recv_sems):
    for cp, landed, k in copies:
        cp.wait_send()
        pltpu.make_async_remote_copy(src_ref=landed, dst_ref=landed, send_sem=send_sems.at[k], recv_sem=recv_sems.at[k],
                                     device_id=_place(), device_id_type=_MESH).wait_recv()


def gather_forward(name, lands):
    nt = len(lands)

    def body(*refs):
        send_sems, recv_sems = refs[2 * nt:]
        copies = _forward_copies(refs[:nt], send_sems, recv_sems)
        for cp, _, _ in copies:
            cp.start()
        _wait_all(copies, send_sems, recv_sems)

    return pl.pallas_call(
        body, name=name, out_shape=[jax.ShapeDtypeStruct(a.shape, a.dtype) for a in lands],
        in_specs=[_ANY] * nt, out_specs=[_ANY] * nt, input_output_aliases={i: i for i in range(nt)},
        scratch_shapes=[pltpu.SemaphoreType.DMA((3 * nt,)), pltpu.SemaphoreType.DMA((3 * nt,))],
    )(*lands)


def forward_start(name, lands):
    nt = len(lands)

    def body(*refs):
        for cp, _, _ in _forward_copies(refs[:nt], refs[nt], refs[nt + 1]):
            cp.start()
        refs[-1][...] = jnp.zeros_like(refs[-1])

    outs = pl.pallas_call(
        body, name=name,
        out_shape=[pltpu.SemaphoreType.DMA((3 * nt,)), pltpu.SemaphoreType.DMA((3 * nt,))]
        + [pltpu.HBM(a.shape, a.dtype) for a in lands] + [jax.ShapeDtypeStruct((8, LANE), F32)],
        in_specs=[_HBM] * nt, out_specs=[_SEM] * 2 + [_HBM] * nt + [pl.BlockSpec(memory_space=pltpu.VMEM)],
        input_output_aliases={i: 2 + i for i in range(nt)},
        compiler_params=pltpu.CompilerParams(has_side_effects=_EFFECT),
    )(*[_in_hbm(a) for a in lands])
    return outs[:2], outs[2:2 + nt], outs[-1]


def forward_wait(name, handle, after):
    sems, lands, _ = handle
    nt = len(lands)

    def body(*refs):
        send_sems, recv_sems = refs[nt], refs[nt + 1]
        _wait_all(_forward_copies(refs[:nt], send_sems, recv_sems), send_sems, recv_sems)

    return pl.pallas_call(
        body, name=name, out_shape=[pltpu.HBM(a.shape, a.dtype) for a in lands],
        in_specs=[_HBM] * nt + [_SEM] * 2 + [_ANY], out_specs=[_HBM] * nt,
        input_output_aliases={i: i for i in range(nt)},
        compiler_params=pltpu.CompilerParams(has_side_effects=_EFFECT),
    )(*lands, *sems, after)


def _chip_copies(p_refs, land_refs, send_sems, recv_sems):
    nt = len(p_refs)
    x, y, c = _place()
    copies = []
    for j, (fx, fy) in enumerate(_CHIP_FLIPS):
        px, py = _flip(x, fx), _flip(y, fy)
        copies += [(pltpu.make_async_remote_copy(
            src_ref=p_refs[t].at[2 * px + py], dst_ref=land_refs[t].at[j], send_sem=send_sems.at[3 * t + j],
            recv_sem=recv_sems.at[3 * t + j], device_id=(px, py, c), device_id_type=_MESH), land_refs[t].at[j], 3 * t + j)
            for t in range(nt)]
    return copies


def _pair_copies(g_refs, land_refs, send_sems, recv_sems):
    nt = len(g_refs)
    x, y, c = _place()
    return [(pltpu.make_async_remote_copy(
        src_ref=g_refs[t].at[2 * k + 1 - c], dst_ref=land_refs[t].at[k], send_sem=send_sems.at[4 * t + k],
        recv_sem=recv_sems.at[4 * t + k], device_id=(x, y, 1 - c), device_id_type=_MESH), land_refs[t].at[k], 4 * t + k)
        for t in range(nt) for k in range(4)]


def exchange_start(name, copies_fn, n_land, parts):
    nt = len(parts)

    def body(*refs):
        p_refs, land_refs = refs[:nt], refs[nt:2 * nt]
        send_sems, recv_sems = refs[2 * nt:2 * nt + 2]
        token = refs[-1]
        for cp, _, _ in copies_fn(p_refs, land_refs, send_sems, recv_sems):
            cp.start()
        token[...] = jnp.zeros_like(token)

    lands = [lax.empty((n_land,) + p.shape[1:], p.dtype) for p in parts]
    outs = pl.pallas_call(
        body, name=name,
        out_shape=[pltpu.SemaphoreType.DMA((n_land * nt,)), pltpu.SemaphoreType.DMA((n_land * nt,))]
        + [pltpu.HBM(p.shape, p.dtype) for p in parts] + [pltpu.HBM(a.shape, a.dtype) for a in lands]
        + [jax.ShapeDtypeStruct((8, LANE), F32)],
        in_specs=[_HBM] * (2 * nt), out_specs=[_SEM] * 2 + [_HBM] * (2 * nt) + [pl.BlockSpec(memory_space=pltpu.VMEM)],
        input_output_aliases={i: 2 + i for i in range(2 * nt)},
        compiler_params=pltpu.CompilerParams(has_side_effects=_EFFECT),
    )(*[_in_hbm(p) for p in parts], *[_in_hbm(a) for a in lands])
    return outs[:2], outs[2:2 + nt], outs[2 + nt:2 + 2 * nt], outs[-1]


def exchange_wait(name, copies_fn, handle, after):
    sems, parts, lands, _ = handle
    nt = len(parts)

    def body(*refs):
        p_refs, land_refs = refs[:nt], refs[nt:2 * nt]
        send_sems, recv_sems = refs[2 * nt:2 * nt + 2]
        for cp, landed, k in copies_fn(p_refs, land_refs, send_sems, recv_sems):
            cp.wait_send()
            pltpu.make_async_remote_copy(src_ref=landed, dst_ref=landed, send_sem=send_sems.at[k], recv_sem=recv_sems.at[k],
                                         device_id=_place(), device_id_type=_MESH).wait_recv()

    outs = pl.pallas_call(
        body, name=name, out_shape=[pltpu.HBM(p.shape, p.dtype) for p in parts] + [pltpu.HBM(a.shape, a.dtype) for a in lands],
        in_specs=[_HBM] * (2 * nt) + [_SEM] * 2 + [_ANY], out_specs=[_HBM] * (2 * nt),
        input_output_aliases={i: i for i in range(2 * nt)},
        compiler_params=pltpu.CompilerParams(has_side_effects=_EFFECT),
    )(*parts, *lands, *sems, after)
    return outs[:nt], outs[nt:]


def _row_tile(rows):
    return rows if rows <= 1024 else 1024


def pair_add(name, g, recv, core):
    _, R, C = g.shape
    tr = _row_tile(R)

    def body(core_ref, g_ref, r_ref, o_ref):
        o_ref[...] = (g_ref[...].astype(F32) + r_ref[...].astype(F32)).astype(o_ref.dtype)

    return pl.pallas_call(
        body, name=name, out_shape=jax.ShapeDtypeStruct((4, R, C), BF),
        grid_spec=pltpu.PrefetchScalarGridSpec(
            num_scalar_prefetch=1, grid=(4, R // tr),
            in_specs=[pl.BlockSpec((None, tr, C), lambda k, i, core_ref: (2 * k + core_ref[0], i, 0)),
                      pl.BlockSpec((None, tr, C), lambda k, i, core_ref: (k, i, 0))],
            out_specs=pl.BlockSpec((None, tr, C), lambda k, i, core_ref: (k, i, 0))),
        compiler_params=_cp("parallel", "parallel"),
    )(core, g, recv)


def chip_add(name, part, recv, chip):
    _, R, C = part.shape
    tr = _row_tile(R)

    def body(chip_ref, p_ref, r_ref, o_ref):
        o_ref[...] = ((p_ref[...].astype(F32) + r_ref[0].astype(F32)) + r_ref[1].astype(F32)) + r_ref[2].astype(F32)

    return pl.pallas_call(
        body, name=name, out_shape=jax.ShapeDtypeStruct((R, C), F32),
        grid_spec=pltpu.PrefetchScalarGridSpec(
            num_scalar_prefetch=1, grid=(R // tr,),
            in_specs=[pl.BlockSpec((None, tr, C), lambda i, chip_ref: (chip_ref[0], i, 0)),
                      pl.BlockSpec((3, tr, C), lambda i, chip_ref: (0, i, 0))],
            out_specs=pl.BlockSpec((tr, C), lambda i, chip_ref: (i, 0))),
        compiler_params=_cp("parallel"),
    )(chip, part, recv)


def reduce_scatter_pair(tag, gs):
    return exchange_start("pair_exchange_start_" + tag, _pair_copies, 4, gs)


def reduce_scatter_chips(tag, handle, core, after):
    gs, from_pair = exchange_wait("pair_exchange_wait_" + tag, _pair_copies, handle, after)
    parts = [pair_add("pair_add_%s_%d" % (tag, t), g, r, core) for t, (g, r) in enumerate(zip(gs, from_pair))]
    return exchange_start("chip_exchange_start_" + tag, _chip_copies, 3, parts)


def reduce_scatter_finish(tag, handle, chip, after):
    parts, landed = exchange_wait("chip_exchange_wait_" + tag, _chip_copies, handle, after)
    return [chip_add("chip_add_%s_%d" % (tag, t), p, r, chip) for t, (p, r) in enumerate(zip(parts, landed))]


def sum_devices(name, g):
    _, R, C = g.shape

    def body(g_ref, o_ref):
        acc = g_ref[0]
        for d in range(1, N_DEV):
            acc = acc + g_ref[d]
        o_ref[...] = acc

    return pl.pallas_call(body, name=name, out_shape=jax.ShapeDtypeStruct((R, C), F32),
                          compiler_params=pltpu.CompilerParams(vmem_limit_bytes=VMEM_LIMIT_BYTES))(g)


_WEIGHTS = ["ada_w", "ada_b", "ln1_g", "ln1_b", "ln2_g", "ln2_b", "ffn_w1", "ffn_w3", "ffn_w2", "mla_w_down", "mla_q_norm",
            "mla_w_uq", "mla_kv_norm", "mla_w_uk", "mla_w_uv", "mla_w_o", "fox_w_in", "fox_b_f", "fox_w_o"]
_SMALL = ["ada_b", "ln1_g", "ln1_b", "ln2_g", "ln2_b", "mla_q_norm", "mla_kv_norm", "fox_b_f"]


def _pad_cols(a, width):
    return jnp.pad(a, ((0, 0), (0, width - a.shape[1])))


def _small_pack(parts, cols):
    rows = [jnp.pad(p.reshape(-1), (0, _round_up(p.size, cols) - p.size)).reshape(-1, cols) for p in parts]
    out = jnp.concatenate(rows, axis=0)
    return jnp.pad(out, ((0, _round_up(out.shape[0], 8) - out.shape[0]), (0, 0)))


def kernel(x, c, positions, ada_w, ada_b, ln1_g, ln1_b, ln2_g, ln2_b, ffn_w1, ffn_w3, ffn_w2, mla_w_down, mla_q_norm, mla_w_uq, mla_kv_norm, mla_w_uk, mla_w_uv, mla_w_o, fox_w_in, fox_b_f, fox_w_o, loss_target, m_ada_w, m_ada_b, m_ln1_g, m_ln1_b, m_ln2_g, m_ln2_b, m_ffn_w1, m_ffn_w3, m_ffn_w2, m_mla_w_down, m_mla_q_norm, m_mla_w_uq, m_mla_kv_norm, m_mla_w_uk, m_mla_w_uv, m_mla_w_o, m_fox_w_in, m_fox_b_f, m_fox_w_o, v_ada_w, v_ada_b, v_ln1_g, v_ln1_b, v_ln2_g, v_ln2_b, v_ffn_w1, v_ffn_w3, v_ffn_w2, v_mla_w_down, v_mla_q_norm, v_mla_w_uq, v_mla_kv_norm, v_mla_w_uk, v_mla_w_uv, v_mla_w_o, v_fox_w_in, v_fox_b_f, v_fox_w_o):
    env = dict(locals())
    W = {n: env[n] for n in _WEIGHTS}
    M = {n: env["m_" + n] for n in _WEIGHTS}
    V = {n: env["v_" + n] for n in _WEIGHTS}

    T, D = x.shape[1], x.shape[2]
    L = ada_w.shape[0]
    QL, KVL = mla_q_norm.shape[1], mla_kv_norm.shape[1]
    HA = mla_w_uq.shape[2] * N_DEV // (NOPE + ROPE)
    HB = fox_b_f.shape[1]
    f_loc = ffn_w1.shape[2]
    FP = _round_up(f_loc, LANE)
    n_ada = ada_w.shape[2]
    n_in = fox_w_in.shape[2]
    n_in_pad = _round_up(n_in, LANE)
    k_loc = mla_w_down.shape[1]
    n_uq, n_uk = mla_w_uq.shape[2], mla_w_uk.shape[2]
    assert D == HB * HEAD == HA * HEAD and n_ada * N_DEV == 6 * D
    alpha = (2 * L) ** 0.25
    xi, yi, ci = _place()
    me = 4 * xi + 2 * yi + ci
    core = jnp.reshape(ci, (1,)).astype(jnp.int32)
    chip = jnp.reshape(2 * xi + yi, (1,)).astype(jnp.int32)

    x2 = x[0]
    tgt = loss_target[0]

    c_all = all_gather("gather_c", [jnp.pad(c, ((0, 7), (0, 0)))], True)[0].reshape(N_DEV, 8, D)[:, 0]
    ada_b_cols = lax.dynamic_slice_in_dim(ada_b, me * n_ada, n_ada, axis=1)
    mod_part = ada_fwd("ada_fwd", c_all, ada_w, ada_b_cols)
    mod_all = all_gather("gather_mod", [mod_part.reshape(L * N_DEV, n_ada)], True)[0].reshape(N_DEV, L, N_DEV, n_ada)
    mod = lax.dynamic_index_in_dim(mod_all, me, axis=2, keepdims=False)
    mod = jnp.moveaxis(mod, 0, 1).reshape(L, 6, D)

    fox_cols = 3 * D + LANE

    def bf(a, pad_rows=0, pad_cols=0):
        return jnp.pad(a.astype(BF), ((0, pad_rows), (0, pad_cols)))

    groups = []
    for i in range(L):
        j = i // 2
        if i % 2 == 0:
            groups.append(("mla%d" % i, [bf(mla_w_down[j]), bf(mla_w_uq[j]), bf(mla_w_uk[j]), bf(mla_w_uv[j]), bf(mla_w_o[j])]))
        else:
            groups.append(("fox%d" % i, [bf(fox_w_in[j], 0, n_in_pad - n_in), bf(fox_w_o[j])]))
        groups.append(("ffn%d" % i, [bf(ffn_w1[i], 0, FP - f_loc), bf(ffn_w3[i], 0, FP - f_loc), bf(ffn_w2[i], FP - f_loc, 0)]))
    handles = {}
    tok = mod
    for gname, shards in groups:
        handles[gname] = gather_start("gather_start_" + gname, shards, tok)
        tok = handles[gname][3]
    mod = mod + tok[0, 0]

    forwarding = {}
    order_of = [gname for gname, _ in groups]

    def prefetch(gname, after):
        if gname is None or order_of.index(gname) < 4:
            return 0.0
        lands = gather_wait("gather_wait_" + gname, handles[gname], after)
        forwarding[gname] = forward_start("gather_forward_start_" + gname, lands)
        return forwarding[gname][2][0, 0]

    def gathered(gname, after):
        if gname in forwarding:
            lands = forward_wait("gather_forward_wait_" + gname, forwarding[gname], after)
        else:
            lands = gather_wait("gather_wait_" + gname, handles[gname], after)
            lands = gather_forward("gather_forward_" + gname, lands)
        return [a.reshape((N_DEV,) + s.shape) for a, s in zip(lands, dict(groups)[gname])]

    def cols_whole(g, keep=None):
        return jnp.moveaxis(g[:, :, :keep], 0, 1).reshape(g.shape[1], -1)

    def modv(i, k):
        return mod[i, k][None, :]

    b_f = _pad_cols(fox_b_f, LANE)

    inv_freq = ROPE_THETA ** (-jnp.arange(0, ROPE, 2, dtype=F32) / ROPE)
    ang = jnp.repeat(positions[0].astype(F32)[:, None] * inv_freq, 2, axis=1)
    cos_k, sin_k = jnp.cos(ang), jnp.sin(ang)
    cos_q, sin_q = jnp.tile(cos_k, (1, HA)), jnp.tile(sin_k, (1, HA))
    rot_k, rot_q = _pair_rotation(ROPE), _pair_rotation(HA * ROPE)
    tk = min(ATT_TILE, T)
    mla_scale = (NOPE + ROPE) ** -0.5
    fox_scale = HEAD ** -0.5

    def modulate(tag, xin, sc, sh):
        return rowwise("modulate_" + tag, lambda xv, scv, shv: xv * (1.0 + scv) + shv, [xin], [sc, sh], [(D, BF)])[0]

    def post(tag, xin, y, g, gam, bet, nsc, nsh):
        def f(xv, yv, gv, gamv, betv, scv, shv):
            xhat, _ = _ln_stats(alpha, xv, yv, gv)
            out = xhat * gamv + betv
            return out, out * (1.0 + scv) + shv
        return rowwise("post_" + tag, f, [xin, y], [g, gam, bet, nsc, nsh], [(D, F32), (D, BF)])

    def post_bwd(tag, dxo, xin, y, g, gam):
        def f(dv, xv, yv, gv, gamv):
            xhat, rstd = _ln_stats(alpha, xv, yv, gv)
            dxh = dv * gamv
            dz = rstd * (dxh - _mean1(dxh) - xhat * _mean1(dxh * xhat))
            return alpha * dz, (1.0 + gv) * dz, _sum0(dv * xhat), _sum0(dv), _sum0(dz * yv)
        return rowwise("post_bwd_" + tag, f, [dxo, xin, y], [g, gam], [(D, F32), (D, BF)], [D, D, D])

    def join_bwd(tag, dxa, dh, sc, xprev, yprev, g, gam, bet):
        def f(dxav, dhv, xpv, ypv, scv, gv, gamv, betv):
            xhat, rstd = _ln_stats(alpha, xpv, ypv, gv)
            dv = dxav + dhv * (1.0 + scv)
            dxh = dv * gamv
            dz = rstd * (dxh - _mean1(dxh) - xhat * _mean1(dxh * xhat))
            return (alpha * dz, (1.0 + gv) * dz, _sum0(dhv * (xhat * gamv + betv)), _sum0(dhv),
                    _sum0(dv * xhat), _sum0(dv), _sum0(dz * ypv))
        return rowwise("join_bwd_" + tag, f, [dxa, dh, xprev, yprev], [sc, g, gam, bet], [(D, F32), (D, BF)], [D] * 5)

    def pre_bwd(tag, dxa, dh, xin, sc):
        def f(dxav, dhv, xv, scv):
            return dxav + dhv * (1.0 + scv), _sum0(dhv * xv), _sum0(dhv)
        return rowwise("pre_bwd_" + tag, f, [dxa, dh, xin], [sc], [(D, F32)], [D, D])

    saved = []
    xc = x2
    h = modulate("l0a", xc, modv(0, 1), modv(0, 0))
    for i in range(L):
        j = i // 2
        tag = "l%d" % i
        s = {"x_a": xc}
        s["h_a"] = h
        if i % 2 == 0:
            wdn, wuq, wuk, wuv, wo = gathered("mla%d" % i, xc)
            w_down, w_o = wdn.reshape(D, -1), wo.reshape(D, D)
            q3 = cols_whole(wuq).reshape(QL, HA, NOPE + ROPE)
            w_uq = jnp.concatenate([q3[..., :NOPE].reshape(QL, HA * NOPE), q3[..., NOPE:].reshape(QL, HA * ROPE)], axis=-1)
            w_ukv = jnp.concatenate([cols_whole(wuk), cols_whole(wuv)], axis=-1)
            s.update(w_down=w_down, w_o=w_o, w_uq=w_uq, w_ukv=w_ukv)
            lat = mm("mla_down_" + tag, h, w_down)

            def mid(latv, ck, sk, qg, kg, p):
                ql, kl, kp = latv[:, :QL], latv[:, QL:QL + KVL], latv[:, QL + KVL:]
                qn = ql * lax.rsqrt(_mean1(ql * ql) + RMS_EPS) * qg
                cn = kl * lax.rsqrt(_mean1(kl * kl) + RMS_EPS) * kg
                return qn, cn, kp * ck + _rot(kp, p) * sk

            qn, ckv, kpe = rowwise("mla_mid_" + tag, mid, [lat, cos_k, sin_k],
                                   [mla_q_norm[j][None], mla_kv_norm[j][None], rot_k], [(QL, BF), (KVL, BF), (ROPE, BF)])
            qf = mm("mla_uq_" + tag, qn, w_uq)

            def qrope(qv, cq, sq, p):
                qp = qv[:, HA * NOPE:]
                return (qp * cq + _rot(qp, p) * sq) * mla_scale

            qpe = rowwise("mla_qrope_" + tag, qrope, [qf, cos_q, sin_q], [rot_q], [(HA * ROPE, BF)])[0]
            qpe_h = jnp.moveaxis(qpe.reshape(T, HA, ROPE), 1, 0)
            kv = mm("mla_ukv_" + tag, ckv, w_ukv)
            ahead = prefetch("ffn%d" % i, kv)
            o, lse = attn_fwd("mla_attn_" + tag, "mla", mla_scale, HA, T, qf, 0, kv, 0, kv, HA, qpe_h, kpe)
            y = mm("mla_o_" + tag, o, w_o)
            s.update(lat=lat, qn=qn, ckv=ckv, kpe=kpe, qf=qf, qpe_h=qpe_h, kv=kv, o=o, lse=lse)
        else:
            win, wo = gathered("fox%d" % i, xc)
            w_in, w_o = _pad_cols(cols_whole(win, keep=n_in), fox_cols), wo.reshape(D, D)
            s.update(w_in=w_in, w_o=w_o)
            proj = mm("fox_in_" + tag, h, w_in, bn=896)
            fz = proj[:, 3 * D:]
            cum = fox_cum("fox_cum_" + tag, fz, b_f[j][None])
            cum_h = cum[:, :HB].T
            ahead = prefetch("ffn%d" % i, cum)
            o, lse = attn_fwd("fox_attn_" + tag, "fox", fox_scale, HB, T, proj, 0, proj, HB, proj, 2 * HB,
                              cum_h.reshape(HB, T // tk, 1, tk), cum_h.reshape(HB, T, 1))
            y = mm("fox_o_" + tag, o, w_o)
            s.update(proj=proj, fz=fz, cum_h=cum_h, o=o, lse=lse)
        s["y_a"] = y
        xc, h = post(tag + "a", xc, y, modv(i, 2) + ahead, ln1_g[i][None], ln1_b[i][None], modv(i, 4), modv(i, 3))
        s["x_f"] = xc
        w1g, w3g, w2g = [w[:, None] for w in gathered("ffn%d" % i, xc)]
        a, b, gte = ffn_up("ffn_up_" + tag, h, w1g, w3g, 0)
        nxt = min(i + 1, L - 1)
        ahead = prefetch(None if i == L - 1 else ("mla%d" if nxt % 2 == 0 else "fox%d") % nxt, gte)
        y = ffn_down("ffn_down_" + tag, gte, w2g, 0)
        s.update(h_f=h, a=a, b=b, gte=gte, y_f=y, w1g=w1g, w3g=w3g, w2g=w2g)
        xc, h = post(tag + "f", xc, y, modv(i, 5) + ahead, ln2_g[i][None], ln2_b[i][None], modv(nxt, 1), modv(nxt, 0))
        saved.append(s)

    def loss_fn(yv, tv):
        e = yv - tv
        return e * (1.0 / D), jnp.zeros((1, LANE), F32) + _sum0(jnp.sum(e * e, axis=1, keepdims=True)) * (0.5 / D)

    dx, loss_part = rowwise("loss", loss_fn, [xc, tgt], [], [(D, F32)], [LANE])

    G = {n: [None] * W[n].shape[0] for n in _WEIGHTS if n != "ada_w"}
    pending, pair_open = [], []

    def scatter(kind, idx, tag, gs):
        handle = reduce_scatter_pair(tag, gs)
        pair_open.append((kind, idx, tag, handle))
        return handle[3][0, 0]

    def scatter_on(after):
        pkind, pidx, ptag, phandle = pair_open.pop()
        chips = reduce_scatter_chips(ptag, phandle, core, after)
        pending.append((pkind, pidx, chips))
        return chips[3][0, 0]
    dmod = [[None] * 6 for _ in range(L)]
    s = saved[L - 1]
    dxa, dy, G["ln2_g"][L - 1], G["ln2_b"][L - 1], dmod[L - 1][5] = post_bwd(
        "l%df" % (L - 1), dx, s["x_f"], s["y_f"], modv(L - 1, 5), ln2_g[L - 1][None])
    for i in reversed(range(L)):
        j = i // 2
        tag = "l%d" % i
        s = saved[i]
        g_w2 = ffn_out_grad("ffn_w2_grad_" + tag, s["gte"], dy)
        da, db = ffn_down_bwd("ffn_down_bwd_" + tag, dy, s["w2g"], s["a"], s["b"], 0)
        order = scatter_on(da) if pair_open else 0.0
        g_w1 = ffn_in_grad("ffn_w1_grad_" + tag, s["h_f"], da)
        g_w3 = ffn_in_grad("ffn_w3_grad_" + tag, s["h_f"], db)
        dh = ffn_dh("ffn_dh_" + tag, da, db, s["w1g"], s["w3g"], 0)
        order = order + scatter("ffn", i, "ffn_" + tag, [g_w1, g_w3, g_w2])
        dxa, dy, dmod[i][4], dmod[i][3], G["ln1_g"][i], G["ln1_b"][i], dmod[i][2] = join_bwd(
            tag + "f", dxa, dh, modv(i, 4) + order, s["x_a"], s["y_a"], modv(i, 2), ln1_g[i][None], ln1_b[i][None])
        if i % 2 == 0:
            g_o = mm("mla_o_grad_" + tag, s["o"], dy, ta=True, out_dtype=BF)
            do = mm("mla_do_" + tag, dy, s["w_o"], tb=True, out_dtype=BF)
            early = scatter_on(do) if i == 0 and pair_open else 0.0
            delta = attn_delta("mla_delta_" + tag, do, s["o"], HA, tk)
            dqn_, dkn_, dv_, dqpe_h, dkpe = attn_bwd("mla_attn_bwd_" + tag, "mla", mla_scale, HA, T, s["qf"], 0, s["kv"], 0,
                                                     s["kv"], HA, s["qpe_h"], s["kpe"], do, s["lse"], delta)
            order = early + (scatter_on(dkpe) if pair_open else 0.0)
            dkv = jnp.concatenate([dkn_, dv_], axis=1)
            g_kv = mm("mla_ukv_grad_" + tag, s["ckv"], dkv, ta=True, out_dtype=BF)
            dckv = mm("mla_dckv_" + tag, dkv, s["w_ukv"], tb=True)

            def qrope_bwd(dv, cq, sq, p):
                dv = dv * mla_scale
                return dv * cq - _rot(dv * sq, p)

            dqpe = rowwise("mla_qrope_bwd_" + tag, qrope_bwd, [jnp.moveaxis(dqpe_h, 0, 1).reshape(T, HA * ROPE), cos_q, sin_q],
                           [rot_q], [(HA * ROPE, BF)])[0]
            dq = jnp.concatenate([dqn_, dqpe], axis=1)
            g_uq = mm("mla_uq_grad_" + tag, s["qn"], dq, ta=True, out_dtype=BF)
            g_uq = jnp.concatenate([g_uq[:, :HA * NOPE].reshape(QL, HA, NOPE), g_uq[:, HA * NOPE:].reshape(QL, HA, ROPE)],
                                   axis=-1).reshape(QL, N_DEV, n_uq)
            dqn = mm("mla_dqn_" + tag, dq, s["w_uq"], tb=True)

            def mid_bwd(latv, dqv, dcv, dkp, ck, sk, qg, kg, p):
                ql, kl = latv[:, :QL], latv[:, QL:QL + KVL]
                rq = lax.rsqrt(_mean1(ql * ql) + RMS_EPS)
                rk = lax.rsqrt(_mean1(kl * kl) + RMS_EPS)
                uq, uk = dqv * qg, dcv * kg
                dql = rq * uq - ql * (rq * rq * rq) * _mean1(uq * ql)
                dkl = rk * uk - kl * (rk * rk * rk) * _mean1(uk * kl)
                return dql, dkl, dkp * ck - _rot(dkp * sk, p), _sum0(dqv * ql * rq), _sum0(dcv * kl * rk)

            dql, dkl, dkp, G["mla_q_norm"][j], G["mla_kv_norm"][j] = rowwise(
                "mla_mid_bwd_" + tag, mid_bwd, [s["lat"], dqn, dckv, dkpe, cos_k, sin_k],
                [mla_q_norm[j][None], mla_kv_norm[j][None], rot_k], [(QL, BF), (KVL, BF), (ROPE, BF)], [QL, KVL])
            dlat = jnp.concatenate([dql, dkl, dkp], axis=1)
            g_dn = mm("mla_down_grad_" + tag, s["h_a"], dlat, ta=True, out_dtype=BF)
            dh = mm("mla_dh_" + tag, dlat, s["w_down"], tb=True)
            g_kv = jnp.moveaxis(g_kv.reshape(KVL, 2, N_DEV, n_uk), 2, 0).reshape(N_DEV, KVL, 2 * n_uk)
            order = order + scatter("mla", j, "mla_" + tag, [
                g_dn.reshape(N_DEV, k_loc, -1), jnp.moveaxis(g_uq, 1, 0), g_kv, g_o.reshape(N_DEV, k_loc, D)])
        else:
            g_o = mm("fox_o_grad_" + tag, s["o"], dy, ta=True, out_dtype=BF)
            do = mm("fox_do_" + tag, dy, s["w_o"], tb=True, out_dtype=BF)
            cum_h = s["cum_h"]
            delta = attn_delta("fox_delta_" + tag, do, s["o"], HB, tk)
            dq_, dk_, dv_, dcq, dck = attn_bwd("fox_attn_bwd_" + tag, "fox", fox_scale, HB, T, s["proj"], 0, s["proj"], HB,
                                               s["proj"], 2 * HB, cum_h.reshape(HB, T // tk, 1, tk), cum_h.reshape(HB, T, 1),
                                               do, s["lse"], delta)
            order = scatter_on(dck) if pair_open else 0.0
            dcum = _pad_cols((dcq.reshape(HB, T) + dck.reshape(HB, T)).T, LANE)
            df, db_f = fox_cum_bwd("fox_cum_bwd_" + tag, s["fz"], b_f[j][None], dcum)
            G["fox_b_f"][j] = db_f[:, :HB]
            dproj = jnp.concatenate([dq_, dk_, dv_, df], axis=1)
            g_in = mm("fox_in_grad_" + tag, s["h_a"], dproj, ta=True, out_dtype=BF, bn=896)[:, :N_DEV * n_in]
            dh = mm("fox_dh_" + tag, dproj, s["w_in"], tb=True, bk=896)
            g_in = jnp.pad(jnp.moveaxis(g_in.reshape(D, N_DEV, n_in), 1, 0), ((0, 0), (0, 0), (0, n_in_pad - n_in)))
            order = order + scatter("fox", j, "fox_" + tag, [g_in, g_o.reshape(N_DEV, k_loc, D)])
        if i > 0:
            p = saved[i - 1]
            dxa, dy, dmod[i][1], dmod[i][0], G["ln2_g"][i - 1], G["ln2_b"][i - 1], dmod[i - 1][5] = join_bwd(
                tag + "a", dxa, dh, modv(i, 1) + order, p["x_f"], p["y_f"], modv(i - 1, 5), ln2_g[i - 1][None], ln2_b[i - 1][None])
        else:
            dx, dmod[i][1], dmod[i][0] = pre_bwd(tag + "a", dxa, dh, s["x_a"], modv(i, 1) + order)

    scatter_on(dx)

    def finish(kind, idx, handle, after):
        res = reduce_scatter_finish("%s_%d" % (kind, idx), handle, chip, after)
        if kind == "ffn":
            G["ffn_w1"][idx], G["ffn_w3"][idx], G["ffn_w2"][idx] = res[0][:, :f_loc], res[1][:, :f_loc], res[2][:f_loc]
        elif kind == "mla":
            G["mla_w_down"][idx], G["mla_w_uq"][idx], G["mla_w_o"][idx] = res[0], res[1], res[3]
            G["mla_w_uk"][idx], G["mla_w_uv"][idx] = res[2][:, :n_uk], res[2][:, n_uk:]
        else:
            G["fox_w_in"][idx], G["fox_w_o"][idx] = res[0][:, :n_in], res[1]

    for kind, idx, handle in pending[:-1]:
        finish(kind, idx, handle, dx)

    dmod_mine = jnp.concatenate([jnp.concatenate(r, axis=1) for r in dmod], axis=0)
    small_parts = [dmod_mine] + [jnp.concatenate(G[n], axis=0) for n in _SMALL[1:]] + [loss_part]
    small = _small_pack(small_parts, 512)
    small_all = all_gather("gather_small", [small], True)[0].reshape(N_DEV, -1, 512)
    small_sum = sum_devices("sum_small", small_all)
    sums, off = [], 0
    for p in small_parts:
        nrow = _round_up(p.size, 512) // 512
        sums.append(small_sum[off:off + nrow].reshape(-1)[:p.size].reshape(p.shape))
        off += nrow
    grads = {n: gsum for n, gsum in zip(_SMALL, sums[:-1])}
    loss = sums[-1][0, 0]
    nrow = L * 6 * D // 512
    dmod_all = small_all[:, :nrow].reshape(N_DEV, L, 6 * D)
    dmod_cols = jnp.moveaxis(lax.dynamic_slice_in_dim(dmod_all, me * n_ada, n_ada, axis=2), 0, 1)

    delta, new_m, new_v = {}, {}, {}
    grads["ada_w"], delta["ada_w"], new_m["ada_w"], new_v["ada_w"] = ada_bwd_adamw(
        "ada_grad_adamw", c_all, dmod_cols, ada_w, m_ada_w, v_ada_w)
    last = ("mla_w_down", "mla_w_uq", "mla_w_uk", "mla_w_uv", "mla_w_o")
    for n in ("ffn_w1", "ffn_w3", "ffn_w2", "fox_w_in", "fox_w_o"):
        grads[n] = jnp.stack(G[n])
    for n in _WEIGHTS:
        if n != "ada_w" and n not in last:
            delta[n], new_m[n], new_v[n] = adamw("adamw_" + n, W[n], grads[n], M[n], V[n])
    finish(*pending[-1], delta["ffn_w2"])
    for n in last:
        grads[n] = jnp.stack(G[n])
        delta[n], new_m[n], new_v[n] = adamw("adamw_" + n, W[n], grads[n], M[n], V[n])

    return (loss, dx[None], *[grads[n] for n in _WEIGHTS], *[delta[n] for n in _WEIGHTS],
            *[new_m[n] for n in _WEIGHTS], *[new_v[n] for n in _WEIGHTS])
```

```python
import math

import numpy as np
import jax
import jax.numpy as jnp
from jax import lax
from jax.experimental import pallas as pl
from jax.experimental.pallas import tpu as pltpu

F32 = jnp.float32
BF = jnp.bfloat16
N_DEV = 8

CHUNK_SHIFT = 6
NOPE = 128
ROPE = 64
HEAD = 128
ROPE_THETA = 10000.0
LN_EPS = 1e-5
RMS_EPS = 1e-6
ADAM_LR = 0.001
ADAM_B1 = 0.9
ADAM_B2 = 0.999
ADAM_EPS = 1e-08
ADAM_WD = 0.01
ADAM_STEP = 10

VMEM_LIMIT_BYTES = 48 * 1024 * 1024
LANE = 128
NEG = -1e30


def _cp(*sem):
    return pltpu.CompilerParams(dimension_semantics=sem, vmem_limit_bytes=VMEM_LIMIT_BYTES)


def _pick(dim, pref):
    if dim <= pref:
        return dim
    for t in range(pref - pref % LANE, LANE - 1, -LANE):
        if dim % t == 0:
            return t
    return dim


def _round_up(v, m):
    return (v + m - 1) // m * m


def _sum0(v):
    return jnp.sum(v, axis=0, keepdims=True)


def _mean1(v):
    return jnp.mean(v, axis=-1, keepdims=True)


_NN = (((1,), (0,)), ((), ()))
_NT = (((1,), (1,)), ((), ()))
_TN = (((0,), (0,)), ((), ()))


def _mm_call(name, grid, ins, in_specs, dns, out_shape, out_spec, add=None, add_spec=None):
    npairs = len(dns)
    nk = grid[-1]
    has_add = add is not None

    def body(*refs):
        add_ref = refs[2 * npairs] if has_add else None
        o_ref = refs[2 * npairs + (1 if has_add else 0)]
        r = None
        for p in range(npairs):
            t = lax.dot_general(refs[2 * p][...].astype(BF), refs[2 * p + 1][...].astype(BF), dns[p], preferred_element_type=F32)
            r = t if r is None else r + t

        def fin(val):
            if has_add:
                val = val + add_ref[...]
            o_ref[...] = val.astype(o_ref.dtype)

        if nk == 1:
            fin(r)
        else:
            acc = refs[-1]
            k = pl.program_id(len(grid) - 1)

            @pl.when(k == 0)
            def _():
                acc[...] = r

            @pl.when(k > 0)
            def _():
                acc[...] += r

            @pl.when(k == nk - 1)
            def _():
                fin(acc[...])

    acc_shape = tuple(b for b in out_spec.block_shape if b is not None)
    args, specs = list(ins), list(in_specs)
    if has_add:
        args.append(add)
        specs.append(add_spec)
    return pl.pallas_call(
        body, name=name, grid=grid, in_specs=specs, out_specs=out_spec, out_shape=out_shape,
        scratch_shapes=[pltpu.VMEM(acc_shape, F32)] if nk > 1 else [],
        compiler_params=_cp(*(["parallel"] * (len(grid) - 1) + ["arbitrary"])),
    )(*args)


def mm(name, a, b, ta=False, tb=False, out_dtype=F32, add=None, bm=1024, bn=1024, bk=2048):
    M, K = (a.shape[1], a.shape[0]) if ta else a.shape
    N = b.shape[0] if tb else b.shape[1]
    bm, bn, bk = _pick(M, bm), _pick(N, bn), _pick(K, bk)
    a_spec = pl.BlockSpec((bk, bm), lambda i, j, k: (k, i)) if ta else pl.BlockSpec((bm, bk), lambda i, j, k: (i, k))
    b_spec = pl.BlockSpec((bn, bk), lambda i, j, k: (j, k)) if tb else pl.BlockSpec((bk, bn), lambda i, j, k: (k, j))
    dn = (((0 if ta else 1,), (1 if tb else 0,)), ((), ()))
    tile = pl.BlockSpec((bm, bn), lambda i, j, k: (i, j))
    return _mm_call(name, (M // bm, N // bn, K // bk), [a, b], [a_spec, b_spec], [dn],
                    jax.ShapeDtypeStruct((M, N), out_dtype), tile, add, tile)


def rowwise(name, fn, rows, vecs, out_rows, out_accs=(), tm=256):
    T = rows[0].shape[0]
    tm = min(tm, T)
    nr, nv, no = len(rows), len(vecs), len(out_rows)

    def body(*refs):
        outs = fn(*[r[...] for r in refs[:nr + nv]])
        if not isinstance(outs, (tuple, list)):
            outs = (outs,)
        o_refs = refs[nr + nv:nr + nv + no]
        a_refs = refs[nr + nv + no:]
        for r, o in zip(o_refs, outs[:no]):
            r[...] = o.astype(r.dtype)
        if a_refs:
            @pl.when(pl.program_id(0) == 0)
            def _():
                for r in a_refs:
                    r[...] = jnp.zeros_like(r)

            for r, o in zip(a_refs, outs[no:]):
                r[...] += o

    in_specs = [pl.BlockSpec((tm, r.shape[1]), lambda i: (i, 0)) for r in rows]
    in_specs += [pl.BlockSpec(v.shape, lambda i: (0, 0)) for v in vecs]
    out_specs = [pl.BlockSpec((tm, w), lambda i: (i, 0)) for w, _ in out_rows]
    out_specs += [pl.BlockSpec((1, w), lambda i: (0, 0)) for w in out_accs]
    out_shape = [jax.ShapeDtypeStruct((T, w), dt) for w, dt in out_rows]
    out_shape += [jax.ShapeDtypeStruct((1, w), F32) for w in out_accs]
    return pl.pallas_call(
        body, name=name, grid=(T // tm,), in_specs=in_specs, out_specs=out_specs, out_shape=out_shape,
        compiler_params=_cp("arbitrary"),
    )(*rows, *vecs)


def _rot(v, p):
    hi = v.astype(BF)
    lo = (v - hi.astype(F32)).astype(BF)
    return jnp.dot(hi, p, preferred_element_type=F32) + jnp.dot(lo, p, preferred_element_type=F32)


def _pair_rotation(width):
    p = np.zeros((width, width), np.float32)
    idx = np.arange(0, width, 2)
    p[idx + 1, idx] = -1.0
    p[idx, idx + 1] = 1.0
    return jnp.asarray(p, BF)


def _ln_stats(alpha, x, y, g):
    z = alpha * x + (1.0 + g) * y
    zc = z - _mean1(z)
    rstd = lax.rsqrt(_mean1(zc * zc) + LN_EPS)
    return zc * rstd, rstd


ATT_G_FWD = 4
ATT_G_BWD = 2
ATT_TILE = 256


def _visible(mode, ks, qs, t):
    krow = ks + lax.broadcasted_iota(jnp.int32, (t, t), 0)
    qcol = qs + lax.broadcasted_iota(jnp.int32, (t, t), 1)
    if mode == "mla":
        return lax.shift_right_logical(qcol, CHUNK_SHIFT) >= lax.shift_right_logical(krow, CHUNK_SHIFT)
    return qcol >= krow


def _scores(mode, k, q, e_k, e_q, ks, qs, t, masked):
    s = lax.dot_general(k, q, _NT, preferred_element_type=F32)
    if mode == "mla":
        s = s + lax.dot_general(e_k, e_q, _NT, preferred_element_type=F32)
    else:
        s = s + e_q - e_k
    if masked:
        s = jnp.where(_visible(mode, ks, qs, t), s, NEG)
    return s


def _attn_specs(mode, G, T, t, q_off, k_off, v_off):
    def col(off):
        return pl.BlockSpec((T, G * HEAD), lambda h: (0, off // G + h))

    stat = pl.BlockSpec((G, T // t, 1, t), lambda h: (h, 0, 0, 0))
    if mode == "mla":
        e_specs = [pl.BlockSpec((G, T, ROPE), lambda h: (h, 0, 0)), pl.BlockSpec((T, ROPE), lambda h: (0, 0))]
    else:
        e_specs = [stat, pl.BlockSpec((G, T, 1), lambda h: (h, 0, 0))]
    return [col(q_off), col(k_off), col(v_off)], e_specs, stat


def _head(ref, rows, g):
    return ref[rows, g * HEAD:(g + 1) * HEAD]


def attn_fwd(name, mode, scale, H, T, q_src, q_off, k_src, k_off, v_src, v_off, e_q, e_k):
    G, t = ATT_G_FWD, min(ATT_TILE, T)
    nq = T // t
    mla = mode == "mla"

    def body(q_ref, k_ref, v_ref, eq_ref, ek_ref, o_ref, lse_ref, acc_ref):
        def q_loop(qi, _):
            qs = pl.multiple_of(qi * t, t)
            qrows = pl.ds(qs, t)
            qh = [(_head(q_ref, qrows, g).astype(F32) * scale).astype(BF) for g in range(G)]
            eqh = [eq_ref[g, qrows, :] if mla else eq_ref[g, qi] for g in range(G)]
            acc_ref[...] = jnp.zeros_like(acc_ref)

            def step(j, carry, masked):
                ks = pl.multiple_of(j * t, t)
                krows = pl.ds(ks, t)
                out, ps, corrs = [], [], []
                ss = [_scores(mode, _head(k_ref, krows, g).astype(BF), qh[g], ek_ref[krows, :] if mla else ek_ref[g, krows, :],
                              eqh[g], ks, qs, t, masked) for g in range(G)]
                for g in range(G):
                    m, l = carry[g]
                    m_new = jnp.maximum(m, jnp.max(ss[g], axis=0, keepdims=True))
                    p = jnp.exp(ss[g] - m_new)
                    corr = jnp.exp(m - m_new)
                    out.append((m_new, corr * l + jnp.sum(p, axis=0, keepdims=True)))
                    ps.append(p.astype(BF))
                    corrs.append(corr)
                pvs = [lax.dot_general(_head(v_ref, krows, g).astype(BF), ps[g], _TN, preferred_element_type=F32) for g in range(G)]
                for g in range(G):
                    acc_ref[g] = corrs[g] * acc_ref[g] + pvs[g]
                return tuple(out)

            init = tuple((jnp.full((1, t), NEG, F32), jnp.zeros((1, t), F32)) for _ in range(G))
            carry = lax.fori_loop(0, qi, lambda j, cr: step(j, cr, False), init)
            carry = step(qi, carry, True)
            for g in range(G):
                m, l = carry[g]
                o_ref[qrows, g * HEAD:(g + 1) * HEAD] = (acc_ref[g] / l).T.astype(o_ref.dtype)
                lse_ref[g, qi] = m + jnp.log(l)
            return 0

        lax.fori_loop(0, nq, q_loop, 0)

    cols, e_specs, stat = _attn_specs(mode, G, T, t, q_off, k_off, v_off)
    return pl.pallas_call(
        body, name=name, grid=(H // G,), in_specs=cols + e_specs,
        out_specs=[pl.BlockSpec((T, G * HEAD), lambda h: (0, h)), stat],
        out_shape=[jax.ShapeDtypeStruct((T, H * HEAD), BF), jax.ShapeDtypeStruct((H, nq, 1, t), F32)],
        scratch_shapes=[pltpu.VMEM((G, HEAD, t), F32)],
        compiler_params=_cp("parallel"),
    )(q_src, k_src, v_src, e_q, e_k)


def attn_bwd(name, mode, scale, H, T, q_src, q_off, k_src, k_off, v_src, v_off, e_q, e_k, do, lse, delta):
    G, t = ATT_G_BWD, min(ATT_TILE, T)
    nq = T // t
    mla = mode == "mla"

    def body(q_ref, k_ref, v_ref, eq_ref, ek_ref, do_ref, lse_ref, dl_ref,
             dq_ref, dk_ref, dv_ref, deq_ref, dek_ref, dk_acc, dv_acc, dq_acc, deq_acc):
        dk_acc[...] = jnp.zeros_like(dk_acc)
        dv_acc[...] = jnp.zeros_like(dv_acc)
        if mla:
            @pl.when(pl.program_id(0) == 0)
            def _():
                dek_ref[...] = jnp.zeros_like(dek_ref)
        else:
            dek_ref[...] = jnp.zeros_like(dek_ref)

        def q_loop(qi, _):
            qs = pl.multiple_of(qi * t, t)
            qrows = pl.ds(qs, t)
            qh = [(_head(q_ref, qrows, g).astype(F32) * scale).astype(BF) for g in range(G)]
            eqh = [eq_ref[g, qrows, :] if mla else eq_ref[g, qi] for g in range(G)]
            doh = [_head(do_ref, qrows, g) for g in range(G)]
            lse_q = [lse_ref[g, qi] for g in range(G)]
            dl_q = [dl_ref[g, qi] for g in range(G)]
            dq_acc[...] = jnp.zeros_like(dq_acc)
            deq_acc[...] = jnp.zeros_like(deq_acc)

            def step(j, carry, masked):
                ks = pl.multiple_of(j * t, t)
                krows = pl.ds(ks, t)
                kh = [_head(k_ref, krows, g).astype(BF) for g in range(G)]
                ekh = [ek_ref[krows, :] if mla else ek_ref[g, krows, :] for g in range(G)]
                ss = [_scores(mode, kh[g], qh[g], ekh[g], eqh[g], ks, qs, t, masked) for g in range(G)]
                dps = [lax.dot_general(_head(v_ref, krows, g).astype(BF), doh[g], _NT, preferred_element_type=F32) for g in range(G)]
                pbs, dss, dsbs = [], [], []
                for g in range(G):
                    p = jnp.exp(ss[g] - lse_q[g])
                    ds = p * (dps[g] - dl_q[g])
                    pbs.append(p.astype(BF))
                    dss.append(ds)
                    dsbs.append(ds.astype(BF))
                for g in range(G):
                    dv_acc[g, krows, :] += jnp.dot(pbs[g], doh[g], preferred_element_type=F32)
                    dk_acc[g, krows, :] += jnp.dot(dsbs[g], qh[g], preferred_element_type=F32)
                    dq_acc[g] += lax.dot_general(dsbs[g], kh[g], _TN, preferred_element_type=F32)
                    if mla:
                        deq_acc[g] += lax.dot_general(dsbs[g], ekh[g], _TN, preferred_element_type=F32)
                        dek_ref[krows, :] += jnp.dot(dsbs[g], eqh[g], preferred_element_type=F32)
                    else:
                        deq_acc[g] += jnp.sum(dss[g], axis=0, keepdims=True)
                        dek_ref[g, krows, :] -= jnp.sum(dss[g], axis=1, keepdims=True)
                return carry

            lax.fori_loop(0, qi, lambda j, cr: step(j, cr, False), 0)
            step(qi, 0, True)
            for g in range(G):
                dq_ref[qrows, g * HEAD:(g + 1) * HEAD] = (dq_acc[g] * scale).astype(dq_ref.dtype)
                if mla:
                    deq_ref[g, qrows, :] = deq_acc[g]
                else:
                    deq_ref[g, qi] = deq_acc[g]
            return 0

        lax.fori_loop(0, nq, q_loop, 0)
        for g in range(G):
            dk_ref[:, g * HEAD:(g + 1) * HEAD] = dk_acc[g].astype(dk_ref.dtype)
            dv_ref[:, g * HEAD:(g + 1) * HEAD] = dv_acc[g].astype(dv_ref.dtype)

    cols, e_specs, stat = _attn_specs(mode, G, T, t, q_off, k_off, v_off)
    heads = pl.BlockSpec((T, G * HEAD), lambda h: (0, h))
    if mla:
        de_shapes = [jax.ShapeDtypeStruct((H, T, ROPE), F32), jax.ShapeDtypeStruct((T, ROPE), F32)]
        deq_scratch = pltpu.VMEM((G, t, ROPE), F32)
    else:
        de_shapes = [jax.ShapeDtypeStruct((H, nq, 1, t), F32), jax.ShapeDtypeStruct((H, T, 1), F32)]
        deq_scratch = pltpu.VMEM((G, 1, t), F32)
    return pl.pallas_call(
        body, name=name, grid=(H // G,),
        in_specs=cols + e_specs + [heads, stat, stat],
        out_specs=[heads, heads, heads] + e_specs,
        out_shape=[jax.ShapeDtypeStruct((T, H * HEAD), BF)] * 3 + de_shapes,
        scratch_shapes=[pltpu.VMEM((G, T, HEAD), F32), pltpu.VMEM((G, T, HEAD), F32), pltpu.VMEM((G, t, HEAD), F32), deq_scratch],
        compiler_params=_cp("arbitrary"),
    )(q_src, k_src, v_src, e_q, e_k, do, lse, delta)


def attn_delta(name, do, o, H, t):
    T, W = do.shape
    ind = np.zeros((W, LANE), np.float32)
    ind[np.arange(W), np.arange(W) // HEAD] = 1.0

    def f(dv, ov, e):
        return _rot(dv.astype(F32) * ov.astype(F32), e)

    d = rowwise(name, f, [do, o], [jnp.asarray(ind, BF)], [(LANE, F32)])[0]
    return d[:, :H].T.reshape(H, T // t, 1, t)


def fox_cum(name, f, b):
    T = f.shape[0]

    def body(f_ref, b_ref, cum_ref):
        z = f_ref[...] + b_ref[...]
        v = jnp.minimum(z, 0.0) - jnp.log(1.0 + jnp.exp(-jnp.abs(z)))
        row = lax.broadcasted_iota(jnp.int32, v.shape, 0)
        s = 1
        while s < T:
            v = v + jnp.where(row >= s, pltpu.roll(v, s, 0), 0.0)
            s *= 2
        cum_ref[...] = v

    return pl.pallas_call(body, name=name, out_shape=jax.ShapeDtypeStruct(f.shape, F32),
                          compiler_params=pltpu.CompilerParams(vmem_limit_bytes=VMEM_LIMIT_BYTES))(f, b)


def fox_cum_bwd(name, f, b, dcum):
    T = f.shape[0]

    def body(f_ref, b_ref, dc_ref, df_ref, db_ref):
        v = dc_ref[...]
        row = lax.broadcasted_iota(jnp.int32, v.shape, 0)
        s = 1
        while s < T:
            v = v + jnp.where(row < T - s, pltpu.roll(v, T - s, 0), 0.0)
            s *= 2
        z = f_ref[...] + b_ref[...]
        df = v / (1.0 + jnp.exp(z))
        df_ref[...] = df.astype(df_ref.dtype)
        db_ref[...] = _sum0(df)

    return pl.pallas_call(body, name=name,
                          out_shape=[jax.ShapeDtypeStruct(f.shape, BF), jax.ShapeDtypeStruct((1, f.shape[1]), F32)],
                          compiler_params=pltpu.CompilerParams(vmem_limit_bytes=VMEM_LIMIT_BYTES))(f, b, dcum)


def ffn_up(name, h, w1g, w3g, l):
    T, D = h.shape
    FP = w1g.shape[3]
    bm = _pick(T, 1024)

    def body(h_ref, w1_ref, w3_ref, a_ref, b_ref, g_ref):
        hv = h_ref[...]
        a = jnp.dot(hv, w1_ref[...], preferred_element_type=F32)
        b = jnp.dot(hv, w3_ref[...], preferred_element_type=F32)
        a_ref[...] = a.astype(BF)
        b_ref[...] = b.astype(BF)
        g_ref[...] = (a * (1.0 / (1.0 + jnp.exp(-a))) * b).astype(BF)

    w_spec = pl.BlockSpec((None, None, D, FP), lambda i, d: (d, l, 0, 0))
    tile = pl.BlockSpec((None, bm, FP), lambda i, d: (d, i, 0))
    return pl.pallas_call(
        body, name=name, grid=(T // bm, N_DEV), in_specs=[pl.BlockSpec((bm, D), lambda i, d: (i, 0)), w_spec, w_spec],
        out_specs=[tile, tile, tile], out_shape=[jax.ShapeDtypeStruct((N_DEV, T, FP), BF)] * 3,
        compiler_params=_cp("parallel", "parallel"),
    )(h, w1g, w3g)


def ffn_down(name, g, w2g, l):
    _, T, FP = g.shape
    D = w2g.shape[3]
    bm, bn = _pick(T, 1024), _pick(D, 1024)
    return _mm_call(
        name, (T // bm, D // bn, N_DEV), [g, w2g],
        [pl.BlockSpec((None, bm, FP), lambda i, j, d: (d, i, 0)), pl.BlockSpec((None, None, FP, bn), lambda i, j, d: (d, l, 0, j))],
        [_NN], jax.ShapeDtypeStruct((T, D), F32), pl.BlockSpec((bm, bn), lambda i, j, d: (i, j)))


def ffn_down_bwd(name, dy, w2g, a, b, l):
    T, D = dy.shape
    FP = w2g.shape[2]
    bm = _pick(T, 1024)

    def body(dy_ref, w2_ref, a_ref, b_ref, da_ref, db_ref):
        dg = lax.dot_general(dy_ref[...], w2_ref[...], _NT, preferred_element_type=F32)
        av = a_ref[...].astype(F32)
        bv = b_ref[...].astype(F32)
        sig = 1.0 / (1.0 + jnp.exp(-av))
        da_ref[...] = (dg * bv * sig * (1.0 + av * (1.0 - sig))).astype(BF)
        db_ref[...] = (dg * av * sig).astype(BF)

    tile = pl.BlockSpec((None, bm, FP), lambda i, d: (d, i, 0))
    return pl.pallas_call(
        body, name=name, grid=(T // bm, N_DEV),
        in_specs=[pl.BlockSpec((bm, D), lambda i, d: (i, 0)), pl.BlockSpec((None, None, FP, D), lambda i, d: (d, l, 0, 0)), tile, tile],
        out_specs=[tile, tile], out_shape=[jax.ShapeDtypeStruct((N_DEV, T, FP), BF)] * 2,
        compiler_params=_cp("parallel", "parallel"),
    )(dy, w2g, a, b)


def ffn_dh(name, da, db, w1g, w3g, l):
    _, T, FP = da.shape
    D = w1g.shape[2]
    bm, bn = _pick(T, 1024), _pick(D, 1024)
    act = pl.BlockSpec((None, bm, FP), lambda i, j, d: (d, i, 0))
    wgt = pl.BlockSpec((None, None, bn, FP), lambda i, j, d: (d, l, j, 0))
    return _mm_call(name, (T // bm, D // bn, N_DEV), [da, w1g, db, w3g], [act, wgt, act, wgt], [_NT, _NT],
                    jax.ShapeDtypeStruct((T, D), F32), pl.BlockSpec((bm, bn), lambda i, j, d: (i, j)))


def ffn_in_grad(name, h, dact):
    T, D = h.shape
    FP = dact.shape[2]
    bm = _pick(D, 1024)
    return _mm_call(
        name, (D // bm, N_DEV, 1), [h, dact],
        [pl.BlockSpec((T, bm), lambda i, d, k: (0, i)), pl.BlockSpec((None, T, FP), lambda i, d, k: (d, 0, 0))],
        [_TN], jax.ShapeDtypeStruct((N_DEV, D, FP), BF), pl.BlockSpec((None, bm, FP), lambda i, d, k: (d, i, 0)))


def ffn_out_grad(name, g, dy):
    _, T, FP = g.shape
    D = dy.shape[1]
    bn = _pick(D, 1024)
    return _mm_call(
        name, (N_DEV, D // bn, 1), [g, dy],
        [pl.BlockSpec((None, T, FP), lambda d, j, k: (d, 0, 0)), pl.BlockSpec((T, bn), lambda d, j, k: (0, j))],
        [_TN], jax.ShapeDtypeStruct((N_DEV, FP, D), BF), pl.BlockSpec((None, FP, bn), lambda d, j, k: (d, 0, j)))


def _adamw_math(w, g, m, v):
    m = ADAM_B1 * m + (1.0 - ADAM_B1) * g
    v = ADAM_B2 * v + (1.0 - ADAM_B2) * (g * g)
    m_hat = m / (1.0 - ADAM_B1 ** ADAM_STEP)
    v_hat = v / (1.0 - ADAM_B2 ** ADAM_STEP)
    delta = -ADAM_LR * (m_hat / (jnp.sqrt(v_hat) + ADAM_EPS) + ADAM_WD * w)
    return delta, m, v


def adamw(name, w, g, m, v):
    shape = w.shape
    n = shape[-1]
    rows = math.prod(shape[:-1])
    tr = rows if rows <= 512 else 256
    assert rows % tr == 0

    def body(w_ref, g_ref, m_ref, v_ref, d_ref, mo_ref, vo_ref):
        d, mn, vn = _adamw_math(w_ref[...], g_ref[...], m_ref[...], v_ref[...])
        d_ref[...] = d
        mo_ref[...] = mn
        vo_ref[...] = vn

    blk = pl.BlockSpec((tr, n), lambda i: (i, 0))
    outs = pl.pallas_call(
        body, name=name, grid=(rows // tr,), in_specs=[blk] * 4, out_specs=[blk] * 3,
        out_shape=[jax.ShapeDtypeStruct((rows, n), F32)] * 3, compiler_params=_cp("parallel"),
    )(*[t.reshape(rows, n) for t in (w, g, m, v)])
    return [t.reshape(shape) for t in outs]


def _silu(v):
    return v * (1.0 / (1.0 + jnp.exp(-v)))


def ada_fwd(name, c_all, ada_w, ada_b_cols):
    L, D, n = ada_w.shape
    bn = _pick(n, 512)

    def body(c_ref, w_ref, b_ref, o_ref):
        act = _silu(c_ref[...]).astype(BF)
        o_ref[...] = jnp.dot(act, w_ref[...].astype(BF), preferred_element_type=F32) + b_ref[...]

    return pl.pallas_call(
        body, name=name, grid=(L, n // bn),
        in_specs=[pl.BlockSpec((N_DEV, D), lambda l, j: (0, 0)), pl.BlockSpec((None, D, bn), lambda l, j: (l, 0, j)),
                  pl.BlockSpec((None, 1, bn), lambda l, j: (l, 0, j))],
        out_specs=pl.BlockSpec((None, N_DEV, bn), lambda l, j: (l, 0, j)),
        out_shape=jax.ShapeDtypeStruct((L, N_DEV, n), F32), compiler_params=_cp("parallel", "parallel"),
    )(c_all, ada_w, ada_b_cols.reshape(L, 1, n))


def ada_bwd_adamw(name, c_all, dmod_cols, w, m, v):
    L, D, n = w.shape
    tr = _pick(D, 256)

    def body(c_ref, dm_ref, w_ref, m_ref, v_ref, g_ref, d_ref, mo_ref, vo_ref):
        act = _silu(c_ref[...]).astype(BF)
        g = lax.dot_general(act, dm_ref[...].astype(BF), _TN, preferred_element_type=F32)
        d, mn, vn = _adamw_math(w_ref[...], g, m_ref[...], v_ref[...])
        g_ref[...] = g
        d_ref[...] = d
        mo_ref[...] = mn
        vo_ref[...] = vn

    blk = pl.BlockSpec((None, tr, n), lambda l, i: (l, i, 0))
    return pl.pallas_call(
        body, name=name, grid=(L, D // tr),
        in_specs=[pl.BlockSpec((N_DEV, tr), lambda l, i: (0, i)), pl.BlockSpec((None, N_DEV, n), lambda l, i: (l, 0, 0)),
                  blk, blk, blk],
        out_specs=[blk] * 4, out_shape=[jax.ShapeDtypeStruct((L, D, n), F32)] * 4,
        compiler_params=_cp("parallel", "parallel"),
    )(c_all, dmod_cols, w, m, v)


_MESH = pl.DeviceIdType.MESH
_ANY = pl.BlockSpec(memory_space=pl.ANY)
_CHIP_FLIPS = ((1, 0), (0, 1), (1, 1))


def _place():
    return lax.axis_index("x"), lax.axis_index("y"), lax.axis_index("c")


def _flip(v, f):
    return 1 - v if f else v


def all_gather(name, shards, in_vmem=False):
    nt = len(shards)

    def body(*refs):
        x_refs, out_refs = refs[:nt], refs[nt:2 * nt]
        send_sems, recv_sems, local_sems = refs[2 * nt:]
        x, y, c = _place()
        me, sibling = (x, y, c), (x, y, 1 - c)
        chips = [(_flip(x, fx), _flip(y, fy)) for fx, fy in _CHIP_FLIPS]

        def rows(t, px, py, pc):
            m_per = shards[t].shape[0]
            return out_refs[t].at[pl.ds((4 * px + 2 * py + pc) * m_per, m_per), :]

        def copy(t, k, block, to, src=None):
            return pltpu.make_async_remote_copy(
                src_ref=rows(t, *block) if src is None else src, dst_ref=rows(t, *block),
                send_sem=send_sems.at[7 * t + k], recv_sem=recv_sems.at[7 * t + k], device_id=to, device_id_type=_MESH)

        mine = [pltpu.make_async_copy(x_refs[t], rows(t, *me), local_sems.at[t]) for t in range(nt)]
        for cp in mine:
            cp.start()
        first = []
        for j, chip in enumerate(chips):
            first += [copy(t, 1 + j, me, (*chip, c), src=x_refs[t]) for t in range(nt)]
        first += [copy(t, 0, me, sibling, src=x_refs[t]) for t in range(nt)]
        for cp in first:
            cp.start()
        passed = []
        for j, chip in enumerate(chips):
            for t in range(nt):
                copy(t, 1 + j, (*chip, c), me).wait_recv()
                passed.append(copy(t, 4 + j, (*chip, c), sibling))
                passed[-1].start()
        for t in range(nt):
            copy(t, 0, sibling, me).wait_recv()
            for j, chip in enumerate(chips):
                copy(t, 4 + j, (*chip, 1 - c), me).wait_recv()
        for cp in first + passed:
            cp.wait_send()
        for cp in mine:
            cp.wait()

    space = pl.BlockSpec(memory_space=pltpu.VMEM) if in_vmem else _ANY
    return pl.pallas_call(
        body, name=name, out_shape=[jax.ShapeDtypeStruct((N_DEV * s.shape[0], s.shape[1]), s.dtype) for s in shards],
        in_specs=[space] * nt, out_specs=[space] * nt,
        scratch_shapes=[pltpu.SemaphoreType.DMA((7 * nt,)), pltpu.SemaphoreType.DMA((7 * nt,)), pltpu.SemaphoreType.DMA((nt,))],
    )(*shards)


_HBM = pl.BlockSpec(memory_space=pltpu.HBM)
_SEM = pl.BlockSpec(memory_space=pltpu.SEMAPHORE)
_EFFECT = pltpu.SideEffectType.DATAFLOW_SIDE_EFFECTING


def _in_hbm(a):
    return pltpu.with_memory_space_constraint(a, pltpu.HBM)


def _gather_first_copies(x_refs, land_refs, send_sems, recv_sems, local_sems):
    nt = len(x_refs)
    x, y, c = _place()
    me = (x, y, c)

    def rows(t, px, py, pc):
        m_per = x_refs[t].shape[0]
        return land_refs[t].at[pl.ds((4 * px + 2 * py + pc) * m_per, m_per), :]

    local = [pltpu.make_async_copy(x_refs[t], rows(t, *me), local_sems.at[t]) for t in range(nt)]
    remote = []
    peers = [(_flip(x, fx), _flip(y, fy), c) for fx, fy in _CHIP_FLIPS] + [(x, y, 1 - c)]
    for k, peer in enumerate(peers):
        remote += [(pltpu.make_async_remote_copy(
            src_ref=x_refs[t], dst_ref=rows(t, *me), send_sem=send_sems.at[4 * t + k], recv_sem=recv_sems.at[4 * t + k],
            device_id=peer, device_id_type=_MESH), rows(t, *peer), 4 * t + k) for t in range(nt)]
    return local, remote


def gather_start(name, shards, after):
    nt = len(shards)

    def body(*refs):
        x_refs, land_refs = refs[:nt], refs[nt:2 * nt]
        send_sems, recv_sems, local_sems = refs[2 * nt + 1:2 * nt + 4]
        token = refs[-1]
        local, remote = _gather_first_copies(x_refs, land_refs, send_sems, recv_sems, local_sems)
        for cp in local:
            cp.start()
        for cp, _, _ in remote:
            cp.start()
        token[...] = jnp.zeros_like(token)

    lands = [lax.empty((N_DEV * s.shape[0], s.shape[1]), s.dtype) for s in shards]
    outs = pl.pallas_call(
        body, name=name,
        out_shape=[pltpu.SemaphoreType.DMA((4 * nt,)), pltpu.SemaphoreType.DMA((4 * nt,)), pltpu.SemaphoreType.DMA((nt,))]
        + [pltpu.HBM(s.shape, s.dtype) for s in shards] + [pltpu.HBM(a.shape, a.dtype) for a in lands]
        + [jax.ShapeDtypeStruct((8, LANE), F32)],
        in_specs=[_HBM] * (2 * nt) + [_ANY], out_specs=[_SEM] * 3 + [_HBM] * (2 * nt) + [pl.BlockSpec(memory_space=pltpu.VMEM)],
        input_output_aliases={i: 3 + i for i in range(2 * nt)},
        compiler_params=pltpu.CompilerParams(has_side_effects=_EFFECT),
    )(*[_in_hbm(s) for s in shards], *[_in_hbm(a) for a in lands], after)
    return outs[:3], outs[3:3 + nt], outs[3 + nt:3 + 2 * nt], outs[-1]


def gather_wait(name, handle, after):
    sems, shards, lands, _ = handle
    nt = len(shards)

    def body(*refs):
        x_refs, land_refs = refs[:nt], refs[nt:2 * nt]
        send_sems, recv_sems, local_sems = refs[2 * nt:2 * nt + 3]
        local, remote = _gather_first_copies(x_refs, land_refs, send_sems, recv_sems, local_sems)
        for cp, landed, k in remote:
            cp.wait_send()
            pltpu.make_async_remote_copy(src_ref=landed, dst_ref=landed, send_sem=send_sems.at[k], recv_sem=recv_sems.at[k],
                                         device_id=_place(), device_id_type=_MESH).wait_recv()
        for cp in local:
            cp.wait()

    outs = pl.pallas_call(
        body, name=name, out_shape=[pltpu.HBM(s.shape, s.dtype) for s in shards] + [pltpu.HBM(a.shape, a.dtype) for a in lands],
        in_specs=[_HBM] * (2 * nt) + [_SEM] * 3 + [_ANY], out_specs=[_HBM] * (2 * nt),
        input_output_aliases={i: i for i in range(2 * nt)},
        compiler_params=pltpu.CompilerParams(has_side_effects=_EFFECT),
    )(*shards, *lands, *sems, after)
    return outs[nt:]


def _forward_copies(land_refs, send_sems, recv_sems):
    nt = len(land_refs)
    x, y, c = _place()
    copies = []
    for j, (fx, fy) in enumerate(_CHIP_FLIPS):
        px, py = _flip(x, fx), _flip(y, fy)
        for t in range(nt):
            m_per = land_refs[t].shape[0] // N_DEV
            mine = land_refs[t].at[pl.ds((4 * px + 2 * py + c) * m_per, m_per), :]
            theirs = land_refs[t].at[pl.ds((4 * px + 2 * py + 1 - c) * m_per, m_per), :]
            copies.append((pltpu.make_async_remote_copy(
                src_ref=mine, dst_ref=mine, send_sem=send_sems.at[3 * t + j], recv_sem=recv_sems.at[3 * t + j],
                device_id=(x, y, 1 - c), device_id_type=_MESH), theirs, 3 * t + j))
    return copies


def _wait_all(copies, send_sems, recv_sems):
    for cp, landed, k in copies:
        cp.wait_send()
        pltpu.make_async_remote_copy(src_ref=landed, dst_ref=landed, send_sem=send_sems.at[k], recv_sem=recv_sems.at[k],
                                     device_id=_place(), device_id_type=_MESH).wait_recv()


def gather_forward(name, lands):
    nt = len(lands)

    def body(*refs):
        send_sems, recv_sems = refs[2 * nt:]
        copies = _forward_copies(refs[:nt], send_sems, recv_sems)
        for cp, _, _ in copies:
            cp.start()
        _wait_all(copies, send_sems, recv_sems)

    return pl.pallas_call(
        body, name=name, out_shape=[jax.ShapeDtypeStruct(a.shape, a.dtype) for a in lands],
        in_specs=[_ANY] * nt, out_specs=[_ANY] * nt, input_output_aliases={i: i for i in range(nt)},
        scratch_shapes=[pltpu.SemaphoreType.DMA((3 * nt,)), pltpu.SemaphoreType.DMA((3 * nt,))],
    )(*lands)


def forward_start(name, lands):
    nt = len(lands)

    def body(*refs):
        for cp, _, _ in _forward_copies(refs[:nt], refs[nt], refs[nt + 1]):
            cp.start()
        refs[-1][...] = jnp.zeros_like(refs[-1])

    outs = pl.pallas_call(
        body, name=name,
        out_shape=[pltpu.SemaphoreType.DMA((3 * nt,)), pltpu.SemaphoreType.DMA((3 * nt,))]
        + [pltpu.HBM(a.shape, a.dtype) for a in lands] + [jax.ShapeDtypeStruct((8, LANE), F32)],
        in_specs=[_HBM] * nt, out_specs=[_SEM] * 2 + [_HBM] * nt + [pl.BlockSpec(memory_space=pltpu.VMEM)],
        input_output_aliases={i: 2 + i for i in range(nt)},
        compiler_params=pltpu.CompilerParams(has_side_effects=_EFFECT),
    )(*[_in_hbm(a) for a in lands])
    return outs[:2], outs[2:2 + nt], outs[-1]


def forward_wait(name, handle, after):
    sems, lands, _ = handle
    nt = len(lands)

    def body(*refs):
        send_sems, recv_sems = refs[nt], refs[nt + 1]
        _wait_all(_forward_copies(refs[:nt], send_sems, recv_sems), send_sems, recv_sems)

    return pl.pallas_call(
        body, name=name, out_shape=[pltpu.HBM(a.shape, a.dtype) for a in lands],
        in_specs=[_HBM] * nt + [_SEM] * 2 + [_ANY], out_specs=[_HBM] * nt,
        input_output_aliases={i: i for i in range(nt)},
        compiler_params=pltpu.CompilerParams(has_side_effects=_EFFECT),
    )(*lands, *sems, after)


def _chip_copies(p_refs, land_refs, send_sems, recv_sems):
    nt = len(p_refs)
    x, y, c = _place()
    copies = []
    for j, (fx, fy) in enumerate(_CHIP_FLIPS):
        px, py = _flip(x, fx), _flip(y, fy)
        copies += [(pltpu.make_async_remote_copy(
            src_ref=p_refs[t].at[2 * px + py], dst_ref=land_refs[t].at[j], send_sem=send_sems.at[3 * t + j],
            recv_sem=recv_sems.at[3 * t + j], device_id=(px, py, c), device_id_type=_MESH), land_refs[t].at[j], 3 * t + j)
            for t in range(nt)]
    return copies


def _pair_copies(g_refs, land_refs, send_sems, recv_sems):
    nt = len(g_refs)
    x, y, c = _place()
    return [(pltpu.make_async_remote_copy(
        src_ref=g_refs[t].at[2 * k + 1 - c], dst_ref=land_refs[t].at[k], send_sem=send_sems.at[4 * t + k],
        recv_sem=recv_sems.at[4 * t + k], device_id=(x, y, 1 - c), device_id_type=_MESH), land_refs[t].at[k], 4 * t + k)
        for t in range(nt) for k in range(4)]


def exchange_start(name, copies_fn, n_land, parts):
    nt = len(parts)

    def body(*refs):
        p_refs, land_refs = refs[:nt], refs[nt:2 * nt]
        send_sems, recv_sems = refs[2 * nt:2 * nt + 2]
        token = refs[-1]
        for cp, _, _ in copies_fn(p_refs, land_refs, send_sems, recv_sems):
            cp.start()
        token[...] = jnp.zeros_like(token)

    lands = [lax.empty((n_land,) + p.shape[1:], p.dtype) for p in parts]
    outs = pl.pallas_call(
        body, name=name,
        out_shape=[pltpu.SemaphoreType.DMA((n_land * nt,)), pltpu.SemaphoreType.DMA((n_land * nt,))]
        + [pltpu.HBM(p.shape, p.dtype) for p in parts] + [pltpu.HBM(a.shape, a.dtype) for a in lands]
        + [jax.ShapeDtypeStruct((8, LANE), F32)],
        in_specs=[_HBM] * (2 * nt), out_specs=[_SEM] * 2 + [_HBM] * (2 * nt) + [pl.BlockSpec(memory_space=pltpu.VMEM)],
        input_output_aliases={i: 2 + i for i in range(2 * nt)},
        compiler_params=pltpu.CompilerParams(has_side_effects=_EFFECT),
    )(*[_in_hbm(p) for p in parts], *[_in_hbm(a) for a in lands])
    return outs[:2], outs[2:2 + nt], outs[2 + nt:2 + 2 * nt], outs[-1]


def exchange_wait(name, copies_fn, handle, after):
    sems, parts, lands, _ = handle
    nt = len(parts)

    def body(*refs):
        p_refs, land_refs = refs[:nt], refs[nt:2 * nt]
        send_sems, recv_sems = refs[2 * nt:2 * nt + 2]
        for cp, landed, k in copies_fn(p_refs, land_refs, send_sems, recv_sems):
            cp.wait_send()
            pltpu.make_async_remote_copy(src_ref=landed, dst_ref=landed, send_sem=send_sems.at[k], recv_sem=recv_sems.at[k],
                                         device_id=_place(), device_id_type=_MESH).wait_recv()

    outs = pl.pallas_call(
        body, name=name, out_shape=[pltpu.HBM(p.shape, p.dtype) for p in parts] + [pltpu.HBM(a.shape, a.dtype) for a in lands],
        in_specs=[_HBM] * (2 * nt) + [_SEM] * 2 + [_ANY], out_specs=[_HBM] * (2 * nt),
        input_output_aliases={i: i for i in range(2 * nt)},
        compiler_params=pltpu.CompilerParams(has_side_effects=_EFFECT),
    )(*parts, *lands, *sems, after)
    return outs[:nt], outs[nt:]


def _row_tile(rows):
    return rows if rows <= 1024 else 1024


def pair_add(name, g, recv, core):
    _, R, C = g.shape
    tr = _row_tile(R)

    def body(core_ref, g_ref, r_ref, o_ref):
        o_ref[...] = (g_ref[...].astype(F32) + r_ref[...].astype(F32)).astype(o_ref.dtype)

    return pl.pallas_call(
        body, name=name, out_shape=jax.ShapeDtypeStruct((4, R, C), BF),
        grid_spec=pltpu.PrefetchScalarGridSpec(
            num_scalar_prefetch=1, grid=(4, R // tr),
            in_specs=[pl.BlockSpec((None, tr, C), lambda k, i, core_ref: (2 * k + core_ref[0], i, 0)),
                      pl.BlockSpec((None, tr, C), lambda k, i, core_ref: (k, i, 0))],
            out_specs=pl.BlockSpec((None, tr, C), lambda k, i, core_ref: (k, i, 0))),
        compiler_params=_cp("parallel", "parallel"),
    )(core, g, recv)


def chip_add(name, part, recv, chip):
    _, R, C = part.shape
    tr = _row_tile(R)

    def body(chip_ref, p_ref, r_ref, o_ref):
        o_ref[...] = ((p_ref[...].astype(F32) + r_ref[0].astype(F32)) + r_ref[1].astype(F32)) + r_ref[2].astype(F32)

    return pl.pallas_call(
        body, name=name, out_shape=jax.ShapeDtypeStruct((R, C), F32),
        grid_spec=pltpu.PrefetchScalarGridSpec(
            num_scalar_prefetch=1, grid=(R // tr,),
            in_specs=[pl.BlockSpec((None, tr, C), lambda i, chip_ref: (chip_ref[0], i, 0)),
                      pl.BlockSpec((3, tr, C), lambda i, chip_ref: (0, i, 0))],
            out_specs=pl.BlockSpec((tr, C), lambda i, chip_ref: (i, 0))),
        compiler_params=_cp("parallel"),
    )(chip, part, recv)


def reduce_scatter_pair(tag, gs):
    return exchange_start("pair_exchange_start_" + tag, _pair_copies, 4, gs)


def reduce_scatter_chips(tag, handle, core, after):
    gs, from_pair = exchange_wait("pair_exchange_wait_" + tag, _pair_copies, handle, after)
    parts = [pair_add("pair_add_%s_%d" % (tag, t), g, r, core) for t, (g, r) in enumerate(zip(gs, from_pair))]
    return exchange_start("chip_exchange_start_" + tag, _chip_copies, 3, parts)


def reduce_scatter_finish(tag, handle, chip, after):
    parts, landed = exchange_wait("chip_exchange_wait_" + tag, _chip_copies, handle, after)
    return [chip_add("chip_add_%s_%d" % (tag, t), p, r, chip) for t, (p, r) in enumerate(zip(parts, landed))]


def sum_devices(name, g):
    _, R, C = g.shape

    def body(g_ref, o_ref):
        acc = g_ref[0]
        for d in range(1, N_DEV):
            acc = acc + g_ref[d]
        o_ref[...] = acc

    return pl.pallas_call(body, name=name, out_shape=jax.ShapeDtypeStruct((R, C), F32),
                          compiler_params=pltpu.CompilerParams(vmem_limit_bytes=VMEM_LIMIT_BYTES))(g)


_WEIGHTS = ["ada_w", "ada_b", "ln1_g", "ln1_b", "ln2_g", "ln2_b", "ffn_w1", "ffn_w3", "ffn_w2", "mla_w_down", "mla_q_norm",
            "mla_w_uq", "mla_kv_norm", "mla_w_uk", "mla_w_uv", "mla_w_o", "fox_w_in", "fox_b_f", "fox_w_o"]
_SMALL = ["ada_b", "ln1_g", "ln1_b", "ln2_g", "ln2_b", "mla_q_norm", "mla_kv_norm", "fox_b_f"]


def _pad_cols(a, width):
    return jnp.pad(a, ((0, 0), (0, width - a.shape[1])))


def _small_pack(parts, cols):
    rows = [jnp.pad(p.reshape(-1), (0, _round_up(p.size, cols) - p.size)).reshape(-1, cols) for p in parts]
    out = jnp.concatenate(rows, axis=0)
    return jnp.pad(out, ((0, _round_up(out.shape[0], 8) - out.shape[0]), (0, 0)))


def kernel(x, c, positions, ada_w, ada_b, ln1_g, ln1_b, ln2_g, ln2_b, ffn_w1, ffn_w3, ffn_w2, mla_w_down, mla_q_norm, mla_w_uq, mla_kv_norm, mla_w_uk, mla_w_uv, mla_w_o, fox_w_in, fox_b_f, fox_w_o, loss_target, m_ada_w, m_ada_b, m_ln1_g, m_ln1_b, m_ln2_g, m_ln2_b, m_ffn_w1, m_ffn_w3, m_ffn_w2, m_mla_w_down, m_mla_q_norm, m_mla_w_uq, m_mla_kv_norm, m_mla_w_uk, m_mla_w_uv, m_mla_w_o, m_fox_w_in, m_fox_b_f, m_fox_w_o, v_ada_w, v_ada_b, v_ln1_g, v_ln1_b, v_ln2_g, v_ln2_b, v_ffn_w1, v_ffn_w3, v_ffn_w2, v_mla_w_down, v_mla_q_norm, v_mla_w_uq, v_mla_kv_norm, v_mla_w_uk, v_mla_w_uv, v_mla_w_o, v_fox_w_in, v_fox_b_f, v_fox_w_o):
    env = dict(locals())
    W = {n: env[n] for n in _WEIGHTS}
    M = {n: env["m_" + n] for n in _WEIGHTS}
    V = {n: env["v_" + n] for n in _WEIGHTS}

    T, D = x.shape[1], x.shape[2]
    L = ada_w.shape[0]
    QL, KVL = mla_q_norm.shape[1], mla_kv_norm.shape[1]
    HA = mla_w_uq.shape[2] * N_DEV // (NOPE + ROPE)
    HB = fox_b_f.shape[1]
    f_loc = ffn_w1.shape[2]
    FP = _round_up(f_loc, LANE)
    n_ada = ada_w.shape[2]
    n_in = fox_w_in.shape[2]
    n_in_pad = _round_up(n_in, LANE)
    k_loc = mla_w_down.shape[1]
    n_uq, n_uk = mla_w_uq.shape[2], mla_w_uk.shape[2]
    assert D == HB * HEAD == HA * HEAD and n_ada * N_DEV == 6 * D
    alpha = (2 * L) ** 0.25
    xi, yi, ci = _place()
    me = 4 * xi + 2 * yi + ci
    core = jnp.reshape(ci, (1,)).astype(jnp.int32)
    chip = jnp.reshape(2 * xi + yi, (1,)).astype(jnp.int32)

    x2 = x[0]
    tgt = loss_target[0]

    c_all = all_gather("gather_c", [jnp.pad(c, ((0, 7), (0, 0)))], True)[0].reshape(N_DEV, 8, D)[:, 0]
    ada_b_cols = lax.dynamic_slice_in_dim(ada_b, me * n_ada, n_ada, axis=1)
    mod_part = ada_fwd("ada_fwd", c_all, ada_w, ada_b_cols)
    mod_all = all_gather("gather_mod", [mod_part.reshape(L * N_DEV, n_ada)], True)[0].reshape(N_DEV, L, N_DEV, n_ada)
    mod = lax.dynamic_index_in_dim(mod_all, me, axis=2, keepdims=False)
    mod = jnp.moveaxis(mod, 0, 1).reshape(L, 6, D)

    fox_cols = 3 * D + LANE

    def bf(a, pad_rows=0, pad_cols=0):
        return jnp.pad(a.astype(BF), ((0, pad_rows), (0, pad_cols)))

    groups = []
    for i in range(L):
        j = i // 2
        if i % 2 == 0:
            groups.append(("mla%d" % i, [bf(mla_w_down[j]), bf(mla_w_uq[j]), bf(mla_w_uk[j]), bf(mla_w_uv[j]), bf(mla_w_o[j])]))
        else:
            groups.append(("fox%d" % i, [bf(fox_w_in[j], 0, n_in_pad - n_in), bf(fox_w_o[j])]))
        groups.append(("ffn%d" % i, [bf(ffn_w1[i], 0, FP - f_loc), bf(ffn_w3[i], 0, FP - f_loc), bf(ffn_w2[i], FP - f_loc, 0)]))
    handles = {}
    tok = mod
    for gname, shards in groups:
        handles[gname] = gather_start("gather_start_" + gname, shards, tok)
        tok = handles[gname][3]
    mod = mod + tok[0, 0]

    forwarding = {}
    order_of = [gname for gname, _ in groups]

    def prefetch(gname, after):
        if gname is None or order_of.index(gname) < 4:
            return 0.0
        lands = gather_wait("gather_wait_" + gname, handles[gname], after)
        forwarding[gname] = forward_start("gather_forward_start_" + gname, lands)
        return forwarding[gname][2][0, 0]

    def gathered(gname, after):
        if gname in forwarding:
            lands = forward_wait("gather_forward_wait_" + gname, forwarding[gname], after)
        else:
            lands = gather_wait("gather_wait_" + gname, handles[gname], after)
            lands = gather_forward("gather_forward_" + gname, lands)
        return [a.reshape((N_DEV,) + s.shape) for a, s in zip(lands, dict(groups)[gname])]

    def cols_whole(g, keep=None):
        return jnp.moveaxis(g[:, :, :keep], 0, 1).reshape(g.shape[1], -1)

    def modv(i, k):
        return mod[i, k][None, :]

    b_f = _pad_cols(fox_b_f, LANE)

    inv_freq = ROPE_THETA ** (-jnp.arange(0, ROPE, 2, dtype=F32) / ROPE)
    ang = jnp.repeat(positions[0].astype(F32)[:, None] * inv_freq, 2, axis=1)
    cos_k, sin_k = jnp.cos(ang), jnp.sin(ang)
    cos_q, sin_q = jnp.tile(cos_k, (1, HA)), jnp.tile(sin_k, (1, HA))
    rot_k, rot_q = _pair_rotation(ROPE), _pair_rotation(HA * ROPE)
    tk = min(ATT_TILE, T)
    mla_scale = (NOPE + ROPE) ** -0.5
    fox_scale = HEAD ** -0.5

    def modulate(tag, xin, sc, sh):
        return rowwise("modulate_" + tag, lambda xv, scv, shv: xv * (1.0 + scv) + shv, [xin], [sc, sh], [(D, BF)])[0]

    def post(tag, xin, y, g, gam, bet, nsc, nsh):
        def f(xv, yv, gv, gamv, betv, scv, shv):
            xhat, _ = _ln_stats(alpha, xv, yv, gv)
            out = xhat * gamv + betv
            return out, out * (1.0 + scv) + shv
        return rowwise("post_" + tag, f, [xin, y], [g, gam, bet, nsc, nsh], [(D, F32), (D, BF)])

    def post_bwd(tag, dxo, xin, y, g, gam):
        def f(dv, xv, yv, gv, gamv):
            xhat, rstd = _ln_stats(alpha, xv, yv, gv)
            dxh = dv * gamv
            dz = rstd * (dxh - _mean1(dxh) - xhat * _mean1(dxh * xhat))
            return alpha * dz, (1.0 + gv) * dz, _sum0(dv * xhat), _sum0(dv), _sum0(dz * yv)
        return rowwise("post_bwd_" + tag, f, [dxo, xin, y], [g, gam], [(D, F32), (D, BF)], [D, D, D])

    def join_bwd(tag, dxa, dh, sc, xprev, yprev, g, gam, bet):
        def f(dxav, dhv, xpv, ypv, scv, gv, gamv, betv):
            xhat, rstd = _ln_stats(alpha, xpv, ypv, gv)
            dv = dxav + dhv * (1.0 + scv)
            dxh = dv * gamv
            dz = rstd * (dxh - _mean1(dxh) - xhat * _mean1(dxh * xhat))
            return (alpha * dz, (1.0 + gv) * dz, _sum0(dhv * (xhat * gamv + betv)), _sum0(dhv),
                    _sum0(dv * xhat), _sum0(dv), _sum0(dz * ypv))
        return rowwise("join_bwd_" + tag, f, [dxa, dh, xprev, yprev], [sc, g, gam, bet], [(D, F32), (D, BF)], [D] * 5)

    def pre_bwd(tag, dxa, dh, xin, sc):
        def f(dxav, dhv, xv, scv):
            return dxav + dhv * (1.0 + scv), _sum0(dhv * xv), _sum0(dhv)
        return rowwise("pre_bwd_" + tag, f, [dxa, dh, xin], [sc], [(D, F32)], [D, D])

    saved = []
    xc = x2
    h = modulate("l0a", xc, modv(0, 1), modv(0, 0))
    for i in range(L):
        j = i // 2
        tag = "l%d" % i
        s = {"x_a": xc}
        s["h_a"] = h
        if i % 2 == 0:
            wdn, wuq, wuk, wuv, wo = gathered("mla%d" % i, xc)
            w_down, w_o = wdn.reshape(D, -1), wo.reshape(D, D)
            q3 = cols_whole(wuq).reshape(QL, HA, NOPE + ROPE)
            w_uq = jnp.concatenate([q3[..., :NOPE].reshape(QL, HA * NOPE), q3[..., NOPE:].reshape(QL, HA * ROPE)], axis=-1)
            w_ukv = jnp.concatenate([cols_whole(wuk), cols_whole(wuv)], axis=-1)
            s.update(w_down=w_down, w_o=w_o, w_uq=w_uq, w_ukv=w_ukv)
            lat = mm("mla_down_" + tag, h, w_down)

            def mid(latv, ck, sk, qg, kg, p):
                ql, kl, kp = latv[:, :QL], latv[:, QL:QL + KVL], latv[:, QL + KVL:]
                qn = ql * lax.rsqrt(_mean1(ql * ql) + RMS_EPS) * qg
                cn = kl * lax.rsqrt(_mean1(kl * kl) + RMS_EPS) * kg
                return qn, cn, kp * ck + _rot(kp, p) * sk

            qn, ckv, kpe = rowwise("mla_mid_" + tag, mid, [lat, cos_k, sin_k],
                                   [mla_q_norm[j][None], mla_kv_norm[j][None], rot_k], [(QL, BF), (KVL, BF), (ROPE, BF)])
            qf = mm("mla_uq_" + tag, qn, w_uq)

            def qrope(qv, cq, sq, p):
                qp = qv[:, HA * NOPE:]
                return (qp * cq + _rot(qp, p) * sq) * mla_scale

            qpe = rowwise("mla_qrope_" + tag, qrope, [qf, cos_q, sin_q], [rot_q], [(HA * ROPE, BF)])[0]
            qpe_h = jnp.moveaxis(qpe.reshape(T, HA, ROPE), 1, 0)
            kv = mm("mla_ukv_" + tag, ckv, w_ukv)
            ahead = prefetch("ffn%d" % i, kv)
            o, lse = attn_fwd("mla_attn_" + tag, "mla", mla_scale, HA, T, qf, 0, kv, 0, kv, HA, qpe_h, kpe)
            y = mm("mla_o_" + tag, o, w_o)
            s.update(lat=lat, qn=qn, ckv=ckv, kpe=kpe, qf=qf, qpe_h=qpe_h, kv=kv, o=o, lse=lse)
        else:
            win, wo = gathered("fox%d" % i, xc)
            w_in, w_o = _pad_cols(cols_whole(win, keep=n_in), fox_cols), wo.reshape(D, D)
            s.update(w_in=w_in, w_o=w_o)
            proj = mm("fox_in_" + tag, h, w_in, bn=896)
            fz = proj[:, 3 * D:]
            cum = fox_cum("fox_cum_" + tag, fz, b_f[j][None])
            cum_h = cum[:, :HB].T
            ahead = prefetch("ffn%d" % i, cum)
            o, lse = attn_fwd("fox_attn_" + tag, "fox", fox_scale, HB, T, proj, 0, proj, HB, proj, 2 * HB,
                              cum_h.reshape(HB, T // tk, 1, tk), cum_h.reshape(HB, T, 1))
            y = mm("fox_o_" + tag, o, w_o)
            s.update(proj=proj, fz=fz, cum_h=cum_h, o=o, lse=lse)
        s["y_a"] = y
        xc, h = post(tag + "a", xc, y, modv(i, 2) + ahead, ln1_g[i][None], ln1_b[i][None], modv(i, 4), modv(i, 3))
        s["x_f"] = xc
        w1g, w3g, w2g = [w[:, None] for w in gathered("ffn%d" % i, xc)]
        a, b, gte = ffn_up("ffn_up_" + tag, h, w1g, w3g, 0)
        nxt = min(i + 1, L - 1)
        ahead = prefetch(None if i == L - 1 else ("mla%d" if nxt % 2 == 0 else "fox%d") % nxt, gte)
        y = ffn_down("ffn_down_" + tag, gte, w2g, 0)
        s.update(h_f=h, a=a, b=b, gte=gte, y_f=y, w1g=w1g, w3g=w3g, w2g=w2g)
        xc, h = post(tag + "f", xc, y, modv(i, 5) + ahead, ln2_g[i][None], ln2_b[i][None], modv(nxt, 1), modv(nxt, 0))
        saved.append(s)

    def loss_fn(yv, tv):
        e = yv - tv
        return e * (1.0 / D), jnp.zeros((1, LANE), F32) + _sum0(jnp.sum(e * e, axis=1, keepdims=True)) * (0.5 / D)

    dx, loss_part = rowwise("loss", loss_fn, [xc, tgt], [], [(D, F32)], [LANE])

    G = {n: [None] * W[n].shape[0] for n in _WEIGHTS if n != "ada_w"}
    pending, pair_open = [], []

    def scatter(kind, idx, tag, gs):
        handle = reduce_scatter_pair(tag, gs)
        pair_open.append((kind, idx, tag, handle))
        return handle[3][0, 0]

    def scatter_on(after):
        pkind, pidx, ptag, phandle = pair_open.pop()
        chips = reduce_scatter_chips(ptag, phandle, core, after)
        pending.append((pkind, pidx, chips))
        return chips[3][0, 0]
    dmod = [[None] * 6 for _ in range(L)]
    s = saved[L - 1]
    dxa, dy, G["ln2_g"][L - 1], G["ln2_b"][L - 1], dmod[L - 1][5] = post_bwd(
        "l%df" % (L - 1), dx, s["x_f"], s["y_f"], modv(L - 1, 5), ln2_g[L - 1][None])
    for i in reversed(range(L)):
        j = i // 2
        tag = "l%d" % i
        s = saved[i]
        g_w2 = ffn_out_grad("ffn_w2_grad_" + tag, s["gte"], dy)
        da, db = ffn_down_bwd("ffn_down_bwd_" + tag, dy, s["w2g"], s["a"], s["b"], 0)
        order = scatter_on(da) if pair_open else 0.0
        g_w1 = ffn_in_grad("ffn_w1_grad_" + tag, s["h_f"], da)
        g_w3 = ffn_in_grad("ffn_w3_grad_" + tag, s["h_f"], db)
        dh = ffn_dh("ffn_dh_" + tag, da, db, s["w1g"], s["w3g"], 0)
        order = order + scatter("ffn", i, "ffn_" + tag, [g_w1, g_w3, g_w2])
        if i == 0:
            order = order + scatter_on(dh)
        dxa, dy, dmod[i][4], dmod[i][3], G["ln1_g"][i], G["ln1_b"][i], dmod[i][2] = join_bwd(
            tag + "f", dxa, dh, modv(i, 4) + order, s["x_a"], s["y_a"], modv(i, 2), ln1_g[i][None], ln1_b[i][None])
        if i % 2 == 0:
            g_o = mm("mla_o_grad_" + tag, s["o"], dy, ta=True, out_dtype=BF)
            do = mm("mla_do_" + tag, dy, s["w_o"], tb=True, out_dtype=BF)
            delta = attn_delta("mla_delta_" + tag, do, s["o"], HA, tk)
            dqn_, dkn_, dv_, dqpe_h, dkpe = attn_bwd("mla_attn_bwd_" + tag, "mla", mla_scale, HA, T, s["qf"], 0, s["kv"], 0,
                                                     s["kv"], HA, s["qpe_h"], s["kpe"], do, s["lse"], delta)
            order = scatter_on(dkpe) if pair_open else 0.0
            dkv = jnp.concatenate([dkn_, dv_], axis=1)
            g_kv = mm("mla_ukv_grad_" + tag, s["ckv"], dkv, ta=True, out_dtype=BF)
            dckv = mm("mla_dckv_" + tag, dkv, s["w_ukv"], tb=True)

            def qrope_bwd(dv, cq, sq, p):
                dv = dv * mla_scale
                return dv * cq - _rot(dv * sq, p)

            dqpe = rowwise("mla_qrope_bwd_" + tag, qrope_bwd, [jnp.moveaxis(dqpe_h, 0, 1).reshape(T, HA * ROPE), cos_q, sin_q],
                           [rot_q], [(HA * ROPE, BF)])[0]
            dq = jnp.concatenate([dqn_, dqpe], axis=1)
            g_uq = mm("mla_uq_grad_" + tag, s["qn"], dq, ta=True, out_dtype=BF)
            g_uq = jnp.concatenate([g_uq[:, :HA * NOPE].reshape(QL, HA, NOPE), g_uq[:, HA * NOPE:].reshape(QL, HA, ROPE)],
                                   axis=-1).reshape(QL, N_DEV, n_uq)
            dqn = mm("mla_dqn_" + tag, dq, s["w_uq"], tb=True)

            def mid_bwd(latv, dqv, dcv, dkp, ck, sk, qg, kg, p):
                ql, kl = latv[:, :QL], latv[:, QL:QL + KVL]
                rq = lax.rsqrt(_mean1(ql * ql) + RMS_EPS)
                rk = lax.rsqrt(_mean1(kl * kl) + RMS_EPS)
                uq, uk = dqv * qg, dcv * kg
                dql = rq * uq - ql * (rq * rq * rq) * _mean1(uq * ql)
                dkl = rk * uk - kl * (rk * rk * rk) * _mean1(uk * kl)
                return dql, dkl, dkp * ck - _rot(dkp * sk, p), _sum0(dqv * ql * rq), _sum0(dcv * kl * rk)

            dql, dkl, dkp, G["mla_q_norm"][j], G["mla_kv_norm"][j] = rowwise(
                "mla_mid_bwd_" + tag, mid_bwd, [s["lat"], dqn, dckv, dkpe, cos_k, sin_k],
                [mla_q_norm[j][None], mla_kv_norm[j][None], rot_k], [(QL, BF), (KVL, BF), (ROPE, BF)], [QL, KVL])
            dlat = jnp.concatenate([dql, dkl, dkp], axis=1)
            g_dn = mm("mla_down_grad_" + tag, s["h_a"], dlat, ta=True, out_dtype=BF)
            dh = mm("mla_dh_" + tag, dlat, s["w_down"], tb=True)
            g_kv = jnp.moveaxis(g_kv.reshape(KVL, 2, N_DEV, n_uk), 2, 0).reshape(N_DEV, KVL, 2 * n_uk)
            order = order + scatter("mla", j, "mla_" + tag, [
                g_dn.reshape(N_DEV, k_loc, -1), jnp.moveaxis(g_uq, 1, 0), g_kv, g_o.reshape(N_DEV, k_loc, D)])
        else:
            g_o = mm("fox_o_grad_" + tag, s["o"], dy, ta=True, out_dtype=BF)
            do = mm("fox_do_" + tag, dy, s["w_o"], tb=True, out_dtype=BF)
            cum_h = s["cum_h"]
            delta = attn_delta("fox_delta_" + tag, do, s["o"], HB, tk)
            dq_, dk_, dv_, dcq, dck = attn_bwd("fox_attn_bwd_" + tag, "fox", fox_scale, HB, T, s["proj"], 0, s["proj"], HB,
                                               s["proj"], 2 * HB, cum_h.reshape(HB, T // tk, 1, tk), cum_h.reshape(HB, T, 1),
                                               do, s["lse"], delta)
            order = scatter_on(dck) if pair_open else 0.0
            dcum = _pad_cols((dcq.reshape(HB, T) + dck.reshape(HB, T)).T, LANE)
            df, db_f = fox_cum_bwd("fox_cum_bwd_" + tag, s["fz"], b_f[j][None], dcum)
            G["fox_b_f"][j] = db_f[:, :HB]
            dproj = jnp.concatenate([dq_, dk_, dv_, df], axis=1)
            g_in = mm("fox_in_grad_" + tag, s["h_a"], dproj, ta=True, out_dtype=BF, bn=896)[:, :N_DEV * n_in]
            dh = mm("fox_dh_" + tag, dproj, s["w_in"], tb=True, bk=896)
            g_in = jnp.pad(jnp.moveaxis(g_in.reshape(D, N_DEV, n_in), 1, 0), ((0, 0), (0, 0), (0, n_in_pad - n_in)))
            order = order + scatter("fox", j, "fox_" + tag, [g_in, g_o.reshape(N_DEV, k_loc, D)])
        if i > 0:
            p = saved[i - 1]
            dxa, dy, dmod[i][1], dmod[i][0], G["ln2_g"][i - 1], G["ln2_b"][i - 1], dmod[i - 1][5] = join_bwd(
                tag + "a", dxa, dh, modv(i, 1) + order, p["x_f"], p["y_f"], modv(i - 1, 5), ln2_g[i - 1][None], ln2_b[i - 1][None])
        else:
            dx, dmod[i][1], dmod[i][0] = pre_bwd(tag + "a", dxa, dh, s["x_a"], modv(i, 1) + order)

    scatter_on(dx)

    def finish(kind, idx, handle, after):
        res = reduce_scatter_finish("%s_%d" % (kind, idx), handle, chip, after)
        if kind == "ffn":
            G["ffn_w1"][idx], G["ffn_w3"][idx], G["ffn_w2"][idx] = res[0][:, :f_loc], res[1][:, :f_loc], res[2][:f_loc]
        elif kind == "mla":
            G["mla_w_down"][idx], G["mla_w_uq"][idx], G["mla_w_o"][idx] = res[0], res[1], res[3]
            G["mla_w_uk"][idx], G["mla_w_uv"][idx] = res[2][:, :n_uk], res[2][:, n_uk:]
        else:
            G["fox_w_in"][idx], G["fox_w_o"][idx] = res[0][:, :n_in], res[1]

    for kind, idx, handle in pending[:-1]:
        finish(kind, idx, handle, dx)

    dmod_mine = jnp.concatenate([jnp.concatenate(r, axis=1) for r in dmod], axis=0)
    small_parts = [dmod_mine] + [jnp.concatenate(G[n], axis=0) for n in _SMALL[1:]] + [loss_part]
    small = _small_pack(small_parts, 512)
    small_all = all_gather("gather_small", [small], True)[0].reshape(N_DEV, -1, 512)
    small_sum = sum_devices("sum_small", small_all)
    sums, off = [], 0
    for p in small_parts:
        nrow = _round_up(p.size, 512) // 512
        sums.append(small_sum[off:off + nrow].reshape(-1)[:p.size].reshape(p.shape))
        off += nrow
    grads = {n: gsum for n, gsum in zip(_SMALL, sums[:-1])}
    loss = sums[-1][0, 0]
    nrow = L * 6 * D // 512
    dmod_all = small_all[:, :nrow].reshape(N_DEV, L, 6 * D)
    dmod_cols = jnp.moveaxis(lax.dynamic_slice_in_dim(dmod_all, me * n_ada, n_ada, axis=2), 0, 1)

    delta, new_m, new_v = {}, {}, {}
    grads["ada_w"], delta["ada_w"], new_m["ada_w"], new_v["ada_w"] = ada_bwd_adamw(
        "ada_grad_adamw", c_all, dmod_cols, ada_w, m_ada_w, v_ada_w)
    last = ("mla_w_down", "mla_w_uq", "mla_w_uk", "mla_w_uv", "mla_w_o")
    for n in ("ffn_w1", "ffn_w3", "ffn_w2", "fox_w_in", "fox_w_o"):
        grads[n] = jnp.stack(G[n])
    for n in _WEIGHTS:
        if n != "ada_w" and n not in last:
            delta[n], new_m[n], new_v[n] = adamw("adamw_" + n, W[n], grads[n], M[n], V[n])
    finish(*pending[-1], delta["ffn_w2"])
    for n in last:
        grads[n] = jnp.stack(G[n])
        delta[n], new_m[n], new_v[n] = adamw("adamw_" + n, W[n], grads[n], M[n], V[n])

    return (loss, dx[None], *[grads[n] for n in _WEIGHTS], *[delta[n] for n in _WEIGHTS],
            *[new_m[n] for n in _WEIGHTS], *[new_v[n] for n in _WEIGHTS])
```
